```python
import functools
import jax
import jax.numpy as jnp
from jax import lax
import numpy as np

D_MODEL = 1024
BATCH = 16
SEQ = 2048
DEPTH = 2
DEC_BATCH = 32
DEC_SEQ = 8
PAST_LEN = 16384
PAGE_SIZE = 128

N_AB_LAYERS = (DEPTH + 1) // 2
N_C_LAYERS = DEPTH // 2

GLA_HEADS = 4
GLA_DK = D_MODEL // 16
GLA_DV = D_MODEL // 8
GLA_LOWRANK = 16
GLA_GATE_NORM = 16.0
GLA_CHUNK = 64

NSA_HEAD_DIM = 64
NSA_HEADS = (D_MODEL // 2) // NSA_HEAD_DIM
NSA_KV_HEADS = 2
NSA_HPG = NSA_HEADS // NSA_KV_HEADS
NSA_BLOCK = 64
NSA_TOP_K = 16
NSA_WINDOW = 512
NSA_CMP_HIDDEN = 2 * NSA_HEAD_DIM
NSA_Q_BLOCK = 32
FORCED_SCORE = 1000.0

GDN_HEADS = 8
GDN_DK = D_MODEL // GDN_HEADS
GDN_DV = D_MODEL // GDN_HEADS
GDN_CONV = 4
GDN_CHUNK = 64

FF_HIDDEN = ((8 * D_MODEL + 3 * 256 - 1) // (3 * 256)) * 256
DEEPNORM_ALPHA = (2.0 * DEPTH) ** 0.25
DEEPNORM_BETA = (8.0 * DEPTH) ** -0.25
LN_EPS = 1e-5
RMS_EPS = 1e-6
L2_EPS = 1e-6

GLA_KEY_W = GLA_HEADS * GLA_DK
GLA_VAL_W = GLA_HEADS * GLA_DV
NSA_Q_W = NSA_HEADS * NSA_HEAD_DIM
NSA_KV_W = NSA_KV_HEADS * NSA_HEAD_DIM
AB_SIZES = (GLA_KEY_W, GLA_KEY_W, GLA_VAL_W, GLA_VAL_W, GLA_LOWRANK, NSA_Q_W, 6 * NSA_KV_W, 3 * NSA_HEADS)
AB_IN = sum(AB_SIZES)
AB_OUT = GLA_VAL_W + NSA_Q_W
GDN_KEY_W = GDN_HEADS * GDN_DK
GDN_VAL_W = GDN_HEADS * GDN_DV
GDN_CONV_CH = 2 * GDN_KEY_W + GDN_VAL_W
C_SIZES = (GDN_CONV_CH, GDN_HEADS, GDN_HEADS, GDN_VAL_W)
C_IN = sum(C_SIZES)

F32 = jnp.float32
NEG = -1e30

kernel_name = 'hybrid_gla_nsa_gdn_decode_step'


def _split(x, sizes):
    return jnp.split(x, [int(s) for s in np.cumsum(sizes)[:-1]], axis=-1)


def _layernorm(x, g, b):
    mu = jnp.mean(x, -1, keepdims=True)
    var = jnp.mean(jnp.square(x - mu), -1, keepdims=True)
    return (x - mu) * lax.rsqrt(var + LN_EPS) * g.astype(F32) + b.astype(F32)


def _rmsnorm(x, g):
    return x * lax.rsqrt(jnp.mean(jnp.square(x), -1, keepdims=True) + RMS_EPS) * g.astype(F32)


def _l2norm(x):
    return x * lax.rsqrt(jnp.sum(jnp.square(x), -1, keepdims=True) + L2_EPS)


def _masked_softmax(s, mask):
    s = jnp.where(mask, s, NEG)
    e = jnp.exp(s - jnp.max(s, -1, keepdims=True)) * mask
    return e / jnp.maximum(jnp.sum(e, -1, keepdims=True), 1e-30)


def _alibi_slopes():
    return jnp.exp2(-8.0 * jnp.arange(1, NSA_HEADS + 1, dtype=F32) / NSA_HEADS)


def _adaln(c, w, b):
    return _split(jax.nn.silu(c) @ w + b, [D_MODEL] * 6)


def _modulate(x, shift, scale):
    return x * (1 + scale[:, None, :]) + shift[:, None, :]


def _post_norm(x, gate, o, g, b):
    z = DEEPNORM_ALPHA * x.astype(F32) + gate[:, None, :].astype(F32) * o.astype(F32)
    return _layernorm(z, g, b).astype(x.dtype)


def _swiglu(h, w_in, w_out):
    a, u = jnp.split(h @ w_in, 2, axis=-1)
    return (jax.nn.silu(a) * u) @ w_out


def _pad_chunks(x, chunk, t_pad):
    x = jnp.pad(x.astype(F32), [(0, 0), (0, t_pad - x.shape[1])] + [(0, 0)] * (x.ndim - 2))
    return jnp.moveaxis(x.reshape((x.shape[0], t_pad // chunk, chunk) + x.shape[2:]), 1, 0)


def _gla(q, k, v, log_a, s0):
    B, T, H, _ = q.shape
    C = min(GLA_CHUNK, T)
    t_pad = -(-T // C) * C
    xs = tuple(_pad_chunks(a, C, t_pad) for a in (q, k, v, log_a))
    causal = jnp.tril(jnp.ones((C, C), dtype=bool))[None, :, :, None, None]

    def step(S, inp):
        qc, kc, vc, ac = inp
        b = jnp.cumsum(ac, axis=1)
        decay = jnp.exp(jnp.where(causal, b[:, :, None] - b[:, None, :], -jnp.inf))
        scores = jnp.einsum('bihk,bjhk,bijhk->bhij', qc, kc, decay)
        o = jnp.einsum('bihk,bhkv->bihv', qc * jnp.exp(b), S) + jnp.einsum('bhij,bjhv->bihv', scores, vc)
        b_last = b[:, -1]
        S = S * jnp.exp(b_last)[..., None] + jnp.einsum('bjhk,bjhv->bhkv', kc * jnp.exp(b_last[:, None] - b), vc)
        return S, o

    S, o = lax.scan(step, s0.astype(F32), xs)
    return jnp.moveaxis(o, 0, 1).reshape(B, t_pad, H, -1)[:, :T], S


def _gdn(q, k, v, g, beta, s0):
    B, T, H, _ = q.shape
    C = min(GDN_CHUNK, T)
    t_pad = -(-T // C) * C
    xs = tuple(_pad_chunks(a, C, t_pad) for a in (q, k, v, g, beta))
    causal = jnp.tril(jnp.ones((C, C), dtype=bool))
    strict = jnp.tril(jnp.ones((C, C), dtype=F32), -1)
    eye = jnp.eye(C, dtype=F32)

    def step(S, inp):
        qc, kc, vc, gc, bc = inp
        d = jnp.moveaxis(jnp.cumsum(gc, axis=1), 1, 2)
        decay = jnp.exp(jnp.where(causal, d[..., :, None] - d[..., None, :], -jnp.inf))
        qh = jnp.moveaxis(qc, 1, 2)
        kh = jnp.moveaxis(kc, 1, 2)
        bh = jnp.moveaxis(bc, 1, 2)[..., None]
        vh = jnp.moveaxis(vc, 1, 2)
        kb = kh * bh
        t_mat = jnp.einsum('bhik,bhjk->bhij', kb, kh) * decay * strict + eye
        u = lax.linalg.triangular_solve(t_mat, vh * bh, left_side=True, lower=True, unit_diagonal=True)
        w = lax.linalg.triangular_solve(t_mat, kb * jnp.exp(d)[..., None], left_side=True, lower=True, unit_diagonal=True)
        v_new = u - jnp.einsum('bhck,bhkv->bhcv', w, S)
        o = (jnp.einsum('bhck,bhkv->bhcv', qh * jnp.exp(d)[..., None], S)
             + jnp.einsum('bhij,bhjv->bhiv', jnp.einsum('bhik,bhjk->bhij', qh, kh) * decay, v_new))
        d_last = d[..., -1]
        S = S * jnp.exp(d_last)[..., None, None] + jnp.einsum('bhck,bhcv->bhkv', kh * jnp.exp(d_last[..., None] - d)[..., None], v_new)
        return S, o

    S, o = lax.scan(step, s0.astype(F32), xs)
    return jnp.transpose(o, (1, 0, 3, 2, 4)).reshape(B, t_pad, H, -1)[:, :T], S


def _compress(kx, pe, w1, w2):
    B, L, G, HD = kx.shape
    n_cmp = L // NSA_BLOCK
    blocks = kx[:, :n_cmp * NSA_BLOCK].astype(F32).reshape(B, n_cmp, NSA_BLOCK, G, HD) + pe.astype(F32)[:, None, :]
    flat = jnp.moveaxis(blocks, 3, 2).reshape(B, n_cmp, G, NSA_BLOCK * HD)
    return jax.nn.silu(flat @ w1) @ w2


def _to_blocks(kx):
    B, L, G, HD = kx.shape
    n_sel = -(-L // NSA_BLOCK)
    kx = jnp.pad(kx, ((0, 0), (0, n_sel * NSA_BLOCK - L), (0, 0), (0, 0)))
    return jnp.transpose(kx.reshape(B, n_sel, NSA_BLOCK, G, HD), (0, 3, 1, 2, 4))


def _nsa_query_block(q, gates, qpos, kc, vc, ks_blk, vs_blk, kw, vw, kwpos):
    B, Tq = q.shape[:2]
    G, HPG, HD, BLK = NSA_KV_HEADS, NSA_HPG, NSA_HEAD_DIM, NSA_BLOCK
    qg = q.astype(F32).reshape(B, Tq, G, HPG, HD) * HD ** -0.5
    slopes = _alibi_slopes().reshape(1, G, HPG, 1, 1)
    tq = qpos.astype(F32)
    n_cmp = kc.shape[1]
    blk = jnp.arange(n_cmp)
    cmp_center = (blk * BLK).astype(F32) + 0.5 * (BLK - 1)
    cmp_mask = (blk * BLK + BLK - 1)[None, :] <= qpos[:, None]
    s_c = jnp.einsum('btghd,bngd->bghtn', qg, kc.astype(F32)) - slopes * (tq[:, None] - cmp_center[None, :])
    p_c = _masked_softmax(s_c, cmp_mask)
    o_c = jnp.einsum('bghtn,bngd->btghd', p_c, vc.astype(F32))
    n_sel = ks_blk.shape[2]
    k_sel = min(NSA_TOP_K, n_sel)
    score = jnp.pad(jnp.sum(p_c, axis=2), ((0, 0), (0, 0), (0, 0), (0, n_sel - n_cmp)))
    sel = jnp.arange(n_sel)[None, :]
    cur = (qpos // BLK)[:, None]
    forced = (sel == 0) | (sel == cur) | (sel == cur - 1)
    visible = sel * BLK <= qpos[:, None]
    score = jnp.where(visible, jnp.where(forced, FORCED_SCORE, score), -1.0)
    _, idx = lax.top_k(score, k_sel)
    bi = jnp.arange(B)[:, None, None, None]
    gi = jnp.arange(G)[None, :, None, None]
    k_g = ks_blk[bi, gi, idx].astype(F32)
    v_g = vs_blk[bi, gi, idx].astype(F32)
    dist_s = qpos[:, None, None] - (idx[..., None] * BLK + jnp.arange(BLK))
    s_s = jnp.einsum('btghd,bgtkpd->bghtkp', qg, k_g) - slopes[..., None] * dist_s[:, :, None].astype(F32)
    p_s = _masked_softmax(s_s.reshape(B, G, HPG, Tq, k_sel * BLK),
                          (dist_s >= 0).reshape(B, G, 1, Tq, k_sel * BLK))
    o_s = jnp.einsum('bghtkp,bgtkpd->btghd', p_s.reshape(B, G, HPG, Tq, k_sel, BLK), v_g)
    dist_w = qpos[:, None] - kwpos[None, :]
    win_mask = (kwpos[None, :] >= 0) & (dist_w >= 0) & (dist_w <= NSA_WINDOW)
    s_w = jnp.einsum('btghd,bsgd->bghts', qg, kw.astype(F32)) - slopes * dist_w.astype(F32)
    p_w = _masked_softmax(s_w, win_mask)
    o_w = jnp.einsum('bghts,bsgd->btghd', p_w, vw.astype(F32))
    gt = gates.astype(F32).reshape(B, Tq, G, HPG, 3, 1)
    o = gt[..., 0, :] * o_c + gt[..., 1, :] * o_s + gt[..., 2, :] * o_w
    return o.reshape(B, Tq, G * HPG * HD)


def _ab_mixer(h, weights, gla_s0, past):
    w_in, w_gk2, b_gk, gla_norm, cmp_pe, cmp_w1, cmp_w2, w_out = weights
    B, T, _ = h.shape
    q_a, k_a, v_a, r_a, gk_a, q_b, kv_b, gate_b = _split((h @ w_in).astype(F32), AB_SIZES)
    log_a = jax.nn.log_sigmoid(gk_a @ w_gk2 + b_gk) / GLA_GATE_NORM
    heads = lambda x, d: x.reshape(B, T, GLA_HEADS, d)
    if gla_s0 is None:
        gla_s0 = jnp.zeros((B, GLA_HEADS, GLA_DK, GLA_DV), F32)
    o_a, s_a = _gla(heads(q_a, GLA_DK) * GLA_DK ** -0.5, heads(k_a, GLA_DK), heads(v_a, GLA_DV), heads(log_a, GLA_DK), gla_s0)
    o_a = (_rmsnorm(o_a, gla_norm) * jax.nn.silu(heads(r_a, GLA_DV))).reshape(B, T, GLA_VAL_W)
    kv = kv_b.reshape(B, T, 6, NSA_KV_HEADS, NSA_HEAD_DIM)
    k_cmp, v_cmp, k_sel, v_sel, k_win, v_win = [kv[:, :, j] for j in range(6)]
    q_n = q_b.reshape(B, T, NSA_HEADS, NSA_HEAD_DIM)
    gates = jax.nn.sigmoid(gate_b).reshape(B, T, NSA_HEADS, 3)
    if past is None:
        kc_f = _compress(k_cmp, cmp_pe[0], cmp_w1[0], cmp_w2[0])
        vc_f = _compress(v_cmp, cmp_pe[1], cmp_w1[1], cmp_w2[1])
        ks_blk, vs_blk = _to_blocks(k_sel), _to_blocks(v_sel)
        pad_w = ((0, 0), (NSA_WINDOW, 0), (0, 0), (0, 0))
        kw_pad, vw_pad = jnp.pad(k_win, pad_w), jnp.pad(v_win, pad_w)
        n_qb = T // NSA_Q_BLOCK
        to_qb = lambda x: jnp.moveaxis(x.reshape((B, n_qb, NSA_Q_BLOCK) + x.shape[2:]), 1, 0)

        def body(args):
            qb, gb, s0 = args
            qpos = s0 + jnp.arange(NSA_Q_BLOCK)
            kwb = lax.dynamic_slice_in_dim(kw_pad, s0, NSA_WINDOW + NSA_Q_BLOCK, axis=1)
            vwb = lax.dynamic_slice_in_dim(vw_pad, s0, NSA_WINDOW + NSA_Q_BLOCK, axis=1)
            kwpos = s0 - NSA_WINDOW + jnp.arange(NSA_WINDOW + NSA_Q_BLOCK)
            return _nsa_query_block(qb, gb, qpos, kc_f, vc_f, ks_blk, vs_blk, kwb, vwb, kwpos)

        o_b = lax.map(body, (to_qb(q_n), to_qb(gates), jnp.arange(n_qb) * NSA_Q_BLOCK))
        o_b = jnp.moveaxis(o_b, 0, 1).reshape(B, T, NSA_Q_W)
        n_keep = min(NSA_WINDOW, T)
        win_k_new, win_v_new = k_win[:, T - n_keep:], v_win[:, T - n_keep:]
    else:
        page_table, pk_c, pv_c, pk_s, pv_s, buf_k, buf_v = past
        past_len = page_table.shape[1] * PAGE_SIZE

        def gather(pool, new):
            rows = pool[page_table].reshape(B, past_len, NSA_KV_HEADS, NSA_HEAD_DIM)
            return jnp.concatenate([rows.astype(F32), new], axis=1)

        kc_f = _compress(gather(pk_c, k_cmp), cmp_pe[0], cmp_w1[0], cmp_w2[0])
        vc_f = _compress(gather(pv_c, v_cmp), cmp_pe[1], cmp_w1[1], cmp_w2[1])
        ks_blk, vs_blk = _to_blocks(gather(pk_s, k_sel)), _to_blocks(gather(pv_s, v_sel))
        kw = jnp.concatenate([buf_k.astype(F32), k_win], axis=1)
        vw = jnp.concatenate([buf_v.astype(F32), v_win], axis=1)
        n_buf = buf_k.shape[1]
        kwpos = past_len - n_buf + jnp.arange(n_buf + T)
        qpos = past_len + jnp.arange(T)
        o_b = _nsa_query_block(q_n, gates, qpos, kc_f, vc_f, ks_blk, vs_blk, kw, vw, kwpos)
        win_k_new, win_v_new = kw[:, T:], vw[:, T:]
    out = jnp.concatenate([o_a, o_b], axis=-1) @ w_out
    return out, (k_cmp, v_cmp, k_sel, v_sel, win_k_new, win_v_new, s_a)


def _c_mixer(h, weights, gdn_s0, conv_buf):
    w_in, conv_w, a_log, dt_bias, norm_g, w_out = weights
    B, T, _ = h.shape
    qkv, b_, a_, z = _split((h @ w_in).astype(F32), C_SIZES)
    if conv_buf is None:
        conv_buf = jnp.zeros((B, GDN_CONV - 1, GDN_CONV_CH), F32)
        gdn_s0 = jnp.zeros((B, GDN_HEADS, GDN_DK, GDN_DV), F32)
    xpad = jnp.concatenate([conv_buf.astype(F32), qkv], axis=1)
    y = lax.conv_general_dilated(xpad, conv_w.astype(F32)[:, None, :], window_strides=(1,), padding='VALID',
                                 dimension_numbers=('NWC', 'WIO', 'NWC'), feature_group_count=GDN_CONV_CH)
    q, k, v = _split(jax.nn.silu(y), (GDN_KEY_W, GDN_KEY_W, GDN_VAL_W))
    q = _l2norm(q.reshape(B, T, GDN_HEADS, GDN_DK)) * GDN_DK ** -0.5
    k = _l2norm(k.reshape(B, T, GDN_HEADS, GDN_DK))
    v = v.reshape(B, T, GDN_HEADS, GDN_DV)
    beta = jax.nn.sigmoid(b_)
    g = -jnp.exp(a_log.astype(F32)) * jax.nn.softplus(a_ + dt_bias.astype(F32))
    o, S = _gdn(q, k, v, g, beta, gdn_s0)
    o = _rmsnorm(o, norm_g) * jax.nn.silu(z.reshape(B, T, GDN_HEADS, GDN_DV))
    out = o.reshape(B, T, GDN_VAL_W) @ w_out
    return out, (S, xpad[:, -(GDN_CONV - 1):])


def _layer(x, mod, mixer, ln_g, ln_b, w_ffn_in, w_ffn_out):
    shift1, scale1, gate1, shift2, scale2, gate2 = mod
    o, st = mixer(_modulate(x, shift1, scale1))
    x = _post_norm(x, gate1, o, ln_g[0], ln_b[0])
    f = _swiglu(_modulate(x, shift2, scale2), w_ffn_in, w_ffn_out)
    x = _post_norm(x, gate2, f, ln_g[1], ln_b[1])
    return x, st


def setup_inputs(seed: int = 0) -> dict:
    key = jax.random.key(seed)
    keys = iter(jax.random.split(key, 48))
    nrm = lambda shape, scale: jax.random.normal(next(keys), shape, F32) * scale
    n_pages = PAST_LEN // PAGE_SIZE
    n_used = DEC_BATCH * n_pages
    n_pool = n_used + -(-n_used // 4)
    win_buf = min(NSA_WINDOW, PAST_LEN)
    pool_shape = (N_AB_LAYERS, n_pool, PAGE_SIZE, NSA_KV_HEADS, NSA_HEAD_DIM)
    win_shape = (N_AB_LAYERS, DEC_BATCH, win_buf, NSA_KV_HEADS, NSA_HEAD_DIM)
    page_table = jax.random.permutation(next(keys), n_pool)[:n_used].reshape(DEC_BATCH, n_pages).astype(jnp.int32)
    dt = jnp.exp(jax.random.uniform(next(keys), (N_C_LAYERS, GDN_HEADS), F32, np.log(1e-3), np.log(1e-1)))
    return {
        'x_prompt': nrm((BATCH, SEQ, D_MODEL), 1.0),
        'x_sample': nrm((DEC_BATCH, DEC_SEQ, D_MODEL), 1.0),
        'c_prompt': nrm((BATCH, D_MODEL), 1.0),
        'c_sample': nrm((DEC_BATCH, D_MODEL), 1.0),
        'page_table': page_table,
        'cache_cmp_k': nrm(pool_shape, 1.0),
        'cache_cmp_v': nrm(pool_shape, 1.0),
        'cache_sel_k': nrm(pool_shape, 1.0),
        'cache_sel_v': nrm(pool_shape, 1.0),
        'state_win_k': nrm(win_shape, 1.0),
        'state_win_v': nrm(win_shape, 1.0),
        'state_gla': nrm((N_AB_LAYERS, DEC_BATCH, GLA_HEADS, GLA_DK, GLA_DV), 0.5),
        'state_gdn': nrm((N_C_LAYERS, DEC_BATCH, GDN_HEADS, GDN_DK, GDN_DV), 0.1),
        'state_gdn_conv': nrm((N_C_LAYERS, DEC_BATCH, GDN_CONV - 1, GDN_CONV_CH), 1.0),
        'w_ada': nrm((DEPTH, D_MODEL, 6 * D_MODEL), 0.5 * D_MODEL ** -0.5),
        'b_ada': nrm((DEPTH, 6 * D_MODEL), 0.01),
        'ln_g': 1.0 + nrm((DEPTH, 2, D_MODEL), 0.01),
        'ln_b': nrm((DEPTH, 2, D_MODEL), 0.01),
        'w_ffn_in': nrm((DEPTH, D_MODEL, 2 * FF_HIDDEN), D_MODEL ** -0.5),
        'w_ffn_out': nrm((DEPTH, FF_HIDDEN, D_MODEL), DEEPNORM_BETA * FF_HIDDEN ** -0.5),
        'ab_w_in': nrm((N_AB_LAYERS, D_MODEL, AB_IN), D_MODEL ** -0.5),
        'ab_w_gk2': nrm((N_AB_LAYERS, GLA_LOWRANK, GLA_KEY_W), GLA_LOWRANK ** -0.5),
        'ab_b_gk': nrm((N_AB_LAYERS, GLA_KEY_W), 0.1),
        'ab_gla_norm': 1.0 + nrm((N_AB_LAYERS, GLA_DV), 0.01),
        'ab_cmp_pe': nrm((N_AB_LAYERS, 2, NSA_BLOCK, NSA_HEAD_DIM), 0.1),
        'ab_cmp_w1': nrm((N_AB_LAYERS, 2, NSA_BLOCK * NSA_HEAD_DIM, NSA_CMP_HIDDEN), (NSA_BLOCK * NSA_HEAD_DIM) ** -0.5),
        'ab_cmp_w2': nrm((N_AB_LAYERS, 2, NSA_CMP_HIDDEN, NSA_HEAD_DIM), NSA_CMP_HIDDEN ** -0.5),
        'ab_w_out': nrm((N_AB_LAYERS, AB_OUT, D_MODEL), DEEPNORM_BETA * AB_OUT ** -0.5),
        'c_w_in': nrm((N_C_LAYERS, D_MODEL, C_IN), D_MODEL ** -0.5),
        'c_conv_w': nrm((N_C_LAYERS, GDN_CONV, GDN_CONV_CH), GDN_CONV ** -0.5),
        'c_a_log': jnp.log(jax.random.uniform(next(keys), (N_C_LAYERS, GDN_HEADS), F32, 1.0, 16.0)),
        'c_dt_bias': dt + jnp.log(-jnp.expm1(-dt)),
        'c_norm': 1.0 + nrm((N_C_LAYERS, GDN_DV), 0.01),
        'c_w_out': nrm((N_C_LAYERS, GDN_VAL_W, D_MODEL), DEEPNORM_BETA * GDN_VAL_W ** -0.5),
    }


def reference(x_prompt, x_sample, c_prompt, c_sample, page_table,
              cache_cmp_k, cache_cmp_v, cache_sel_k, cache_sel_v, state_win_k, state_win_v,
              state_gla, state_gdn, state_gdn_conv,
              w_ada, b_ada, ln_g, ln_b, w_ffn_in, w_ffn_out,
              ab_w_in, ab_w_gk2, ab_b_gk, ab_gla_norm, ab_cmp_pe, ab_cmp_w1, ab_cmp_w2, ab_w_out,
              c_w_in, c_conv_w, c_a_log, c_dt_bias, c_norm, c_w_out):
    xp, xs = x_prompt, x_sample
    ab_p, ab_s, c_p, c_s = [], [], [], []
    for layer in range(DEPTH):
        ffn = (ln_g[layer], ln_b[layer], w_ffn_in[layer], w_ffn_out[layer])
        mod_p = _adaln(c_prompt, w_ada[layer], b_ada[layer])
        mod_s = _adaln(c_sample, w_ada[layer], b_ada[layer])
        if layer % 2 == 0:
            i = layer // 2
            w = (ab_w_in[i], ab_w_gk2[i], ab_b_gk[i], ab_gla_norm[i], ab_cmp_pe[i], ab_cmp_w1[i], ab_cmp_w2[i], ab_w_out[i])
            past = (page_table, cache_cmp_k[i], cache_cmp_v[i], cache_sel_k[i], cache_sel_v[i], state_win_k[i], state_win_v[i])
            xp, st_p = _layer(xp, mod_p, functools.partial(_ab_mixer, weights=w, gla_s0=None, past=None), *ffn)
            xs, st_s = _layer(xs, mod_s, functools.partial(_ab_mixer, weights=w, gla_s0=state_gla[i], past=past), *ffn)
            ab_p.append(st_p)
            ab_s.append(st_s)
        else:
            i = layer // 2
            w = (c_w_in[i], c_conv_w[i], c_a_log[i], c_dt_bias[i], c_norm[i], c_w_out[i])
            xp, st_p = _layer(xp, mod_p, functools.partial(_c_mixer, weights=w, gdn_s0=None, conv_buf=None), *ffn)
            xs, st_s = _layer(xs, mod_s, functools.partial(_c_mixer, weights=w, gdn_s0=state_gdn[i], conv_buf=state_gdn_conv[i]), *ffn)
            c_p.append(st_p)
            c_s.append(st_s)
    p_cmp_k, p_cmp_v, p_sel_k, p_sel_v, p_win_k, p_win_v, p_gla = [jnp.stack(z) for z in zip(*ab_p)]
    s_cmp_k, s_cmp_v, s_sel_k, s_sel_v, s_win_k, s_win_v, s_gla = [jnp.stack(z) for z in zip(*ab_s)]
    p_gdn, p_conv = [jnp.stack(z) for z in zip(*c_p)]
    s_gdn, s_conv = [jnp.stack(z) for z in zip(*c_s)]
    return (xp, xs, p_cmp_k, p_cmp_v, p_sel_k, p_sel_v, p_win_k, p_win_v, p_gla, p_gdn, p_conv,
            s_cmp_k, s_cmp_v, s_sel_k, s_sel_v, s_win_k, s_win_v, s_gla, s_gdn, s_conv)
```

```python
import functools

import jax
import jax.numpy as jnp
from jax import lax
from jax.experimental import pallas as pl
from jax.experimental.pallas import tpu as pltpu

F32 = jnp.float32
BF16 = jnp.bfloat16

D_MODEL = 1024
DEPTH = 2
PAGE_SIZE = 128
GLA_HEADS = 4
GLA_DK = 64
GLA_DV = 128
GLA_LOWRANK = 16
GLA_GATE_NORM = 16.0
NSA_HEAD_DIM = 64
NSA_HEADS = 8
NSA_KV_HEADS = 2
NSA_HPG = 4
NSA_BLOCK = 64
NSA_TOP_K = 16
NSA_WINDOW = 512
NSA_CMP_HIDDEN = 128
FORCED_SCORE = 1000.0
GDN_HEADS = 8
GDN_DK = 128
GDN_DV = 128
GDN_CONV = 4
FF_HIDDEN = 2816
DEEPNORM_ALPHA = (2.0 * DEPTH) ** 0.25
LN_EPS = 1e-5
RMS_EPS = 1e-6
L2_EPS = 1e-6
NEG = -1e30

GLA_KEY_W = GLA_HEADS * GLA_DK
GLA_VAL_W = GLA_HEADS * GLA_DV
NSA_Q_W = NSA_HEADS * NSA_HEAD_DIM
NSA_KV_W = NSA_KV_HEADS * NSA_HEAD_DIM
GDN_W = GDN_HEADS * GDN_DK
GDN_CONV_CH = 3 * GDN_W

LANES = 128
SUBLANES = 8
VMEM_LIMIT = 56 * 1024 * 1024

AB_COLS = 2944
AB_SMALL_BLK = 22
AB_KV_BLK = 16
GATE_LANE0 = GLA_LOWRANK
C_COLS = 4224
C_SMALL_BLK = 32
GDN_A_LANE0 = GDN_HEADS


def _cparams(sem):
    return pltpu.CompilerParams(dimension_semantics=sem, vmem_limit_bytes=VMEM_LIMIT)


def _silu(x):
    return x * (1.0 / (1.0 + jnp.exp(-x)))


def _sigmoid(x):
    return 1.0 / (1.0 + jnp.exp(-x))


def _softplus(x):
    return jnp.maximum(x, 0.0) + jnp.log(1.0 + jnp.exp(-jnp.abs(x)))


def _mm(a, b):
    return jnp.dot(a.astype(BF16), b.astype(BF16), preferred_element_type=F32)


def _mm3(a, b):
    a_hi = a.astype(BF16)
    b_hi = b.astype(BF16)
    a_lo = (a - a_hi.astype(F32)).astype(BF16)
    b_lo = (b - b_hi.astype(F32)).astype(BF16)
    dot = lambda x, y: jnp.dot(x, y, preferred_element_type=F32)
    return dot(a_hi, b_hi) + (dot(a_lo, b_hi) + dot(a_hi, b_lo))


def _mm_nt(a, b):
    return lax.dot_general(a.astype(BF16), b.astype(BF16), (((1,), (1,)), ((), ())),
                           preferred_element_type=F32)


def _mm_tn(a, b):
    return lax.dot_general(a.astype(BF16), b.astype(BF16), (((0,), (0,)), ((), ())),
                           preferred_element_type=F32)


def _cumsum_rows(x):
    n = x.shape[0]
    row = lax.broadcasted_iota(jnp.int32, x.shape, 0)
    s = 1
    while s < n:
        x = x + jnp.where(row >= s, pltpu.roll(x, s, axis=0), 0.0)
        s *= 2
    return x


def _masked_softmax(s, mask):
    s = jnp.where(mask, s, NEG)
    m = jnp.max(s, axis=-1, keepdims=True)
    e = jnp.where(mask, jnp.exp(s - m), 0.0)
    den = jnp.maximum(jnp.sum(e, axis=-1, keepdims=True), 1e-30)
    return e * (1.0 / den)


def _layernorm(z, g, b):
    mu = jnp.mean(z, axis=-1, keepdims=True)
    zc = z - mu
    var = jnp.mean(zc * zc, axis=-1, keepdims=True)
    return zc * lax.rsqrt(var + LN_EPS) * g + b


def _adaln_kernel(c_ref, w_ref, b_ref, o_ref):
    c = _silu(c_ref[...])
    o_ref[0] = _mm(c, w_ref[0]) + b_ref[0]


def _adaln(c_all, w_ada, b_ada):
    n = c_all.shape[0]
    tn = 1536
    nt = (6 * D_MODEL) // tn
    return pl.pallas_call(
        _adaln_kernel,
        out_shape=jax.ShapeDtypeStruct((DEPTH, n, 6 * D_MODEL), F32),
        grid=(DEPTH, nt),
        in_specs=[
            pl.BlockSpec((n, D_MODEL), lambda l, j: (0, 0)),
            pl.BlockSpec((1, D_MODEL, tn), lambda l, j: (l, 0, j)),
            pl.BlockSpec((1, 1, tn), lambda l, j: (l, 0, j)),
        ],
        out_specs=pl.BlockSpec((1, n, tn), lambda l, j: (l, 0, j)),
        compiler_params=_cparams(("arbitrary", "arbitrary")),
        name="adaln",
    )(c_all, w_ada, b_ada.reshape(DEPTH, 1, 6 * D_MODEL))


def _modmm_kernel(x_ref, sh_ref, sc_ref, w_ref, o_ref):
    bb, tt, d = x_ref.shape
    h = x_ref[...] * (1.0 + sc_ref[...]) + sh_ref[...]
    o_ref[...] = _mm(h.reshape(bb * tt, d), w_ref[...])


def _row_tiling(B, T, max_rows):
    if T >= max_rows:
        return 1, max_rows
    bb = min(B, max_rows // T)
    return bb, T


def _modmm(x, shift, scale, w_bf16, max_rows):
    B, T, D = x.shape
    N = w_bf16.shape[1]
    bb, tt = _row_tiling(B, T, max_rows)
    nt = T // tt
    return pl.pallas_call(
        _modmm_kernel,
        out_shape=jax.ShapeDtypeStruct((B * T, N), F32),
        grid=(B // bb, nt),
        in_specs=[
            pl.BlockSpec((bb, tt, D), lambda i, j: (i, j, 0)),
            pl.BlockSpec((bb, 1, D), lambda i, j: (i, 0, 0)),
            pl.BlockSpec((bb, 1, D), lambda i, j: (i, 0, 0)),
            pl.BlockSpec((D, N), lambda i, j: (0, 0)),
        ],
        out_specs=pl.BlockSpec((bb * tt, N), lambda i, j: (i * nt + j, 0)),
        compiler_params=_cparams(("arbitrary", "arbitrary")),
        name="modmm",
    )(x, shift[:, None, :], scale[:, None, :], w_bf16)


def _outproj_kernel(*refs, n_in):
    a_refs = refs[:n_in]
    w_refs = refs[n_in:2 * n_in]
    x_ref, gate_ref, g_ref, b_ref, o_ref = refs[2 * n_in:]
    bb, tt, d = x_ref.shape
    acc = _mm(a_refs[0][...], w_refs[0][...])
    for a_ref, w_ref in zip(a_refs[1:], w_refs[1:]):
        acc = acc + _mm(a_ref[...], w_ref[...])
    z = DEEPNORM_ALPHA * x_ref[...] + gate_ref[...] * acc.reshape(bb, tt, d)
    o_ref[...] = _layernorm(z, g_ref[...], b_ref[...])


def _outproj_ln(acts, ws, x, gate, ln_g, ln_b, max_rows):
    B, T, D = x.shape
    bb, tt = _row_tiling(B, T, max_rows)
    nt = T // tt
    n_in = len(acts)
    in_specs = []
    for a in acts:
        in_specs.append(pl.BlockSpec((bb * tt, a.shape[1]), lambda i, j: (i * nt + j, 0)))
    for w in ws:
        in_specs.append(pl.BlockSpec(w.shape, lambda i, j: (0, 0)))
    in_specs += [
        pl.BlockSpec((bb, tt, D), lambda i, j: (i, j, 0)),
        pl.BlockSpec((bb, 1, D), lambda i, j: (i, 0, 0)),
        pl.BlockSpec((1, 1, D), lambda i, j: (0, 0, 0)),
        pl.BlockSpec((1, 1, D), lambda i, j: (0, 0, 0)),
    ]
    return pl.pallas_call(
        functools.partial(_outproj_kernel, n_in=n_in),
        out_shape=jax.ShapeDtypeStruct((B, T, D), F32),
        grid=(B // bb, nt),
        in_specs=in_specs,
        out_specs=pl.BlockSpec((bb, tt, D), lambda i, j: (i, j, 0)),
        compiler_params=_cparams(("arbitrary", "arbitrary")),
        name="outproj_ln",
    )(*acts, *ws, x, gate[:, None, :], ln_g.reshape(1, 1, D), ln_b.reshape(1, 1, D))


def _ffn_kernel(x_ref, sh_ref, sc_ref, gate_ref, wa_ref, wu_ref, wo_ref, g_ref, b_ref, o_ref,
                xm_sc, acc_sc):
    j = pl.program_id(2)
    bb, tt, d = x_ref.shape

    @pl.when(j == 0)
    def _():
        h = x_ref[...] * (1.0 + sc_ref[...]) + sh_ref[...]
        xm_sc[...] = h.reshape(bb * tt, d).astype(BF16)
        acc_sc[...] = jnp.zeros_like(acc_sc)

    xm = xm_sc[...]
    a = jnp.dot(xm, wa_ref[...], preferred_element_type=F32)
    u = jnp.dot(xm, wu_ref[...], preferred_element_type=F32)
    acc_sc[...] += _mm(_silu(a) * u, wo_ref[...])

    @pl.when(j == pl.num_programs(2) - 1)
    def _():
        z = DEEPNORM_ALPHA * x_ref[...] + gate_ref[...] * acc_sc[...].reshape(bb, tt, d)
        o_ref[...] = _layernorm(z, g_ref[...], b_ref[...])


def _ffn_ln(x, shift, scale, gate, w_in_bf16, w_out_bf16, ln_g, ln_b, max_rows):
    B, T, D = x.shape
    bb, tt = _row_tiling(B, T, max_rows)
    nt = T // tt
    th = 256
    nh = FF_HIDDEN // th
    vec = lambda v: v[:, None, :]
    return pl.pallas_call(
        _ffn_kernel,
        out_shape=jax.ShapeDtypeStruct((B, T, D), F32),
        grid=(B // bb, nt, nh),
        in_specs=[
            pl.BlockSpec((bb, tt, D), lambda i, t, j: (i, t, 0)),
            pl.BlockSpec((bb, 1, D), lambda i, t, j: (i, 0, 0)),
            pl.BlockSpec((bb, 1, D), lambda i, t, j: (i, 0, 0)),
            pl.BlockSpec((bb, 1, D), lambda i, t, j: (i, 0, 0)),
            pl.BlockSpec((D, th), lambda i, t, j: (0, j)),
            pl.BlockSpec((D, th), lambda i, t, j: (0, nh + j)),
            pl.BlockSpec((th, D), lambda i, t, j: (j, 0)),
            pl.BlockSpec((1, 1, D), lambda i, t, j: (0, 0, 0)),
            pl.BlockSpec((1, 1, D), lambda i, t, j: (0, 0, 0)),
        ],
        out_specs=pl.BlockSpec((bb, tt, D), lambda i, t, j: (i, t, 0)),
        scratch_shapes=[pltpu.VMEM((bb * tt, D), BF16), pltpu.VMEM((bb * tt, D), F32)],
        compiler_params=_cparams(("arbitrary", "arbitrary", "arbitrary")),
        name="ffn_ln",
    )(x, vec(shift), vec(scale), vec(gate), w_in_bf16, w_in_bf16, w_out_bf16,
      ln_g.reshape(1, 1, D), ln_b.reshape(1, 1, D))


def _gla_kernel(q_ref, k_ref, v_ref, r_ref, sm_ref, wgk_ref, bgk_ref, gn_ref, s0_ref,
                o_ref, sfin_ref, S_sc, q_sc, k_sc, b_sc, v_sc, *, C):
    t = pl.program_id(1)
    KW, VW = GLA_KEY_W, GLA_VAL_W

    hk = lax.broadcasted_iota(jnp.int32, (KW, VW), 0) // GLA_DK
    hv = lax.broadcasted_iota(jnp.int32, (KW, VW), 1) // GLA_DV
    same_head = hk == hv

    @pl.when(t == 0)
    def _():
        rows = []
        for h in range(GLA_HEADS):
            pieces = [s0_ref[0, h] if h2 == h else jnp.zeros((GLA_DK, GLA_DV), F32)
                      for h2 in range(GLA_HEADS)]
            rows.append(jnp.concatenate(pieces, axis=1))
        S_sc[...] = jnp.concatenate(rows, axis=0)

    gk = sm_ref[:, 0:GLA_LOWRANK]
    pre = _mm(gk, wgk_ref[...]) + bgk_ref[...]
    log_a = (jnp.minimum(pre, 0.0) - jnp.log(1.0 + jnp.exp(-jnp.abs(pre)))) * (1.0 / GLA_GATE_NORM)
    b = _cumsum_rows(log_a)
    q = q_ref[...] * (GLA_DK ** -0.5)
    k = k_ref[...]
    v = v_ref[...]
    q_sc[...] = q
    k_sc[...] = k
    b_sc[...] = b
    v_sc[...] = v

    S = S_sc[...]
    o_inter = _mm(q * jnp.exp(b), S)

    seg = same_head.astype(BF16)
    row = lax.broadcasted_iota(jnp.int32, (C, VW), 0)

    def body(j, acc):
        kj = k_sc[pl.ds(j, 1), :]
        bj = b_sc[pl.ds(j, 1), :]
        vj = v_sc[pl.ds(j, 1), :]
        p = q_sc[...] * kj * jnp.exp(jnp.minimum(b_sc[...] - bj, 0.0))
        s = jnp.dot(p.astype(BF16), seg, preferred_element_type=F32)
        return acc + jnp.where(row >= j, s, 0.0) * vj

    o = o_inter + lax.fori_loop(0, C, body, jnp.zeros((C, VW), F32))

    b_last = b[C - 1:C, :]
    kd = k * jnp.exp(b_last - b)
    upd = _mm_tn(kd, v)
    tail = jnp.broadcast_to(b_last, (SUBLANES, KW))
    dcol = jnp.exp(tail.T[:, 0:1])
    S_new = S * dcol + jnp.where(same_head, upd, 0.0)
    S_sc[...] = S_new

    outs = []
    for h in range(GLA_HEADS):
        oh = o[:, h * GLA_DV:(h + 1) * GLA_DV]
        ms = jnp.mean(oh * oh, axis=-1, keepdims=True)
        rh = r_ref[:, h * GLA_DV:(h + 1) * GLA_DV]
        outs.append(oh * lax.rsqrt(ms + RMS_EPS) * gn_ref[...] * _silu(rh))
    o_ref[...] = jnp.concatenate(outs, axis=1).astype(o_ref.dtype)

    @pl.when(t == pl.num_programs(1) - 1)
    def _():
        for h in range(GLA_HEADS):
            sfin_ref[0, h] = S_new[h * GLA_DK:(h + 1) * GLA_DK, h * GLA_DV:(h + 1) * GLA_DV]


def _gla(proj, B, T, w_gk2, b_gk, gla_norm, s0, C):
    nt = T // C
    return pl.pallas_call(
        functools.partial(_gla_kernel, C=C),
        out_shape=(jax.ShapeDtypeStruct((B * T, GLA_VAL_W), BF16),
                   jax.ShapeDtypeStruct((B, GLA_HEADS, GLA_DK, GLA_DV), F32)),
        grid=(B, nt),
        in_specs=[
            pl.BlockSpec((C, GLA_KEY_W), lambda b, t: (b * nt + t, 0)),
            pl.BlockSpec((C, GLA_KEY_W), lambda b, t: (b * nt + t, 1)),
            pl.BlockSpec((C, GLA_VAL_W), lambda b, t: (b * nt + t, 1)),
            pl.BlockSpec((C, GLA_VAL_W), lambda b, t: (b * nt + t, 2)),
            pl.BlockSpec((C, LANES), lambda b, t: (b * nt + t, AB_SMALL_BLK)),
            pl.BlockSpec((GLA_LOWRANK, GLA_KEY_W), lambda b, t: (0, 0)),
            pl.BlockSpec((1, GLA_KEY_W), lambda b, t: (0, 0)),
            pl.BlockSpec((1, GLA_DV), lambda b, t: (0, 0)),
            pl.BlockSpec((1, GLA_HEADS, GLA_DK, GLA_DV), lambda b, t: (b, 0, 0, 0)),
        ],
        out_specs=(pl.BlockSpec((C, GLA_VAL_W), lambda b, t: (b * nt + t, 0)),
                   pl.BlockSpec((1, GLA_HEADS, GLA_DK, GLA_DV), lambda b, t: (b, 0, 0, 0))),
        scratch_shapes=[
            pltpu.VMEM((GLA_KEY_W, GLA_VAL_W), F32),
            pltpu.VMEM((C, GLA_KEY_W), F32),
            pltpu.VMEM((C, GLA_KEY_W), F32),
            pltpu.VMEM((C, GLA_KEY_W), F32),
            pltpu.VMEM((C, GLA_VAL_W), F32),
        ],
        compiler_params=_cparams(("arbitrary", "arbitrary")),
        name="gla",
    )(proj, proj, proj, proj, proj, w_gk2, b_gk.reshape(1, GLA_KEY_W), gla_norm.reshape(1, GLA_DV), s0)


def _compress_pages(x_ref, n_pages, pe_ref, w1_ref, w2_ref):
    outs = []
    for half in range(PAGE_SIZE // NSA_BLOCK):
        pieces = [x_ref[pl.ds(half * NSA_BLOCK + tk, n_pages, stride=PAGE_SIZE), :] for tk in range(NSA_BLOCK)]
        flat = jnp.concatenate(pieces, axis=1) + pe_ref[...]
        acc = _mm(flat, w1_ref[...])
        outs.append(_mm(_silu(acc), w2_ref[...]))
    return jnp.concatenate(outs, axis=1)


def _compress_dense_kernel(xk_ref, xv_ref, pek_ref, pev_ref, w1k_ref, w1v_ref, w2k_ref, w2v_ref,
                           ok_ref, ov_ref, *, n_pages):
    ok_ref[0] = _compress_pages(xk_ref, n_pages, pek_ref, w1k_ref, w2k_ref)
    ov_ref[0] = _compress_pages(xv_ref, n_pages, pev_ref, w1v_ref, w2v_ref)


def _cmp_weights(cmp_pe, cmp_w1, cmp_w2):
    out = []
    for i in range(2):
        pe2 = jnp.concatenate([cmp_pe[i], cmp_pe[i]], axis=1).reshape(1, NSA_BLOCK * LANES)
        w1 = cmp_w1[i].reshape(NSA_BLOCK, NSA_HEAD_DIM, NSA_CMP_HIDDEN)
        z1 = jnp.zeros_like(w1)
        w1bd = jnp.concatenate([jnp.concatenate([w1, z1], axis=2),
                                jnp.concatenate([z1, w1], axis=2)], axis=1).astype(BF16)
        w1bd = w1bd.reshape(NSA_BLOCK * LANES, 2 * NSA_CMP_HIDDEN)
        w2 = cmp_w2[i]
        z2 = jnp.zeros_like(w2)
        w2bd = jnp.concatenate([jnp.concatenate([w2, z2], axis=1),
                                jnp.concatenate([z2, w2], axis=1)], axis=0).astype(BF16)
        out.append((pe2, w1bd, w2bd))
    return out


def _compress_dense(proj, B, T, cw):
    n_pages = T // PAGE_SIZE
    (pek, w1k, w2k), (pev, w1v, w2v) = cw
    full = lambda a: pl.BlockSpec(a.shape, lambda b: (0,) * a.ndim)
    ok, ov = pl.pallas_call(
        functools.partial(_compress_dense_kernel, n_pages=n_pages),
        out_shape=(jax.ShapeDtypeStruct((B, n_pages, 2 * LANES), F32),) * 2,
        grid=(B,),
        in_specs=[
            pl.BlockSpec((T, LANES), lambda b: (b, AB_KV_BLK)),
            pl.BlockSpec((T, LANES), lambda b: (b, AB_KV_BLK + 1)),
            full(pek), full(pev), full(w1k), full(w1v), full(w2k), full(w2v),
        ],
        out_specs=(pl.BlockSpec((1, n_pages, 2 * LANES), lambda b: (b, 0, 0)),) * 2,
        compiler_params=_cparams(("arbitrary",)),
        name="compress_dense",
    )(proj, proj, pek, pev, w1k, w1v, w2k, w2v)
    n_blk = T // NSA_BLOCK
    return ok.reshape(B, n_blk, LANES), ov.reshape(B, n_blk, LANES)


def _gather_pages(pt_ref, b, pool_ref, buf_ref, sem, n_pages, start):
    def body(p, carry):
        page = pt_ref[b, p]
        cp = pltpu.make_async_copy(pool_ref.at[pl.ds(page * PAGE_SIZE, PAGE_SIZE), :],
                                   buf_ref.at[pl.ds(p * PAGE_SIZE, PAGE_SIZE), :], sem)
        if start:
            cp.start()
        else:
            cp.wait()
        return carry
    lax.fori_loop(0, n_pages, body, 0)


def _compress_paged_kernel(pt_ref, poolk_ref, poolv_ref, pek_ref, pev_ref, w1k_ref, w1v_ref,
                           w2k_ref, w2v_ref, ok_ref, ov_ref, bufk, bufv, sems, *, n_pages):
    b = pl.program_id(0)
    _gather_pages(pt_ref, b, poolk_ref, bufk, sems.at[0], n_pages, True)
    _gather_pages(pt_ref, b, poolv_ref, bufv, sems.at[1], n_pages, True)
    _gather_pages(pt_ref, b, poolk_ref, bufk, sems.at[0], n_pages, False)
    ok_ref[0] = _compress_pages(bufk, n_pages, pek_ref, w1k_ref, w2k_ref)
    _gather_pages(pt_ref, b, poolv_ref, bufv, sems.at[1], n_pages, False)
    ov_ref[0] = _compress_pages(bufv, n_pages, pev_ref, w1v_ref, w2v_ref)


def _compress_paged(page_table, pool_k, pool_v, cw):
    B, n_pages = page_table.shape
    (pek, w1k, w2k), (pev, w1v, w2v) = cw
    full = lambda a: pl.BlockSpec(a.shape, lambda b, pt: (0,) * a.ndim)
    rows = pool_k.shape[0] * PAGE_SIZE
    ok, ov = pl.pallas_call(
        functools.partial(_compress_paged_kernel, n_pages=n_pages),
        out_shape=(jax.ShapeDtypeStruct((B, n_pages, 2 * LANES), F32),) * 2,
        grid_spec=pltpu.PrefetchScalarGridSpec(
            num_scalar_prefetch=1,
            grid=(B,),
            in_specs=[
                pl.BlockSpec(memory_space=pl.ANY),
                pl.BlockSpec(memory_space=pl.ANY),
                full(pek), full(pev), full(w1k), full(w1v), full(w2k), full(w2v),
            ],
            out_specs=(pl.BlockSpec((1, n_pages, 2 * LANES), lambda b, pt: (b, 0, 0)),) * 2,
            scratch_shapes=[
                pltpu.VMEM((n_pages * PAGE_SIZE, LANES), F32),
                pltpu.VMEM((n_pages * PAGE_SIZE, LANES), F32),
                pltpu.SemaphoreType.DMA((2,)),
            ],
        ),
        compiler_params=_cparams(("arbitrary",)),
        name="compress_paged",
    )(page_table, pool_k.reshape(rows, LANES), pool_v.reshape(rows, LANES),
      pek, pev, w1k, w1v, w2k, w2v)
    n_blk = n_pages * (PAGE_SIZE // NSA_BLOCK)
    return ok.reshape(B, n_blk, LANES), ov.reshape(B, n_blk, LANES)


def _stack_queries(q, g, tq):
    rows = []
    for hl in range(NSA_HPG):
        qh = q[:, hl * NSA_HEAD_DIM:(hl + 1) * NSA_HEAD_DIM]
        rows.append(jnp.concatenate([qh, qh], axis=1))
    qs = jnp.concatenate(rows, axis=0) * (NSA_HEAD_DIM ** -0.5)
    half = lax.broadcasted_iota(jnp.int32, qs.shape, 1) // NSA_HEAD_DIM
    return jnp.where(half == g, qs, 0.0).astype(BF16)


def _row_slopes(g, tq):
    hl = lax.broadcasted_iota(jnp.int32, (NSA_HPG * tq, 1), 0) // tq
    s = jnp.where(hl == 0, 0.5, jnp.where(hl == 1, 0.25, jnp.where(hl == 2, 0.125, 0.0625)))
    return s * jnp.where(g == 0, 1.0, 0.0625)


def _gate_columns(sm, g, tq):
    sig = _sigmoid(sm)
    lane = lax.broadcasted_iota(jnp.int32, sm.shape, 1)
    cols = []
    for br in range(3):
        per_head = []
        for hl in range(NSA_HPG):
            target = GATE_LANE0 + 3 * (NSA_HPG * g + hl) + br
            per_head.append(jnp.sum(jnp.where(lane == target, sig, 0.0), axis=-1, keepdims=True))
        cols.append(jnp.concatenate(per_head, axis=0))
    return cols


def _topk_select(score, k_sel):
    n = score.shape[1]
    idx = lax.broadcasted_iota(jnp.int32, score.shape, 1)

    def body(j, rank):
        col = jnp.sum(jnp.where(idx == j, score, 0.0), axis=-1, keepdims=True)
        beats = (col > score) | ((col == score) & (j < idx))
        return rank + beats.astype(jnp.int32)

    rank = lax.fori_loop(0, n, body, jnp.zeros(score.shape, jnp.int32))
    return rank < k_sel


def _unstack_heads(o, g, tq):
    og = jnp.where(g == 0, o[:, 0:NSA_HEAD_DIM], o[:, NSA_HEAD_DIM:2 * NSA_HEAD_DIM])
    return jnp.concatenate([og[hl * tq:(hl + 1) * tq] for hl in range(NSA_HPG)], axis=1)


def _nsa_prompt_kernel(q_ref, sm_ref, kc_ref, vc_ref, ks_ref, vs_ref, kw_ref, vw_ref, o_ref,
                       ksb, vsb, kwb, vwb, *, T, TQ, WIN):
    g = pl.program_id(1)
    qt = pl.program_id(2)
    R = NSA_HPG * TQ
    n_blk = T // NSA_BLOCK

    @pl.when((g == 0) & (qt == 0))
    def _():
        ksb[...] = ks_ref[...].astype(BF16)
        vsb[...] = vs_ref[...].astype(BF16)
        kwb[...] = kw_ref[...].astype(BF16)
        vwb[...] = vw_ref[...].astype(BF16)

    qs = _stack_queries(q_ref[...], g, TQ)
    slope = _row_slopes(g, TQ)
    tq_i = qt * TQ + lax.broadcasted_iota(jnp.int32, (R, 1), 0) % TQ
    tq_f = tq_i.astype(F32)
    gc, gs, gw = _gate_columns(sm_ref[...], g, TQ)

    n_i = lax.broadcasted_iota(jnp.int32, (1, n_blk), 1)
    center = (n_i * NSA_BLOCK).astype(F32) + 0.5 * (NSA_BLOCK - 1)
    s_c = _mm_nt(qs, kc_ref[0]) - slope * (tq_f - center)
    p_c = _masked_softmax(s_c, (n_i * NSA_BLOCK + NSA_BLOCK - 1) <= tq_i)
    o_c = _mm(p_c, vc_ref[0])

    score = p_c[0:TQ]
    for hl in range(1, NSA_HPG):
        score = score + p_c[hl * TQ:(hl + 1) * TQ]
    tq1 = tq_i[0:TQ]
    cur = tq1 // NSA_BLOCK
    forced = (n_i == 0) | (n_i == cur) | (n_i == cur - 1)
    visible = n_i * NSA_BLOCK <= tq1
    score = jnp.where(visible, jnp.where(forced, FORCED_SCORE, score), -1.0)
    sel = _topk_select(score, min(NSA_TOP_K, n_blk)).astype(F32)
    sel_rows = jnp.concatenate([sel] * NSA_HPG, axis=0)

    key_i = lax.broadcasted_iota(jnp.int32, (1, T), 1)
    expand = (lax.broadcasted_iota(jnp.int32, (n_blk, T), 0)
              == lax.broadcasted_iota(jnp.int32, (n_blk, T), 1) // NSA_BLOCK).astype(BF16)
    sel_keys = jnp.dot(sel_rows.astype(BF16), expand, preferred_element_type=F32) > 0.5
    s_s = _mm_nt(qs, ksb[...]) - slope * (tq_f - key_i.astype(F32))
    p_s = _masked_softmax(s_s, sel_keys & (key_i <= tq_i))
    o_s = jnp.dot(p_s.astype(BF16), vsb[...], preferred_element_type=F32)

    start = pl.multiple_of(jnp.maximum(qt * TQ + TQ - WIN, 0), TQ)
    kpos = start + lax.broadcasted_iota(jnp.int32, (1, WIN), 1)
    dist = tq_i - kpos
    s_w = _mm_nt(qs, kwb[pl.ds(start, WIN), :]) - slope * dist.astype(F32)
    p_w = _masked_softmax(s_w, (dist >= 0) & (dist <= NSA_WINDOW))
    o_w = jnp.dot(p_w.astype(BF16), vwb[pl.ds(start, WIN), :], preferred_element_type=F32)

    o = gc * o_c + gs * o_s + gw * o_w
    o_ref[...] = _unstack_heads(o, g, TQ).astype(o_ref.dtype)


def _nsa_prompt(proj, B, T, kc, vc):
    TQ = NSA_BLOCK
    nq = T // TQ
    WIN = min(NSA_WINDOW + TQ, T)
    kv = lambda j: pl.BlockSpec((T, LANES), lambda b, g, t: (b, AB_KV_BLK + j))
    n_blk = T // NSA_BLOCK
    return pl.pallas_call(
        functools.partial(_nsa_prompt_kernel, T=T, TQ=TQ, WIN=WIN),
        out_shape=jax.ShapeDtypeStruct((B * T, NSA_Q_W), BF16),
        grid=(B, NSA_KV_HEADS, nq),
        in_specs=[
            pl.BlockSpec((TQ, 256), lambda b, g, t: (b * nq + t, 6 + g)),
            pl.BlockSpec((TQ, LANES), lambda b, g, t: (b * nq + t, AB_SMALL_BLK)),
            pl.BlockSpec((1, n_blk, LANES), lambda b, g, t: (b, 0, 0)),
            pl.BlockSpec((1, n_blk, LANES), lambda b, g, t: (b, 0, 0)),
            kv(2), kv(3), kv(4), kv(5),
        ],
        out_specs=pl.BlockSpec((TQ, 256), lambda b, g, t: (b * nq + t, g)),
        scratch_shapes=[pltpu.VMEM((T, LANES), BF16)] * 4,
        compiler_params=_cparams(("arbitrary", "arbitrary", "arbitrary")),
        name="nsa_prompt",
    )(proj, proj, kc, vc, proj, proj, proj, proj)


def _nsa_sample_kernel(pt_ref, q_ref, sm_ref, kc_ref, vc_ref, poolk_ref, poolv_ref,
                       kn_ref, vn_ref, wk_ref, wv_ref, kwn_ref, vwn_ref,
                       o_ref, wko_ref, wvo_ref, bufk, bufv, sc_sc, sems, *, n_pages, TQ, CH):
    b = pl.program_id(0)
    past = n_pages * PAGE_SIZE
    n_cmp = past // NSA_BLOCK
    n_sel = n_cmp + 1
    R = NSA_HPG * TQ
    n_buf = wk_ref.shape[1]

    _gather_pages(pt_ref, b, poolk_ref, bufk, sems.at[0], n_pages, True)
    _gather_pages(pt_ref, b, poolv_ref, bufv, sems.at[1], n_pages, True)

    wko_ref[0, 0:n_buf - TQ, :] = wk_ref[0, TQ:n_buf, :]
    wko_ref[0, n_buf - TQ:n_buf, :] = kwn_ref[...]
    wvo_ref[0, 0:n_buf - TQ, :] = wv_ref[0, TQ:n_buf, :]
    wvo_ref[0, n_buf - TQ:n_buf, :] = vwn_ref[...]

    kc = kc_ref[0]
    vc = vc_ref[0]
    n_i = lax.broadcasted_iota(jnp.int32, (1, n_cmp), 1)
    center = (n_i * NSA_BLOCK).astype(F32) + 0.5 * (NSA_BLOCK - 1)
    tq_i = past + lax.broadcasted_iota(jnp.int32, (R, 1), 0) % TQ
    tq_f = tq_i.astype(F32)
    new_i = past + lax.broadcasted_iota(jnp.int32, (1, TQ), 1)
    wpos = past - n_buf + lax.broadcasted_iota(jnp.int32, (1, n_buf), 1)

    qs_g, slope_g, sel_g, o_c_g, gates_g = [], [], [], [], []
    for g in range(NSA_KV_HEADS):
        qs = _stack_queries(q_ref[:, g * 256:(g + 1) * 256], g, TQ)
        slope = _row_slopes(g, TQ)
        s_c = _mm_nt(qs, kc) - slope * (tq_f - center)
        p_c = _masked_softmax(s_c, (n_i * NSA_BLOCK + NSA_BLOCK - 1) <= tq_i)
        o_c_g.append(_mm(p_c, vc))
        score = p_c[0:TQ]
        for hl in range(1, NSA_HPG):
            score = score + p_c[hl * TQ:(hl + 1) * TQ]
        cur = n_cmp
        forced = (n_i == 0) | (n_i == cur - 1)
        score = jnp.where(forced, FORCED_SCORE, score)
        sel = _topk_select(score, min(NSA_TOP_K, n_sel) - 1).astype(F32)
        sel_g.append(jnp.concatenate([sel] * NSA_HPG, axis=0).astype(BF16))
        qs_g.append(qs)
        slope_g.append(slope)
        gates_g.append(_gate_columns(sm_ref[...], g, TQ))

    _gather_pages(pt_ref, b, poolk_ref, bufk, sems.at[0], n_pages, False)
    n_ch = past // CH
    blk_per_ch = CH // NSA_BLOCK
    for g in range(NSA_KV_HEADS):
        for c in range(n_ch):
            kpos = c * CH + lax.broadcasted_iota(jnp.int32, (1, CH), 1)
            expand = (lax.broadcasted_iota(jnp.int32, (n_cmp, CH), 0)
                      == c * blk_per_ch + lax.broadcasted_iota(jnp.int32, (n_cmp, CH), 1) // NSA_BLOCK
                      ).astype(BF16)
            selk = jnp.dot(sel_g[g], expand, preferred_element_type=F32) > 0.5
            s = _mm_nt(qs_g[g], bufk[c * CH:(c + 1) * CH, :]) - slope_g[g] * (tq_f - kpos.astype(F32))
            sc_sc[g * R:(g + 1) * R, c * CH:(c + 1) * CH] = jnp.where(selk, s, NEG)

    _gather_pages(pt_ref, b, poolv_ref, bufv, sems.at[1], n_pages, False)
    outs = []
    for g in range(NSA_KV_HEADS):
        qs, slope = qs_g[g], slope_g[g]
        gc, gs, gw = gates_g[g]
        s_n = _mm_nt(qs, kn_ref[...]) - slope * (tq_i - new_i).astype(F32)
        s_n = jnp.where(new_i <= tq_i, s_n, NEG)
        s_p = sc_sc[g * R:(g + 1) * R, :]
        m = jnp.maximum(jnp.max(s_p, axis=-1, keepdims=True), jnp.max(s_n, axis=-1, keepdims=True))
        e_p = jnp.where(s_p > 0.5 * NEG, jnp.exp(s_p - m), 0.0)
        e_n = jnp.where(new_i <= tq_i, jnp.exp(s_n - m), 0.0)
        den = jnp.maximum(jnp.sum(e_p, axis=-1, keepdims=True) + jnp.sum(e_n, axis=-1, keepdims=True), 1e-30)
        acc = _mm(e_n, vn_ref[...])
        for c in range(n_ch):
            acc = acc + _mm(e_p[:, c * CH:(c + 1) * CH], bufv[c * CH:(c + 1) * CH, :])
        o_s = acc * (1.0 / den)

        d_o = tq_i - wpos
        d_n = tq_i - new_i
        s_wo = _mm_nt(qs, wk_ref[0]) - slope * d_o.astype(F32)
        s_wn = _mm_nt(qs, kwn_ref[...]) - slope * d_n.astype(F32)
        m_o = (wpos >= 0) & (d_o >= 0) & (d_o <= NSA_WINDOW)
        m_n = (d_n >= 0) & (d_n <= NSA_WINDOW)
        s_wo = jnp.where(m_o, s_wo, NEG)
        s_wn = jnp.where(m_n, s_wn, NEG)
        mw = jnp.maximum(jnp.max(s_wo, axis=-1, keepdims=True), jnp.max(s_wn, axis=-1, keepdims=True))
        e_o = jnp.where(m_o, jnp.exp(s_wo - mw), 0.0)
        e_w = jnp.where(m_n, jnp.exp(s_wn - mw), 0.0)
        den_w = jnp.maximum(jnp.sum(e_o, axis=-1, keepdims=True) + jnp.sum(e_w, axis=-1, keepdims=True), 1e-30)
        o_w = (_mm(e_o, wv_ref[0]) + _mm(e_w, vwn_ref[...])) * (1.0 / den_w)

        o = gc * o_c_g[g] + gs * o_s + gw * o_w
        outs.append(_unstack_heads(o, g, TQ))
    o_ref[...] = jnp.concatenate(outs, axis=1).astype(o_ref.dtype)


def _nsa_sample(proj, page_table, kc, vc, pool_k, pool_v, win_k, win_v):
    B, n_pages = page_table.shape
    TQ = proj.shape[0] // B
    past = n_pages * PAGE_SIZE
    n_cmp = past // NSA_BLOCK
    n_buf = win_k.shape[1]
    CH = min(2048, past)
    rows = pool_k.shape[0] * PAGE_SIZE
    R2 = NSA_KV_HEADS * NSA_HPG * TQ
    kvn = lambda j: pl.BlockSpec((TQ, LANES), lambda b, pt: (b, AB_KV_BLK + j))
    return pl.pallas_call(
        functools.partial(_nsa_sample_kernel, n_pages=n_pages, TQ=TQ, CH=CH),
        out_shape=(jax.ShapeDtypeStruct((B * TQ, NSA_Q_W), BF16),
                   jax.ShapeDtypeStruct((B, n_buf, LANES), F32),
                   jax.ShapeDtypeStruct((B, n_buf, LANES), F32)),
        grid_spec=pltpu.PrefetchScalarGridSpec(
            num_scalar_prefetch=1,
            grid=(B,),
            in_specs=[
                pl.BlockSpec((TQ, NSA_Q_W), lambda b, pt: (b, 3)),
                pl.BlockSpec((TQ, LANES), lambda b, pt: (b, AB_SMALL_BLK)),
                pl.BlockSpec((1, n_cmp, LANES), lambda b, pt: (b, 0, 0)),
                pl.BlockSpec((1, n_cmp, LANES), lambda b, pt: (b, 0, 0)),
                pl.BlockSpec(memory_space=pl.ANY),
                pl.BlockSpec(memory_space=pl.ANY),
                kvn(2), kvn(3),
                pl.BlockSpec((1, n_buf, LANES), lambda b, pt: (b, 0, 0)),
                pl.BlockSpec((1, n_buf, LANES), lambda b, pt: (b, 0, 0)),
                kvn(4), kvn(5),
            ],
            out_specs=(pl.BlockSpec((TQ, NSA_Q_W), lambda b, pt: (b, 0)),
                       pl.BlockSpec((1, n_buf, LANES), lambda b, pt: (b, 0, 0)),
                       pl.BlockSpec((1, n_buf, LANES), lambda b, pt: (b, 0, 0))),
            scratch_shapes=[
                pltpu.VMEM((past, LANES), F32),
                pltpu.VMEM((past, LANES), F32),
                pltpu.VMEM((R2, past), F32),
                pltpu.SemaphoreType.DMA((2,)),
            ],
        ),
        compiler_params=_cparams(("arbitrary",)),
        name="nsa_sample",
    )(page_table, proj, proj, kc, vc, pool_k.reshape(rows, LANES), pool_v.reshape(rows, LANES),
      proj, proj, win_k, win_v, proj, proj)


def _gdn_kernel(qkv_ref, z_ref, sm_ref, cw_ref, alog_ref, dtb_ref, ng_ref, s0_ref, cb_ref,
                o_ref, sfin_ref, S_sc, prev_sc, *, C):
    t = pl.program_id(1)

    @pl.when(t == 0)
    def _():
        S_sc[...] = s0_ref[0]
        prev_sc[...] = cb_ref[0]

    def conv(c0):
        xe = jnp.concatenate([prev_sc[:, c0:c0 + LANES], qkv_ref[:, c0:c0 + LANES]], axis=0)
        y = xe[SUBLANES:] * cw_ref[GDN_CONV - 1:GDN_CONV, c0:c0 + LANES]
        for s in range(1, GDN_CONV):
            y = y + pltpu.roll(xe, s, axis=0)[SUBLANES:] * cw_ref[GDN_CONV - 1 - s:GDN_CONV - s, c0:c0 + LANES]
        return _silu(y)

    sm = sm_ref[...]
    beta = _sigmoid(sm)
    gt = -jnp.exp(alog_ref[...]) * _softplus(sm + dtb_ref[...])
    d = _cumsum_rows(gt)
    dT = d.T
    ii = lax.broadcasted_iota(jnp.int32, (C, C), 0)
    jj = lax.broadcasted_iota(jnp.int32, (C, C), 1)
    eye = (ii == jj).astype(F32)

    for h in range(GDN_HEADS):
        qh = conv(h * GDN_DK)
        kh = conv(GDN_W + h * GDN_DK)
        vh = conv(2 * GDN_W + h * GDN_DV)
        qh = qh * lax.rsqrt(jnp.sum(qh * qh, axis=-1, keepdims=True) + L2_EPS) * (GDN_DK ** -0.5)
        kh = kh * lax.rsqrt(jnp.sum(kh * kh, axis=-1, keepdims=True) + L2_EPS)
        beta_c = beta[:, h:h + 1]
        d_c = d[:, GDN_A_LANE0 + h:GDN_A_LANE0 + h + 1]
        d_r = dT[GDN_A_LANE0 + h:GDN_A_LANE0 + h + 1, :]
        decay = jnp.exp(jnp.minimum(d_c - d_r, 0.0))
        kb = kh * beta_c
        a = jnp.where(ii > jj, _mm_nt(kb, kh) * decay, 0.0)
        tinv = eye - a
        p = a
        n = 2
        while n < C:
            p = _mm3(p, p)
            tinv = tinv + _mm3(tinv, p)
            n *= 2
        u = _mm(tinv, vh * beta_c)
        w = _mm(tinv, kb * jnp.exp(d_c))
        S = S_sc[h]
        v_new = u - _mm(w, S)
        qk = jnp.where(ii >= jj, _mm_nt(qh, kh) * decay, 0.0)
        oh = _mm(qh * jnp.exp(d_c), S) + _mm(qk, v_new)
        d_last = d_c[C - 1:C, :]
        S_sc[h] = S * jnp.exp(d_last) + _mm_tn(kh * jnp.exp(d_last - d_c), v_new)
        ms = jnp.mean(oh * oh, axis=-1, keepdims=True)
        zh = z_ref[:, h * GDN_DV:(h + 1) * GDN_DV]
        o_ref[:, h * GDN_DV:(h + 1) * GDN_DV] = (
            oh * lax.rsqrt(ms + RMS_EPS) * ng_ref[...] * _silu(zh)).astype(o_ref.dtype)

    prev_sc[...] = qkv_ref[C - SUBLANES:C, :]

    @pl.when(t == pl.num_programs(1) - 1)
    def _():
        sfin_ref[0] = S_sc[...]


def _gdn(proj, B, T, conv_w, a_log, dt_bias, norm_g, s0, conv_buf8, C):
    nt = T // C
    pad = lambda v: jnp.zeros((1, LANES), F32).at[0, GDN_A_LANE0:GDN_A_LANE0 + GDN_HEADS].set(v)
    return pl.pallas_call(
        functools.partial(_gdn_kernel, C=C),
        out_shape=(jax.ShapeDtypeStruct((B * T, GDN_W), BF16),
                   jax.ShapeDtypeStruct((B, GDN_HEADS, GDN_DK, GDN_DV), F32)),
        grid=(B, nt),
        in_specs=[
            pl.BlockSpec((C, GDN_CONV_CH), lambda b, t: (b * nt + t, 0)),
            pl.BlockSpec((C, GDN_W), lambda b, t: (b * nt + t, 3)),
            pl.BlockSpec((C, LANES), lambda b, t: (b * nt + t, C_SMALL_BLK)),
            pl.BlockSpec((GDN_CONV, GDN_CONV_CH), lambda b, t: (0, 0)),
            pl.BlockSpec((1, LANES), lambda b, t: (0, 0)),
            pl.BlockSpec((1, LANES), lambda b, t: (0, 0)),
            pl.BlockSpec((1, GDN_DV), lambda b, t: (0, 0)),
            pl.BlockSpec((1, GDN_HEADS, GDN_DK, GDN_DV), lambda b, t: (b, 0, 0, 0)),
            pl.BlockSpec((1, SUBLANES, GDN_CONV_CH), lambda b, t: (b, 0, 0)),
        ],
        out_specs=(pl.BlockSpec((C, GDN_W), lambda b, t: (b * nt + t, 0)),
                   pl.BlockSpec((1, GDN_HEADS, GDN_DK, GDN_DV), lambda b, t: (b, 0, 0, 0))),
        scratch_shapes=[
            pltpu.VMEM((GDN_HEADS, GDN_DK, GDN_DV), F32),
            pltpu.VMEM((SUBLANES, GDN_CONV_CH), F32),
        ],
        compiler_params=_cparams(("arbitrary", "arbitrary")),
        name="gdn",
    )(proj, proj, proj, conv_w, pad(a_log), pad(dt_bias), norm_g.reshape(1, GDN_DV), s0, conv_buf8)


def _ab_in_weight(w):
    big = w[:, :GLA_KEY_W * 2 + GLA_VAL_W * 2]
    gk = w[:, 1536:1536 + GLA_LOWRANK]
    rest = w[:, 1536 + GLA_LOWRANK:]
    q_b = rest[:, :NSA_Q_W]
    kv = rest[:, NSA_Q_W:NSA_Q_W + 6 * NSA_KV_W]
    gate = rest[:, NSA_Q_W + 6 * NSA_KV_W:]
    small = jnp.concatenate([gk, gate, jnp.zeros((D_MODEL, LANES - GLA_LOWRANK - 3 * NSA_HEADS), w.dtype)], axis=1)
    return jnp.concatenate([big, q_b, kv, small], axis=1).astype(BF16)


def _c_in_weight(w):
    qkv = w[:, :GDN_CONV_CH]
    ba = w[:, GDN_CONV_CH:GDN_CONV_CH + 2 * GDN_HEADS]
    z = w[:, GDN_CONV_CH + 2 * GDN_HEADS:]
    small = jnp.concatenate([ba, jnp.zeros((D_MODEL, LANES - 2 * GDN_HEADS), w.dtype)], axis=1)
    return jnp.concatenate([qkv, z, small], axis=1).astype(BF16)


def _kv_out(proj, B, T, j):
    return proj[:, (AB_KV_BLK + j) * LANES:(AB_KV_BLK + j + 1) * LANES].reshape(B, T, NSA_KV_HEADS, NSA_HEAD_DIM)


PROMPT_ROWS = 512
FFN_ROWS = 1024


def kernel(x_prompt, x_sample, c_prompt, c_sample, page_table, cache_cmp_k, cache_cmp_v, cache_sel_k, cache_sel_v, state_win_k, state_win_v, state_gla, state_gdn, state_gdn_conv, w_ada, b_ada, ln_g, ln_b, w_ffn_in, w_ffn_out, ab_w_in, ab_w_gk2, ab_b_gk, ab_gla_norm, ab_cmp_pe, ab_cmp_w1, ab_cmp_w2, ab_w_out, c_w_in, c_conv_w, c_a_log, c_dt_bias, c_norm, c_w_out):
    Bp, Tp, _ = x_prompt.shape
    Bs, Ts, _ = x_sample.shape
    n_pool = cache_cmp_k.shape[1]

    mods = _adaln(jnp.concatenate([c_prompt, c_sample], axis=0), w_ada, b_ada)

    def layer_mods(layer):
        m = mods[layer]
        parts = [m[:, i * D_MODEL:(i + 1) * D_MODEL] for i in range(6)]
        return [p[:Bp] for p in parts], [p[Bp:] for p in parts]

    xp, xs = x_prompt, x_sample
    ab_p, ab_s, c_p, c_s = [], [], [], []
    for layer in range(DEPTH):
        mp, ms = layer_mods(layer)
        wf_in = w_ffn_in[layer].astype(BF16)
        wf_out = w_ffn_out[layer].astype(BF16)
        i = layer // 2
        if layer % 2 == 0:
            w_in = _ab_in_weight(ab_w_in[i])
            w_out = ab_w_out[i].astype(BF16)
            wo_a, wo_b = w_out[:GLA_VAL_W], w_out[GLA_VAL_W:]
            cw = _cmp_weights(ab_cmp_pe[i], ab_cmp_w1[i], ab_cmp_w2[i])

            proj = _modmm(xp, mp[0], mp[1], w_in, PROMPT_ROWS)
            zero_state = jnp.zeros((Bp, GLA_HEADS, GLA_DK, GLA_DV), F32)
            o_a, s_a = _gla(proj, Bp, Tp, ab_w_gk2[i], ab_b_gk[i], ab_gla_norm[i], zero_state, min(64, Tp))
            kc, vc = _compress_dense(proj, Bp, Tp, cw)
            o_b = _nsa_prompt(proj, Bp, Tp, kc, vc)
            x1 = _outproj_ln([o_a, o_b], [wo_a, wo_b], xp, mp[2], ln_g[layer, 0], ln_b[layer, 0], PROMPT_ROWS)
            n_keep = min(NSA_WINDOW, Tp)
            ab_p.append(tuple(_kv_out(proj, Bp, Tp, j) for j in range(4))
                        + (_kv_out(proj, Bp, Tp, 4)[:, Tp - n_keep:], _kv_out(proj, Bp, Tp, 5)[:, Tp - n_keep:], s_a))
            xp = _ffn_ln(x1, mp[3], mp[4], mp[5], wf_in, wf_out, ln_g[layer, 1], ln_b[layer, 1], FFN_ROWS)

            proj = _modmm(xs, ms[0], ms[1], w_in, PROMPT_ROWS)
            o_a, s_a = _gla(proj, Bs, Ts, ab_w_gk2[i], ab_b_gk[i], ab_gla_norm[i], state_gla[i], min(64, Ts))
            pool = lambda c: c[i].reshape(n_pool, PAGE_SIZE, LANES)
            kc, vc = _compress_paged(page_table, pool(cache_cmp_k), pool(cache_cmp_v), cw)
            n_buf = state_win_k.shape[2]
            o_b, win_k, win_v = _nsa_sample(proj, page_table, kc, vc, pool(cache_sel_k), pool(cache_sel_v),
                                            state_win_k[i].reshape(Bs, n_buf, LANES),
                                            state_win_v[i].reshape(Bs, n_buf, LANES))
            x1 = _outproj_ln([o_a, o_b], [wo_a, wo_b], xs, ms[2], ln_g[layer, 0], ln_b[layer, 0], PROMPT_ROWS)
            ab_s.append(tuple(_kv_out(proj, Bs, Ts, j) for j in range(4))
                        + (win_k.reshape(Bs, n_buf, NSA_KV_HEADS, NSA_HEAD_DIM),
                           win_v.reshape(Bs, n_buf, NSA_KV_HEADS, NSA_HEAD_DIM), s_a))
            xs = _ffn_ln(x1, ms[3], ms[4], ms[5], wf_in, wf_out, ln_g[layer, 1], ln_b[layer, 1], FFN_ROWS)
        else:
            w_in = _c_in_weight(c_w_in[i])
            w_out = c_w_out[i].astype(BF16)
            keep = GDN_CONV - 1

            proj = _modmm(xp, mp[0], mp[1], w_in, PROMPT_ROWS // 2)
            o_c, s_c = _gdn(proj, Bp, Tp, c_conv_w[i], c_a_log[i], c_dt_bias[i], c_norm[i],
                            jnp.zeros((Bp, GDN_HEADS, GDN_DK, GDN_DV), F32),
                            jnp.zeros((Bp, SUBLANES, GDN_CONV_CH), F32), min(64, Tp))
            x1 = _outproj_ln([o_c], [w_out], xp, mp[2], ln_g[layer, 0], ln_b[layer, 0], PROMPT_ROWS)
            c_p.append((s_c, proj[:, :GDN_CONV_CH].reshape(Bp, Tp, GDN_CONV_CH)[:, Tp - keep:]))
            xp = _ffn_ln(x1, mp[3], mp[4], mp[5], wf_in, wf_out, ln_g[layer, 1], ln_b[layer, 1], FFN_ROWS)

            proj = _modmm(xs, ms[0], ms[1], w_in, PROMPT_ROWS // 2)
            conv8 = jnp.concatenate([jnp.zeros((Bs, SUBLANES - keep, GDN_CONV_CH), F32), state_gdn_conv[i]], axis=1)
            o_c, s_c = _gdn(proj, Bs, Ts, c_conv_w[i], c_a_log[i], c_dt_bias[i], c_norm[i],
                            state_gdn[i], conv8, min(64, Ts))
            x1 = _outproj_ln([o_c], [w_out], xs, ms[2], ln_g[layer, 0], ln_b[layer, 0], PROMPT_ROWS)
            qkv_s = proj[:, :GDN_CONV_CH].reshape(Bs, Ts, GDN_CONV_CH)
            c_s.append((s_c, jnp.concatenate([state_gdn_conv[i], qkv_s], axis=1)[:, -keep:]))
            xs = _ffn_ln(x1, ms[3], ms[4], ms[5], wf_in, wf_out, ln_g[layer, 1], ln_b[layer, 1], FFN_ROWS)

    stack = lambda sts: [jnp.stack(z) for z in zip(*sts)]
    p_ab, s_ab = stack(ab_p), stack(ab_s)
    p_c, s_c = stack(c_p), stack(c_s)
    return (xp, xs, *p_ab, *p_c, *s_ab, *s_c)
```

```python
import functools

import jax
import jax.numpy as jnp
from jax import lax
from jax.experimental import pallas as pl
from jax.experimental.pallas import tpu as pltpu

F32 = jnp.float32
BF16 = jnp.bfloat16

D_MODEL = 1024
DEPTH = 2
PAGE_SIZE = 128
GLA_HEADS = 4
GLA_DK = 64
GLA_DV = 128
GLA_LOWRANK = 16
GLA_GATE_NORM = 16.0
NSA_HEAD_DIM = 64
NSA_HEADS = 8
NSA_KV_HEADS = 2
NSA_HPG = 4
NSA_BLOCK = 64
NSA_TOP_K = 16
NSA_WINDOW = 512
NSA_CMP_HIDDEN = 128
FORCED_SCORE = 1000.0
GDN_HEADS = 8
GDN_DK = 128
GDN_DV = 128
GDN_CONV = 4
FF_HIDDEN = 2816
DEEPNORM_ALPHA = (2.0 * DEPTH) ** 0.25
LN_EPS = 1e-5
RMS_EPS = 1e-6
L2_EPS = 1e-6
NEG = -1e30

GLA_KEY_W = GLA_HEADS * GLA_DK
GLA_VAL_W = GLA_HEADS * GLA_DV
NSA_Q_W = NSA_HEADS * NSA_HEAD_DIM
NSA_KV_W = NSA_KV_HEADS * NSA_HEAD_DIM
GDN_W = GDN_HEADS * GDN_DK
GDN_CONV_CH = 3 * GDN_W

LANES = 128
SUBLANES = 8
VMEM_LIMIT = 56 * 1024 * 1024

AB_COLS = 2944
AB_SMALL_BLK = 22
AB_KV_BLK = 16
GATE_LANE0 = GLA_LOWRANK
C_COLS = 4224
C_SMALL_BLK = 32
GDN_A_LANE0 = GDN_HEADS
FEAT_BLK = 64
FEAT_OFF = 65
SEL_BIG = 131072.0


def _cparams(sem):
    return pltpu.CompilerParams(dimension_semantics=sem, vmem_limit_bytes=VMEM_LIMIT)


def _silu(x):
    return x * (1.0 / (1.0 + jnp.exp(-x)))


def _sigmoid(x):
    return 1.0 / (1.0 + jnp.exp(-x))


def _softplus(x):
    return jnp.maximum(x, 0.0) + jnp.log(1.0 + jnp.exp(-jnp.abs(x)))


def _mm(a, b):
    return jnp.dot(a.astype(BF16), b.astype(BF16), preferred_element_type=F32)


def _split_bf16(x):
    hi = x.astype(BF16)
    return hi, (x - hi.astype(F32)).astype(BF16)


def _mm3(a, b):
    dot = lambda x, y: jnp.dot(x, y, preferred_element_type=F32)
    return dot(a[0], b[0]) + (dot(a[1], b[0]) + dot(a[0], b[1]))


def _mm_nt(a, b):
    return lax.dot_general(a.astype(BF16), b.astype(BF16), (((1,), (1,)), ((), ())),
                           preferred_element_type=F32)


def _mm_tn(a, b):
    return lax.dot_general(a.astype(BF16), b.astype(BF16), (((0,), (0,)), ((), ())),
                           preferred_element_type=F32)


def _cumsum_rows(x):
    n = x.shape[0]
    row = lax.broadcasted_iota(jnp.int32, x.shape, 0)
    s = 1
    while s < n:
        x = x + jnp.where(row >= s, pltpu.roll(x, s, axis=0), 0.0)
        s *= 2
    return x


def _masked_softmax(s, mask):
    s = jnp.where(mask, s, NEG)
    m = jnp.max(s, axis=-1, keepdims=True)
    e = jnp.where(mask, jnp.exp(s - m), 0.0)
    den = jnp.maximum(jnp.sum(e, axis=-1, keepdims=True), 1e-30)
    return e * (1.0 / den)


def _layernorm(z, g, b):
    mu = jnp.mean(z, axis=-1, keepdims=True)
    zc = z - mu
    var = jnp.mean(zc * zc, axis=-1, keepdims=True)
    return zc * lax.rsqrt(var + LN_EPS) * g + b


def _adaln_kernel(c_ref, w_ref, b_ref, o_ref):
    c = _silu(c_ref[...])
    o_ref[0] = _mm(c, w_ref[0]) + b_ref[0]


def _adaln(c_all, w_ada, b_ada):
    n = c_all.shape[0]
    tn = 1536
    nt = (6 * D_MODEL) // tn
    return pl.pallas_call(
        _adaln_kernel,
        out_shape=jax.ShapeDtypeStruct((DEPTH, n, 6 * D_MODEL), F32),
        grid=(DEPTH, nt),
        in_specs=[
            pl.BlockSpec((n, D_MODEL), lambda l, j: (0, 0)),
            pl.BlockSpec((1, D_MODEL, tn), lambda l, j: (l, 0, j)),
            pl.BlockSpec((1, 1, tn), lambda l, j: (l, 0, j)),
        ],
        out_specs=pl.BlockSpec((1, n, tn), lambda l, j: (l, 0, j)),
        compiler_params=_cparams(("arbitrary", "arbitrary")),
        name="adaln",
    )(c_all, w_ada, b_ada.reshape(DEPTH, 1, 6 * D_MODEL))


def _modmm_kernel(x_ref, sh_ref, sc_ref, w_ref, o_ref):
    bb, tt, d = x_ref.shape
    h = x_ref[...] * (1.0 + sc_ref[...]) + sh_ref[...]
    o_ref[...] = _mm(h.reshape(bb * tt, d), w_ref[...])


def _row_tiling(B, T, max_rows):
    if T >= max_rows:
        return 1, max_rows
    bb = min(B, max_rows // T)
    return bb, T


def _modmm(x, shift, scale, w_bf16, max_rows):
    B, T, D = x.shape
    N = w_bf16.shape[1]
    bb, tt = _row_tiling(B, T, max_rows)
    nt = T // tt
    return pl.pallas_call(
        _modmm_kernel,
        out_shape=jax.ShapeDtypeStruct((B * T, N), F32),
        grid=(B // bb, nt),
        in_specs=[
            pl.BlockSpec((bb, tt, D), lambda i, j: (i, j, 0)),
            pl.BlockSpec((bb, 1, D), lambda i, j: (i, 0, 0)),
            pl.BlockSpec((bb, 1, D), lambda i, j: (i, 0, 0)),
            pl.BlockSpec((D, N), lambda i, j: (0, 0)),
        ],
        out_specs=pl.BlockSpec((bb * tt, N), lambda i, j: (i * nt + j, 0)),
        compiler_params=_cparams(("arbitrary", "arbitrary")),
        name="modmm",
    )(x, shift[:, None, :], scale[:, None, :], w_bf16)


def _outproj_kernel(*refs, n_in):
    a_refs = refs[:n_in]
    w_refs = refs[n_in:2 * n_in]
    x_ref, gate_ref, g_ref, b_ref, o_ref = refs[2 * n_in:]
    bb, tt, d = x_ref.shape
    acc = _mm(a_refs[0][...], w_refs[0][...])
    for a_ref, w_ref in zip(a_refs[1:], w_refs[1:]):
        acc = acc + _mm(a_ref[...], w_ref[...])
    z = DEEPNORM_ALPHA * x_ref[...] + gate_ref[...] * acc.reshape(bb, tt, d)
    o_ref[...] = _layernorm(z, g_ref[...], b_ref[...])


def _outproj_ln(acts, ws, x, gate, ln_g, ln_b, max_rows):
    B, T, D = x.shape
    bb, tt = _row_tiling(B, T, max_rows)
    nt = T // tt
    n_in = len(acts)
    in_specs = []
    for a in acts:
        in_specs.append(pl.BlockSpec((bb * tt, a.shape[1]), lambda i, j: (i * nt + j, 0)))
    for w in ws:
        in_specs.append(pl.BlockSpec(w.shape, lambda i, j: (0, 0)))
    in_specs += [
        pl.BlockSpec((bb, tt, D), lambda i, j: (i, j, 0)),
        pl.BlockSpec((bb, 1, D), lambda i, j: (i, 0, 0)),
        pl.BlockSpec((1, 1, D), lambda i, j: (0, 0, 0)),
        pl.BlockSpec((1, 1, D), lambda i, j: (0, 0, 0)),
    ]
    return pl.pallas_call(
        functools.partial(_outproj_kernel, n_in=n_in),
        out_shape=jax.ShapeDtypeStruct((B, T, D), F32),
        grid=(B // bb, nt),
        in_specs=in_specs,
        out_specs=pl.BlockSpec((bb, tt, D), lambda i, j: (i, j, 0)),
        compiler_params=_cparams(("arbitrary", "arbitrary")),
        name="outproj_ln",
    )(*acts, *ws, x, gate[:, None, :], ln_g.reshape(1, 1, D), ln_b.reshape(1, 1, D))


def _ffn_kernel(x_ref, sh_ref, sc_ref, gate_ref, wa_ref, wu_ref, wo_ref, g_ref, b_ref, o_ref,
                xm_sc, acc_sc):
    j = pl.program_id(2)
    bb, tt, d = x_ref.shape

    @pl.when(j == 0)
    def _():
        h = x_ref[...] * (1.0 + sc_ref[...]) + sh_ref[...]
        xm_sc[...] = h.reshape(bb * tt, d).astype(BF16)
        acc_sc[...] = jnp.zeros_like(acc_sc)

    xm = xm_sc[...]
    a = jnp.dot(xm, wa_ref[...], preferred_element_type=F32)
    u = jnp.dot(xm, wu_ref[...], preferred_element_type=F32)
    acc_sc[...] += _mm(_silu(a) * u, wo_ref[...])

    @pl.when(j == pl.num_programs(2) - 1)
    def _():
        z = DEEPNORM_ALPHA * x_ref[...] + gate_ref[...] * acc_sc[...].reshape(bb, tt, d)
        o_ref[...] = _layernorm(z, g_ref[...], b_ref[...])


def _ffn_ln(x, shift, scale, gate, w_in_bf16, w_out_bf16, ln_g, ln_b, max_rows):
    B, T, D = x.shape
    bb, tt = _row_tiling(B, T, max_rows)
    nt = T // tt
    th = 256
    nh = FF_HIDDEN // th
    vec = lambda v: v[:, None, :]
    return pl.pallas_call(
        _ffn_kernel,
        out_shape=jax.ShapeDtypeStruct((B, T, D), F32),
        grid=(B // bb, nt, nh),
        in_specs=[
            pl.BlockSpec((bb, tt, D), lambda i, t, j: (i, t, 0)),
            pl.BlockSpec((bb, 1, D), lambda i, t, j: (i, 0, 0)),
            pl.BlockSpec((bb, 1, D), lambda i, t, j: (i, 0, 0)),
            pl.BlockSpec((bb, 1, D), lambda i, t, j: (i, 0, 0)),
            pl.BlockSpec((D, th), lambda i, t, j: (0, j)),
            pl.BlockSpec((D, th), lambda i, t, j: (0, nh + j)),
            pl.BlockSpec((th, D), lambda i, t, j: (j, 0)),
            pl.BlockSpec((1, 1, D), lambda i, t, j: (0, 0, 0)),
            pl.BlockSpec((1, 1, D), lambda i, t, j: (0, 0, 0)),
        ],
        out_specs=pl.BlockSpec((bb, tt, D), lambda i, t, j: (i, t, 0)),
        scratch_shapes=[pltpu.VMEM((bb * tt, D), BF16), pltpu.VMEM((bb * tt, D), F32)],
        compiler_params=_cparams(("arbitrary", "arbitrary", "arbitrary")),
        name="ffn_ln",
    )(x, vec(shift), vec(scale), vec(gate), w_in_bf16, w_in_bf16, w_out_bf16,
      ln_g.reshape(1, 1, D), ln_b.reshape(1, 1, D))


def _gla_kernel(q_ref, k_ref, v_ref, r_ref, sm_ref, wgk_ref, bgk_ref, gn_ref, s0_ref,
                o_ref, sfin_ref, S_sc, q_sc, k_sc, b_sc, v_sc, o_sc, *, C):
    t = pl.program_id(1)
    KW, VW = GLA_KEY_W, GLA_VAL_W

    hk = lax.broadcasted_iota(jnp.int32, (KW, VW), 0) // GLA_DK
    hv = lax.broadcasted_iota(jnp.int32, (KW, VW), 1) // GLA_DV
    same_head = hk == hv

    @pl.when(t == 0)
    def _():
        rows = []
        for h in range(GLA_HEADS):
            pieces = [s0_ref[0, h] if h2 == h else jnp.zeros((GLA_DK, GLA_DV), F32)
                      for h2 in range(GLA_HEADS)]
            rows.append(jnp.concatenate(pieces, axis=1))
        S_sc[...] = jnp.concatenate(rows, axis=0)

    gk = sm_ref[:, 0:GLA_LOWRANK]
    pre = _mm(gk, wgk_ref[...]) + bgk_ref[...]
    log_a = (jnp.minimum(pre, 0.0) - jnp.log(1.0 + jnp.exp(-jnp.abs(pre)))) * (1.0 / GLA_GATE_NORM)
    b = _cumsum_rows(log_a)
    q = q_ref[...] * (GLA_DK ** -0.5)
    k = k_ref[...]
    v = v_ref[...]
    q_sc[...] = q
    k_sc[...] = k
    b_sc[...] = b
    v_sc[...] = v

    S = S_sc[...]
    o_sc[...] = _mm(q * jnp.exp(b), S)

    seg = same_head.astype(BF16)
    JG = min(C, 2 * SUBLANES)
    for r0 in range(0, C, JG):
        rows = C - r0
        qg = q_sc[r0:C, :]
        bg = b_sc[r0:C, :]
        local = lax.broadcasted_iota(jnp.int32, (JG, KW), 0)
        ps = []
        for jj in range(JG):
            j = r0 + jj
            p = qg * k_sc[j:j + 1, :] * jnp.exp(jnp.minimum(bg - b_sc[j:j + 1, :], 0.0))
            head = jnp.where(local >= jj, p[0:JG], 0.0)
            p = head if rows == JG else jnp.concatenate([head, p[JG:]], axis=0)
            ps.append(p.astype(BF16))
        s = jnp.dot(jnp.concatenate(ps, axis=0), seg, preferred_element_type=F32)
        contrib = s[0:rows] * v_sc[r0:r0 + 1, :]
        for jj in range(1, JG):
            contrib = contrib + s[jj * rows:(jj + 1) * rows] * v_sc[r0 + jj:r0 + jj + 1, :]
        o_sc[r0:C, :] += contrib
    o = o_sc[...]

    b_last = b[C - 1:C, :]
    kd = k * jnp.exp(b_last - b)
    upd = _mm_tn(kd, v)
    tail = jnp.broadcast_to(b_last, (SUBLANES, KW))
    dcol = jnp.exp(tail.T[:, 0:1])
    S_new = S * dcol + jnp.where(same_head, upd, 0.0)
    S_sc[...] = S_new

    outs = []
    for h in range(GLA_HEADS):
        oh = o[:, h * GLA_DV:(h + 1) * GLA_DV]
        ms = jnp.mean(oh * oh, axis=-1, keepdims=True)
        rh = r_ref[:, h * GLA_DV:(h + 1) * GLA_DV]
        outs.append(oh * lax.rsqrt(ms + RMS_EPS) * gn_ref[...] * _silu(rh))
    o_ref[...] = jnp.concatenate(outs, axis=1).astype(o_ref.dtype)

    @pl.when(t == pl.num_programs(1) - 1)
    def _():
        for h in range(GLA_HEADS):
            sfin_ref[0, h] = S_new[h * GLA_DK:(h + 1) * GLA_DK, h * GLA_DV:(h + 1) * GLA_DV]


def _gla(proj, B, T, w_gk2, b_gk, gla_norm, s0, C):
    nt = T // C
    return pl.pallas_call(
        functools.partial(_gla_kernel, C=C),
        out_shape=(jax.ShapeDtypeStruct((B * T, GLA_VAL_W), BF16),
                   jax.ShapeDtypeStruct((B, GLA_HEADS, GLA_DK, GLA_DV), F32)),
        grid=(B, nt),
        in_specs=[
            pl.BlockSpec((C, GLA_KEY_W), lambda b, t: (b * nt + t, 0)),
            pl.BlockSpec((C, GLA_KEY_W), lambda b, t: (b * nt + t, 1)),
            pl.BlockSpec((C, GLA_VAL_W), lambda b, t: (b * nt + t, 1)),
            pl.BlockSpec((C, GLA_VAL_W), lambda b, t: (b * nt + t, 2)),
            pl.BlockSpec((C, LANES), lambda b, t: (b * nt + t, AB_SMALL_BLK)),
            pl.BlockSpec((GLA_LOWRANK, GLA_KEY_W), lambda b, t: (0, 0)),
            pl.BlockSpec((1, GLA_KEY_W), lambda b, t: (0, 0)),
            pl.BlockSpec((1, GLA_DV), lambda b, t: (0, 0)),
            pl.BlockSpec((1, GLA_HEADS, GLA_DK, GLA_DV), lambda b, t: (b, 0, 0, 0)),
        ],
        out_specs=(pl.BlockSpec((C, GLA_VAL_W), lambda b, t: (b * nt + t, 0)),
                   pl.BlockSpec((1, GLA_HEADS, GLA_DK, GLA_DV), lambda b, t: (b, 0, 0, 0))),
        scratch_shapes=[
            pltpu.VMEM((GLA_KEY_W, GLA_VAL_W), F32),
            pltpu.VMEM((C, GLA_KEY_W), F32),
            pltpu.VMEM((C, GLA_KEY_W), F32),
            pltpu.VMEM((C, GLA_KEY_W), F32),
            pltpu.VMEM((C, GLA_VAL_W), F32),
            pltpu.VMEM((C, GLA_VAL_W), F32),
        ],
        compiler_params=_cparams(("arbitrary", "arbitrary")),
        name="gla",
    )(proj, proj, proj, proj, proj, w_gk2, b_gk.reshape(1, GLA_KEY_W), gla_norm.reshape(1, GLA_DV), s0)


def _compress_pages(x_ref, n_pages, pe_ref, w1_ref, w2_ref):
    outs = []
    for half in range(PAGE_SIZE // NSA_BLOCK):
        pieces = [x_ref[pl.ds(half * NSA_BLOCK + tk, n_pages, stride=PAGE_SIZE), :] for tk in range(NSA_BLOCK)]
        flat = jnp.concatenate(pieces, axis=1) + pe_ref[...]
        acc = _mm(flat, w1_ref[...])
        outs.append(_mm(_silu(acc), w2_ref[...]))
    return jnp.concatenate(outs, axis=1)


def _compress_dense_kernel(xk_ref, xv_ref, pek_ref, pev_ref, w1k_ref, w1v_ref, w2k_ref, w2v_ref,
                           ok_ref, ov_ref, *, n_pages):
    ok_ref[0] = _compress_pages(xk_ref, n_pages, pek_ref, w1k_ref, w2k_ref)
    ov_ref[0] = _compress_pages(xv_ref, n_pages, pev_ref, w1v_ref, w2v_ref)


def _cmp_weights(cmp_pe, cmp_w1, cmp_w2):
    out = []
    for i in range(2):
        pe2 = jnp.concatenate([cmp_pe[i], cmp_pe[i]], axis=1).reshape(1, NSA_BLOCK * LANES)
        w1 = cmp_w1[i].reshape(NSA_BLOCK, NSA_HEAD_DIM, NSA_CMP_HIDDEN)
        z1 = jnp.zeros_like(w1)
        w1bd = jnp.concatenate([jnp.concatenate([w1, z1], axis=2),
                                jnp.concatenate([z1, w1], axis=2)], axis=1).astype(BF16)
        w1bd = w1bd.reshape(NSA_BLOCK * LANES, 2 * NSA_CMP_HIDDEN)
        w2 = cmp_w2[i]
        z2 = jnp.zeros_like(w2)
        w2bd = jnp.concatenate([jnp.concatenate([w2, z2], axis=1),
                                jnp.concatenate([z2, w2], axis=1)], axis=0).astype(BF16)
        out.append((pe2, w1bd, w2bd))
    return out


def _compress_dense(proj, B, T, cw):
    n_pages = T // PAGE_SIZE
    (pek, w1k, w2k), (pev, w1v, w2v) = cw
    full = lambda a: pl.BlockSpec(a.shape, lambda b: (0,) * a.ndim)
    ok, ov = pl.pallas_call(
        functools.partial(_compress_dense_kernel, n_pages=n_pages),
        out_shape=(jax.ShapeDtypeStruct((B, n_pages, 2 * LANES), F32),) * 2,
        grid=(B,),
        in_specs=[
            pl.BlockSpec((T, LANES), lambda b: (b, AB_KV_BLK)),
            pl.BlockSpec((T, LANES), lambda b: (b, AB_KV_BLK + 1)),
            full(pek), full(pev), full(w1k), full(w1v), full(w2k), full(w2v),
        ],
        out_specs=(pl.BlockSpec((1, n_pages, 2 * LANES), lambda b: (b, 0, 0)),) * 2,
        compiler_params=_cparams(("arbitrary",)),
        name="compress_dense",
    )(proj, proj, pek, pev, w1k, w1v, w2k, w2v)
    n_blk = T // NSA_BLOCK
    return ok.reshape(B, n_blk, LANES), ov.reshape(B, n_blk, LANES)


def _gather_pages(pt_ref, b, pool_ref, buf_ref, sem, n_pages, start):
    def body(p, carry):
        page = pt_ref[b, p]
        cp = pltpu.make_async_copy(pool_ref.at[pl.ds(page * PAGE_SIZE, PAGE_SIZE), :],
                                   buf_ref.at[pl.ds(p * PAGE_SIZE, PAGE_SIZE), :], sem)
        if start:
            cp.start()
        else:
            cp.wait()
        return carry
    lax.fori_loop(0, n_pages, body, 0)


def _compress_pages_t(x_ref, n_pages, pe_ref, w1_ref, w2_ref):
    per_g = []
    for g in range(NSA_KV_HEADS):
        pieces = [x_ref[pl.ds(g * NSA_HEAD_DIM + d, n_pages, stride=PAGE_SIZE), :] for d in range(NSA_HEAD_DIM)]
        flat = jnp.concatenate(pieces, axis=1) + pe_ref[...]
        acc = _mm(flat, w1_ref[...])
        per_g.append(_mm(_silu(acc), w2_ref[...]))
    hd = NSA_HEAD_DIM
    return jnp.concatenate([per_g[0][:, 0:hd], per_g[1][:, 0:hd], per_g[0][:, hd:2 * hd], per_g[1][:, hd:2 * hd]], axis=1)


def _cmp_weights_t(cmp_pe, cmp_w1, cmp_w2):
    out = []
    for i in range(2):
        pe_t = jnp.concatenate([cmp_pe[i].T, cmp_pe[i].T], axis=1).reshape(1, NSA_HEAD_DIM * PAGE_SIZE)
        w1 = jnp.transpose(cmp_w1[i].reshape(NSA_BLOCK, NSA_HEAD_DIM, NSA_CMP_HIDDEN), (1, 0, 2))
        z1 = jnp.zeros_like(w1)
        w1t = jnp.concatenate([jnp.concatenate([w1, z1], axis=2),
                               jnp.concatenate([z1, w1], axis=2)], axis=1).astype(BF16)
        w1t = w1t.reshape(NSA_HEAD_DIM * PAGE_SIZE, 2 * NSA_CMP_HIDDEN)
        w2 = cmp_w2[i]
        z2 = jnp.zeros_like(w2)
        w2bd = jnp.concatenate([jnp.concatenate([w2, z2], axis=1),
                                jnp.concatenate([z2, w2], axis=1)], axis=0).astype(BF16)
        out.append((pe_t, w1t, w2bd))
    return out


def _compress_paged_kernel(pt_ref, poolk_ref, poolv_ref, pek_ref, pev_ref, w1k_ref, w1v_ref,
                           w2k_ref, w2v_ref, ok_ref, ov_ref, bufk, bufv, sems, *, n_pages):
    b = pl.program_id(0)
    _gather_pages(pt_ref, b, poolk_ref, bufk, sems.at[0], n_pages, True)
    _gather_pages(pt_ref, b, poolv_ref, bufv, sems.at[1], n_pages, True)
    _gather_pages(pt_ref, b, poolk_ref, bufk, sems.at[0], n_pages, False)
    ok_ref[0] = _compress_pages_t(bufk, n_pages, pek_ref, w1k_ref, w2k_ref)
    _gather_pages(pt_ref, b, poolv_ref, bufv, sems.at[1], n_pages, False)
    ov_ref[0] = _compress_pages_t(bufv, n_pages, pev_ref, w1v_ref, w2v_ref)


def _compress_paged(page_table, pool_k, pool_v, cw):
    B, n_pages = page_table.shape
    (pek, w1k, w2k), (pev, w1v, w2v) = cw
    full = lambda a: pl.BlockSpec(a.shape, lambda b, pt: (0,) * a.ndim)
    ok, ov = pl.pallas_call(
        functools.partial(_compress_paged_kernel, n_pages=n_pages),
        out_shape=(jax.ShapeDtypeStruct((B, n_pages, 2 * LANES), F32),) * 2,
        grid_spec=pltpu.PrefetchScalarGridSpec(
            num_scalar_prefetch=1,
            grid=(B,),
            in_specs=[
                pl.BlockSpec(memory_space=pl.ANY),
                pl.BlockSpec(memory_space=pl.ANY),
                full(pek), full(pev), full(w1k), full(w1v), full(w2k), full(w2v),
            ],
            out_specs=(pl.BlockSpec((1, n_pages, 2 * LANES), lambda b, pt: (b, 0, 0)),) * 2,
            scratch_shapes=[
                pltpu.VMEM((n_pages * PAGE_SIZE, LANES), F32),
                pltpu.VMEM((n_pages * PAGE_SIZE, LANES), F32),
                pltpu.SemaphoreType.DMA((2,)),
            ],
        ),
        compiler_params=_cparams(("arbitrary",)),
        name="compress_paged",
    )(page_table, pool_k, pool_v, pek, pev, w1k, w1v, w2k, w2v)
    n_blk = n_pages * (PAGE_SIZE // NSA_BLOCK)
    return ok.reshape(B, n_blk, LANES), ov.reshape(B, n_blk, LANES)


def _stack_queries(q, g, tq):
    rows = []
    for hl in range(NSA_HPG):
        qh = q[:, hl * NSA_HEAD_DIM:(hl + 1) * NSA_HEAD_DIM]
        rows.append(jnp.concatenate([qh, qh], axis=1))
    qs = jnp.concatenate(rows, axis=0) * (NSA_HEAD_DIM ** -0.5)
    half = lax.broadcasted_iota(jnp.int32, qs.shape, 1) // NSA_HEAD_DIM
    return jnp.where(half == g, qs, 0.0).astype(BF16)


def _row_slopes(g, tq):
    hl = lax.broadcasted_iota(jnp.int32, (NSA_HPG * tq, 1), 0) // tq
    s = jnp.where(hl == 0, 0.5, jnp.where(hl == 1, 0.25, jnp.where(hl == 2, 0.125, 0.0625)))
    return s * jnp.where(g == 0, 1.0, 0.0625)


def _gate_columns(sm, g, tq):
    sig = _sigmoid(sm)
    lane = lax.broadcasted_iota(jnp.int32, sm.shape, 1)
    cols = []
    for br in range(3):
        per_head = []
        for hl in range(NSA_HPG):
            target = GATE_LANE0 + 3 * (NSA_HPG * g + hl) + br
            per_head.append(jnp.sum(jnp.where(lane == target, sig, 0.0), axis=-1, keepdims=True))
        cols.append(jnp.concatenate(per_head, axis=0))
    return cols


def _topk_select(score, k_sel, n):
    idx = lax.broadcasted_iota(jnp.int32, score.shape, 1)
    rank = jnp.zeros(score.shape, F32)
    for j in range(n):
        col = score[:, j:j + 1]
        beats = (col > score) | ((col >= score) & (idx > j))
        rank = rank + jnp.where(beats, 1.0, 0.0)
    return rank < k_sel


def _unstack_heads(o, g, tq):
    og = jnp.where(g == 0, o[:, 0:NSA_HEAD_DIM], o[:, NSA_HEAD_DIM:2 * NSA_HEAD_DIM])
    return jnp.concatenate([og[hl * tq:(hl + 1) * tq] for hl in range(NSA_HPG)], axis=1)


def _key_features(T, onehot):
    j = lax.broadcasted_iota(jnp.int32, (T, LANES), 0)
    lane = lax.broadcasted_iota(jnp.int32, (T, LANES), 1)
    blk = j // NSA_BLOCK
    f = jnp.where(lane == FEAT_BLK, blk.astype(F32),
                  jnp.where(lane == FEAT_OFF, (j % NSA_BLOCK).astype(F32), 0.0))
    if onehot:
        f = jnp.where(lane == blk, 1.0, f)
    return f.astype(BF16)


def _nsa_prompt_kernel(q_ref, sm_ref, kc_ref, vc_ref, ks_ref, vs_ref, kw_ref, vw_ref, o_ref,
                       ksb, vsb, kwb, vwb, *, T, TQ, WIN):
    g = pl.program_id(1)
    qt = pl.program_id(2)
    R = NSA_HPG * TQ
    n_blk = T // NSA_BLOCK

    @pl.when((g == 0) & (qt == 0))
    def _():
        ksb[:, 0:LANES] = ks_ref[...].astype(BF16)
        ksb[:, LANES:2 * LANES] = _key_features(T, True)
        kwb[:, 0:LANES] = kw_ref[...].astype(BF16)
        kwb[:, LANES:2 * LANES] = _key_features(T, False)
        vsb[...] = vs_ref[...].astype(BF16)
        vwb[...] = vw_ref[...].astype(BF16)

    qs = _stack_queries(q_ref[...], g, TQ)
    slope = _row_slopes(g, TQ)
    off_q = lax.broadcasted_iota(jnp.int32, (R, 1), 0) % TQ
    tq_i = qt * TQ + off_q
    tq_f = tq_i.astype(F32)
    gc, gs, gw = _gate_columns(sm_ref[...], g, TQ)
    lane = lax.broadcasted_iota(jnp.int32, (R, LANES), 1)
    slope_feat = jnp.where(lane == FEAT_BLK, slope * NSA_BLOCK, jnp.where(lane == FEAT_OFF, slope, 0.0))
    q_plain = jnp.concatenate([qs, slope_feat.astype(BF16)], axis=1)

    zpad = jnp.zeros((LANES - n_blk, LANES), F32)
    n_i = lax.broadcasted_iota(jnp.int32, (1, LANES), 1)
    center = (n_i * NSA_BLOCK).astype(F32) + 0.5 * (NSA_BLOCK - 1)
    s_c = _mm_nt(qs, jnp.concatenate([kc_ref[0], zpad], axis=0)) - slope * (tq_f - center)
    p_c = _masked_softmax(s_c, (n_i * NSA_BLOCK + NSA_BLOCK - 1) <= tq_i)
    o_c = _mm(p_c, jnp.concatenate([vc_ref[0], zpad], axis=0))

    score = p_c[0:TQ]
    for hl in range(1, NSA_HPG):
        score = score + p_c[hl * TQ:(hl + 1) * TQ]
    tq1 = tq_i[0:TQ]
    cur = tq1 // NSA_BLOCK
    forced = (n_i == 0) | (n_i == cur) | (n_i == cur - 1)
    visible = n_i * NSA_BLOCK <= tq1
    score = jnp.where(visible, jnp.where(forced, FORCED_SCORE, score), -1.0)
    sel = _topk_select(score, min(NSA_TOP_K, n_blk), n_blk)

    sel_bias = jnp.where(sel & (n_i < cur), 0.0, -SEL_BIG)
    sel_bias = jnp.concatenate([sel_bias] * NSA_HPG, axis=0)
    q_sel = jnp.concatenate([qs, jnp.where(lane >= FEAT_BLK, slope_feat, sel_bias).astype(BF16)], axis=1)
    cur0 = pl.multiple_of(qt * TQ, TQ)
    nt_dims = (((1,), (1,)), ((), ()))
    s_past = lax.dot_general(q_sel, ksb[...], nt_dims, preferred_element_type=F32)
    s_cur = lax.dot_general(q_plain, ksb[pl.ds(cur0, TQ), :], nt_dims, preferred_element_type=F32)
    off_k = lax.broadcasted_iota(jnp.int32, (1, TQ), 1)
    s_cur = jnp.where(off_k <= off_q, s_cur, NEG)
    m = jnp.maximum(jnp.max(s_past, axis=-1, keepdims=True), jnp.max(s_cur, axis=-1, keepdims=True))
    e_past = jnp.exp(s_past - m)
    e_cur = jnp.exp(s_cur - m)
    den = jnp.sum(e_past, axis=-1, keepdims=True) + jnp.sum(e_cur, axis=-1, keepdims=True)
    o_s = (jnp.dot(e_past.astype(BF16), vsb[...], preferred_element_type=F32)
           + jnp.dot(e_cur.astype(BF16), vsb[pl.ds(cur0, TQ), :], preferred_element_type=F32)) * (1.0 / den)

    start = pl.multiple_of(jnp.maximum(qt * TQ + TQ - WIN, 0), TQ)
    kpos = start + lax.broadcasted_iota(jnp.int32, (1, WIN), 1)
    dist = tq_i - kpos
    s_w = lax.dot_general(q_plain, kwb[pl.ds(start, WIN), :], nt_dims, preferred_element_type=F32)
    s_w = jnp.where((dist >= 0) & (dist <= NSA_WINDOW), s_w, NEG)
    e_w = jnp.exp(s_w - jnp.max(s_w, axis=-1, keepdims=True))
    o_w = jnp.dot(e_w.astype(BF16), vwb[pl.ds(start, WIN), :], preferred_element_type=F32) * (
        1.0 / jnp.sum(e_w, axis=-1, keepdims=True))

    o = gc * o_c + gs * o_s + gw * o_w
    o_ref[...] = _unstack_heads(o, g, TQ).astype(o_ref.dtype)


def _nsa_prompt(proj, B, T, kc, vc):
    TQ = NSA_BLOCK
    nq = T // TQ
    WIN = min(NSA_WINDOW + TQ, T)
    kv = lambda j: pl.BlockSpec((T, LANES), lambda b, g, t: (b, AB_KV_BLK + j))
    n_blk = T // NSA_BLOCK
    return pl.pallas_call(
        functools.partial(_nsa_prompt_kernel, T=T, TQ=TQ, WIN=WIN),
        out_shape=jax.ShapeDtypeStruct((B * T, NSA_Q_W), BF16),
        grid=(B, NSA_KV_HEADS, nq),
        in_specs=[
            pl.BlockSpec((TQ, 256), lambda b, g, t: (b * nq + t, 6 + g)),
            pl.BlockSpec((TQ, LANES), lambda b, g, t: (b * nq + t, AB_SMALL_BLK)),
            pl.BlockSpec((1, n_blk, LANES), lambda b, g, t: (b, 0, 0)),
            pl.BlockSpec((1, n_blk, LANES), lambda b, g, t: (b, 0, 0)),
            kv(2), kv(3), kv(4), kv(5),
        ],
        out_specs=pl.BlockSpec((TQ, 256), lambda b, g, t: (b * nq + t, g)),
        scratch_shapes=[pltpu.VMEM((T, 2 * LANES), BF16), pltpu.VMEM((T, LANES), BF16),
                        pltpu.VMEM((T, 2 * LANES), BF16), pltpu.VMEM((T, LANES), BF16)],
        compiler_params=_cparams(("arbitrary", "arbitrary", "arbitrary")),
        name="nsa_prompt",
    )(proj, proj, kc, vc, proj, proj, proj, proj)


def _nsa_sample_kernel(pt_ref, q_ref, sm_ref, kc_ref, vc_ref, poolk_ref, poolv_ref,
                       kn_ref, vn_ref, wk_ref, wv_ref, kwn_ref, vwn_ref,
                       o_ref, wko_ref, wvo_ref, bufk, bufv, sc_sc, sems, *, n_pages, TQ):
    b = pl.program_id(0)
    past = n_pages * PAGE_SIZE
    n_cmp = past // NSA_BLOCK
    n_sel = n_cmp + 1
    R = NSA_HPG * TQ
    R2 = NSA_KV_HEADS * R
    n_buf = wk_ref.shape[2]
    G = range(NSA_KV_HEADS)

    _gather_pages(pt_ref, b, poolk_ref, bufk, sems.at[0], n_pages, True)
    _gather_pages(pt_ref, b, poolv_ref, bufv, sems.at[1], n_pages, True)

    qs = jnp.concatenate([_stack_queries(q_ref[:, g * 256:(g + 1) * 256], g, TQ) for g in G], axis=0)
    slope = jnp.concatenate([_row_slopes(g, TQ) for g in G], axis=0)
    gates = [_gate_columns(sm_ref[...], g, TQ) for g in G]
    gc, gs, gw = [jnp.concatenate([gates[g][br] for g in G], axis=0) for br in range(3)]
    tq_i = past + lax.broadcasted_iota(jnp.int32, (R2, 1), 0) % TQ
    tq_f = tq_i.astype(F32)
    new_i = past + lax.broadcasted_iota(jnp.int32, (1, TQ), 1)

    n_i = lax.broadcasted_iota(jnp.int32, (1, n_cmp), 1)
    center = (n_i * NSA_BLOCK).astype(F32) + 0.5 * (NSA_BLOCK - 1)
    s_c = _mm_nt(qs, kc_ref[0]) - slope * (tq_f - center)
    p_c = _masked_softmax(s_c, (n_i * NSA_BLOCK + NSA_BLOCK - 1) <= tq_i)
    o_c = _mm(p_c, vc_ref[0])

    bias_rows = []
    for g in G:
        score = p_c[g * R:g * R + TQ]
        for hl in range(1, NSA_HPG):
            score = score + p_c[g * R + hl * TQ:g * R + (hl + 1) * TQ]
        forced = (n_i == 0) | (n_i == n_cmp - 1)
        score = jnp.where(forced, FORCED_SCORE, score)
        sel = _topk_select(score, min(NSA_TOP_K, n_sel) - 1, n_cmp)
        bias_rows += [jnp.where(sel, 0.0, NEG)] * NSA_HPG
    sel_bias = jnp.concatenate(bias_rows, axis=0)

    wk_t = wk_ref[0]
    wv_t = wv_ref[0]
    wpos = past - n_buf + lax.broadcasted_iota(jnp.int32, (1, n_buf), 1)
    d_o = tq_i - wpos
    d_n = tq_i - new_i
    m_o = (wpos >= 0) & (d_o >= 0) & (d_o <= NSA_WINDOW)
    m_n = (d_n >= 0) & (d_n <= NSA_WINDOW)
    s_wo = jnp.where(m_o, _mm(qs, wk_t) - slope * d_o.astype(F32), NEG)
    s_wn = jnp.where(m_n, _mm_nt(qs, kwn_ref[...]) - slope * d_n.astype(F32), NEG)
    mw = jnp.maximum(jnp.max(s_wo, axis=-1, keepdims=True), jnp.max(s_wn, axis=-1, keepdims=True))
    e_o = jnp.where(m_o, jnp.exp(s_wo - mw), 0.0)
    e_w = jnp.where(m_n, jnp.exp(s_wn - mw), 0.0)
    den_w = jnp.maximum(jnp.sum(e_o, axis=-1, keepdims=True) + jnp.sum(e_w, axis=-1, keepdims=True), 1e-30)
    o_w = (_mm_nt(e_o, wv_t) + _mm(e_w, vwn_ref[...])) * (1.0 / den_w)

    lane_w = lax.broadcasted_iota(jnp.int32, (LANES, n_buf), 1)

    def shifted(old_t, new):
        slots = jnp.concatenate([jnp.zeros((LANES - TQ, LANES), F32), new], axis=0)
        tail = jnp.concatenate([jnp.zeros((LANES, n_buf - LANES), F32), slots.T], axis=1)
        return jnp.where(lane_w >= n_buf - TQ, tail, pltpu.roll(old_t, n_buf - TQ, axis=1))

    wko_ref[0] = shifted(wk_t, kwn_ref[...])
    wvo_ref[0] = shifted(wv_t, vwn_ref[...])

    _gather_pages(pt_ref, b, poolk_ref, bufk, sems.at[0], n_pages, False)
    tok = lax.broadcasted_iota(jnp.int32, (1, PAGE_SIZE), 1)
    second = tok >= NSA_BLOCK
    m_run = jnp.full((R2, PAGE_SIZE), NEG, F32)
    for p in range(n_pages):
        s = jnp.dot(qs, bufk[p * PAGE_SIZE:(p + 1) * PAGE_SIZE, :].astype(BF16), preferred_element_type=F32)
        bias = jnp.where(second, sel_bias[:, 2 * p + 1:2 * p + 2], sel_bias[:, 2 * p:2 * p + 1])
        s = s + bias - slope * (tq_f - (p * PAGE_SIZE + tok).astype(F32))
        sc_sc[p] = s
        m_run = jnp.maximum(m_run, s)
    s_n = jnp.where(new_i <= tq_i, _mm_nt(qs, kn_ref[...]) - slope * (tq_i - new_i).astype(F32), NEG)
    m = jnp.maximum(jnp.max(m_run, axis=-1, keepdims=True), jnp.max(s_n, axis=-1, keepdims=True))

    _gather_pages(pt_ref, b, poolv_ref, bufv, sems.at[1], n_pages, False)
    e_n = jnp.exp(s_n - m)
    acc = _mm(e_n, vn_ref[...])
    den_run = jnp.zeros((R2, PAGE_SIZE), F32)
    for p in range(n_pages):
        e = jnp.exp(sc_sc[p] - m)
        den_run = den_run + e
        acc = acc + _mm_nt(e, bufv[p * PAGE_SIZE:(p + 1) * PAGE_SIZE, :])
    den = jnp.sum(den_run, axis=-1, keepdims=True) + jnp.sum(e_n, axis=-1, keepdims=True)
    o_s = acc * (1.0 / den)

    o = gc * o_c + gs * o_s + gw * o_w
    o_ref[...] = jnp.concatenate([_unstack_heads(o[g * R:(g + 1) * R], g, TQ) for g in G], axis=1).astype(o_ref.dtype)


def _nsa_sample(proj, page_table, kc, vc, pool_k, pool_v, win_k, win_v):
    B, n_pages = page_table.shape
    TQ = proj.shape[0] // B
    past = n_pages * PAGE_SIZE
    n_cmp = past // NSA_BLOCK
    n_buf = win_k.shape[2]
    R2 = NSA_KV_HEADS * NSA_HPG * TQ
    kvn = lambda j: pl.BlockSpec((TQ, LANES), lambda b, pt: (b, AB_KV_BLK + j))
    win = pl.BlockSpec((1, LANES, n_buf), lambda b, pt: (b, 0, 0))
    return pl.pallas_call(
        functools.partial(_nsa_sample_kernel, n_pages=n_pages, TQ=TQ),
        out_shape=(jax.ShapeDtypeStruct((B * TQ, NSA_Q_W), BF16),
                   jax.ShapeDtypeStruct((B, LANES, n_buf), F32),
                   jax.ShapeDtypeStruct((B, LANES, n_buf), F32)),
        grid_spec=pltpu.PrefetchScalarGridSpec(
            num_scalar_prefetch=1,
            grid=(B,),
            in_specs=[
                pl.BlockSpec((TQ, NSA_Q_W), lambda b, pt: (b, 3)),
                pl.BlockSpec((TQ, LANES), lambda b, pt: (b, AB_SMALL_BLK)),
                pl.BlockSpec((1, n_cmp, LANES), lambda b, pt: (b, 0, 0)),
                pl.BlockSpec((1, n_cmp, LANES), lambda b, pt: (b, 0, 0)),
                pl.BlockSpec(memory_space=pl.ANY),
                pl.BlockSpec(memory_space=pl.ANY),
                kvn(2), kvn(3), win, win, kvn(4), kvn(5),
            ],
            out_specs=(pl.BlockSpec((TQ, NSA_Q_W), lambda b, pt: (b, 0)), win, win),
            scratch_shapes=[
                pltpu.VMEM((past, LANES), F32),
                pltpu.VMEM((past, LANES), F32),
                pltpu.VMEM((n_pages, R2, PAGE_SIZE), F32),
                pltpu.SemaphoreType.DMA((2,)),
            ],
        ),
        compiler_params=_cparams(("arbitrary",)),
        name="nsa_sample",
    )(page_table, proj, proj, kc, vc, pool_k, pool_v, proj, proj, win_k, win_v, proj, proj)


def _gdn_kernel(qkv_ref, z_ref, sm_ref, cw_ref, alog_ref, dtb_ref, ng_ref, s0_ref, cb_ref,
                o_ref, sfin_ref, S_sc, prev_sc, *, C):
    t = pl.program_id(1)

    @pl.when(t == 0)
    def _():
        S_sc[...] = s0_ref[0]
        prev_sc[...] = cb_ref[0]

    def conv(c0):
        xe = jnp.concatenate([prev_sc[:, c0:c0 + LANES], qkv_ref[:, c0:c0 + LANES]], axis=0)
        y = xe[SUBLANES:] * cw_ref[GDN_CONV - 1:GDN_CONV, c0:c0 + LANES]
        for s in range(1, GDN_CONV):
            y = y + pltpu.roll(xe, s, axis=0)[SUBLANES:] * cw_ref[GDN_CONV - 1 - s:GDN_CONV - s, c0:c0 + LANES]
        return _silu(y)

    sm = sm_ref[...]
    beta = _sigmoid(sm)
    gt = -jnp.exp(alog_ref[...]) * _softplus(sm + dtb_ref[...])
    d = _cumsum_rows(gt)
    dT = d.T
    ii = lax.broadcasted_iota(jnp.int32, (C, C), 0)
    jj = lax.broadcasted_iota(jnp.int32, (C, C), 1)
    eye = (ii == jj).astype(F32)

    H = range(GDN_HEADS)
    q, k, v = [], [], []
    for h in H:
        qh = conv(h * GDN_DK)
        kh = conv(GDN_W + h * GDN_DK)
        q.append(qh * lax.rsqrt(jnp.sum(qh * qh, axis=-1, keepdims=True) + L2_EPS) * (GDN_DK ** -0.5))
        k.append(kh * lax.rsqrt(jnp.sum(kh * kh, axis=-1, keepdims=True) + L2_EPS))
        v.append(conv(2 * GDN_W + h * GDN_DV))
    beta_c = [beta[:, h:h + 1] for h in H]
    d_c = [d[:, GDN_A_LANE0 + h:GDN_A_LANE0 + h + 1] for h in H]
    decay = [jnp.exp(jnp.minimum(d_c[h] - dT[GDN_A_LANE0 + h:GDN_A_LANE0 + h + 1, :], 0.0)) for h in H]
    kb = [k[h] * beta_c[h] for h in H]
    g_kk = [_mm_nt(kb[h], k[h]) for h in H]
    g_qk = [_mm_nt(q[h], k[h]) for h in H]
    a = [jnp.where(ii > jj, g_kk[h] * decay[h], 0.0) for h in H]
    qk = [jnp.where(ii >= jj, g_qk[h] * decay[h], 0.0) for h in H]
    tinv = [eye - a[h] for h in H]
    p_split = [_split_bf16(a[h]) for h in H]
    n = 2
    while n < C:
        p = [_mm3(p_split[h], p_split[h]) for h in H]
        p_split = [_split_bf16(p[h]) for h in H]
        tinv = [tinv[h] + _mm3(_split_bf16(tinv[h]), p_split[h]) for h in H]
        n *= 2
    u = [_mm(tinv[h], v[h] * beta_c[h]) for h in H]
    w = [_mm(tinv[h], kb[h] * jnp.exp(d_c[h])) for h in H]
    S = [S_sc[h] for h in H]
    w_s = [_mm(w[h], S[h]) for h in H]
    q_s = [_mm(q[h] * jnp.exp(d_c[h]), S[h]) for h in H]
    v_new = [u[h] - w_s[h] for h in H]
    o = [q_s[h] + _mm(qk[h], v_new[h]) for h in H]
    d_last = [d_c[h][C - 1:C, :] for h in H]
    upd = [_mm_tn(k[h] * jnp.exp(d_last[h] - d_c[h]), v_new[h]) for h in H]
    for h in H:
        S_sc[h] = S[h] * jnp.exp(d_last[h]) + upd[h]
        ms = jnp.mean(o[h] * o[h], axis=-1, keepdims=True)
        zh = z_ref[:, h * GDN_DV:(h + 1) * GDN_DV]
        o_ref[:, h * GDN_DV:(h + 1) * GDN_DV] = (
            o[h] * lax.rsqrt(ms + RMS_EPS) * ng_ref[...] * _silu(zh)).astype(o_ref.dtype)

    prev_sc[...] = qkv_ref[C - SUBLANES:C, :]

    @pl.when(t == pl.num_programs(1) - 1)
    def _():
        sfin_ref[0] = S_sc[...]


def _gdn(proj, B, T, conv_w, a_log, dt_bias, norm_g, s0, conv_buf8, C):
    nt = T // C
    pad = lambda v: jnp.zeros((1, LANES), F32).at[0, GDN_A_LANE0:GDN_A_LANE0 + GDN_HEADS].set(v)
    return pl.pallas_call(
        functools.partial(_gdn_kernel, C=C),
        out_shape=(jax.ShapeDtypeStruct((B * T, GDN_W), BF16),
                   jax.ShapeDtypeStruct((B, GDN_HEADS, GDN_DK, GDN_DV), F32)),
        grid=(B, nt),
        in_specs=[
            pl.BlockSpec((C, GDN_CONV_CH), lambda b, t: (b * nt + t, 0)),
            pl.BlockSpec((C, GDN_W), lambda b, t: (b * nt + t, 3)),
            pl.BlockSpec((C, LANES), lambda b, t: (b * nt + t, C_SMALL_BLK)),
            pl.BlockSpec((GDN_CONV, GDN_CONV_CH), lambda b, t: (0, 0)),
            pl.BlockSpec((1, LANES), lambda b, t: (0, 0)),
            pl.BlockSpec((1, LANES), lambda b, t: (0, 0)),
            pl.BlockSpec((1, GDN_DV), lambda b, t: (0, 0)),
            pl.BlockSpec((1, GDN_HEADS, GDN_DK, GDN_DV), lambda b, t: (b, 0, 0, 0)),
            pl.BlockSpec((1, SUBLANES, GDN_CONV_CH), lambda b, t: (b, 0, 0)),
        ],
        out_specs=(pl.BlockSpec((C, GDN_W), lambda b, t: (b * nt + t, 0)),
                   pl.BlockSpec((1, GDN_HEADS, GDN_DK, GDN_DV), lambda b, t: (b, 0, 0, 0))),
        scratch_shapes=[
            pltpu.VMEM((GDN_HEADS, GDN_DK, GDN_DV), F32),
            pltpu.VMEM((SUBLANES, GDN_CONV_CH), F32),
        ],
        compiler_params=_cparams(("arbitrary", "arbitrary")),
        name="gdn",
    )(proj, proj, proj, conv_w, pad(a_log), pad(dt_bias), norm_g.reshape(1, GDN_DV), s0, conv_buf8)


def _ab_in_weight(w):
    big = w[:, :GLA_KEY_W * 2 + GLA_VAL_W * 2]
    gk = w[:, 1536:1536 + GLA_LOWRANK]
    rest = w[:, 1536 + GLA_LOWRANK:]
    q_b = rest[:, :NSA_Q_W]
    kv = rest[:, NSA_Q_W:NSA_Q_W + 6 * NSA_KV_W]
    gate = rest[:, NSA_Q_W + 6 * NSA_KV_W:]
    small = jnp.concatenate([gk, gate, jnp.zeros((D_MODEL, LANES - GLA_LOWRANK - 3 * NSA_HEADS), w.dtype)], axis=1)
    return jnp.concatenate([big, q_b, kv, small], axis=1).astype(BF16)


def _c_in_weight(w):
    qkv = w[:, :GDN_CONV_CH]
    ba = w[:, GDN_CONV_CH:GDN_CONV_CH + 2 * GDN_HEADS]
    z = w[:, GDN_CONV_CH + 2 * GDN_HEADS:]
    small = jnp.concatenate([ba, jnp.zeros((D_MODEL, LANES - 2 * GDN_HEADS), w.dtype)], axis=1)
    return jnp.concatenate([qkv, z, small], axis=1).astype(BF16)


def _kv_out(proj, B, T, j):
    return proj[:, (AB_KV_BLK + j) * LANES:(AB_KV_BLK + j + 1) * LANES].reshape(B, T, NSA_KV_HEADS, NSA_HEAD_DIM)


PROMPT_ROWS = 512
FFN_ROWS = 1024


def kernel(x_prompt, x_sample, c_prompt, c_sample, page_table, cache_cmp_k, cache_cmp_v, cache_sel_k, cache_sel_v, state_win_k, state_win_v, state_gla, state_gdn, state_gdn_conv, w_ada, b_ada, ln_g, ln_b, w_ffn_in, w_ffn_out, ab_w_in, ab_w_gk2, ab_b_gk, ab_gla_norm, ab_cmp_pe, ab_cmp_w1, ab_cmp_w2, ab_w_out, c_w_in, c_conv_w, c_a_log, c_dt_bias, c_norm, c_w_out):
    Bp, Tp, _ = x_prompt.shape
    Bs, Ts, _ = x_sample.shape
    n_pool = cache_cmp_k.shape[1]

    mods = _adaln(jnp.concatenate([c_prompt, c_sample], axis=0), w_ada, b_ada)

    def layer_mods(layer):
        m = mods[layer]
        parts = [m[:, i * D_MODEL:(i + 1) * D_MODEL] for i in range(6)]
        return [p[:Bp] for p in parts], [p[Bp:] for p in parts]

    xp, xs = x_prompt, x_sample
    ab_p, ab_s, c_p, c_s = [], [], [], []
    for layer in range(DEPTH):
        mp, ms = layer_mods(layer)
        wf_in = w_ffn_in[layer].astype(BF16)
        wf_out = w_ffn_out[layer].astype(BF16)
        i = layer // 2
        if layer % 2 == 0:
            w_in = _ab_in_weight(ab_w_in[i])
            w_out = ab_w_out[i].astype(BF16)
            wo_a, wo_b = w_out[:GLA_VAL_W], w_out[GLA_VAL_W:]
            cw = _cmp_weights(ab_cmp_pe[i], ab_cmp_w1[i], ab_cmp_w2[i])
            cw_t = _cmp_weights_t(ab_cmp_pe[i], ab_cmp_w1[i], ab_cmp_w2[i])

            proj = _modmm(xp, mp[0], mp[1], w_in, PROMPT_ROWS)
            zero_state = jnp.zeros((Bp, GLA_HEADS, GLA_DK, GLA_DV), F32)
            o_a, s_a = _gla(proj, Bp, Tp, ab_w_gk2[i], ab_b_gk[i], ab_gla_norm[i], zero_state, min(64, Tp))
            kc, vc = _compress_dense(proj, Bp, Tp, cw)
            o_b = _nsa_prompt(proj, Bp, Tp, kc, vc)
            x1 = _outproj_ln([o_a, o_b], [wo_a, wo_b], xp, mp[2], ln_g[layer, 0], ln_b[layer, 0], PROMPT_ROWS)
            n_keep = min(NSA_WINDOW, Tp)
            ab_p.append(tuple(_kv_out(proj, Bp, Tp, j) for j in range(4))
                        + (_kv_out(proj, Bp, Tp, 4)[:, Tp - n_keep:], _kv_out(proj, Bp, Tp, 5)[:, Tp - n_keep:], s_a))
            xp = _ffn_ln(x1, mp[3], mp[4], mp[5], wf_in, wf_out, ln_g[layer, 1], ln_b[layer, 1], FFN_ROWS)

            proj = _modmm(xs, ms[0], ms[1], w_in, PROMPT_ROWS)
            o_a, s_a = _gla(proj, Bs, Ts, ab_w_gk2[i], ab_b_gk[i], ab_gla_norm[i], state_gla[i], min(64, Ts))
            pool = lambda c: jnp.transpose(c[i], (0, 2, 3, 1)).reshape(n_pool * PAGE_SIZE, LANES)
            n_buf = state_win_k.shape[2]
            win_t = lambda w: jnp.transpose(w[i], (0, 2, 3, 1)).reshape(Bs, LANES, n_buf)
            kc, vc = _compress_paged(page_table, pool(cache_cmp_k), pool(cache_cmp_v), cw_t)
            o_b, win_k, win_v = _nsa_sample(proj, page_table, kc, vc, pool(cache_sel_k), pool(cache_sel_v),
                                            win_t(state_win_k), win_t(state_win_v))
            win_out = lambda w: jnp.transpose(w.reshape(Bs, NSA_KV_HEADS, NSA_HEAD_DIM, n_buf), (0, 3, 1, 2))
            x1 = _outproj_ln([o_a, o_b], [wo_a, wo_b], xs, ms[2], ln_g[layer, 0], ln_b[layer, 0], PROMPT_ROWS)
            ab_s.append(tuple(_kv_out(proj, Bs, Ts, j) for j in range(4))
                        + (win_out(win_k), win_out(win_v), s_a))
            xs = _ffn_ln(x1, ms[3], ms[4], ms[5], wf_in, wf_out, ln_g[layer, 1], ln_b[layer, 1], FFN_ROWS)
        else:
            w_in = _c_in_weight(c_w_in[i])
            w_out = c_w_out[i].astype(BF16)
            keep = GDN_CONV - 1

            proj = _modmm(xp, mp[0], mp[1], w_in, PROMPT_ROWS // 2)
            o_c, s_c = _gdn(proj, Bp, Tp, c_conv_w[i], c_a_log[i], c_dt_bias[i], c_norm[i],
                            jnp.zeros((Bp, GDN_HEADS, GDN_DK, GDN_DV), F32),
                            jnp.zeros((Bp, SUBLANES, GDN_CONV_CH), F32), min(64, Tp))
            x1 = _outproj_ln([o_c], [w_out], xp, mp[2], ln_g[layer, 0], ln_b[layer, 0], PROMPT_ROWS)
            c_p.append((s_c, proj[:, :GDN_CONV_CH].reshape(Bp, Tp, GDN_CONV_CH)[:, Tp - keep:]))
            xp = _ffn_ln(x1, mp[3], mp[4], mp[5], wf_in, wf_out, ln_g[layer, 1], ln_b[layer, 1], FFN_ROWS)

            proj = _modmm(xs, ms[0], ms[1], w_in, PROMPT_ROWS // 2)
            conv8 = jnp.concatenate([jnp.zeros((Bs, SUBLANES - keep, GDN_CONV_CH), F32), state_gdn_conv[i]], axis=1)
            o_c, s_c = _gdn(proj, Bs, Ts, c_conv_w[i], c_a_log[i], c_dt_bias[i], c_norm[i],
                            state_gdn[i], conv8, min(64, Ts))
            x1 = _outproj_ln([o_c], [w_out], xs, ms[2], ln_g[layer, 0], ln_b[layer, 0], PROMPT_ROWS)
            qkv_s = proj[:, :GDN_CONV_CH].reshape(Bs, Ts, GDN_CONV_CH)
            c_s.append((s_c, jnp.concatenate([state_gdn_conv[i], qkv_s], axis=1)[:, -keep:]))
            xs = _ffn_ln(x1, ms[3], ms[4], ms[5], wf_in, wf_out, ln_g[layer, 1], ln_b[layer, 1], FFN_ROWS)

    stack = lambda sts: [jnp.stack(z) for z in zip(*sts)]
    p_ab, s_ab = stack(ab_p), stack(ab_s)
    p_c, s_c = stack(c_p), stack(c_s)
    return (xp, xs, *p_ab, *p_c, *s_ab, *s_c)
```

```python
import functools

import jax
import jax.numpy as jnp
from jax import lax
from jax.experimental import pallas as pl
from jax.experimental.pallas import tpu as pltpu

F32 = jnp.float32
BF16 = jnp.bfloat16

D_MODEL = 1024
DEPTH = 2
PAGE_SIZE = 128
GLA_HEADS = 4
GLA_DK = 64
GLA_DV = 128
GLA_LOWRANK = 16
GLA_GATE_NORM = 16.0
NSA_HEAD_DIM = 64
NSA_HEADS = 8
NSA_KV_HEADS = 2
NSA_HPG = 4
NSA_BLOCK = 64
NSA_TOP_K = 16
NSA_WINDOW = 512
NSA_CMP_HIDDEN = 128
FORCED_SCORE = 1000.0
GDN_HEADS = 8
GDN_DK = 128
GDN_DV = 128
GDN_CONV = 4
FF_HIDDEN = 2816
DEEPNORM_ALPHA = (2.0 * DEPTH) ** 0.25
LN_EPS = 1e-5
RMS_EPS = 1e-6
L2_EPS = 1e-6
NEG = -1e30

GLA_KEY_W = GLA_HEADS * GLA_DK
GLA_VAL_W = GLA_HEADS * GLA_DV
NSA_Q_W = NSA_HEADS * NSA_HEAD_DIM
NSA_KV_W = NSA_KV_HEADS * NSA_HEAD_DIM
GDN_W = GDN_HEADS * GDN_DK
GDN_CONV_CH = 3 * GDN_W

LANES = 128
SUBLANES = 8
VMEM_LIMIT = 56 * 1024 * 1024

AB_COLS = 2944
AB_SMALL_BLK = 22
AB_KV_BLK = 16
GATE_LANE0 = GLA_LOWRANK
C_COLS = 4224
C_SMALL_BLK = 32
GDN_A_LANE0 = GDN_HEADS
FEAT_BLK = 64
FEAT_OFF = 65
FEAT_PAD = 66
SEL_BIG = 131072.0


def _cparams(sem):
    return pltpu.CompilerParams(dimension_semantics=sem, vmem_limit_bytes=VMEM_LIMIT)


def _silu(x):
    return x * (1.0 / (1.0 + jnp.exp(-x)))


def _sigmoid(x):
    return 1.0 / (1.0 + jnp.exp(-x))


def _softplus(x):
    return jnp.maximum(x, 0.0) + jnp.log(1.0 + jnp.exp(-jnp.abs(x)))


def _mm(a, b):
    return jnp.dot(a.astype(BF16), b.astype(BF16), preferred_element_type=F32)


def _split_bf16(x):
    hi = x.astype(BF16)
    return hi, (x - hi.astype(F32)).astype(BF16)


def _mm3(a, b):
    dot = lambda x, y: jnp.dot(x, y, preferred_element_type=F32)
    return dot(a[0], b[0]) + (dot(a[1], b[0]) + dot(a[0], b[1]))


def _mm_nt(a, b):
    return lax.dot_general(a.astype(BF16), b.astype(BF16), (((1,), (1,)), ((), ())),
                           preferred_element_type=F32)


def _mm_tn(a, b):
    return lax.dot_general(a.astype(BF16), b.astype(BF16), (((0,), (0,)), ((), ())),
                           preferred_element_type=F32)


def _cumsum_rows(x):
    n = x.shape[0]
    row = lax.broadcasted_iota(jnp.int32, x.shape, 0)
    s = 1
    while s < n:
        x = x + jnp.where(row >= s, pltpu.roll(x, s, axis=0), 0.0)
        s *= 2
    return x


def _masked_softmax(s, mask):
    s = jnp.where(mask, s, NEG)
    m = jnp.max(s, axis=-1, keepdims=True)
    e = jnp.where(mask, jnp.exp(s - m), 0.0)
    den = jnp.maximum(jnp.sum(e, axis=-1, keepdims=True), 1e-30)
    return e * (1.0 / den)


def _layernorm(z, g, b):
    mu = jnp.mean(z, axis=-1, keepdims=True)
    zc = z - mu
    var = jnp.mean(zc * zc, axis=-1, keepdims=True)
    return zc * lax.rsqrt(var + LN_EPS) * g + b


def _adaln_kernel(c_ref, w_ref, b_ref, o_ref):
    c = _silu(c_ref[...])
    o_ref[0] = _mm(c, w_ref[0]) + b_ref[0]


def _adaln(c_all, w_ada, b_ada):
    n = c_all.shape[0]
    tn = 1536
    nt = (6 * D_MODEL) // tn
    return pl.pallas_call(
        _adaln_kernel,
        out_shape=jax.ShapeDtypeStruct((DEPTH, n, 6 * D_MODEL), F32),
        grid=(DEPTH, nt),
        in_specs=[
            pl.BlockSpec((n, D_MODEL), lambda l, j: (0, 0)),
            pl.BlockSpec((1, D_MODEL, tn), lambda l, j: (l, 0, j)),
            pl.BlockSpec((1, 1, tn), lambda l, j: (l, 0, j)),
        ],
        out_specs=pl.BlockSpec((1, n, tn), lambda l, j: (l, 0, j)),
        compiler_params=_cparams(("arbitrary", "arbitrary")),
        name="adaln",
    )(c_all, w_ada, b_ada.reshape(DEPTH, 1, 6 * D_MODEL))


def _modmm_kernel(x_ref, sh_ref, sc_ref, w_ref, o_ref):
    bb, tt, d = x_ref.shape
    h = x_ref[...] * (1.0 + sc_ref[...]) + sh_ref[...]
    o_ref[...] = _mm(h.reshape(bb * tt, d), w_ref[...])


def _row_tiling(B, T, max_rows):
    if T >= max_rows:
        return 1, max_rows
    bb = min(B, max_rows // T)
    return bb, T


def _modmm(x, shift, scale, w_bf16, max_rows):
    B, T, D = x.shape
    N = w_bf16.shape[1]
    bb, tt = _row_tiling(B, T, max_rows)
    nt = T // tt
    return pl.pallas_call(
        _modmm_kernel,
        out_shape=jax.ShapeDtypeStruct((B * T, N), F32),
        grid=(B // bb, nt),
        in_specs=[
            pl.BlockSpec((bb, tt, D), lambda i, j: (i, j, 0)),
            pl.BlockSpec((bb, 1, D), lambda i, j: (i, 0, 0)),
            pl.BlockSpec((bb, 1, D), lambda i, j: (i, 0, 0)),
            pl.BlockSpec((D, N), lambda i, j: (0, 0)),
        ],
        out_specs=pl.BlockSpec((bb * tt, N), lambda i, j: (i * nt + j, 0)),
        compiler_params=_cparams(("arbitrary", "arbitrary")),
        name="modmm",
    )(x, shift[:, None, :], scale[:, None, :], w_bf16)


def _outproj_kernel(*refs, n_in):
    a_refs = refs[:n_in]
    w_refs = refs[n_in:2 * n_in]
    x_ref, gate_ref, g_ref, b_ref, o_ref = refs[2 * n_in:]
    bb, tt, d = x_ref.shape
    acc = _mm(a_refs[0][...], w_refs[0][...])
    for a_ref, w_ref in zip(a_refs[1:], w_refs[1:]):
        acc = acc + _mm(a_ref[...], w_ref[...])
    z = DEEPNORM_ALPHA * x_ref[...] + gate_ref[...] * acc.reshape(bb, tt, d)
    o_ref[...] = _layernorm(z, g_ref[...], b_ref[...])


def _outproj_ln(acts, ws, x, gate, ln_g, ln_b, max_rows):
    B, T, D = x.shape
    bb, tt = _row_tiling(B, T, max_rows)
    nt = T // tt
    n_in = len(acts)
    in_specs = []
    for a in acts:
        in_specs.append(pl.BlockSpec((bb * tt, a.shape[1]), lambda i, j: (i * nt + j, 0)))
    for w in ws:
        in_specs.append(pl.BlockSpec(w.shape, lambda i, j: (0, 0)))
    in_specs += [
        pl.BlockSpec((bb, tt, D), lambda i, j: (i, j, 0)),
        pl.BlockSpec((bb, 1, D), lambda i, j: (i, 0, 0)),
        pl.BlockSpec((1, 1, D), lambda i, j: (0, 0, 0)),
        pl.BlockSpec((1, 1, D), lambda i, j: (0, 0, 0)),
    ]
    return pl.pallas_call(
        functools.partial(_outproj_kernel, n_in=n_in),
        out_shape=jax.ShapeDtypeStruct((B, T, D), F32),
        grid=(B // bb, nt),
        in_specs=in_specs,
        out_specs=pl.BlockSpec((bb, tt, D), lambda i, j: (i, j, 0)),
        compiler_params=_cparams(("arbitrary", "arbitrary")),
        name="outproj_ln",
    )(*acts, *ws, x, gate[:, None, :], ln_g.reshape(1, 1, D), ln_b.reshape(1, 1, D))


def _ffn_kernel(x_ref, sh_ref, sc_ref, gate_ref, wa_ref, wu_ref, wo_ref, g_ref, b_ref, o_ref,
                xm_sc, acc_sc):
    j = pl.program_id(2)
    bb, tt, d = x_ref.shape

    @pl.when(j == 0)
    def _():
        h = x_ref[...] * (1.0 + sc_ref[...]) + sh_ref[...]
        xm_sc[...] = h.reshape(bb * tt, d).astype(BF16)
        acc_sc[...] = jnp.zeros_like(acc_sc)

    xm = xm_sc[...]
    a = jnp.dot(xm, wa_ref[...], preferred_element_type=F32)
    u = jnp.dot(xm, wu_ref[...], preferred_element_type=F32)
    acc_sc[...] += _mm(_silu(a) * u, wo_ref[...])

    @pl.when(j == pl.num_programs(2) - 1)
    def _():
        z = DEEPNORM_ALPHA * x_ref[...] + gate_ref[...] * acc_sc[...].reshape(bb, tt, d)
        o_ref[...] = _layernorm(z, g_ref[...], b_ref[...])


def _ffn_ln(x, shift, scale, gate, w_in_bf16, w_out_bf16, ln_g, ln_b, max_rows):
    B, T, D = x.shape
    bb, tt = _row_tiling(B, T, max_rows)
    nt = T // tt
    th = 256
    nh = FF_HIDDEN // th
    vec = lambda v: v[:, None, :]
    return pl.pallas_call(
        _ffn_kernel,
        out_shape=jax.ShapeDtypeStruct((B, T, D), F32),
        grid=(B // bb, nt, nh),
        in_specs=[
            pl.BlockSpec((bb, tt, D), lambda i, t, j: (i, t, 0)),
            pl.BlockSpec((bb, 1, D), lambda i, t, j: (i, 0, 0)),
            pl.BlockSpec((bb, 1, D), lambda i, t, j: (i, 0, 0)),
            pl.BlockSpec((bb, 1, D), lambda i, t, j: (i, 0, 0)),
            pl.BlockSpec((D, th), lambda i, t, j: (0, j)),
            pl.BlockSpec((D, th), lambda i, t, j: (0, nh + j)),
            pl.BlockSpec((th, D), lambda i, t, j: (j, 0)),
            pl.BlockSpec((1, 1, D), lambda i, t, j: (0, 0, 0)),
            pl.BlockSpec((1, 1, D), lambda i, t, j: (0, 0, 0)),
        ],
        out_specs=pl.BlockSpec((bb, tt, D), lambda i, t, j: (i, t, 0)),
        scratch_shapes=[pltpu.VMEM((bb * tt, D), BF16), pltpu.VMEM((bb * tt, D), F32)],
        compiler_params=_cparams(("arbitrary", "arbitrary", "arbitrary")),
        name="ffn_ln",
    )(x, vec(shift), vec(scale), vec(gate), w_in_bf16, w_in_bf16, w_out_bf16,
      ln_g.reshape(1, 1, D), ln_b.reshape(1, 1, D))


def _gla_kernel(q_ref, k_ref, v_ref, r_ref, sm_ref, wgk_ref, bgk_ref, gn_ref, s0_ref,
                o_ref, sfin_ref, S_sc, q_sc, k_sc, b_sc, v_sc, o_sc, *, C):
    t = pl.program_id(1)
    KW, VW = GLA_KEY_W, GLA_VAL_W

    hk = lax.broadcasted_iota(jnp.int32, (KW, VW), 0) // GLA_DK
    hv = lax.broadcasted_iota(jnp.int32, (KW, VW), 1) // GLA_DV
    same_head = hk == hv

    @pl.when(t == 0)
    def _():
        rows = []
        for h in range(GLA_HEADS):
            pieces = [s0_ref[0, h] if h2 == h else jnp.zeros((GLA_DK, GLA_DV), F32)
                      for h2 in range(GLA_HEADS)]
            rows.append(jnp.concatenate(pieces, axis=1))
        S_sc[...] = jnp.concatenate(rows, axis=0)

    gk = sm_ref[:, 0:GLA_LOWRANK]
    pre = _mm(gk, wgk_ref[...]) + bgk_ref[...]
    log_a = (jnp.minimum(pre, 0.0) - jnp.log(1.0 + jnp.exp(-jnp.abs(pre)))) * (1.0 / GLA_GATE_NORM)
    b = _cumsum_rows(log_a)
    q = q_ref[...] * (GLA_DK ** -0.5)
    k = k_ref[...]
    v = v_ref[...]
    q_sc[...] = q
    k_sc[...] = k
    b_sc[...] = b
    v_sc[...] = v

    S = S_sc[...]
    o_sc[...] = _mm(q * jnp.exp(b), S)

    seg = same_head.astype(BF16)
    JG = min(C, 2 * SUBLANES)
    for r0 in range(0, C, JG):
        rows = C - r0
        qg = q_sc[r0:C, :]
        bg = b_sc[r0:C, :]
        local = lax.broadcasted_iota(jnp.int32, (JG, KW), 0)
        ps = []
        for jj in range(JG):
            j = r0 + jj
            p = qg * k_sc[j:j + 1, :] * jnp.exp(jnp.minimum(bg - b_sc[j:j + 1, :], 0.0))
            head = jnp.where(local >= jj, p[0:JG], 0.0)
            p = head if rows == JG else jnp.concatenate([head, p[JG:]], axis=0)
            ps.append(p.astype(BF16))
        s = jnp.dot(jnp.concatenate(ps, axis=0), seg, preferred_element_type=F32)
        contrib = s[0:rows] * v_sc[r0:r0 + 1, :]
        for jj in range(1, JG):
            contrib = contrib + s[jj * rows:(jj + 1) * rows] * v_sc[r0 + jj:r0 + jj + 1, :]
        o_sc[r0:C, :] += contrib
    o = o_sc[...]

    b_last = b[C - 1:C, :]
    kd = k * jnp.exp(b_last - b)
    upd = _mm_tn(kd, v)
    tail = jnp.broadcast_to(b_last, (SUBLANES, KW))
    dcol = jnp.exp(tail.T[:, 0:1])
    S_new = S * dcol + jnp.where(same_head, upd, 0.0)
    S_sc[...] = S_new

    outs = []
    for h in range(GLA_HEADS):
        oh = o[:, h * GLA_DV:(h + 1) * GLA_DV]
        ms = jnp.mean(oh * oh, axis=-1, keepdims=True)
        rh = r_ref[:, h * GLA_DV:(h + 1) * GLA_DV]
        outs.append(oh * lax.rsqrt(ms + RMS_EPS) * gn_ref[...] * _silu(rh))
    o_ref[...] = jnp.concatenate(outs, axis=1).astype(o_ref.dtype)

    @pl.when(t == pl.num_programs(1) - 1)
    def _():
        for h in range(GLA_HEADS):
            sfin_ref[0, h] = S_new[h * GLA_DK:(h + 1) * GLA_DK, h * GLA_DV:(h + 1) * GLA_DV]


def _gla(proj, B, T, w_gk2, b_gk, gla_norm, s0, C):
    nt = T // C
    return pl.pallas_call(
        functools.partial(_gla_kernel, C=C),
        out_shape=(jax.ShapeDtypeStruct((B * T, GLA_VAL_W), BF16),
                   jax.ShapeDtypeStruct((B, GLA_HEADS, GLA_DK, GLA_DV), F32)),
        grid=(B, nt),
        in_specs=[
            pl.BlockSpec((C, GLA_KEY_W), lambda b, t: (b * nt + t, 0)),
            pl.BlockSpec((C, GLA_KEY_W), lambda b, t: (b * nt + t, 1)),
            pl.BlockSpec((C, GLA_VAL_W), lambda b, t: (b * nt + t, 1)),
            pl.BlockSpec((C, GLA_VAL_W), lambda b, t: (b * nt + t, 2)),
            pl.BlockSpec((C, LANES), lambda b, t: (b * nt + t, AB_SMALL_BLK)),
            pl.BlockSpec((GLA_LOWRANK, GLA_KEY_W), lambda b, t: (0, 0)),
            pl.BlockSpec((1, GLA_KEY_W), lambda b, t: (0, 0)),
            pl.BlockSpec((1, GLA_DV), lambda b, t: (0, 0)),
            pl.BlockSpec((1, GLA_HEADS, GLA_DK, GLA_DV), lambda b, t: (b, 0, 0, 0)),
        ],
        out_specs=(pl.BlockSpec((C, GLA_VAL_W), lambda b, t: (b * nt + t, 0)),
                   pl.BlockSpec((1, GLA_HEADS, GLA_DK, GLA_DV), lambda b, t: (b, 0, 0, 0))),
        scratch_shapes=[
            pltpu.VMEM((GLA_KEY_W, GLA_VAL_W), F32),
            pltpu.VMEM((C, GLA_KEY_W), F32),
            pltpu.VMEM((C, GLA_KEY_W), F32),
            pltpu.VMEM((C, GLA_KEY_W), F32),
            pltpu.VMEM((C, GLA_VAL_W), F32),
            pltpu.VMEM((C, GLA_VAL_W), F32),
        ],
        compiler_params=_cparams(("arbitrary", "arbitrary")),
        name="gla",
    )(proj, proj, proj, proj, proj, w_gk2, b_gk.reshape(1, GLA_KEY_W), gla_norm.reshape(1, GLA_DV), s0)


def _compress_pages(x_ref, n_pages, pe_ref, w1_ref, w2_ref):
    outs = []
    for half in range(PAGE_SIZE // NSA_BLOCK):
        pieces = [x_ref[pl.ds(half * NSA_BLOCK + tk, n_pages, stride=PAGE_SIZE), :] for tk in range(NSA_BLOCK)]
        flat = jnp.concatenate(pieces, axis=1) + pe_ref[...]
        acc = _mm(flat, w1_ref[...])
        outs.append(_mm(_silu(acc), w2_ref[...]))
    return jnp.concatenate(outs, axis=1)


def _compress_dense_kernel(xk_ref, xv_ref, pek_ref, pev_ref, w1k_ref, w1v_ref, w2k_ref, w2v_ref,
                           ok_ref, ov_ref, *, n_pages):
    ok_ref[0] = _compress_pages(xk_ref, n_pages, pek_ref, w1k_ref, w2k_ref)
    ov_ref[0] = _compress_pages(xv_ref, n_pages, pev_ref, w1v_ref, w2v_ref)


def _cmp_weights(cmp_pe, cmp_w1, cmp_w2):
    out = []
    for i in range(2):
        pe2 = jnp.concatenate([cmp_pe[i], cmp_pe[i]], axis=1).reshape(1, NSA_BLOCK * LANES)
        w1 = cmp_w1[i].reshape(NSA_BLOCK, NSA_HEAD_DIM, NSA_CMP_HIDDEN)
        z1 = jnp.zeros_like(w1)
        w1bd = jnp.concatenate([jnp.concatenate([w1, z1], axis=2),
                                jnp.concatenate([z1, w1], axis=2)], axis=1).astype(BF16)
        w1bd = w1bd.reshape(NSA_BLOCK * LANES, 2 * NSA_CMP_HIDDEN)
        w2 = cmp_w2[i]
        z2 = jnp.zeros_like(w2)
        w2bd = jnp.concatenate([jnp.concatenate([w2, z2], axis=1),
                                jnp.concatenate([z2, w2], axis=1)], axis=0).astype(BF16)
        out.append((pe2, w1bd, w2bd))
    return out


def _compress_dense(proj, B, T, cw):
    n_pages = T // PAGE_SIZE
    (pek, w1k, w2k), (pev, w1v, w2v) = cw
    full = lambda a: pl.BlockSpec(a.shape, lambda b: (0,) * a.ndim)
    ok, ov = pl.pallas_call(
        functools.partial(_compress_dense_kernel, n_pages=n_pages),
        out_shape=(jax.ShapeDtypeStruct((B, n_pages, 2 * LANES), F32),) * 2,
        grid=(B,),
        in_specs=[
            pl.BlockSpec((T, LANES), lambda b: (b, AB_KV_BLK)),
            pl.BlockSpec((T, LANES), lambda b: (b, AB_KV_BLK + 1)),
            full(pek), full(pev), full(w1k), full(w1v), full(w2k), full(w2v),
        ],
        out_specs=(pl.BlockSpec((1, n_pages, 2 * LANES), lambda b: (b, 0, 0)),) * 2,
        compiler_params=_cparams(("arbitrary",)),
        name="compress_dense",
    )(proj, proj, pek, pev, w1k, w1v, w2k, w2v)
    n_blk = T // NSA_BLOCK
    return ok.reshape(B, n_blk, LANES), ov.reshape(B, n_blk, LANES)


def _gather_pages(pt_ref, b, pool_ref, buf_ref, sem, n_pages, start):
    def body(p, carry):
        page = pt_ref[b, p]
        cp = pltpu.make_async_copy(pool_ref.at[pl.ds(page * PAGE_SIZE, PAGE_SIZE), :],
                                   buf_ref.at[pl.ds(p * PAGE_SIZE, PAGE_SIZE), :], sem)
        if start:
            cp.start()
        else:
            cp.wait()
        return carry
    lax.fori_loop(0, n_pages, body, 0)


def _compress_pages_t(x_ref, n_pages, pe_ref, w1_ref, w2_ref):
    per_g = []
    for g in range(NSA_KV_HEADS):
        pieces = [x_ref[pl.ds(g * NSA_HEAD_DIM + d, n_pages, stride=PAGE_SIZE), :] for d in range(NSA_HEAD_DIM)]
        flat = jnp.concatenate(pieces, axis=1) + pe_ref[...]
        acc = _mm(flat, w1_ref[...])
        per_g.append(_mm(_silu(acc), w2_ref[...]))
    hd = NSA_HEAD_DIM
    return jnp.concatenate([per_g[0][:, 0:hd], per_g[1][:, 0:hd], per_g[0][:, hd:2 * hd], per_g[1][:, hd:2 * hd]], axis=1)


def _cmp_weights_t(cmp_pe, cmp_w1, cmp_w2):
    out = []
    for i in range(2):
        pe_t = jnp.concatenate([cmp_pe[i].T, cmp_pe[i].T], axis=1).reshape(1, NSA_HEAD_DIM * PAGE_SIZE)
        w1 = jnp.transpose(cmp_w1[i].reshape(NSA_BLOCK, NSA_HEAD_DIM, NSA_CMP_HIDDEN), (1, 0, 2))
        z1 = jnp.zeros_like(w1)
        w1t = jnp.concatenate([jnp.concatenate([w1, z1], axis=2),
                               jnp.concatenate([z1, w1], axis=2)], axis=1).astype(BF16)
        w1t = w1t.reshape(NSA_HEAD_DIM * PAGE_SIZE, 2 * NSA_CMP_HIDDEN)
        w2 = cmp_w2[i]
        z2 = jnp.zeros_like(w2)
        w2bd = jnp.concatenate([jnp.concatenate([w2, z2], axis=1),
                                jnp.concatenate([z2, w2], axis=1)], axis=0).astype(BF16)
        out.append((pe_t, w1t, w2bd))
    return out


def _compress_paged_kernel(pt_ref, poolk_ref, poolv_ref, pek_ref, pev_ref, w1k_ref, w1v_ref,
                           w2k_ref, w2v_ref, ok_ref, ov_ref, bufk, bufv, sems, *, n_pages):
    b = pl.program_id(0)
    _gather_pages(pt_ref, b, poolk_ref, bufk, sems.at[0], n_pages, True)
    _gather_pages(pt_ref, b, poolv_ref, bufv, sems.at[1], n_pages, True)
    _gather_pages(pt_ref, b, poolk_ref, bufk, sems.at[0], n_pages, False)
    ok_ref[0] = _compress_pages_t(bufk, n_pages, pek_ref, w1k_ref, w2k_ref)
    _gather_pages(pt_ref, b, poolv_ref, bufv, sems.at[1], n_pages, False)
    ov_ref[0] = _compress_pages_t(bufv, n_pages, pev_ref, w1v_ref, w2v_ref)


def _compress_paged(page_table, pool_k, pool_v, cw):
    B, n_pages = page_table.shape
    (pek, w1k, w2k), (pev, w1v, w2v) = cw
    full = lambda a: pl.BlockSpec(a.shape, lambda b, pt: (0,) * a.ndim)
    ok, ov = pl.pallas_call(
        functools.partial(_compress_paged_kernel, n_pages=n_pages),
        out_shape=(jax.ShapeDtypeStruct((B, n_pages, 2 * LANES), F32),) * 2,
        grid_spec=pltpu.PrefetchScalarGridSpec(
            num_scalar_prefetch=1,
            grid=(B,),
            in_specs=[
                pl.BlockSpec(memory_space=pl.ANY),
                pl.BlockSpec(memory_space=pl.ANY),
                full(pek), full(pev), full(w1k), full(w1v), full(w2k), full(w2v),
            ],
            out_specs=(pl.BlockSpec((1, n_pages, 2 * LANES), lambda b, pt: (b, 0, 0)),) * 2,
            scratch_shapes=[
                pltpu.VMEM((n_pages * PAGE_SIZE, LANES), F32),
                pltpu.VMEM((n_pages * PAGE_SIZE, LANES), F32),
                pltpu.SemaphoreType.DMA((2,)),
            ],
        ),
        compiler_params=_cparams(("arbitrary",)),
        name="compress_paged",
    )(page_table, pool_k, pool_v, pek, pev, w1k, w1v, w2k, w2v)
    n_blk = n_pages * (PAGE_SIZE // NSA_BLOCK)
    return ok.reshape(B, n_blk, LANES), ov.reshape(B, n_blk, LANES)


def _stack_queries(q, g, tq):
    rows = []
    for hl in range(NSA_HPG):
        qh = q[:, hl * NSA_HEAD_DIM:(hl + 1) * NSA_HEAD_DIM]
        rows.append(jnp.concatenate([qh, qh], axis=1))
    qs = jnp.concatenate(rows, axis=0) * (NSA_HEAD_DIM ** -0.5)
    half = lax.broadcasted_iota(jnp.int32, qs.shape, 1) // NSA_HEAD_DIM
    return jnp.where(half == g, qs, 0.0).astype(BF16)


def _row_slopes(g, tq):
    hl = lax.broadcasted_iota(jnp.int32, (NSA_HPG * tq, 1), 0) // tq
    s = jnp.where(hl == 0, 0.5, jnp.where(hl == 1, 0.25, jnp.where(hl == 2, 0.125, 0.0625)))
    return s * jnp.where(g == 0, 1.0, 0.0625)


def _gate_columns(sm, g, tq):
    sig = _sigmoid(sm)
    lane = lax.broadcasted_iota(jnp.int32, sm.shape, 1)
    cols = []
    for br in range(3):
        per_head = []
        for hl in range(NSA_HPG):
            target = GATE_LANE0 + 3 * (NSA_HPG * g + hl) + br
            per_head.append(jnp.sum(jnp.where(lane == target, sig, 0.0), axis=-1, keepdims=True))
        cols.append(jnp.concatenate(per_head, axis=0))
    return cols


def _topk_select(score, k_sel, n):
    idx = lax.broadcasted_iota(jnp.int32, score.shape, 1)
    rank = jnp.zeros(score.shape, F32)
    for j in range(n):
        col = score[:, j:j + 1]
        beats = (col > score) | ((col >= score) & (idx > j))
        rank = rank + jnp.where(beats, 1.0, 0.0)
    return rank < k_sel


def _topk_select_t(score, k_sel, n):
    st = score.T[0:n]
    idx = lax.broadcasted_iota(jnp.int32, st.shape, 0)
    rank = jnp.zeros(st.shape, F32)
    for j in range(n):
        row = st[j:j + 1, :]
        beats = (row > st) | ((row >= st) & (idx > j))
        rank = rank + jnp.where(beats, 1.0, 0.0)
    sel_t = jnp.where(rank < k_sel, 1.0, 0.0)
    sel_t = jnp.concatenate([sel_t, jnp.zeros((score.shape[1] - n, st.shape[1]), F32)], axis=0)
    return sel_t.T > 0.5


def _unstack_heads(o, g, tq):
    og = jnp.where(g == 0, o[:, 0:NSA_HEAD_DIM], o[:, NSA_HEAD_DIM:2 * NSA_HEAD_DIM])
    return jnp.concatenate([og[hl * tq:(hl + 1) * tq] for hl in range(NSA_HPG)], axis=1)


def _key_features(T, onehot):
    j = lax.broadcasted_iota(jnp.int32, (T, LANES), 0)
    lane = lax.broadcasted_iota(jnp.int32, (T, LANES), 1)
    blk = j // NSA_BLOCK
    f = jnp.where(lane == FEAT_BLK, blk.astype(F32),
                  jnp.where(lane == FEAT_OFF, (j % NSA_BLOCK).astype(F32), 0.0))
    if onehot:
        f = jnp.where(lane == blk, 1.0, f)
    return f.astype(BF16)


def _ones_column(rows):
    lane = lax.broadcasted_iota(jnp.int32, (rows, LANES), 1)
    return jnp.where(lane == 0, 1.0, 0.0).astype(BF16)


def _nsa_prompt_kernel(q_ref, sm_ref, kc_ref, vc_ref, ks_ref, vs_ref, kw_ref, vw_ref, o_ref,
                       ksb, vsb, kwb, vwb, s_sc, m_sc, acc_sc, *, T, TQ, WIN, CH):
    qt = pl.program_id(1)
    R = NSA_HPG * TQ
    R2 = NSA_KV_HEADS * R
    n_blk = T // NSA_BLOCK
    G = range(NSA_KV_HEADS)
    PAD = NSA_WINDOW

    @pl.when(qt == 0)
    def _():
        ksb[:, 0:LANES] = ks_ref[...].astype(BF16)
        ksb[:, LANES:2 * LANES] = _key_features(T, True)
        vsb[:, 0:LANES] = vs_ref[...].astype(BF16)
        vsb[:, LANES:2 * LANES] = _ones_column(T)
        lane = lax.broadcasted_iota(jnp.int32, (PAD, LANES), 1)
        kwb[0:PAD, 0:LANES] = jnp.zeros((PAD, LANES), BF16)
        kwb[0:PAD, LANES:2 * LANES] = jnp.where(lane == FEAT_PAD, 1.0, 0.0).astype(BF16)
        kwb[PAD:PAD + T, 0:LANES] = kw_ref[...].astype(BF16)
        kwb[PAD:PAD + T, LANES:2 * LANES] = _key_features(T, False)
        vwb[0:PAD, :] = jnp.zeros((PAD, 2 * LANES), BF16)
        vwb[PAD:PAD + T, 0:LANES] = vw_ref[...].astype(BF16)
        vwb[PAD:PAD + T, LANES:2 * LANES] = _ones_column(T)

    qs = jnp.concatenate([_stack_queries(q_ref[:, g * 256:(g + 1) * 256], g, TQ) for g in G], axis=0)
    slope = jnp.concatenate([_row_slopes(g, TQ) for g in G], axis=0)
    gates = [_gate_columns(sm_ref[...], g, TQ) for g in G]
    gc, gs, gw = [jnp.concatenate([gates[g][br] for g in G], axis=0) for br in range(3)]
    off_q = lax.broadcasted_iota(jnp.int32, (R2, 1), 0) % TQ
    tq_i = qt * TQ + off_q
    tq_f = tq_i.astype(F32)
    lane = lax.broadcasted_iota(jnp.int32, (R2, LANES), 1)
    feat = jnp.where(lane == FEAT_BLK, slope * NSA_BLOCK,
                     jnp.where(lane == FEAT_OFF, slope, jnp.where(lane == FEAT_PAD, -SEL_BIG, 0.0)))
    q_plain = jnp.concatenate([qs, feat.astype(BF16)], axis=1)

    zpad = jnp.zeros((LANES - n_blk, LANES), F32)
    n_i = lax.broadcasted_iota(jnp.int32, (1, LANES), 1)
    center = (n_i * NSA_BLOCK).astype(F32) + 0.5 * (NSA_BLOCK - 1)
    s_c = _mm_nt(qs, jnp.concatenate([kc_ref[0], zpad], axis=0)) - slope * (tq_f - center)
    p_c = _masked_softmax(s_c, (n_i * NSA_BLOCK + NSA_BLOCK - 1) <= tq_i)
    o_c = _mm(p_c, jnp.concatenate([vc_ref[0], zpad], axis=0))

    scores = []
    for g in G:
        sc = p_c[g * R:g * R + TQ]
        for hl in range(1, NSA_HPG):
            sc = sc + p_c[g * R + hl * TQ:g * R + (hl + 1) * TQ]
        scores.append(sc)
    score = jnp.concatenate(scores, axis=0)
    tq1 = tq_i[0:NSA_KV_HEADS * TQ]
    cur = tq1 // NSA_BLOCK
    forced = (n_i == 0) | (n_i == cur) | (n_i == cur - 1)
    visible = n_i * NSA_BLOCK <= tq1
    score = jnp.where(visible, jnp.where(forced, FORCED_SCORE, score), -1.0)
    sel = _topk_select_t(score, min(NSA_TOP_K, n_blk), n_blk)

    sel_bias = jnp.where(sel & (n_i < cur), 0.0, -SEL_BIG)
    sel_bias = jnp.concatenate([sel_bias[g * TQ:(g + 1) * TQ] for g in G for _ in range(NSA_HPG)], axis=0)
    q_sel = jnp.concatenate([qs, jnp.where(lane >= FEAT_BLK, feat, sel_bias).astype(BF16)], axis=1)
    cur0 = pl.multiple_of(qt * TQ, TQ)
    nt_dims = (((1,), (1,)), ((), ()))
    n_ch = T // CH
    m_sc[...] = jnp.full((R2, LANES), NEG, F32)
    for c in range(n_ch):
        @pl.when(c * CH < cur0)
        def _():
            s = lax.dot_general(q_sel, ksb[c * CH:(c + 1) * CH, :], nt_dims, preferred_element_type=F32)
            s_sc[:, c * CH:(c + 1) * CH] = s
            mm = m_sc[...]
            for i in range(CH // LANES):
                mm = jnp.maximum(mm, s[:, i * LANES:(i + 1) * LANES])
            m_sc[...] = mm
    s_cur = lax.dot_general(q_plain, ksb[pl.ds(cur0, TQ), :], nt_dims, preferred_element_type=F32)
    off_k = lax.broadcasted_iota(jnp.int32, (1, TQ), 1)
    s_cur = jnp.where(off_k <= off_q, s_cur, NEG)
    m = jnp.maximum(jnp.max(m_sc[...], axis=-1, keepdims=True), jnp.max(s_cur, axis=-1, keepdims=True))
    e_cur = jnp.exp(s_cur - m)
    acc_sc[...] = jnp.dot(e_cur.astype(BF16), vsb[pl.ds(cur0, TQ), :], preferred_element_type=F32)
    for c in range(n_ch):
        @pl.when(c * CH < cur0)
        def _():
            e = jnp.exp(s_sc[:, c * CH:(c + 1) * CH] - m)
            acc_sc[...] += jnp.dot(e.astype(BF16), vsb[c * CH:(c + 1) * CH, :], preferred_element_type=F32)
    acc = acc_sc[...]
    o_s = acc[:, 0:LANES] * (1.0 / acc[:, LANES:LANES + 1])

    s_w = lax.dot_general(q_plain, kwb[pl.ds(cur0, WIN), :], nt_dims, preferred_element_type=F32)
    c_first = lax.broadcasted_iota(jnp.int32, (1, LANES), 1)
    c_tail = lax.broadcasted_iota(jnp.int32, (1, WIN - PAD), 1)
    s_w = jnp.concatenate([jnp.where(c_first >= off_q, s_w[:, 0:LANES], NEG),
                           s_w[:, LANES:PAD],
                           jnp.where(c_tail <= off_q, s_w[:, PAD:WIN], NEG)], axis=1)
    e_w = jnp.exp(s_w - jnp.max(s_w, axis=-1, keepdims=True))
    acc_w = jnp.dot(e_w.astype(BF16), vwb[pl.ds(cur0, WIN), :], preferred_element_type=F32)
    o_w = acc_w[:, 0:LANES] * (1.0 / acc_w[:, LANES:LANES + 1])

    o = gc * o_c + gs * o_s + gw * o_w
    o_ref[...] = jnp.concatenate([_unstack_heads(o[g * R:(g + 1) * R], g, TQ) for g in G], axis=1).astype(o_ref.dtype)


def _nsa_prompt(proj, B, T, kc, vc):
    TQ = NSA_BLOCK
    nq = T // TQ
    WIN = NSA_WINDOW + TQ
    CH = min(512, T)
    R2 = NSA_KV_HEADS * NSA_HPG * TQ
    kv = lambda j: pl.BlockSpec((T, LANES), lambda b, t: (b, AB_KV_BLK + j))
    n_blk = T // NSA_BLOCK
    return pl.pallas_call(
        functools.partial(_nsa_prompt_kernel, T=T, TQ=TQ, WIN=WIN, CH=CH),
        out_shape=jax.ShapeDtypeStruct((B * T, NSA_Q_W), BF16),
        grid=(B, nq),
        in_specs=[
            pl.BlockSpec((TQ, NSA_Q_W), lambda b, t: (b * nq + t, 3)),
            pl.BlockSpec((TQ, LANES), lambda b, t: (b * nq + t, AB_SMALL_BLK)),
            pl.BlockSpec((1, n_blk, LANES), lambda b, t: (b, 0, 0)),
            pl.BlockSpec((1, n_blk, LANES), lambda b, t: (b, 0, 0)),
            kv(2), kv(3), kv(4), kv(5),
        ],
        out_specs=pl.BlockSpec((TQ, NSA_Q_W), lambda b, t: (b * nq + t, 0)),
        scratch_shapes=[pltpu.VMEM((T, 2 * LANES), BF16), pltpu.VMEM((T, 2 * LANES), BF16),
                        pltpu.VMEM((T + NSA_WINDOW, 2 * LANES), BF16), pltpu.VMEM((T + NSA_WINDOW, 2 * LANES), BF16),
                        pltpu.VMEM((R2, T), F32), pltpu.VMEM((R2, LANES), F32), pltpu.VMEM((R2, 2 * LANES), F32)],
        compiler_params=_cparams(("arbitrary", "arbitrary")),
        name="nsa_prompt",
    )(proj, proj, kc, vc, proj, proj, proj, proj)


def _nsa_sample_kernel(pt_ref, q_ref, sm_ref, kc_ref, vc_ref, poolk_ref, poolv_ref,
                       kn_ref, vn_ref, wk_ref, wv_ref, kwn_ref, vwn_ref,
                       o_ref, wko_ref, wvo_ref, bufk, bufv, sc_sc, sems, *, n_pages, TQ):
    b = pl.program_id(0)
    past = n_pages * PAGE_SIZE
    n_cmp = past // NSA_BLOCK
    n_sel = n_cmp + 1
    R = NSA_HPG * TQ
    R2 = NSA_KV_HEADS * R
    n_buf = wk_ref.shape[2]
    G = range(NSA_KV_HEADS)

    _gather_pages(pt_ref, b, poolk_ref, bufk, sems.at[0], n_pages, True)
    _gather_pages(pt_ref, b, poolv_ref, bufv, sems.at[1], n_pages, True)

    qs = jnp.concatenate([_stack_queries(q_ref[:, g * 256:(g + 1) * 256], g, TQ) for g in G], axis=0)
    slope = jnp.concatenate([_row_slopes(g, TQ) for g in G], axis=0)
    gates = [_gate_columns(sm_ref[...], g, TQ) for g in G]
    gc, gs, gw = [jnp.concatenate([gates[g][br] for g in G], axis=0) for br in range(3)]
    tq_i = past + lax.broadcasted_iota(jnp.int32, (R2, 1), 0) % TQ
    tq_f = tq_i.astype(F32)
    new_i = past + lax.broadcasted_iota(jnp.int32, (1, TQ), 1)

    n_i = lax.broadcasted_iota(jnp.int32, (1, n_cmp), 1)
    center = (n_i * NSA_BLOCK).astype(F32) + 0.5 * (NSA_BLOCK - 1)
    s_c = _mm_nt(qs, kc_ref[0]) - slope * (tq_f - center)
    p_c = _masked_softmax(s_c, (n_i * NSA_BLOCK + NSA_BLOCK - 1) <= tq_i)
    o_c = _mm(p_c, vc_ref[0])

    bias_rows = []
    for g in G:
        score = p_c[g * R:g * R + TQ]
        for hl in range(1, NSA_HPG):
            score = score + p_c[g * R + hl * TQ:g * R + (hl + 1) * TQ]
        forced = (n_i == 0) | (n_i == n_cmp - 1)
        score = jnp.where(forced, FORCED_SCORE, score)
        sel = _topk_select(score, min(NSA_TOP_K, n_sel) - 1, n_cmp)
        bias_rows += [jnp.where(sel, 0.0, NEG)] * NSA_HPG
    sel_bias = jnp.concatenate(bias_rows, axis=0)

    wk_t = wk_ref[0]
    wv_t = wv_ref[0]
    wpos = past - n_buf + lax.broadcasted_iota(jnp.int32, (1, n_buf), 1)
    d_o = tq_i - wpos
    d_n = tq_i - new_i
    m_o = (wpos >= 0) & (d_o >= 0) & (d_o <= NSA_WINDOW)
    m_n = (d_n >= 0) & (d_n <= NSA_WINDOW)
    s_wo = jnp.where(m_o, _mm(qs, wk_t) - slope * d_o.astype(F32), NEG)
    s_wn = jnp.where(m_n, _mm_nt(qs, kwn_ref[...]) - slope * d_n.astype(F32), NEG)
    mw = jnp.maximum(jnp.max(s_wo, axis=-1, keepdims=True), jnp.max(s_wn, axis=-1, keepdims=True))
    e_o = jnp.where(m_o, jnp.exp(s_wo - mw), 0.0)
    e_w = jnp.where(m_n, jnp.exp(s_wn - mw), 0.0)
    den_w = jnp.maximum(jnp.sum(e_o, axis=-1, keepdims=True) + jnp.sum(e_w, axis=-1, keepdims=True), 1e-30)
    o_w = (_mm_nt(e_o, wv_t) + _mm(e_w, vwn_ref[...])) * (1.0 / den_w)

    lane_w = lax.broadcasted_iota(jnp.int32, (LANES, n_buf), 1)

    def shifted(old_t, new):
        slots = jnp.concatenate([jnp.zeros((LANES - TQ, LANES), F32), new], axis=0)
        tail = jnp.concatenate([jnp.zeros((LANES, n_buf - LANES), F32), slots.T], axis=1)
        return jnp.where(lane_w >= n_buf - TQ, tail, pltpu.roll(old_t, n_buf - TQ, axis=1))

    wko_ref[0] = shifted(wk_t, kwn_ref[...])
    wvo_ref[0] = shifted(wv_t, vwn_ref[...])

    _gather_pages(pt_ref, b, poolk_ref, bufk, sems.at[0], n_pages, False)
    tok = lax.broadcasted_iota(jnp.int32, (1, PAGE_SIZE), 1)
    second = tok >= NSA_BLOCK
    m_run = jnp.full((R2, PAGE_SIZE), NEG, F32)
    for p in range(n_pages):
        s = jnp.dot(qs, bufk[p * PAGE_SIZE:(p + 1) * PAGE_SIZE, :].astype(BF16), preferred_element_type=F32)
        bias = jnp.where(second, sel_bias[:, 2 * p + 1:2 * p + 2], sel_bias[:, 2 * p:2 * p + 1])
        s = s + bias - slope * (tq_f - (p * PAGE_SIZE + tok).astype(F32))
        sc_sc[p] = s
        m_run = jnp.maximum(m_run, s)
    s_n = jnp.where(new_i <= tq_i, _mm_nt(qs, kn_ref[...]) - slope * (tq_i - new_i).astype(F32), NEG)
    m = jnp.maximum(jnp.max(m_run, axis=-1, keepdims=True), jnp.max(s_n, axis=-1, keepdims=True))

    _gather_pages(pt_ref, b, poolv_ref, bufv, sems.at[1], n_pages, False)
    e_n = jnp.exp(s_n - m)
    acc = _mm(e_n, vn_ref[...])
    den_run = jnp.zeros((R2, PAGE_SIZE), F32)
    for p in range(n_pages):
        e = jnp.exp(sc_sc[p] - m)
        den_run = den_run + e
        acc = acc + _mm_nt(e, bufv[p * PAGE_SIZE:(p + 1) * PAGE_SIZE, :])
    den = jnp.sum(den_run, axis=-1, keepdims=True) + jnp.sum(e_n, axis=-1, keepdims=True)
    o_s = acc * (1.0 / den)

    o = gc * o_c + gs * o_s + gw * o_w
    o_ref[...] = jnp.concatenate([_unstack_heads(o[g * R:(g + 1) * R], g, TQ) for g in G], axis=1).astype(o_ref.dtype)


def _nsa_sample(proj, page_table, kc, vc, pool_k, pool_v, win_k, win_v):
    B, n_pages = page_table.shape
    TQ = proj.shape[0] // B
    past = n_pages * PAGE_SIZE
    n_cmp = past // NSA_BLOCK
    n_buf = win_k.shape[2]
    R2 = NSA_KV_HEADS * NSA_HPG * TQ
    kvn = lambda j: pl.BlockSpec((TQ, LANES), lambda b, pt: (b, AB_KV_BLK + j))
    win = pl.BlockSpec((1, LANES, n_buf), lambda b, pt: (b, 0, 0))
    return pl.pallas_call(
        functools.partial(_nsa_sample_kernel, n_pages=n_pages, TQ=TQ),
        out_shape=(jax.ShapeDtypeStruct((B * TQ, NSA_Q_W), BF16),
                   jax.ShapeDtypeStruct((B, LANES, n_buf), F32),
                   jax.ShapeDtypeStruct((B, LANES, n_buf), F32)),
        grid_spec=pltpu.PrefetchScalarGridSpec(
            num_scalar_prefetch=1,
            grid=(B,),
            in_specs=[
                pl.BlockSpec((TQ, NSA_Q_W), lambda b, pt: (b, 3)),
                pl.BlockSpec((TQ, LANES), lambda b, pt: (b, AB_SMALL_BLK)),
                pl.BlockSpec((1, n_cmp, LANES), lambda b, pt: (b, 0, 0)),
                pl.BlockSpec((1, n_cmp, LANES), lambda b, pt: (b, 0, 0)),
                pl.BlockSpec(memory_space=pl.ANY),
                pl.BlockSpec(memory_space=pl.ANY),
                kvn(2), kvn(3), win, win, kvn(4), kvn(5),
            ],
            out_specs=(pl.BlockSpec((TQ, NSA_Q_W), lambda b, pt: (b, 0)), win, win),
            scratch_shapes=[
                pltpu.VMEM((past, LANES), F32),
                pltpu.VMEM((past, LANES), F32),
                pltpu.VMEM((n_pages, R2, PAGE_SIZE), F32),
                pltpu.SemaphoreType.DMA((2,)),
            ],
        ),
        compiler_params=_cparams(("arbitrary",)),
        name="nsa_sample",
    )(page_table, proj, proj, kc, vc, pool_k, pool_v, proj, proj, win_k, win_v, proj, proj)


def _gdn_kernel(qkv_ref, z_ref, sm_ref, cw_ref, alog_ref, dtb_ref, ng_ref, s0_ref, cb_ref,
                o_ref, sfin_ref, S_sc, prev_sc, *, C):
    t = pl.program_id(1)

    @pl.when(t == 0)
    def _():
        S_sc[...] = s0_ref[0]
        prev_sc[...] = cb_ref[0]

    def conv(c0):
        xe = jnp.concatenate([prev_sc[:, c0:c0 + LANES], qkv_ref[:, c0:c0 + LANES]], axis=0)
        y = xe[SUBLANES:] * cw_ref[GDN_CONV - 1:GDN_CONV, c0:c0 + LANES]
        for s in range(1, GDN_CONV):
            y = y + pltpu.roll(xe, s, axis=0)[SUBLANES:] * cw_ref[GDN_CONV - 1 - s:GDN_CONV - s, c0:c0 + LANES]
        return _silu(y)

    sm = sm_ref[...]
    beta = _sigmoid(sm)
    gt = -jnp.exp(alog_ref[...]) * _softplus(sm + dtb_ref[...])
    d = _cumsum_rows(gt)
    dT = d.T
    ii = lax.broadcasted_iota(jnp.int32, (C, C), 0)
    jj = lax.broadcasted_iota(jnp.int32, (C, C), 1)
    eye = (ii == jj).astype(F32)

    H = range(GDN_HEADS)
    q, k, v = [], [], []
    for h in H:
        qh = conv(h * GDN_DK)
        kh = conv(GDN_W + h * GDN_DK)
        q.append(qh * lax.rsqrt(jnp.sum(qh * qh, axis=-1, keepdims=True) + L2_EPS) * (GDN_DK ** -0.5))
        k.append(kh * lax.rsqrt(jnp.sum(kh * kh, axis=-1, keepdims=True) + L2_EPS))
        v.append(conv(2 * GDN_W + h * GDN_DV))
    beta_c = [beta[:, h:h + 1] for h in H]
    d_c = [d[:, GDN_A_LANE0 + h:GDN_A_LANE0 + h + 1] for h in H]
    decay = [jnp.exp(jnp.minimum(d_c[h] - dT[GDN_A_LANE0 + h:GDN_A_LANE0 + h + 1, :], 0.0)) for h in H]
    kb = [k[h] * beta_c[h] for h in H]
    g_kk = [_mm_nt(kb[h], k[h]) for h in H]
    g_qk = [_mm_nt(q[h], k[h]) for h in H]
    a = [jnp.where(ii > jj, g_kk[h] * decay[h], 0.0) for h in H]
    qk = [jnp.where(ii >= jj, g_qk[h] * decay[h], 0.0) for h in H]
    tinv = [eye - a[h] for h in H]
    p_split = [_split_bf16(a[h]) for h in H]
    n = 2
    while n < C:
        p = [_mm3(p_split[h], p_split[h]) for h in H]
        p_split = [_split_bf16(p[h]) for h in H]
        tinv = [tinv[h] + _mm3(_split_bf16(tinv[h]), p_split[h]) for h in H]
        n *= 2
    u = [_mm(tinv[h], v[h] * beta_c[h]) for h in H]
    w = [_mm(tinv[h], kb[h] * jnp.exp(d_c[h])) for h in H]
    S = [S_sc[h] for h in H]
    w_s = [_mm(w[h], S[h]) for h in H]
    q_s = [_mm(q[h] * jnp.exp(d_c[h]), S[h]) for h in H]
    v_new = [u[h] - w_s[h] for h in H]
    o = [q_s[h] + _mm(qk[h], v_new[h]) for h in H]
    d_last = [d_c[h][C - 1:C, :] for h in H]
    upd = [_mm_tn(k[h] * jnp.exp(d_last[h] - d_c[h]), v_new[h]) for h in H]
    for h in H:
        S_sc[h] = S[h] * jnp.exp(d_last[h]) + upd[h]
        ms = jnp.mean(o[h] * o[h], axis=-1, keepdims=True)
        zh = z_ref[:, h * GDN_DV:(h + 1) * GDN_DV]
        o_ref[:, h * GDN_DV:(h + 1) * GDN_DV] = (
            o[h] * lax.rsqrt(ms + RMS_EPS) * ng_ref[...] * _silu(zh)).astype(o_ref.dtype)

    prev_sc[...] = qkv_ref[C - SUBLANES:C, :]

    @pl.when(t == pl.num_programs(1) - 1)
    def _():
        sfin_ref[0] = S_sc[...]


def _gdn(proj, B, T, conv_w, a_log, dt_bias, norm_g, s0, conv_buf8, C):
    nt = T // C
    pad = lambda v: jnp.zeros((1, LANES), F32).at[0, GDN_A_LANE0:GDN_A_LANE0 + GDN_HEADS].set(v)
    return pl.pallas_call(
        functools.partial(_gdn_kernel, C=C),
        out_shape=(jax.ShapeDtypeStruct((B * T, GDN_W), BF16),
                   jax.ShapeDtypeStruct((B, GDN_HEADS, GDN_DK, GDN_DV), F32)),
        grid=(B, nt),
        in_specs=[
            pl.BlockSpec((C, GDN_CONV_CH), lambda b, t: (b * nt + t, 0)),
            pl.BlockSpec((C, GDN_W), lambda b, t: (b * nt + t, 3)),
            pl.BlockSpec((C, LANES), lambda b, t: (b * nt + t, C_SMALL_BLK)),
            pl.BlockSpec((GDN_CONV, GDN_CONV_CH), lambda b, t: (0, 0)),
            pl.BlockSpec((1, LANES), lambda b, t: (0, 0)),
            pl.BlockSpec((1, LANES), lambda b, t: (0, 0)),
            pl.BlockSpec((1, GDN_DV), lambda b, t: (0, 0)),
            pl.BlockSpec((1, GDN_HEADS, GDN_DK, GDN_DV), lambda b, t: (b, 0, 0, 0)),
            pl.BlockSpec((1, SUBLANES, GDN_CONV_CH), lambda b, t: (b, 0, 0)),
        ],
        out_specs=(pl.BlockSpec((C, GDN_W), lambda b, t: (b * nt + t, 0)),
                   pl.BlockSpec((1, GDN_HEADS, GDN_DK, GDN_DV), lambda b, t: (b, 0, 0, 0))),
        scratch_shapes=[
            pltpu.VMEM((GDN_HEADS, GDN_DK, GDN_DV), F32),
            pltpu.VMEM((SUBLANES, GDN_CONV_CH), F32),
        ],
        compiler_params=_cparams(("arbitrary", "arbitrary")),
        name="gdn",
    )(proj, proj, proj, conv_w, pad(a_log), pad(dt_bias), norm_g.reshape(1, GDN_DV), s0, conv_buf8)


def _ab_in_weight(w):
    big = w[:, :GLA_KEY_W * 2 + GLA_VAL_W * 2]
    gk = w[:, 1536:1536 + GLA_LOWRANK]
    rest = w[:, 1536 + GLA_LOWRANK:]
    q_b = rest[:, :NSA_Q_W]
    kv = rest[:, NSA_Q_W:NSA_Q_W + 6 * NSA_KV_W]
    gate = rest[:, NSA_Q_W + 6 * NSA_KV_W:]
    small = jnp.concatenate([gk, gate, jnp.zeros((D_MODEL, LANES - GLA_LOWRANK - 3 * NSA_HEADS), w.dtype)], axis=1)
    return jnp.concatenate([big, q_b, kv, small], axis=1).astype(BF16)


def _c_in_weight(w):
    qkv = w[:, :GDN_CONV_CH]
    ba = w[:, GDN_CONV_CH:GDN_CONV_CH + 2 * GDN_HEADS]
    z = w[:, GDN_CONV_CH + 2 * GDN_HEADS:]
    small = jnp.concatenate([ba, jnp.zeros((D_MODEL, LANES - 2 * GDN_HEADS), w.dtype)], axis=1)
    return jnp.concatenate([qkv, z, small], axis=1).astype(BF16)


def _kv_out(proj, B, T, j):
    return proj[:, (AB_KV_BLK + j) * LANES:(AB_KV_BLK + j + 1) * LANES].reshape(B, T, NSA_KV_HEADS, NSA_HEAD_DIM)


PROMPT_ROWS = 512
FFN_ROWS = 1024


def kernel(x_prompt, x_sample, c_prompt, c_sample, page_table, cache_cmp_k, cache_cmp_v, cache_sel_k, cache_sel_v, state_win_k, state_win_v, state_gla, state_gdn, state_gdn_conv, w_ada, b_ada, ln_g, ln_b, w_ffn_in, w_ffn_out, ab_w_in, ab_w_gk2, ab_b_gk, ab_gla_norm, ab_cmp_pe, ab_cmp_w1, ab_cmp_w2, ab_w_out, c_w_in, c_conv_w, c_a_log, c_dt_bias, c_norm, c_w_out):
    Bp, Tp, _ = x_prompt.shape
    Bs, Ts, _ = x_sample.shape
    n_pool = cache_cmp_k.shape[1]

    mods = _adaln(jnp.concatenate([c_prompt, c_sample], axis=0), w_ada, b_ada)

    def layer_mods(layer):
        m = mods[layer]
        parts = [m[:, i * D_MODEL:(i + 1) * D_MODEL] for i in range(6)]
        return [p[:Bp] for p in parts], [p[Bp:] for p in parts]

    xp, xs = x_prompt, x_sample
    ab_p, ab_s, c_p, c_s = [], [], [], []
    for layer in range(DEPTH):
        mp, ms = layer_mods(layer)
        wf_in = w_ffn_in[layer].astype(BF16)
        wf_out = w_ffn_out[layer].astype(BF16)
        i = layer // 2
        if layer % 2 == 0:
            w_in = _ab_in_weight(ab_w_in[i])
            w_out = ab_w_out[i].astype(BF16)
            wo_a, wo_b = w_out[:GLA_VAL_W], w_out[GLA_VAL_W:]
            cw = _cmp_weights(ab_cmp_pe[i], ab_cmp_w1[i], ab_cmp_w2[i])
            cw_t = _cmp_weights_t(ab_cmp_pe[i], ab_cmp_w1[i], ab_cmp_w2[i])

            proj = _modmm(xp, mp[0], mp[1], w_in, PROMPT_ROWS)
            zero_state = jnp.zeros((Bp, GLA_HEADS, GLA_DK, GLA_DV), F32)
            o_a, s_a = _gla(proj, Bp, Tp, ab_w_gk2[i], ab_b_gk[i], ab_gla_norm[i], zero_state, min(64, Tp))
            kc, vc = _compress_dense(proj, Bp, Tp, cw)
            o_b = _nsa_prompt(proj, Bp, Tp, kc, vc)
            x1 = _outproj_ln([o_a, o_b], [wo_a, wo_b], xp, mp[2], ln_g[layer, 0], ln_b[layer, 0], PROMPT_ROWS)
            n_keep = min(NSA_WINDOW, Tp)
            ab_p.append(tuple(_kv_out(proj, Bp, Tp, j) for j in range(4))
                        + (_kv_out(proj, Bp, Tp, 4)[:, Tp - n_keep:], _kv_out(proj, Bp, Tp, 5)[:, Tp - n_keep:], s_a))
            xp = _ffn_ln(x1, mp[3], mp[4], mp[5], wf_in, wf_out, ln_g[layer, 1], ln_b[layer, 1], FFN_ROWS)

            proj = _modmm(xs, ms[0], ms[1], w_in, PROMPT_ROWS)
            o_a, s_a = _gla(proj, Bs, Ts, ab_w_gk2[i], ab_b_gk[i], ab_gla_norm[i], state_gla[i], min(64, Ts))
            pool = lambda c: jnp.transpose(c[i], (0, 2, 3, 1)).reshape(n_pool * PAGE_SIZE, LANES)
            n_buf = state_win_k.shape[2]
            win_t = lambda w: jnp.transpose(w[i], (0, 2, 3, 1)).reshape(Bs, LANES, n_buf)
            kc, vc = _compress_paged(page_table, pool(cache_cmp_k), pool(cache_cmp_v), cw_t)
            o_b, win_k, win_v = _nsa_sample(proj, page_table, kc, vc, pool(cache_sel_k), pool(cache_sel_v),
                                            win_t(state_win_k), win_t(state_win_v))
            win_out = lambda w: jnp.transpose(w.reshape(Bs, NSA_KV_HEADS, NSA_HEAD_DIM, n_buf), (0, 3, 1, 2))
            x1 = _outproj_ln([o_a, o_b], [wo_a, wo_b], xs, ms[2], ln_g[layer, 0], ln_b[layer, 0], PROMPT_ROWS)
            ab_s.append(tuple(_kv_out(proj, Bs, Ts, j) for j in range(4))
                        + (win_out(win_k), win_out(win_v), s_a))
            xs = _ffn_ln(x1, ms[3], ms[4], ms[5], wf_in, wf_out, ln_g[layer, 1], ln_b[layer, 1], FFN_ROWS)
        else:
            w_in = _c_in_weight(c_w_in[i])
            w_out = c_w_out[i].astype(BF16)
            keep = GDN_CONV - 1

            proj = _modmm(xp, mp[0], mp[1], w_in, PROMPT_ROWS // 2)
            o_c, s_c = _gdn(proj, Bp, Tp, c_conv_w[i], c_a_log[i], c_dt_bias[i], c_norm[i],
                            jnp.zeros((Bp, GDN_HEADS, GDN_DK, GDN_DV), F32),
                            jnp.zeros((Bp, SUBLANES, GDN_CONV_CH), F32), min(64, Tp))
            x1 = _outproj_ln([o_c], [w_out], xp, mp[2], ln_g[layer, 0], ln_b[layer, 0], PROMPT_ROWS)
            c_p.append((s_c, proj.reshape(Bp, Tp, C_COLS)[:, Tp - keep:, :GDN_CONV_CH]))
            xp = _ffn_ln(x1, mp[3], mp[4], mp[5], wf_in, wf_out, ln_g[layer, 1], ln_b[layer, 1], FFN_ROWS)

            proj = _modmm(xs, ms[0], ms[1], w_in, PROMPT_ROWS // 2)
            conv8 = jnp.concatenate([jnp.zeros((Bs, SUBLANES - keep, GDN_CONV_CH), F32), state_gdn_conv[i]], axis=1)
            o_c, s_c = _gdn(proj, Bs, Ts, c_conv_w[i], c_a_log[i], c_dt_bias[i], c_norm[i],
                            state_gdn[i], conv8, min(64, Ts))
            x1 = _outproj_ln([o_c], [w_out], xs, ms[2], ln_g[layer, 0], ln_b[layer, 0], PROMPT_ROWS)
            qkv_s = proj.reshape(Bs, Ts, C_COLS)[:, :, :GDN_CONV_CH]
            c_s.append((s_c, jnp.concatenate([state_gdn_conv[i], qkv_s], axis=1)[:, -keep:]))
            xs = _ffn_ln(x1, ms[3], ms[4], ms[5], wf_in, wf_out, ln_g[layer, 1], ln_b[layer, 1], FFN_ROWS)

    stack = lambda sts: [jnp.stack(z) for z in zip(*sts)]
    p_ab, s_ab = stack(ab_p), stack(ab_s)
    p_c, s_c = stack(c_p), stack(c_s)
    return (xp, xs, *p_ab, *p_c, *s_ab, *s_c)
```

```python
import functools

import jax
import jax.numpy as jnp
from jax import lax
from jax.experimental import pallas as pl
from jax.experimental.pallas import tpu as pltpu

F32 = jnp.float32
BF16 = jnp.bfloat16

D_MODEL = 1024
DEPTH = 2
PAGE_SIZE = 128
GLA_HEADS = 4
GLA_DK = 64
GLA_DV = 128
GLA_LOWRANK = 16
GLA_GATE_NORM = 16.0
NSA_HEAD_DIM = 64
NSA_HEADS = 8
NSA_KV_HEADS = 2
NSA_HPG = 4
NSA_BLOCK = 64
NSA_TOP_K = 16
NSA_WINDOW = 512
NSA_CMP_HIDDEN = 128
FORCED_SCORE = 1000.0
GDN_HEADS = 8
GDN_DK = 128
GDN_DV = 128
GDN_CONV = 4
FF_HIDDEN = 2816
DEEPNORM_ALPHA = (2.0 * DEPTH) ** 0.25
LN_EPS = 1e-5
RMS_EPS = 1e-6
L2_EPS = 1e-6
NEG = -1e30

GLA_KEY_W = GLA_HEADS * GLA_DK
GLA_VAL_W = GLA_HEADS * GLA_DV
NSA_Q_W = NSA_HEADS * NSA_HEAD_DIM
NSA_KV_W = NSA_KV_HEADS * NSA_HEAD_DIM
GDN_W = GDN_HEADS * GDN_DK
GDN_CONV_CH = 3 * GDN_W

LANES = 128
SUBLANES = 8
VMEM_LIMIT = 56 * 1024 * 1024

AB_COLS = 2944
AB_SMALL_BLK = 22
AB_KV_BLK = 16
GATE_LANE0 = GLA_LOWRANK
C_COLS = 4224
C_SMALL_BLK = 32
GDN_A_LANE0 = GDN_HEADS
FEAT_BLK = 64
FEAT_OFF = 65
FEAT_PAD = 66
SEL_BIG = 131072.0


def _cparams(sem):
    return pltpu.CompilerParams(dimension_semantics=sem, vmem_limit_bytes=VMEM_LIMIT)


def _silu(x):
    return x * (1.0 / (1.0 + jnp.exp(-x)))


def _sigmoid(x):
    return 1.0 / (1.0 + jnp.exp(-x))


def _softplus(x):
    return jnp.maximum(x, 0.0) + jnp.log(1.0 + jnp.exp(-jnp.abs(x)))


def _mm(a, b):
    return jnp.dot(a.astype(BF16), b.astype(BF16), preferred_element_type=F32)


def _split_bf16(x):
    hi = x.astype(BF16)
    return hi, (x - hi.astype(F32)).astype(BF16)


def _mm3(a, b):
    dot = lambda x, y: jnp.dot(x, y, preferred_element_type=F32)
    return dot(a[0], b[0]) + (dot(a[1], b[0]) + dot(a[0], b[1]))


def _mm_nt(a, b):
    return lax.dot_general(a.astype(BF16), b.astype(BF16), (((1,), (1,)), ((), ())),
                           preferred_element_type=F32)


def _mm_tn(a, b):
    return lax.dot_general(a.astype(BF16), b.astype(BF16), (((0,), (0,)), ((), ())),
                           preferred_element_type=F32)


def _cumsum_rows(x):
    n = x.shape[0]
    row = lax.broadcasted_iota(jnp.int32, x.shape, 0)
    s = 1
    while s < n:
        x = x + jnp.where(row >= s, pltpu.roll(x, s, axis=0), 0.0)
        s *= 2
    return x


def _masked_softmax(s, mask):
    s = jnp.where(mask, s, NEG)
    m = jnp.max(s, axis=-1, keepdims=True)
    e = jnp.where(mask, jnp.exp(s - m), 0.0)
    den = jnp.maximum(jnp.sum(e, axis=-1, keepdims=True), 1e-30)
    return e * (1.0 / den)


def _layernorm(z, g, b):
    mu = jnp.mean(z, axis=-1, keepdims=True)
    zc = z - mu
    var = jnp.mean(zc * zc, axis=-1, keepdims=True)
    return zc * lax.rsqrt(var + LN_EPS) * g + b


def _adaln_kernel(c_ref, w_ref, b_ref, o_ref):
    c = _silu(c_ref[...])
    o_ref[0] = _mm(c, w_ref[0]) + b_ref[0]


def _adaln(c_all, w_ada, b_ada):
    n = c_all.shape[0]
    tn = 1536
    nt = (6 * D_MODEL) // tn
    return pl.pallas_call(
        _adaln_kernel,
        out_shape=jax.ShapeDtypeStruct((DEPTH, n, 6 * D_MODEL), F32),
        grid=(DEPTH, nt),
        in_specs=[
            pl.BlockSpec((n, D_MODEL), lambda l, j: (0, 0)),
            pl.BlockSpec((1, D_MODEL, tn), lambda l, j: (l, 0, j)),
            pl.BlockSpec((1, 1, tn), lambda l, j: (l, 0, j)),
        ],
        out_specs=pl.BlockSpec((1, n, tn), lambda l, j: (l, 0, j)),
        compiler_params=_cparams(("arbitrary", "arbitrary")),
        name="adaln",
    )(c_all, w_ada, b_ada.reshape(DEPTH, 1, 6 * D_MODEL))


def _modmm_kernel(x_ref, sh_ref, sc_ref, w_ref, o_ref):
    bb, tt, d = x_ref.shape
    h = x_ref[...] * (1.0 + sc_ref[...]) + sh_ref[...]
    o_ref[...] = _mm(h.reshape(bb * tt, d), w_ref[...])


def _row_tiling(B, T, max_rows):
    if T >= max_rows:
        return 1, max_rows
    bb = min(B, max_rows // T)
    return bb, T


def _modmm(x, shift, scale, w_bf16, max_rows):
    B, T, D = x.shape
    N = w_bf16.shape[1]
    bb, tt = _row_tiling(B, T, max_rows)
    nt = T // tt
    return pl.pallas_call(
        _modmm_kernel,
        out_shape=jax.ShapeDtypeStruct((B * T, N), F32),
        grid=(B // bb, nt),
        in_specs=[
            pl.BlockSpec((bb, tt, D), lambda i, j: (i, j, 0)),
            pl.BlockSpec((bb, 1, D), lambda i, j: (i, 0, 0)),
            pl.BlockSpec((bb, 1, D), lambda i, j: (i, 0, 0)),
            pl.BlockSpec((D, N), lambda i, j: (0, 0)),
        ],
        out_specs=pl.BlockSpec((bb * tt, N), lambda i, j: (i * nt + j, 0)),
        compiler_params=_cparams(("arbitrary", "arbitrary")),
        name="modmm",
    )(x, shift[:, None, :], scale[:, None, :], w_bf16)


def _outproj_kernel(*refs, n_in):
    a_refs = refs[:n_in]
    w_refs = refs[n_in:2 * n_in]
    x_ref, gate_ref, g_ref, b_ref, o_ref = refs[2 * n_in:]
    bb, tt, d = x_ref.shape
    acc = _mm(a_refs[0][...], w_refs[0][...])
    for a_ref, w_ref in zip(a_refs[1:], w_refs[1:]):
        acc = acc + _mm(a_ref[...], w_ref[...])
    z = DEEPNORM_ALPHA * x_ref[...] + gate_ref[...] * acc.reshape(bb, tt, d)
    o_ref[...] = _layernorm(z, g_ref[...], b_ref[...])


def _outproj_ln(acts, ws, x, gate, ln_g, ln_b, max_rows):
    B, T, D = x.shape
    bb, tt = _row_tiling(B, T, max_rows)
    nt = T // tt
    n_in = len(acts)
    in_specs = []
    for a in acts:
        in_specs.append(pl.BlockSpec((bb * tt, a.shape[1]), lambda i, j: (i * nt + j, 0)))
    for w in ws:
        in_specs.append(pl.BlockSpec(w.shape, lambda i, j: (0, 0)))
    in_specs += [
        pl.BlockSpec((bb, tt, D), lambda i, j: (i, j, 0)),
        pl.BlockSpec((bb, 1, D), lambda i, j: (i, 0, 0)),
        pl.BlockSpec((1, 1, D), lambda i, j: (0, 0, 0)),
        pl.BlockSpec((1, 1, D), lambda i, j: (0, 0, 0)),
    ]
    return pl.pallas_call(
        functools.partial(_outproj_kernel, n_in=n_in),
        out_shape=jax.ShapeDtypeStruct((B, T, D), F32),
        grid=(B // bb, nt),
        in_specs=in_specs,
        out_specs=pl.BlockSpec((bb, tt, D), lambda i, j: (i, j, 0)),
        compiler_params=_cparams(("arbitrary", "arbitrary")),
        name="outproj_ln",
    )(*acts, *ws, x, gate[:, None, :], ln_g.reshape(1, 1, D), ln_b.reshape(1, 1, D))


def _ffn_kernel(x_ref, sh_ref, sc_ref, gate_ref, wa_ref, wu_ref, wo_ref, g_ref, b_ref, o_ref,
                xm_sc, acc_sc):
    j = pl.program_id(2)
    bb, tt, d = x_ref.shape

    @pl.when(j == 0)
    def _():
        h = x_ref[...] * (1.0 + sc_ref[...]) + sh_ref[...]
        xm_sc[...] = h.reshape(bb * tt, d).astype(BF16)
        acc_sc[...] = jnp.zeros_like(acc_sc)

    xm = xm_sc[...]
    a = jnp.dot(xm, wa_ref[...], preferred_element_type=F32)
    u = jnp.dot(xm, wu_ref[...], preferred_element_type=F32)
    acc_sc[...] += _mm(_silu(a) * u, wo_ref[...])

    @pl.when(j == pl.num_programs(2) - 1)
    def _():
        z = DEEPNORM_ALPHA * x_ref[...] + gate_ref[...] * acc_sc[...].reshape(bb, tt, d)
        o_ref[...] = _layernorm(z, g_ref[...], b_ref[...])


def _ffn_ln(x, shift, scale, gate, w_in_bf16, w_out_bf16, ln_g, ln_b, max_rows):
    B, T, D = x.shape
    bb, tt = _row_tiling(B, T, max_rows)
    nt = T // tt
    th = 256
    nh = FF_HIDDEN // th
    vec = lambda v: v[:, None, :]
    return pl.pallas_call(
        _ffn_kernel,
        out_shape=jax.ShapeDtypeStruct((B, T, D), F32),
        grid=(B // bb, nt, nh),
        in_specs=[
            pl.BlockSpec((bb, tt, D), lambda i, t, j: (i, t, 0)),
            pl.BlockSpec((bb, 1, D), lambda i, t, j: (i, 0, 0)),
            pl.BlockSpec((bb, 1, D), lambda i, t, j: (i, 0, 0)),
            pl.BlockSpec((bb, 1, D), lambda i, t, j: (i, 0, 0)),
            pl.BlockSpec((D, th), lambda i, t, j: (0, j)),
            pl.BlockSpec((D, th), lambda i, t, j: (0, nh + j)),
            pl.BlockSpec((th, D), lambda i, t, j: (j, 0)),
            pl.BlockSpec((1, 1, D), lambda i, t, j: (0, 0, 0)),
            pl.BlockSpec((1, 1, D), lambda i, t, j: (0, 0, 0)),
        ],
        out_specs=pl.BlockSpec((bb, tt, D), lambda i, t, j: (i, t, 0)),
        scratch_shapes=[pltpu.VMEM((bb * tt, D), BF16), pltpu.VMEM((bb * tt, D), F32)],
        compiler_params=_cparams(("arbitrary", "arbitrary", "arbitrary")),
        name="ffn_ln",
    )(x, vec(shift), vec(scale), vec(gate), w_in_bf16, w_in_bf16, w_out_bf16,
      ln_g.reshape(1, 1, D), ln_b.reshape(1, 1, D))


def _gla_kernel(q_ref, k_ref, v_ref, r_ref, sm_ref, wgk_ref, bgk_ref, gn_ref, s0_ref,
                o_ref, sfin_ref, S_sc, q_sc, k_sc, b_sc, v_sc, o_sc, *, C):
    t = pl.program_id(1)
    KW, VW = GLA_KEY_W, GLA_VAL_W

    hk = lax.broadcasted_iota(jnp.int32, (KW, VW), 0) // GLA_DK
    hv = lax.broadcasted_iota(jnp.int32, (KW, VW), 1) // GLA_DV
    same_head = hk == hv

    @pl.when(t == 0)
    def _():
        rows = []
        for h in range(GLA_HEADS):
            pieces = [s0_ref[0, h] if h2 == h else jnp.zeros((GLA_DK, GLA_DV), F32)
                      for h2 in range(GLA_HEADS)]
            rows.append(jnp.concatenate(pieces, axis=1))
        S_sc[...] = jnp.concatenate(rows, axis=0)

    gk = sm_ref[:, 0:GLA_LOWRANK]
    pre = _mm(gk, wgk_ref[...]) + bgk_ref[...]
    log_a = (jnp.minimum(pre, 0.0) - jnp.log(1.0 + jnp.exp(-jnp.abs(pre)))) * (1.0 / GLA_GATE_NORM)
    b = _cumsum_rows(log_a)
    q = q_ref[...] * (GLA_DK ** -0.5)
    k = k_ref[...]
    v = v_ref[...]
    q_sc[...] = q
    k_sc[...] = k
    b_sc[...] = b
    v_sc[...] = v

    S = S_sc[...]
    o_sc[...] = _mm(q * jnp.exp(b), S)

    seg = same_head.astype(BF16)
    JG = min(C, 2 * SUBLANES)
    for r0 in range(0, C, JG):
        rows = C - r0
        qg = q_sc[r0:C, :]
        bg = b_sc[r0:C, :]
        local = lax.broadcasted_iota(jnp.int32, (JG, KW), 0)
        ps = []
        for jj in range(JG):
            j = r0 + jj
            p = qg * k_sc[j:j + 1, :] * jnp.exp(jnp.minimum(bg - b_sc[j:j + 1, :], 0.0))
            head = jnp.where(local >= jj, p[0:JG], 0.0)
            p = head if rows == JG else jnp.concatenate([head, p[JG:]], axis=0)
            ps.append(p.astype(BF16))
        s = jnp.dot(jnp.concatenate(ps, axis=0), seg, preferred_element_type=F32)
        contrib = s[0:rows] * v_sc[r0:r0 + 1, :]
        for jj in range(1, JG):
            contrib = contrib + s[jj * rows:(jj + 1) * rows] * v_sc[r0 + jj:r0 + jj + 1, :]
        o_sc[r0:C, :] += contrib
    o = o_sc[...]

    b_last = b[C - 1:C, :]
    kd = k * jnp.exp(b_last - b)
    upd = _mm_tn(kd, v)
    tail = jnp.broadcast_to(b_last, (SUBLANES, KW))
    dcol = jnp.exp(tail.T[:, 0:1])
    S_new = S * dcol + jnp.where(same_head, upd, 0.0)
    S_sc[...] = S_new

    outs = []
    for h in range(GLA_HEADS):
        oh = o[:, h * GLA_DV:(h + 1) * GLA_DV]
        ms = jnp.mean(oh * oh, axis=-1, keepdims=True)
        rh = r_ref[:, h * GLA_DV:(h + 1) * GLA_DV]
        outs.append(oh * lax.rsqrt(ms + RMS_EPS) * gn_ref[...] * _silu(rh))
    o_ref[...] = jnp.concatenate(outs, axis=1).astype(o_ref.dtype)

    @pl.when(t == pl.num_programs(1) - 1)
    def _():
        for h in range(GLA_HEADS):
            sfin_ref[0, h] = S_new[h * GLA_DK:(h + 1) * GLA_DK, h * GLA_DV:(h + 1) * GLA_DV]


def _gla(proj, B, T, w_gk2, b_gk, gla_norm, s0, C):
    nt = T // C
    return pl.pallas_call(
        functools.partial(_gla_kernel, C=C),
        out_shape=(jax.ShapeDtypeStruct((B * T, GLA_VAL_W), BF16),
                   jax.ShapeDtypeStruct((B, GLA_HEADS, GLA_DK, GLA_DV), F32)),
        grid=(B, nt),
        in_specs=[
            pl.BlockSpec((C, GLA_KEY_W), lambda b, t: (b * nt + t, 0)),
            pl.BlockSpec((C, GLA_KEY_W), lambda b, t: (b * nt + t, 1)),
            pl.BlockSpec((C, GLA_VAL_W), lambda b, t: (b * nt + t, 1)),
            pl.BlockSpec((C, GLA_VAL_W), lambda b, t: (b * nt + t, 2)),
            pl.BlockSpec((C, LANES), lambda b, t: (b * nt + t, AB_SMALL_BLK)),
            pl.BlockSpec((GLA_LOWRANK, GLA_KEY_W), lambda b, t: (0, 0)),
            pl.BlockSpec((1, GLA_KEY_W), lambda b, t: (0, 0)),
            pl.BlockSpec((1, GLA_DV), lambda b, t: (0, 0)),
            pl.BlockSpec((1, GLA_HEADS, GLA_DK, GLA_DV), lambda b, t: (b, 0, 0, 0)),
        ],
        out_specs=(pl.BlockSpec((C, GLA_VAL_W), lambda b, t: (b * nt + t, 0)),
                   pl.BlockSpec((1, GLA_HEADS, GLA_DK, GLA_DV), lambda b, t: (b, 0, 0, 0))),
        scratch_shapes=[
            pltpu.VMEM((GLA_KEY_W, GLA_VAL_W), F32),
            pltpu.VMEM((C, GLA_KEY_W), F32),
            pltpu.VMEM((C, GLA_KEY_W), F32),
            pltpu.VMEM((C, GLA_KEY_W), F32),
            pltpu.VMEM((C, GLA_VAL_W), F32),
            pltpu.VMEM((C, GLA_VAL_W), F32),
        ],
        compiler_params=_cparams(("arbitrary", "arbitrary")),
        name="gla",
    )(proj, proj, proj, proj, proj, w_gk2, b_gk.reshape(1, GLA_KEY_W), gla_norm.reshape(1, GLA_DV), s0)


def _compress_pages(x_ref, n_pages, pe_ref, w1_ref, w2_ref):
    outs = []
    for half in range(PAGE_SIZE // NSA_BLOCK):
        pieces = [x_ref[pl.ds(half * NSA_BLOCK + tk, n_pages, stride=PAGE_SIZE), :] for tk in range(NSA_BLOCK)]
        flat = jnp.concatenate(pieces, axis=1) + pe_ref[...]
        acc = _mm(flat, w1_ref[...])
        outs.append(_mm(_silu(acc), w2_ref[...]))
    return jnp.concatenate(outs, axis=1)


def _compress_dense_kernel(xk_ref, xv_ref, pek_ref, pev_ref, w1k_ref, w1v_ref, w2k_ref, w2v_ref,
                           ok_ref, ov_ref, *, n_pages):
    ok_ref[0] = _compress_pages(xk_ref, n_pages, pek_ref, w1k_ref, w2k_ref)
    ov_ref[0] = _compress_pages(xv_ref, n_pages, pev_ref, w1v_ref, w2v_ref)


def _cmp_weights(cmp_pe, cmp_w1, cmp_w2):
    out = []
    for i in range(2):
        pe2 = jnp.concatenate([cmp_pe[i], cmp_pe[i]], axis=1).reshape(1, NSA_BLOCK * LANES)
        w1 = cmp_w1[i].reshape(NSA_BLOCK, NSA_HEAD_DIM, NSA_CMP_HIDDEN)
        z1 = jnp.zeros_like(w1)
        w1bd = jnp.concatenate([jnp.concatenate([w1, z1], axis=2),
                                jnp.concatenate([z1, w1], axis=2)], axis=1).astype(BF16)
        w1bd = w1bd.reshape(NSA_BLOCK * LANES, 2 * NSA_CMP_HIDDEN)
        w2 = cmp_w2[i]
        z2 = jnp.zeros_like(w2)
        w2bd = jnp.concatenate([jnp.concatenate([w2, z2], axis=1),
                                jnp.concatenate([z2, w2], axis=1)], axis=0).astype(BF16)
        out.append((pe2, w1bd, w2bd))
    return out


def _compress_dense(proj, B, T, cw):
    n_pages = T // PAGE_SIZE
    (pek, w1k, w2k), (pev, w1v, w2v) = cw
    full = lambda a: pl.BlockSpec(a.shape, lambda b: (0,) * a.ndim)
    ok, ov = pl.pallas_call(
        functools.partial(_compress_dense_kernel, n_pages=n_pages),
        out_shape=(jax.ShapeDtypeStruct((B, n_pages, 2 * LANES), F32),) * 2,
        grid=(B,),
        in_specs=[
            pl.BlockSpec((T, LANES), lambda b: (b, AB_KV_BLK)),
            pl.BlockSpec((T, LANES), lambda b: (b, AB_KV_BLK + 1)),
            full(pek), full(pev), full(w1k), full(w1v), full(w2k), full(w2v),
        ],
        out_specs=(pl.BlockSpec((1, n_pages, 2 * LANES), lambda b: (b, 0, 0)),) * 2,
        compiler_params=_cparams(("arbitrary",)),
        name="compress_dense",
    )(proj, proj, pek, pev, w1k, w1v, w2k, w2v)
    n_blk = T // NSA_BLOCK
    return ok.reshape(B, n_blk, LANES), ov.reshape(B, n_blk, LANES)


def _gather_pages(pt_ref, b, pool_ref, buf_ref, sem, n_pages, start):
    def body(p, carry):
        page = pt_ref[b, p]
        cp = pltpu.make_async_copy(pool_ref.at[pl.ds(page * PAGE_SIZE, PAGE_SIZE), :],
                                   buf_ref.at[pl.ds(p * PAGE_SIZE, PAGE_SIZE), :], sem)
        if start:
            cp.start()
        else:
            cp.wait()
        return carry
    lax.fori_loop(0, n_pages, body, 0)


def _gather_pages_dmajor(pt_ref, b, pool_ref, buf_ref, sem, n_pages, start):
    def body(p, carry):
        page = pt_ref[b, p]
        cp = pltpu.make_async_copy(pool_ref.at[pl.ds(page * PAGE_SIZE, PAGE_SIZE), :],
                                   buf_ref.at[:, p, :], sem)
        if start:
            cp.start()
        else:
            cp.wait()
        return carry
    lax.fori_loop(0, n_pages, body, 0)


def _compress_pages_t(x_ref, n_pages, pe_ref, w1_ref, w2_ref):
    per_g = []
    for g in range(NSA_KV_HEADS):
        pieces = [x_ref[g * NSA_HEAD_DIM + d] for d in range(NSA_HEAD_DIM)]
        flat = jnp.concatenate(pieces, axis=1) + pe_ref[...]
        acc = _mm(flat, w1_ref[...])
        per_g.append(_mm(_silu(acc), w2_ref[...]))
    hd = NSA_HEAD_DIM
    return jnp.concatenate([per_g[0][:, 0:hd], per_g[1][:, 0:hd], per_g[0][:, hd:2 * hd], per_g[1][:, hd:2 * hd]], axis=1)


def _cmp_weights_t(cmp_pe, cmp_w1, cmp_w2):
    out = []
    for i in range(2):
        pe_t = jnp.concatenate([cmp_pe[i].T, cmp_pe[i].T], axis=1).reshape(1, NSA_HEAD_DIM * PAGE_SIZE)
        w1 = jnp.transpose(cmp_w1[i].reshape(NSA_BLOCK, NSA_HEAD_DIM, NSA_CMP_HIDDEN), (1, 0, 2))
        z1 = jnp.zeros_like(w1)
        w1t = jnp.concatenate([jnp.concatenate([w1, z1], axis=2),
                               jnp.concatenate([z1, w1], axis=2)], axis=1).astype(BF16)
        w1t = w1t.reshape(NSA_HEAD_DIM * PAGE_SIZE, 2 * NSA_CMP_HIDDEN)
        w2 = cmp_w2[i]
        z2 = jnp.zeros_like(w2)
        w2bd = jnp.concatenate([jnp.concatenate([w2, z2], axis=1),
                                jnp.concatenate([z2, w2], axis=1)], axis=0).astype(BF16)
        out.append((pe_t, w1t, w2bd))
    return out


def _compress_paged_kernel(pt_ref, poolk_ref, poolv_ref, pek_ref, pev_ref, w1k_ref, w1v_ref,
                           w2k_ref, w2v_ref, ok_ref, ov_ref, bufk, bufv, sems, *, n_pages):
    b = pl.program_id(0)
    _gather_pages_dmajor(pt_ref, b, poolk_ref, bufk, sems.at[0], n_pages, True)
    _gather_pages_dmajor(pt_ref, b, poolv_ref, bufv, sems.at[1], n_pages, True)
    _gather_pages_dmajor(pt_ref, b, poolk_ref, bufk, sems.at[0], n_pages, False)
    ok_ref[0] = _compress_pages_t(bufk, n_pages, pek_ref, w1k_ref, w2k_ref)
    _gather_pages_dmajor(pt_ref, b, poolv_ref, bufv, sems.at[1], n_pages, False)
    ov_ref[0] = _compress_pages_t(bufv, n_pages, pev_ref, w1v_ref, w2v_ref)


def _compress_paged(page_table, pool_k, pool_v, cw):
    B, n_pages = page_table.shape
    (pek, w1k, w2k), (pev, w1v, w2v) = cw
    full = lambda a: pl.BlockSpec(a.shape, lambda b, pt: (0,) * a.ndim)
    ok, ov = pl.pallas_call(
        functools.partial(_compress_paged_kernel, n_pages=n_pages),
        out_shape=(jax.ShapeDtypeStruct((B, n_pages, 2 * LANES), F32),) * 2,
        grid_spec=pltpu.PrefetchScalarGridSpec(
            num_scalar_prefetch=1,
            grid=(B,),
            in_specs=[
                pl.BlockSpec(memory_space=pl.ANY),
                pl.BlockSpec(memory_space=pl.ANY),
                full(pek), full(pev), full(w1k), full(w1v), full(w2k), full(w2v),
            ],
            out_specs=(pl.BlockSpec((1, n_pages, 2 * LANES), lambda b, pt: (b, 0, 0)),) * 2,
            scratch_shapes=[
                pltpu.VMEM((PAGE_SIZE, n_pages, LANES), F32),
                pltpu.VMEM((PAGE_SIZE, n_pages, LANES), F32),
                pltpu.SemaphoreType.DMA((2,)),
            ],
        ),
        compiler_params=_cparams(("arbitrary",)),
        name="compress_paged",
    )(page_table, pool_k, pool_v, pek, pev, w1k, w1v, w2k, w2v)
    n_blk = n_pages * (PAGE_SIZE // NSA_BLOCK)
    return ok.reshape(B, n_blk, LANES), ov.reshape(B, n_blk, LANES)


def _stack_queries(q, g, tq):
    rows = []
    for hl in range(NSA_HPG):
        qh = q[:, hl * NSA_HEAD_DIM:(hl + 1) * NSA_HEAD_DIM]
        rows.append(jnp.concatenate([qh, qh], axis=1))
    qs = jnp.concatenate(rows, axis=0) * (NSA_HEAD_DIM ** -0.5)
    half = lax.broadcasted_iota(jnp.int32, qs.shape, 1) // NSA_HEAD_DIM
    return jnp.where(half == g, qs, 0.0).astype(BF16)


def _row_slopes(g, tq):
    hl = lax.broadcasted_iota(jnp.int32, (NSA_HPG * tq, 1), 0) // tq
    s = jnp.where(hl == 0, 0.5, jnp.where(hl == 1, 0.25, jnp.where(hl == 2, 0.125, 0.0625)))
    return s * jnp.where(g == 0, 1.0, 0.0625)


def _gate_columns(sm, g, tq):
    sig = _sigmoid(sm)
    lane = lax.broadcasted_iota(jnp.int32, sm.shape, 1)
    cols = []
    for br in range(3):
        per_head = []
        for hl in range(NSA_HPG):
            target = GATE_LANE0 + 3 * (NSA_HPG * g + hl) + br
            per_head.append(jnp.sum(jnp.where(lane == target, sig, 0.0), axis=-1, keepdims=True))
        cols.append(jnp.concatenate(per_head, axis=0))
    return cols


def _topk_select(score, k_sel, n):
    idx = lax.broadcasted_iota(jnp.int32, score.shape, 1)
    rank = jnp.zeros(score.shape, F32)
    for j in range(n):
        col = score[:, j:j + 1]
        beats = (col > score) | ((col >= score) & (idx > j))
        rank = rank + jnp.where(beats, 1.0, 0.0)
    return rank < k_sel


def _topk_select_t(score, k_sel, n):
    st = score.T[0:n]
    idx = lax.broadcasted_iota(jnp.int32, st.shape, 0)
    rank = jnp.zeros(st.shape, F32)
    for j in range(n):
        row = st[j:j + 1, :]
        beats = (row > st) | ((row >= st) & (idx > j))
        rank = rank + jnp.where(beats, 1.0, 0.0)
    sel_t = jnp.where(rank < k_sel, 1.0, 0.0)
    sel_t = jnp.concatenate([sel_t, jnp.zeros((score.shape[1] - n, st.shape[1]), F32)], axis=0)
    return sel_t.T > 0.5


def _unstack_heads(o, g, tq):
    og = jnp.where(g == 0, o[:, 0:NSA_HEAD_DIM], o[:, NSA_HEAD_DIM:2 * NSA_HEAD_DIM])
    return jnp.concatenate([og[hl * tq:(hl + 1) * tq] for hl in range(NSA_HPG)], axis=1)


def _key_features(T, onehot):
    j = lax.broadcasted_iota(jnp.int32, (T, LANES), 0)
    lane = lax.broadcasted_iota(jnp.int32, (T, LANES), 1)
    blk = j // NSA_BLOCK
    f = jnp.where(lane == FEAT_BLK, blk.astype(F32),
                  jnp.where(lane == FEAT_OFF, (j % NSA_BLOCK).astype(F32), 0.0))
    if onehot:
        f = jnp.where(lane == blk, 1.0, f)
    return f.astype(BF16)


def _ones_column(rows):
    lane = lax.broadcasted_iota(jnp.int32, (rows, LANES), 1)
    return jnp.where(lane == 0, 1.0, 0.0).astype(BF16)


def _nsa_prompt_kernel(q_ref, sm_ref, kc_ref, vc_ref, ks_ref, vs_ref, kw_ref, vw_ref, o_ref,
                       ksb, vsb, kwb, vwb, s_sc, m_sc, acc_sc, *, T, TQ, WIN, CH):
    qt = pl.program_id(1)
    R = NSA_HPG * TQ
    R2 = NSA_KV_HEADS * R
    n_blk = T // NSA_BLOCK
    G = range(NSA_KV_HEADS)
    PAD = NSA_WINDOW

    @pl.when(qt == 0)
    def _():
        ksb[:, 0:LANES] = ks_ref[...].astype(BF16)
        ksb[:, LANES:2 * LANES] = _key_features(T, True)
        vsb[:, 0:LANES] = vs_ref[...].astype(BF16)
        vsb[:, LANES:2 * LANES] = _ones_column(T)
        lane = lax.broadcasted_iota(jnp.int32, (PAD, LANES), 1)
        kwb[0:PAD, 0:LANES] = jnp.zeros((PAD, LANES), BF16)
        kwb[0:PAD, LANES:2 * LANES] = jnp.where(lane == FEAT_PAD, 1.0, 0.0).astype(BF16)
        kwb[PAD:PAD + T, 0:LANES] = kw_ref[...].astype(BF16)
        kwb[PAD:PAD + T, LANES:2 * LANES] = _key_features(T, False)
        vwb[0:PAD, :] = jnp.zeros((PAD, 2 * LANES), BF16)
        vwb[PAD:PAD + T, 0:LANES] = vw_ref[...].astype(BF16)
        vwb[PAD:PAD + T, LANES:2 * LANES] = _ones_column(T)

    qs = jnp.concatenate([_stack_queries(q_ref[:, g * 256:(g + 1) * 256], g, TQ) for g in G], axis=0)
    slope = jnp.concatenate([_row_slopes(g, TQ) for g in G], axis=0)
    gates = [_gate_columns(sm_ref[...], g, TQ) for g in G]
    gc, gs, gw = [jnp.concatenate([gates[g][br] for g in G], axis=0) for br in range(3)]
    off_q = lax.broadcasted_iota(jnp.int32, (R2, 1), 0) % TQ
    tq_i = qt * TQ + off_q
    tq_f = tq_i.astype(F32)
    lane = lax.broadcasted_iota(jnp.int32, (R2, LANES), 1)
    feat = jnp.where(lane == FEAT_BLK, slope * NSA_BLOCK,
                     jnp.where(lane == FEAT_OFF, slope, jnp.where(lane == FEAT_PAD, -SEL_BIG, 0.0)))
    q_plain = jnp.concatenate([qs, feat.astype(BF16)], axis=1)

    zpad = jnp.zeros((LANES - n_blk, LANES), F32)
    n_i = lax.broadcasted_iota(jnp.int32, (1, LANES), 1)
    center = (n_i * NSA_BLOCK).astype(F32) + 0.5 * (NSA_BLOCK - 1)
    s_c = _mm_nt(qs, jnp.concatenate([kc_ref[0], zpad], axis=0)) - slope * (tq_f - center)
    p_c = _masked_softmax(s_c, (n_i * NSA_BLOCK + NSA_BLOCK - 1) <= tq_i)
    o_c = _mm(p_c, jnp.concatenate([vc_ref[0], zpad], axis=0))

    scores = []
    for g in G:
        sc = p_c[g * R:g * R + TQ]
        for hl in range(1, NSA_HPG):
            sc = sc + p_c[g * R + hl * TQ:g * R + (hl + 1) * TQ]
        scores.append(sc)
    score = jnp.concatenate(scores, axis=0)
    tq1 = tq_i[0:NSA_KV_HEADS * TQ]
    cur = tq1 // NSA_BLOCK
    forced = (n_i == 0) | (n_i == cur) | (n_i == cur - 1)
    visible = n_i * NSA_BLOCK <= tq1
    score = jnp.where(visible, jnp.where(forced, FORCED_SCORE, score), -1.0)
    sel = _topk_select_t(score, min(NSA_TOP_K, n_blk), n_blk)

    sel_bias = jnp.where(sel & (n_i < cur), 0.0, -SEL_BIG)
    sel_bias = jnp.concatenate([sel_bias[g * TQ:(g + 1) * TQ] for g in G for _ in range(NSA_HPG)], axis=0)
    q_sel = jnp.concatenate([qs, jnp.where(lane >= FEAT_BLK, feat, sel_bias).astype(BF16)], axis=1)
    cur0 = pl.multiple_of(qt * TQ, TQ)
    nt_dims = (((1,), (1,)), ((), ()))
    n_ch = T // CH
    m_sc[...] = jnp.full((R2, LANES), NEG, F32)
    for c in range(n_ch):
        @pl.when(c * CH < cur0)
        def _():
            s = lax.dot_general(q_sel, ksb[c * CH:(c + 1) * CH, :], nt_dims, preferred_element_type=F32)
            s_sc[:, c * CH:(c + 1) * CH] = s
            mm = m_sc[...]
            for i in range(CH // LANES):
                mm = jnp.maximum(mm, s[:, i * LANES:(i + 1) * LANES])
            m_sc[...] = mm
    s_cur = lax.dot_general(q_plain, ksb[pl.ds(cur0, TQ), :], nt_dims, preferred_element_type=F32)
    off_k = lax.broadcasted_iota(jnp.int32, (1, TQ), 1)
    s_cur = jnp.where(off_k <= off_q, s_cur, NEG)
    m = jnp.maximum(jnp.max(m_sc[...], axis=-1, keepdims=True), jnp.max(s_cur, axis=-1, keepdims=True))
    e_cur = jnp.exp(s_cur - m)
    acc_sc[...] = jnp.dot(e_cur.astype(BF16), vsb[pl.ds(cur0, TQ), :], preferred_element_type=F32)
    for c in range(n_ch):
        @pl.when(c * CH < cur0)
        def _():
            e = jnp.exp(s_sc[:, c * CH:(c + 1) * CH] - m)
            acc_sc[...] += jnp.dot(e.astype(BF16), vsb[c * CH:(c + 1) * CH, :], preferred_element_type=F32)
    acc = acc_sc[...]
    o_s = acc[:, 0:LANES] * (1.0 / acc[:, LANES:LANES + 1])

    s_w = lax.dot_general(q_plain, kwb[pl.ds(cur0, WIN), :], nt_dims, preferred_element_type=F32)
    c_first = lax.broadcasted_iota(jnp.int32, (1, LANES), 1)
    c_tail = lax.broadcasted_iota(jnp.int32, (1, WIN - PAD), 1)
    s_w = jnp.concatenate([jnp.where(c_first >= off_q, s_w[:, 0:LANES], NEG),
                           s_w[:, LANES:PAD],
                           jnp.where(c_tail <= off_q, s_w[:, PAD:WIN], NEG)], axis=1)
    e_w = jnp.exp(s_w - jnp.max(s_w, axis=-1, keepdims=True))
    acc_w = jnp.dot(e_w.astype(BF16), vwb[pl.ds(cur0, WIN), :], preferred_element_type=F32)
    o_w = acc_w[:, 0:LANES] * (1.0 / acc_w[:, LANES:LANES + 1])

    o = gc * o_c + gs * o_s + gw * o_w
    o_ref[...] = jnp.concatenate([_unstack_heads(o[g * R:(g + 1) * R], g, TQ) for g in G], axis=1).astype(o_ref.dtype)


def _nsa_prompt(proj, B, T, kc, vc):
    TQ = NSA_BLOCK
    nq = T // TQ
    WIN = NSA_WINDOW + TQ
    CH = min(512, T)
    R2 = NSA_KV_HEADS * NSA_HPG * TQ
    kv = lambda j: pl.BlockSpec((T, LANES), lambda b, t: (b, AB_KV_BLK + j))
    n_blk = T // NSA_BLOCK
    return pl.pallas_call(
        functools.partial(_nsa_prompt_kernel, T=T, TQ=TQ, WIN=WIN, CH=CH),
        out_shape=jax.ShapeDtypeStruct((B * T, NSA_Q_W), BF16),
        grid=(B, nq),
        in_specs=[
            pl.BlockSpec((TQ, NSA_Q_W), lambda b, t: (b * nq + t, 3)),
            pl.BlockSpec((TQ, LANES), lambda b, t: (b * nq + t, AB_SMALL_BLK)),
            pl.BlockSpec((1, n_blk, LANES), lambda b, t: (b, 0, 0)),
            pl.BlockSpec((1, n_blk, LANES), lambda b, t: (b, 0, 0)),
            kv(2), kv(3), kv(4), kv(5),
        ],
        out_specs=pl.BlockSpec((TQ, NSA_Q_W), lambda b, t: (b * nq + t, 0)),
        scratch_shapes=[pltpu.VMEM((T, 2 * LANES), BF16), pltpu.VMEM((T, 2 * LANES), BF16),
                        pltpu.VMEM((T + NSA_WINDOW, 2 * LANES), BF16), pltpu.VMEM((T + NSA_WINDOW, 2 * LANES), BF16),
                        pltpu.VMEM((R2, T), F32), pltpu.VMEM((R2, LANES), F32), pltpu.VMEM((R2, 2 * LANES), F32)],
        compiler_params=_cparams(("arbitrary", "arbitrary")),
        name="nsa_prompt",
    )(proj, proj, kc, vc, proj, proj, proj, proj)


def _nsa_sample_kernel(pt_ref, q_ref, sm_ref, kc_ref, vc_ref, poolk_ref, poolv_ref,
                       kn_ref, vn_ref, wk_ref, wv_ref, kwn_ref, vwn_ref,
                       o_ref, wko_ref, wvo_ref, bufk, bufv, sc_sc, sems, *, n_pages, TQ):
    b = pl.program_id(0)
    past = n_pages * PAGE_SIZE
    n_cmp = past // NSA_BLOCK
    n_sel = n_cmp + 1
    R = NSA_HPG * TQ
    R2 = NSA_KV_HEADS * R
    n_buf = wk_ref.shape[2]
    G = range(NSA_KV_HEADS)

    _gather_pages(pt_ref, b, poolk_ref, bufk, sems.at[0], n_pages, True)
    _gather_pages(pt_ref, b, poolv_ref, bufv, sems.at[1], n_pages, True)

    qs = jnp.concatenate([_stack_queries(q_ref[:, g * 256:(g + 1) * 256], g, TQ) for g in G], axis=0)
    slope = jnp.concatenate([_row_slopes(g, TQ) for g in G], axis=0)
    gates = [_gate_columns(sm_ref[...], g, TQ) for g in G]
    gc, gs, gw = [jnp.concatenate([gates[g][br] for g in G], axis=0) for br in range(3)]
    tq_i = past + lax.broadcasted_iota(jnp.int32, (R2, 1), 0) % TQ
    tq_f = tq_i.astype(F32)
    new_i = past + lax.broadcasted_iota(jnp.int32, (1, TQ), 1)

    n_i = lax.broadcasted_iota(jnp.int32, (1, n_cmp), 1)
    center = (n_i * NSA_BLOCK).astype(F32) + 0.5 * (NSA_BLOCK - 1)
    s_c = _mm_nt(qs, kc_ref[0]) - slope * (tq_f - center)
    p_c = _masked_softmax(s_c, (n_i * NSA_BLOCK + NSA_BLOCK - 1) <= tq_i)
    o_c = _mm(p_c, vc_ref[0])

    bias_rows = []
    for g in G:
        score = p_c[g * R:g * R + TQ]
        for hl in range(1, NSA_HPG):
            score = score + p_c[g * R + hl * TQ:g * R + (hl + 1) * TQ]
        forced = (n_i == 0) | (n_i == n_cmp - 1)
        score = jnp.where(forced, FORCED_SCORE, score)
        sel = _topk_select(score, min(NSA_TOP_K, n_sel) - 1, n_cmp)
        bias_rows += [jnp.where(sel, 0.0, NEG)] * NSA_HPG
    sel_bias = jnp.concatenate(bias_rows, axis=0)

    wk_t = wk_ref[0]
    wv_t = wv_ref[0]
    wpos = past - n_buf + lax.broadcasted_iota(jnp.int32, (1, n_buf), 1)
    d_o = tq_i - wpos
    d_n = tq_i - new_i
    m_o = (wpos >= 0) & (d_o >= 0) & (d_o <= NSA_WINDOW)
    m_n = (d_n >= 0) & (d_n <= NSA_WINDOW)
    s_wo = jnp.where(m_o, _mm(qs, wk_t) - slope * d_o.astype(F32), NEG)
    s_wn = jnp.where(m_n, _mm_nt(qs, kwn_ref[...]) - slope * d_n.astype(F32), NEG)
    mw = jnp.maximum(jnp.max(s_wo, axis=-1, keepdims=True), jnp.max(s_wn, axis=-1, keepdims=True))
    e_o = jnp.where(m_o, jnp.exp(s_wo - mw), 0.0)
    e_w = jnp.where(m_n, jnp.exp(s_wn - mw), 0.0)
    den_w = jnp.maximum(jnp.sum(e_o, axis=-1, keepdims=True) + jnp.sum(e_w, axis=-1, keepdims=True), 1e-30)
    o_w = (_mm_nt(e_o, wv_t) + _mm(e_w, vwn_ref[...])) * (1.0 / den_w)

    lane_w = lax.broadcasted_iota(jnp.int32, (LANES, n_buf), 1)

    def shifted(old_t, new):
        slots = jnp.concatenate([jnp.zeros((LANES - TQ, LANES), F32), new], axis=0)
        tail = jnp.concatenate([jnp.zeros((LANES, n_buf - LANES), F32), slots.T], axis=1)
        return jnp.where(lane_w >= n_buf - TQ, tail, pltpu.roll(old_t, n_buf - TQ, axis=1))

    wko_ref[0] = shifted(wk_t, kwn_ref[...])
    wvo_ref[0] = shifted(wv_t, vwn_ref[...])

    _gather_pages(pt_ref, b, poolk_ref, bufk, sems.at[0], n_pages, False)
    tok = lax.broadcasted_iota(jnp.int32, (1, PAGE_SIZE), 1)
    second = tok >= NSA_BLOCK
    m_run = jnp.full((R2, PAGE_SIZE), NEG, F32)
    for p in range(n_pages):
        s = jnp.dot(qs, bufk[p * PAGE_SIZE:(p + 1) * PAGE_SIZE, :].astype(BF16), preferred_element_type=F32)
        bias = jnp.where(second, sel_bias[:, 2 * p + 1:2 * p + 2], sel_bias[:, 2 * p:2 * p + 1])
        s = s + bias - slope * (tq_f - (p * PAGE_SIZE + tok).astype(F32))
        sc_sc[p] = s
        m_run = jnp.maximum(m_run, s)
    s_n = jnp.where(new_i <= tq_i, _mm_nt(qs, kn_ref[...]) - slope * (tq_i - new_i).astype(F32), NEG)
    m = jnp.maximum(jnp.max(m_run, axis=-1, keepdims=True), jnp.max(s_n, axis=-1, keepdims=True))

    _gather_pages(pt_ref, b, poolv_ref, bufv, sems.at[1], n_pages, False)
    e_n = jnp.exp(s_n - m)
    acc = _mm(e_n, vn_ref[...])
    den_run = jnp.zeros((R2, PAGE_SIZE), F32)
    for p in range(n_pages):
        e = jnp.exp(sc_sc[p] - m)
        den_run = den_run + e
        acc = acc + _mm_nt(e, bufv[p * PAGE_SIZE:(p + 1) * PAGE_SIZE, :])
    den = jnp.sum(den_run, axis=-1, keepdims=True) + jnp.sum(e_n, axis=-1, keepdims=True)
    o_s = acc * (1.0 / den)

    o = gc * o_c + gs * o_s + gw * o_w
    o_ref[...] = jnp.concatenate([_unstack_heads(o[g * R:(g + 1) * R], g, TQ) for g in G], axis=1).astype(o_ref.dtype)


def _nsa_sample(proj, page_table, kc, vc, pool_k, pool_v, win_k, win_v):
    B, n_pages = page_table.shape
    TQ = proj.shape[0] // B
    past = n_pages * PAGE_SIZE
    n_cmp = past // NSA_BLOCK
    n_buf = win_k.shape[2]
    R2 = NSA_KV_HEADS * NSA_HPG * TQ
    kvn = lambda j: pl.BlockSpec((TQ, LANES), lambda b, pt: (b, AB_KV_BLK + j))
    win = pl.BlockSpec((1, LANES, n_buf), lambda b, pt: (b, 0, 0))
    return pl.pallas_call(
        functools.partial(_nsa_sample_kernel, n_pages=n_pages, TQ=TQ),
        out_shape=(jax.ShapeDtypeStruct((B * TQ, NSA_Q_W), BF16),
                   jax.ShapeDtypeStruct((B, LANES, n_buf), F32),
                   jax.ShapeDtypeStruct((B, LANES, n_buf), F32)),
        grid_spec=pltpu.PrefetchScalarGridSpec(
            num_scalar_prefetch=1,
            grid=(B,),
            in_specs=[
                pl.BlockSpec((TQ, NSA_Q_W), lambda b, pt: (b, 3)),
                pl.BlockSpec((TQ, LANES), lambda b, pt: (b, AB_SMALL_BLK)),
                pl.BlockSpec((1, n_cmp, LANES), lambda b, pt: (b, 0, 0)),
                pl.BlockSpec((1, n_cmp, LANES), lambda b, pt: (b, 0, 0)),
                pl.BlockSpec(memory_space=pl.ANY),
                pl.BlockSpec(memory_space=pl.ANY),
                kvn(2), kvn(3), win, win, kvn(4), kvn(5),
            ],
            out_specs=(pl.BlockSpec((TQ, NSA_Q_W), lambda b, pt: (b, 0)), win, win),
            scratch_shapes=[
                pltpu.VMEM((past, LANES), F32),
                pltpu.VMEM((past, LANES), F32),
                pltpu.VMEM((n_pages, R2, PAGE_SIZE), F32),
                pltpu.SemaphoreType.DMA((2,)),
            ],
        ),
        compiler_params=_cparams(("arbitrary",)),
        name="nsa_sample",
    )(page_table, proj, proj, kc, vc, pool_k, pool_v, proj, proj, win_k, win_v, proj, proj)


def _gdn_kernel(qkv_ref, z_ref, sm_ref, cw_ref, alog_ref, dtb_ref, ng_ref, s0_ref, cb_ref,
                o_ref, sfin_ref, S_sc, prev_sc, *, C, NB):
    t = pl.program_id(1)

    @pl.when(t == 0)
    def _():
        S_sc[...] = s0_ref[...]
        prev_sc[...] = cb_ref[...]

    def conv(n, c0):
        xe = jnp.concatenate([prev_sc[n, :, c0:c0 + LANES], qkv_ref[n, :, c0:c0 + LANES]], axis=0)
        y = xe[SUBLANES:] * cw_ref[GDN_CONV - 1:GDN_CONV, c0:c0 + LANES]
        for s in range(1, GDN_CONV):
            y = y + pltpu.roll(xe, s, axis=0)[SUBLANES:] * cw_ref[GDN_CONV - 1 - s:GDN_CONV - s, c0:c0 + LANES]
        return _silu(y)

    ii = lax.broadcasted_iota(jnp.int32, (C, C), 0)
    jj = lax.broadcasted_iota(jnp.int32, (C, C), 1)
    eye = (ii == jj).astype(F32)

    CH = [(n, h) for n in range(NB) for h in range(GDN_HEADS)]
    X = range(len(CH))
    beta_c, d_c, d_r = [], [], []
    for n in range(NB):
        sm = sm_ref[n]
        beta = _sigmoid(sm)
        gt = -jnp.exp(alog_ref[...]) * _softplus(sm + dtb_ref[...])
        d = _cumsum_rows(gt)
        dT = d.T
        for h in range(GDN_HEADS):
            beta_c.append(beta[:, h:h + 1])
            d_c.append(d[:, GDN_A_LANE0 + h:GDN_A_LANE0 + h + 1])
            d_r.append(dT[GDN_A_LANE0 + h:GDN_A_LANE0 + h + 1, :])
    q, k, v = [], [], []
    for n, h in CH:
        qh = conv(n, h * GDN_DK)
        kh = conv(n, GDN_W + h * GDN_DK)
        q.append(qh * lax.rsqrt(jnp.sum(qh * qh, axis=-1, keepdims=True) + L2_EPS) * (GDN_DK ** -0.5))
        k.append(kh * lax.rsqrt(jnp.sum(kh * kh, axis=-1, keepdims=True) + L2_EPS))
        v.append(conv(n, 2 * GDN_W + h * GDN_DV))
    decay = [jnp.exp(jnp.minimum(d_c[x] - d_r[x], 0.0)) for x in X]
    kb = [k[x] * beta_c[x] for x in X]
    g_kk = [_mm_nt(kb[x], k[x]) for x in X]
    g_qk = [_mm_nt(q[x], k[x]) for x in X]
    a = [jnp.where(ii > jj, g_kk[x] * decay[x], 0.0) for x in X]
    qk = [jnp.where(ii >= jj, g_qk[x] * decay[x], 0.0) for x in X]
    tinv = [eye - a[x] for x in X]
    p_split = [_split_bf16(a[x]) for x in X]
    n2 = 2
    while n2 < C:
        p = [_mm3(p_split[x], p_split[x]) for x in X]
        p_split = [_split_bf16(p[x]) for x in X]
        tinv = [tinv[x] + _mm3(_split_bf16(tinv[x]), p_split[x]) for x in X]
        n2 *= 2
    u = [_mm(tinv[x], v[x] * beta_c[x]) for x in X]
    w = [_mm(tinv[x], kb[x] * jnp.exp(d_c[x])) for x in X]
    S = [S_sc[n, h] for n, h in CH]
    w_s = [_mm(w[x], S[x]) for x in X]
    q_s = [_mm(q[x] * jnp.exp(d_c[x]), S[x]) for x in X]
    v_new = [u[x] - w_s[x] for x in X]
    o = [q_s[x] + _mm(qk[x], v_new[x]) for x in X]
    d_last = [d_c[x][C - 1:C, :] for x in X]
    upd = [_mm_tn(k[x] * jnp.exp(d_last[x] - d_c[x]), v_new[x]) for x in X]
    for x, (n, h) in enumerate(CH):
        S_sc[n, h] = S[x] * jnp.exp(d_last[x]) + upd[x]
        ms = jnp.mean(o[x] * o[x], axis=-1, keepdims=True)
        zh = z_ref[n, :, h * GDN_DV:(h + 1) * GDN_DV]
        o_ref[n, :, h * GDN_DV:(h + 1) * GDN_DV] = (
            o[x] * lax.rsqrt(ms + RMS_EPS) * ng_ref[...] * _silu(zh)).astype(o_ref.dtype)

    for n in range(NB):
        prev_sc[n] = qkv_ref[n, C - SUBLANES:C, :]

    @pl.when(t == pl.num_programs(1) - 1)
    def _():
        sfin_ref[...] = S_sc[...]


def _gdn(proj, B, T, conv_w, a_log, dt_bias, norm_g, s0, conv_buf8, C):
    nt = T // C
    NB = 2 if B % 2 == 0 else 1
    pad = lambda v: jnp.zeros((1, LANES), F32).at[0, GDN_A_LANE0:GDN_A_LANE0 + GDN_HEADS].set(v)
    proj3 = proj.reshape(B, T, C_COLS)
    o, s_fin = pl.pallas_call(
        functools.partial(_gdn_kernel, C=C, NB=NB),
        out_shape=(jax.ShapeDtypeStruct((B, T, GDN_W), BF16),
                   jax.ShapeDtypeStruct((B, GDN_HEADS, GDN_DK, GDN_DV), F32)),
        grid=(B // NB, nt),
        in_specs=[
            pl.BlockSpec((NB, C, GDN_CONV_CH), lambda b, t: (b, t, 0)),
            pl.BlockSpec((NB, C, GDN_W), lambda b, t: (b, t, 3)),
            pl.BlockSpec((NB, C, LANES), lambda b, t: (b, t, C_SMALL_BLK)),
            pl.BlockSpec((GDN_CONV, GDN_CONV_CH), lambda b, t: (0, 0)),
            pl.BlockSpec((1, LANES), lambda b, t: (0, 0)),
            pl.BlockSpec((1, LANES), lambda b, t: (0, 0)),
            pl.BlockSpec((1, GDN_DV), lambda b, t: (0, 0)),
            pl.BlockSpec((NB, GDN_HEADS, GDN_DK, GDN_DV), lambda b, t: (b, 0, 0, 0)),
            pl.BlockSpec((NB, SUBLANES, GDN_CONV_CH), lambda b, t: (b, 0, 0)),
        ],
        out_specs=(pl.BlockSpec((NB, C, GDN_W), lambda b, t: (b, t, 0)),
                   pl.BlockSpec((NB, GDN_HEADS, GDN_DK, GDN_DV), lambda b, t: (b, 0, 0, 0))),
        scratch_shapes=[
            pltpu.VMEM((NB, GDN_HEADS, GDN_DK, GDN_DV), F32),
            pltpu.VMEM((NB, SUBLANES, GDN_CONV_CH), F32),
        ],
        compiler_params=_cparams(("arbitrary", "arbitrary")),
        name="gdn",
    )(proj3, proj3, proj3, conv_w, pad(a_log), pad(dt_bias), norm_g.reshape(1, GDN_DV), s0, conv_buf8)
    return o.reshape(B * T, GDN_W), s_fin


def _ab_in_weight(w):
    big = w[:, :GLA_KEY_W * 2 + GLA_VAL_W * 2]
    gk = w[:, 1536:1536 + GLA_LOWRANK]
    rest = w[:, 1536 + GLA_LOWRANK:]
    q_b = rest[:, :NSA_Q_W]
    kv = rest[:, NSA_Q_W:NSA_Q_W + 6 * NSA_KV_W]
    gate = rest[:, NSA_Q_W + 6 * NSA_KV_W:]
    small = jnp.concatenate([gk, gate, jnp.zeros((D_MODEL, LANES - GLA_LOWRANK - 3 * NSA_HEADS), w.dtype)], axis=1)
    return jnp.concatenate([big, q_b, kv, small], axis=1).astype(BF16)


def _c_in_weight(w):
    qkv = w[:, :GDN_CONV_CH]
    ba = w[:, GDN_CONV_CH:GDN_CONV_CH + 2 * GDN_HEADS]
    z = w[:, GDN_CONV_CH + 2 * GDN_HEADS:]
    small = jnp.concatenate([ba, jnp.zeros((D_MODEL, LANES - 2 * GDN_HEADS), w.dtype)], axis=1)
    return jnp.concatenate([qkv, z, small], axis=1).astype(BF16)


def _kv_out(proj, B, T, j):
    return proj[:, (AB_KV_BLK + j) * LANES:(AB_KV_BLK + j + 1) * LANES].reshape(B, T, NSA_KV_HEADS, NSA_HEAD_DIM)


PROMPT_ROWS = 512
FFN_ROWS = 1024


def kernel(x_prompt, x_sample, c_prompt, c_sample, page_table, cache_cmp_k, cache_cmp_v, cache_sel_k, cache_sel_v, state_win_k, state_win_v, state_gla, state_gdn, state_gdn_conv, w_ada, b_ada, ln_g, ln_b, w_ffn_in, w_ffn_out, ab_w_in, ab_w_gk2, ab_b_gk, ab_gla_norm, ab_cmp_pe, ab_cmp_w1, ab_cmp_w2, ab_w_out, c_w_in, c_conv_w, c_a_log, c_dt_bias, c_norm, c_w_out):
    Bp, Tp, _ = x_prompt.shape
    Bs, Ts, _ = x_sample.shape
    n_pool = cache_cmp_k.shape[1]

    mods = _adaln(jnp.concatenate([c_prompt, c_sample], axis=0), w_ada, b_ada)

    def layer_mods(layer):
        m = mods[layer]
        parts = [m[:, i * D_MODEL:(i + 1) * D_MODEL] for i in range(6)]
        return [p[:Bp] for p in parts], [p[Bp:] for p in parts]

    xp, xs = x_prompt, x_sample
    ab_p, ab_s, c_p, c_s = [], [], [], []
    for layer in range(DEPTH):
        mp, ms = layer_mods(layer)
        wf_in = w_ffn_in[layer].astype(BF16)
        wf_out = w_ffn_out[layer].astype(BF16)
        i = layer // 2
        if layer % 2 == 0:
            w_in = _ab_in_weight(ab_w_in[i])
            w_out = ab_w_out[i].astype(BF16)
            wo_a, wo_b = w_out[:GLA_VAL_W], w_out[GLA_VAL_W:]
            cw = _cmp_weights(ab_cmp_pe[i], ab_cmp_w1[i], ab_cmp_w2[i])
            cw_t = _cmp_weights_t(ab_cmp_pe[i], ab_cmp_w1[i], ab_cmp_w2[i])

            proj = _modmm(xp, mp[0], mp[1], w_in, PROMPT_ROWS)
            zero_state = jnp.zeros((Bp, GLA_HEADS, GLA_DK, GLA_DV), F32)
            o_a, s_a = _gla(proj, Bp, Tp, ab_w_gk2[i], ab_b_gk[i], ab_gla_norm[i], zero_state, min(64, Tp))
            kc, vc = _compress_dense(proj, Bp, Tp, cw)
            o_b = _nsa_prompt(proj, Bp, Tp, kc, vc)
            x1 = _outproj_ln([o_a, o_b], [wo_a, wo_b], xp, mp[2], ln_g[layer, 0], ln_b[layer, 0], PROMPT_ROWS)
            n_keep = min(NSA_WINDOW, Tp)
            ab_p.append(tuple(_kv_out(proj, Bp, Tp, j) for j in range(4))
                        + (_kv_out(proj, Bp, Tp, 4)[:, Tp - n_keep:], _kv_out(proj, Bp, Tp, 5)[:, Tp - n_keep:], s_a))
            xp = _ffn_ln(x1, mp[3], mp[4], mp[5], wf_in, wf_out, ln_g[layer, 1], ln_b[layer, 1], FFN_ROWS)

            proj = _modmm(xs, ms[0], ms[1], w_in, PROMPT_ROWS)
            o_a, s_a = _gla(proj, Bs, Ts, ab_w_gk2[i], ab_b_gk[i], ab_gla_norm[i], state_gla[i], min(64, Ts))
            pool = lambda c: jnp.transpose(c[i], (0, 2, 3, 1)).reshape(n_pool * PAGE_SIZE, LANES)
            n_buf = state_win_k.shape[2]
            win_t = lambda w: jnp.transpose(w[i], (0, 2, 3, 1)).reshape(Bs, LANES, n_buf)
            kc, vc = _compress_paged(page_table, pool(cache_cmp_k), pool(cache_cmp_v), cw_t)
            o_b, win_k, win_v = _nsa_sample(proj, page_table, kc, vc, pool(cache_sel_k), pool(cache_sel_v),
                                            win_t(state_win_k), win_t(state_win_v))
            win_out = lambda w: jnp.transpose(w.reshape(Bs, NSA_KV_HEADS, NSA_HEAD_DIM, n_buf), (0, 3, 1, 2))
            x1 = _outproj_ln([o_a, o_b], [wo_a, wo_b], xs, ms[2], ln_g[layer, 0], ln_b[layer, 0], PROMPT_ROWS)
            ab_s.append(tuple(_kv_out(proj, Bs, Ts, j) for j in range(4))
                        + (win_out(win_k), win_out(win_v), s_a))
            xs = _ffn_ln(x1, ms[3], ms[4], ms[5], wf_in, wf_out, ln_g[layer, 1], ln_b[layer, 1], FFN_ROWS)
        else:
            w_in = _c_in_weight(c_w_in[i])
            w_out = c_w_out[i].astype(BF16)
            keep = GDN_CONV - 1

            proj = _modmm(xp, mp[0], mp[1], w_in, PROMPT_ROWS // 2)
            o_c, s_c = _gdn(proj, Bp, Tp, c_conv_w[i], c_a_log[i], c_dt_bias[i], c_norm[i],
                            jnp.zeros((Bp, GDN_HEADS, GDN_DK, GDN_DV), F32),
                            jnp.zeros((Bp, SUBLANES, GDN_CONV_CH), F32), min(64, Tp))
            x1 = _outproj_ln([o_c], [w_out], xp, mp[2], ln_g[layer, 0], ln_b[layer, 0], PROMPT_ROWS)
            c_p.append((s_c, proj.reshape(Bp, Tp, C_COLS)[:, Tp - keep:, :GDN_CONV_CH]))
            xp = _ffn_ln(x1, mp[3], mp[4], mp[5], wf_in, wf_out, ln_g[layer, 1], ln_b[layer, 1], FFN_ROWS)

            proj = _modmm(xs, ms[0], ms[1], w_in, PROMPT_ROWS // 2)
            conv8 = jnp.concatenate([jnp.zeros((Bs, SUBLANES - keep, GDN_CONV_CH), F32), state_gdn_conv[i]], axis=1)
            o_c, s_c = _gdn(proj, Bs, Ts, c_conv_w[i], c_a_log[i], c_dt_bias[i], c_norm[i],
                            state_gdn[i], conv8, min(64, Ts))
            x1 = _outproj_ln([o_c], [w_out], xs, ms[2], ln_g[layer, 0], ln_b[layer, 0], PROMPT_ROWS)
            qkv_s = proj.reshape(Bs, Ts, C_COLS)[:, :, :GDN_CONV_CH]
            c_s.append((s_c, jnp.concatenate([state_gdn_conv[i], qkv_s], axis=1)[:, -keep:]))
            xs = _ffn_ln(x1, ms[3], ms[4], ms[5], wf_in, wf_out, ln_g[layer, 1], ln_b[layer, 1], FFN_ROWS)

    stack = lambda sts: [jnp.stack(z) for z in zip(*sts)]
    p_ab, s_ab = stack(ab_p), stack(ab_s)
    p_c, s_c = stack(c_p), stack(c_s)
    return (xp, xs, *p_ab, *p_c, *s_ab, *s_c)
```

```python
import functools

import jax
import jax.numpy as jnp
from jax import lax
from jax.experimental import pallas as pl
from jax.experimental.pallas import tpu as pltpu

F32 = jnp.float32
BF16 = jnp.bfloat16

D_MODEL = 1024
DEPTH = 2
PAGE_SIZE = 128
GLA_HEADS = 4
GLA_DK = 64
GLA_DV = 128
GLA_LOWRANK = 16
GLA_GATE_NORM = 16.0
NSA_HEAD_DIM = 64
NSA_HEADS = 8
NSA_KV_HEADS = 2
NSA_HPG = 4
NSA_BLOCK = 64
NSA_TOP_K = 16
NSA_WINDOW = 512
NSA_CMP_HIDDEN = 128
FORCED_SCORE = 1000.0
GDN_HEADS = 8
GDN_DK = 128
GDN_DV = 128
GDN_CONV = 4
FF_HIDDEN = 2816
DEEPNORM_ALPHA = (2.0 * DEPTH) ** 0.25
LN_EPS = 1e-5
RMS_EPS = 1e-6
L2_EPS = 1e-6
NEG = -1e30

GLA_KEY_W = GLA_HEADS * GLA_DK
GLA_VAL_W = GLA_HEADS * GLA_DV
NSA_Q_W = NSA_HEADS * NSA_HEAD_DIM
NSA_KV_W = NSA_KV_HEADS * NSA_HEAD_DIM
GDN_W = GDN_HEADS * GDN_DK
GDN_CONV_CH = 3 * GDN_W

LANES = 128
SUBLANES = 8
VMEM_LIMIT = 56 * 1024 * 1024

AB_COLS = 2944
AB_SMALL_BLK = 22
AB_KV_BLK = 16
GATE_LANE0 = GLA_LOWRANK
C_COLS = 4224
C_SMALL_BLK = 32
GDN_A_LANE0 = GDN_HEADS
FEAT_BLK = 64
FEAT_OFF = 65
FEAT_PAD = 66
SEL_BIG = 131072.0


def _cparams(sem):
    return pltpu.CompilerParams(dimension_semantics=sem, vmem_limit_bytes=VMEM_LIMIT)


def _silu(x):
    return x * (1.0 / (1.0 + jnp.exp(-x)))


def _sigmoid(x):
    return 1.0 / (1.0 + jnp.exp(-x))


def _softplus(x):
    return jnp.maximum(x, 0.0) + jnp.log(1.0 + jnp.exp(-jnp.abs(x)))


def _mm(a, b):
    return jnp.dot(a.astype(BF16), b.astype(BF16), preferred_element_type=F32)


def _split_bf16(x):
    hi = x.astype(BF16)
    return hi, (x - hi.astype(F32)).astype(BF16)


def _mm3(a, b):
    dot = lambda x, y: jnp.dot(x, y, preferred_element_type=F32)
    return dot(a[0], b[0]) + (dot(a[1], b[0]) + dot(a[0], b[1]))


def _mm_nt(a, b):
    return lax.dot_general(a.astype(BF16), b.astype(BF16), (((1,), (1,)), ((), ())),
                           preferred_element_type=F32)


def _mm_tn(a, b):
    return lax.dot_general(a.astype(BF16), b.astype(BF16), (((0,), (0,)), ((), ())),
                           preferred_element_type=F32)


def _cumsum_rows(x):
    n = x.shape[0]
    row = lax.broadcasted_iota(jnp.int32, x.shape, 0)
    s = 1
    while s < n:
        x = x + jnp.where(row >= s, pltpu.roll(x, s, axis=0), 0.0)
        s *= 2
    return x


def _masked_softmax(s, mask):
    s = jnp.where(mask, s, NEG)
    m = jnp.max(s, axis=-1, keepdims=True)
    e = jnp.where(mask, jnp.exp(s - m), 0.0)
    den = jnp.maximum(jnp.sum(e, axis=-1, keepdims=True), 1e-30)
    return e * (1.0 / den)


def _layernorm(z, g, b):
    mu = jnp.mean(z, axis=-1, keepdims=True)
    zc = z - mu
    var = jnp.mean(zc * zc, axis=-1, keepdims=True)
    return zc * lax.rsqrt(var + LN_EPS) * g + b


def _adaln_kernel(c_ref, w_ref, b_ref, o_ref):
    c = _silu(c_ref[...])
    o_ref[0] = _mm(c, w_ref[0]) + b_ref[0]


def _adaln(c_all, w_ada, b_ada):
    n = c_all.shape[0]
    tn = 1536
    nt = (6 * D_MODEL) // tn
    return pl.pallas_call(
        _adaln_kernel,
        out_shape=jax.ShapeDtypeStruct((DEPTH, n, 6 * D_MODEL), F32),
        grid=(DEPTH, nt),
        in_specs=[
            pl.BlockSpec((n, D_MODEL), lambda l, j: (0, 0)),
            pl.BlockSpec((1, D_MODEL, tn), lambda l, j: (l, 0, j)),
            pl.BlockSpec((1, 1, tn), lambda l, j: (l, 0, j)),
        ],
        out_specs=pl.BlockSpec((1, n, tn), lambda l, j: (l, 0, j)),
        compiler_params=_cparams(("arbitrary", "arbitrary")),
        name="adaln",
    )(c_all, w_ada, b_ada.reshape(DEPTH, 1, 6 * D_MODEL))


def _modmm_kernel(x_ref, sh_ref, sc_ref, w_ref, o_ref, *t_refs, t_col0):
    bb, tt, d = x_ref.shape
    h = x_ref[...] * (1.0 + sc_ref[...]) + sh_ref[...]
    res = _mm(h.reshape(bb * tt, d), w_ref[...])
    o_ref[...] = res
    for t_ref in t_refs:
        for j in range(t_ref.shape[0]):
            t_ref[j, 0] = res[:, t_col0 + j * LANES:t_col0 + (j + 1) * LANES].T


def _row_tiling(B, T, max_rows):
    if T >= max_rows:
        return 1, max_rows
    bb = min(B, max_rows // T)
    return bb, T


def _modmm(x, shift, scale, w_bf16, max_rows, t_cols=None):
    B, T, D = x.shape
    N = w_bf16.shape[1]
    bb, tt = _row_tiling(B, T, max_rows)
    nt = T // tt
    out_shape = jax.ShapeDtypeStruct((B * T, N), F32)
    out_specs = pl.BlockSpec((bb * tt, N), lambda i, j: (i * nt + j, 0))
    t_col0 = 0
    if t_cols is not None:
        assert bb == 1
        t_col0, n_t = t_cols
        out_shape = (out_shape, jax.ShapeDtypeStruct((n_t, B, LANES, T), F32))
        out_specs = (out_specs, pl.BlockSpec((n_t, 1, LANES, tt), lambda i, j: (0, i, 0, j)))
    return pl.pallas_call(
        functools.partial(_modmm_kernel, t_col0=t_col0),
        out_shape=out_shape,
        grid=(B // bb, nt),
        in_specs=[
            pl.BlockSpec((bb, tt, D), lambda i, j: (i, j, 0)),
            pl.BlockSpec((bb, 1, D), lambda i, j: (i, 0, 0)),
            pl.BlockSpec((bb, 1, D), lambda i, j: (i, 0, 0)),
            pl.BlockSpec((D, N), lambda i, j: (0, 0)),
        ],
        out_specs=out_specs,
        compiler_params=_cparams(("arbitrary", "arbitrary")),
        name="modmm",
    )(x, shift[:, None, :], scale[:, None, :], w_bf16)


def _outproj_kernel(*refs, n_in):
    a_refs = refs[:n_in]
    w_refs = refs[n_in:2 * n_in]
    x_ref, gate_ref, g_ref, b_ref, o_ref = refs[2 * n_in:]
    bb, tt, d = x_ref.shape
    acc = _mm(a_refs[0][...], w_refs[0][...])
    for a_ref, w_ref in zip(a_refs[1:], w_refs[1:]):
        acc = acc + _mm(a_ref[...], w_ref[...])
    z = DEEPNORM_ALPHA * x_ref[...] + gate_ref[...] * acc.reshape(bb, tt, d)
    o_ref[...] = _layernorm(z, g_ref[...], b_ref[...])


def _outproj_ln(acts, ws, x, gate, ln_g, ln_b, max_rows):
    B, T, D = x.shape
    bb, tt = _row_tiling(B, T, max_rows)
    nt = T // tt
    n_in = len(acts)
    in_specs = []
    for a in acts:
        in_specs.append(pl.BlockSpec((bb * tt, a.shape[1]), lambda i, j: (i * nt + j, 0)))
    for w in ws:
        in_specs.append(pl.BlockSpec(w.shape, lambda i, j: (0, 0)))
    in_specs += [
        pl.BlockSpec((bb, tt, D), lambda i, j: (i, j, 0)),
        pl.BlockSpec((bb, 1, D), lambda i, j: (i, 0, 0)),
        pl.BlockSpec((1, 1, D), lambda i, j: (0, 0, 0)),
        pl.BlockSpec((1, 1, D), lambda i, j: (0, 0, 0)),
    ]
    return pl.pallas_call(
        functools.partial(_outproj_kernel, n_in=n_in),
        out_shape=jax.ShapeDtypeStruct((B, T, D), F32),
        grid=(B // bb, nt),
        in_specs=in_specs,
        out_specs=pl.BlockSpec((bb, tt, D), lambda i, j: (i, j, 0)),
        compiler_params=_cparams(("arbitrary", "arbitrary")),
        name="outproj_ln",
    )(*acts, *ws, x, gate[:, None, :], ln_g.reshape(1, 1, D), ln_b.reshape(1, 1, D))


def _ffn_kernel(x_ref, sh_ref, sc_ref, gate_ref, wa_ref, wu_ref, wo_ref, g_ref, b_ref, o_ref,
                xm_sc, acc_sc):
    j = pl.program_id(2)
    bb, tt, d = x_ref.shape

    @pl.when(j == 0)
    def _():
        h = x_ref[...] * (1.0 + sc_ref[...]) + sh_ref[...]
        xm_sc[...] = h.reshape(bb * tt, d).astype(BF16)
        acc_sc[...] = jnp.zeros_like(acc_sc)

    xm = xm_sc[...]
    a = jnp.dot(xm, wa_ref[...], preferred_element_type=F32)
    u = jnp.dot(xm, wu_ref[...], preferred_element_type=F32)
    acc_sc[...] += _mm(_silu(a) * u, wo_ref[...])

    @pl.when(j == pl.num_programs(2) - 1)
    def _():
        z = DEEPNORM_ALPHA * x_ref[...] + gate_ref[...] * acc_sc[...].reshape(bb, tt, d)
        o_ref[...] = _layernorm(z, g_ref[...], b_ref[...])


def _ffn_ln(x, shift, scale, gate, w_in_bf16, w_out_bf16, ln_g, ln_b, max_rows):
    B, T, D = x.shape
    bb, tt = _row_tiling(B, T, max_rows)
    nt = T // tt
    th = 256
    nh = FF_HIDDEN // th
    vec = lambda v: v[:, None, :]
    return pl.pallas_call(
        _ffn_kernel,
        out_shape=jax.ShapeDtypeStruct((B, T, D), F32),
        grid=(B // bb, nt, nh),
        in_specs=[
            pl.BlockSpec((bb, tt, D), lambda i, t, j: (i, t, 0)),
            pl.BlockSpec((bb, 1, D), lambda i, t, j: (i, 0, 0)),
            pl.BlockSpec((bb, 1, D), lambda i, t, j: (i, 0, 0)),
            pl.BlockSpec((bb, 1, D), lambda i, t, j: (i, 0, 0)),
            pl.BlockSpec((D, th), lambda i, t, j: (0, j)),
            pl.BlockSpec((D, th), lambda i, t, j: (0, nh + j)),
            pl.BlockSpec((th, D), lambda i, t, j: (j, 0)),
            pl.BlockSpec((1, 1, D), lambda i, t, j: (0, 0, 0)),
            pl.BlockSpec((1, 1, D), lambda i, t, j: (0, 0, 0)),
        ],
        out_specs=pl.BlockSpec((bb, tt, D), lambda i, t, j: (i, t, 0)),
        scratch_shapes=[pltpu.VMEM((bb * tt, D), BF16), pltpu.VMEM((bb * tt, D), F32)],
        compiler_params=_cparams(("arbitrary", "arbitrary", "arbitrary")),
        name="ffn_ln",
    )(x, vec(shift), vec(scale), vec(gate), w_in_bf16, w_in_bf16, w_out_bf16,
      ln_g.reshape(1, 1, D), ln_b.reshape(1, 1, D))


def _gla_kernel(q_ref, k_ref, v_ref, r_ref, sm_ref, wgk_ref, bgk_ref, gn_ref, s0_ref,
                o_ref, sfin_ref, S_sc, q_sc, k_sc, b_sc, v_sc, o_sc, *, C):
    t = pl.program_id(1)
    KW, VW = GLA_KEY_W, GLA_VAL_W

    hk = lax.broadcasted_iota(jnp.int32, (KW, VW), 0) // GLA_DK
    hv = lax.broadcasted_iota(jnp.int32, (KW, VW), 1) // GLA_DV
    same_head = hk == hv

    @pl.when(t == 0)
    def _():
        rows = []
        for h in range(GLA_HEADS):
            pieces = [s0_ref[0, h] if h2 == h else jnp.zeros((GLA_DK, GLA_DV), F32)
                      for h2 in range(GLA_HEADS)]
            rows.append(jnp.concatenate(pieces, axis=1))
        S_sc[...] = jnp.concatenate(rows, axis=0)

    gk = sm_ref[:, 0:GLA_LOWRANK]
    pre = _mm(gk, wgk_ref[...]) + bgk_ref[...]
    log_a = (jnp.minimum(pre, 0.0) - jnp.log(1.0 + jnp.exp(-jnp.abs(pre)))) * (1.0 / GLA_GATE_NORM)
    b = _cumsum_rows(log_a)
    q = q_ref[...] * (GLA_DK ** -0.5)
    k = k_ref[...]
    v = v_ref[...]
    q_sc[...] = q
    k_sc[...] = k
    b_sc[...] = b
    v_sc[...] = v

    S = S_sc[...]
    o_sc[...] = _mm(q * jnp.exp(b), S)

    seg = same_head.astype(BF16)
    JG = min(C, 2 * SUBLANES)
    for r0 in range(0, C, JG):
        rows = C - r0
        qg = q_sc[r0:C, :]
        bg = b_sc[r0:C, :]
        local = lax.broadcasted_iota(jnp.int32, (JG, KW), 0)
        ps = []
        for jj in range(JG):
            j = r0 + jj
            p = qg * k_sc[j:j + 1, :] * jnp.exp(jnp.minimum(bg - b_sc[j:j + 1, :], 0.0))
            head = jnp.where(local >= jj, p[0:JG], 0.0)
            p = head if rows == JG else jnp.concatenate([head, p[JG:]], axis=0)
            ps.append(p.astype(BF16))
        s = jnp.dot(jnp.concatenate(ps, axis=0), seg, preferred_element_type=F32)
        contrib = s[0:rows] * v_sc[r0:r0 + 1, :]
        for jj in range(1, JG):
            contrib = contrib + s[jj * rows:(jj + 1) * rows] * v_sc[r0 + jj:r0 + jj + 1, :]
        o_sc[r0:C, :] += contrib
    o = o_sc[...]

    b_last = b[C - 1:C, :]
    kd = k * jnp.exp(b_last - b)
    upd = _mm_tn(kd, v)
    tail = jnp.broadcast_to(b_last, (SUBLANES, KW))
    dcol = jnp.exp(tail.T[:, 0:1])
    S_new = S * dcol + jnp.where(same_head, upd, 0.0)
    S_sc[...] = S_new

    outs = []
    for h in range(GLA_HEADS):
        oh = o[:, h * GLA_DV:(h + 1) * GLA_DV]
        ms = jnp.mean(oh * oh, axis=-1, keepdims=True)
        rh = r_ref[:, h * GLA_DV:(h + 1) * GLA_DV]
        outs.append(oh * lax.rsqrt(ms + RMS_EPS) * gn_ref[...] * _silu(rh))
    o_ref[...] = jnp.concatenate(outs, axis=1).astype(o_ref.dtype)

    @pl.when(t == pl.num_programs(1) - 1)
    def _():
        for h in range(GLA_HEADS):
            sfin_ref[0, h] = S_new[h * GLA_DK:(h + 1) * GLA_DK, h * GLA_DV:(h + 1) * GLA_DV]


def _gla(proj, B, T, w_gk2, b_gk, gla_norm, s0, C):
    nt = T // C
    return pl.pallas_call(
        functools.partial(_gla_kernel, C=C),
        out_shape=(jax.ShapeDtypeStruct((B * T, GLA_VAL_W), BF16),
                   jax.ShapeDtypeStruct((B, GLA_HEADS, GLA_DK, GLA_DV), F32)),
        grid=(B, nt),
        in_specs=[
            pl.BlockSpec((C, GLA_KEY_W), lambda b, t: (b * nt + t, 0)),
            pl.BlockSpec((C, GLA_KEY_W), lambda b, t: (b * nt + t, 1)),
            pl.BlockSpec((C, GLA_VAL_W), lambda b, t: (b * nt + t, 1)),
            pl.BlockSpec((C, GLA_VAL_W), lambda b, t: (b * nt + t, 2)),
            pl.BlockSpec((C, LANES), lambda b, t: (b * nt + t, AB_SMALL_BLK)),
            pl.BlockSpec((GLA_LOWRANK, GLA_KEY_W), lambda b, t: (0, 0)),
            pl.BlockSpec((1, GLA_KEY_W), lambda b, t: (0, 0)),
            pl.BlockSpec((1, GLA_DV), lambda b, t: (0, 0)),
            pl.BlockSpec((1, GLA_HEADS, GLA_DK, GLA_DV), lambda b, t: (b, 0, 0, 0)),
        ],
        out_specs=(pl.BlockSpec((C, GLA_VAL_W), lambda b, t: (b * nt + t, 0)),
                   pl.BlockSpec((1, GLA_HEADS, GLA_DK, GLA_DV), lambda b, t: (b, 0, 0, 0))),
        scratch_shapes=[
            pltpu.VMEM((GLA_KEY_W, GLA_VAL_W), F32),
            pltpu.VMEM((C, GLA_KEY_W), F32),
            pltpu.VMEM((C, GLA_KEY_W), F32),
            pltpu.VMEM((C, GLA_KEY_W), F32),
            pltpu.VMEM((C, GLA_VAL_W), F32),
            pltpu.VMEM((C, GLA_VAL_W), F32),
        ],
        compiler_params=_cparams(("arbitrary", "arbitrary")),
        name="gla",
    )(proj, proj, proj, proj, proj, w_gk2, b_gk.reshape(1, GLA_KEY_W), gla_norm.reshape(1, GLA_DV), s0)


def _compress_pages(x_ref, n_pages, pe_ref, w1_ref, w2_ref):
    outs = []
    for half in range(PAGE_SIZE // NSA_BLOCK):
        pieces = [x_ref[pl.ds(half * NSA_BLOCK + tk, n_pages, stride=PAGE_SIZE), :] for tk in range(NSA_BLOCK)]
        flat = jnp.concatenate(pieces, axis=1) + pe_ref[...]
        acc = _mm(flat, w1_ref[...])
        outs.append(_mm(_silu(acc), w2_ref[...]))
    return jnp.concatenate(outs, axis=1)


def _compress_dense_kernel(xk_ref, xv_ref, pek_ref, pev_ref, w1k_ref, w1v_ref, w2k_ref, w2v_ref,
                           ok_ref, ov_ref, *, n_pages):
    ok_ref[0] = _compress_pages(xk_ref, n_pages, pek_ref, w1k_ref, w2k_ref)
    ov_ref[0] = _compress_pages(xv_ref, n_pages, pev_ref, w1v_ref, w2v_ref)


def _cmp_weights(cmp_pe, cmp_w1, cmp_w2):
    out = []
    for i in range(2):
        pe2 = jnp.concatenate([cmp_pe[i], cmp_pe[i]], axis=1).reshape(1, NSA_BLOCK * LANES)
        w1 = cmp_w1[i].reshape(NSA_BLOCK, NSA_HEAD_DIM, NSA_CMP_HIDDEN)
        z1 = jnp.zeros_like(w1)
        w1bd = jnp.concatenate([jnp.concatenate([w1, z1], axis=2),
                                jnp.concatenate([z1, w1], axis=2)], axis=1).astype(BF16)
        w1bd = w1bd.reshape(NSA_BLOCK * LANES, 2 * NSA_CMP_HIDDEN)
        w2 = cmp_w2[i]
        z2 = jnp.zeros_like(w2)
        w2bd = jnp.concatenate([jnp.concatenate([w2, z2], axis=1),
                                jnp.concatenate([z2, w2], axis=1)], axis=0).astype(BF16)
        out.append((pe2, w1bd, w2bd))
    return out


def _compress_dense(proj, B, T, cw):
    n_pages = T // PAGE_SIZE
    (pek, w1k, w2k), (pev, w1v, w2v) = cw
    full = lambda a: pl.BlockSpec(a.shape, lambda b: (0,) * a.ndim)
    ok, ov = pl.pallas_call(
        functools.partial(_compress_dense_kernel, n_pages=n_pages),
        out_shape=(jax.ShapeDtypeStruct((B, n_pages, 2 * LANES), F32),) * 2,
        grid=(B,),
        in_specs=[
            pl.BlockSpec((T, LANES), lambda b: (b, AB_KV_BLK)),
            pl.BlockSpec((T, LANES), lambda b: (b, AB_KV_BLK + 1)),
            full(pek), full(pev), full(w1k), full(w1v), full(w2k), full(w2v),
        ],
        out_specs=(pl.BlockSpec((1, n_pages, 2 * LANES), lambda b: (b, 0, 0)),) * 2,
        compiler_params=_cparams(("arbitrary",)),
        name="compress_dense",
    )(proj, proj, pek, pev, w1k, w1v, w2k, w2v)
    n_blk = T // NSA_BLOCK
    return ok.reshape(B, n_blk, LANES), ov.reshape(B, n_blk, LANES)


def _gather_pages(pt_ref, b, pool_ref, buf_ref, sem, n_pages, start):
    def body(p, carry):
        page = pt_ref[b, p]
        cp = pltpu.make_async_copy(pool_ref.at[pl.ds(page * PAGE_SIZE, PAGE_SIZE), :],
                                   buf_ref.at[pl.ds(p * PAGE_SIZE, PAGE_SIZE), :], sem)
        if start:
            cp.start()
        else:
            cp.wait()
        return carry
    lax.fori_loop(0, n_pages, body, 0)


def _gather_pages_dmajor(pt_ref, b, pool_ref, buf_ref, sem, n_pages, start):
    def body(p, carry):
        page = pt_ref[b, p]
        cp = pltpu.make_async_copy(pool_ref.at[pl.ds(page * PAGE_SIZE, PAGE_SIZE), :],
                                   buf_ref.at[:, p, :], sem)
        if start:
            cp.start()
        else:
            cp.wait()
        return carry
    lax.fori_loop(0, n_pages, body, 0)


def _compress_pages_t(x_ref, n_pages, pe_ref, w1_ref, w2_ref):
    per_g = []
    for g in range(NSA_KV_HEADS):
        pieces = [x_ref[g * NSA_HEAD_DIM + d] for d in range(NSA_HEAD_DIM)]
        flat = jnp.concatenate(pieces, axis=1) + pe_ref[...]
        acc = _mm(flat, w1_ref[...])
        per_g.append(_mm(_silu(acc), w2_ref[...]))
    hd = NSA_HEAD_DIM
    return jnp.concatenate([per_g[0][:, 0:hd], per_g[1][:, 0:hd], per_g[0][:, hd:2 * hd], per_g[1][:, hd:2 * hd]], axis=1)


def _cmp_weights_t(cmp_pe, cmp_w1, cmp_w2):
    out = []
    for i in range(2):
        pe_t = jnp.concatenate([cmp_pe[i].T, cmp_pe[i].T], axis=1).reshape(1, NSA_HEAD_DIM * PAGE_SIZE)
        w1 = jnp.transpose(cmp_w1[i].reshape(NSA_BLOCK, NSA_HEAD_DIM, NSA_CMP_HIDDEN), (1, 0, 2))
        z1 = jnp.zeros_like(w1)
        w1t = jnp.concatenate([jnp.concatenate([w1, z1], axis=2),
                               jnp.concatenate([z1, w1], axis=2)], axis=1).astype(BF16)
        w1t = w1t.reshape(NSA_HEAD_DIM * PAGE_SIZE, 2 * NSA_CMP_HIDDEN)
        w2 = cmp_w2[i]
        z2 = jnp.zeros_like(w2)
        w2bd = jnp.concatenate([jnp.concatenate([w2, z2], axis=1),
                                jnp.concatenate([z2, w2], axis=1)], axis=0).astype(BF16)
        out.append((pe_t, w1t, w2bd))
    return out


def _compress_paged_kernel(pt_ref, poolk_ref, poolv_ref, pek_ref, pev_ref, w1k_ref, w1v_ref,
                           w2k_ref, w2v_ref, ok_ref, ov_ref, bufk, bufv, sems, *, n_pages):
    b = pl.program_id(0)
    _gather_pages_dmajor(pt_ref, b, poolk_ref, bufk, sems.at[0], n_pages, True)
    _gather_pages_dmajor(pt_ref, b, poolv_ref, bufv, sems.at[1], n_pages, True)
    _gather_pages_dmajor(pt_ref, b, poolk_ref, bufk, sems.at[0], n_pages, False)
    ok_ref[0] = _compress_pages_t(bufk, n_pages, pek_ref, w1k_ref, w2k_ref)
    _gather_pages_dmajor(pt_ref, b, poolv_ref, bufv, sems.at[1], n_pages, False)
    ov_ref[0] = _compress_pages_t(bufv, n_pages, pev_ref, w1v_ref, w2v_ref)


def _compress_paged(page_table, pool_k, pool_v, cw):
    B, n_pages = page_table.shape
    (pek, w1k, w2k), (pev, w1v, w2v) = cw
    full = lambda a: pl.BlockSpec(a.shape, lambda b, pt: (0,) * a.ndim)
    ok, ov = pl.pallas_call(
        functools.partial(_compress_paged_kernel, n_pages=n_pages),
        out_shape=(jax.ShapeDtypeStruct((B, n_pages, 2 * LANES), F32),) * 2,
        grid_spec=pltpu.PrefetchScalarGridSpec(
            num_scalar_prefetch=1,
            grid=(B,),
            in_specs=[
                pl.BlockSpec(memory_space=pl.ANY),
                pl.BlockSpec(memory_space=pl.ANY),
                full(pek), full(pev), full(w1k), full(w1v), full(w2k), full(w2v),
            ],
            out_specs=(pl.BlockSpec((1, n_pages, 2 * LANES), lambda b, pt: (b, 0, 0)),) * 2,
            scratch_shapes=[
                pltpu.VMEM((PAGE_SIZE, n_pages, LANES), F32),
                pltpu.VMEM((PAGE_SIZE, n_pages, LANES), F32),
                pltpu.SemaphoreType.DMA((2,)),
            ],
        ),
        compiler_params=_cparams(("arbitrary",)),
        name="compress_paged",
    )(page_table, pool_k, pool_v, pek, pev, w1k, w1v, w2k, w2v)
    n_blk = n_pages * (PAGE_SIZE // NSA_BLOCK)
    return ok.reshape(B, n_blk, LANES), ov.reshape(B, n_blk, LANES)


def _stack_queries(q, g, tq):
    rows = []
    for hl in range(NSA_HPG):
        qh = q[:, hl * NSA_HEAD_DIM:(hl + 1) * NSA_HEAD_DIM]
        rows.append(jnp.concatenate([qh, qh], axis=1))
    qs = jnp.concatenate(rows, axis=0) * (NSA_HEAD_DIM ** -0.5)
    half = lax.broadcasted_iota(jnp.int32, qs.shape, 1) // NSA_HEAD_DIM
    return jnp.where(half == g, qs, 0.0).astype(BF16)


def _row_slopes(g, tq):
    hl = lax.broadcasted_iota(jnp.int32, (NSA_HPG * tq, 1), 0) // tq
    s = jnp.where(hl == 0, 0.5, jnp.where(hl == 1, 0.25, jnp.where(hl == 2, 0.125, 0.0625)))
    return s * jnp.where(g == 0, 1.0, 0.0625)


def _gate_columns(sm, g, tq):
    sig = _sigmoid(sm)
    lane = lax.broadcasted_iota(jnp.int32, sm.shape, 1)
    cols = []
    for br in range(3):
        per_head = []
        for hl in range(NSA_HPG):
            target = GATE_LANE0 + 3 * (NSA_HPG * g + hl) + br
            per_head.append(jnp.sum(jnp.where(lane == target, sig, 0.0), axis=-1, keepdims=True))
        cols.append(jnp.concatenate(per_head, axis=0))
    return cols


def _topk_select(score, k_sel, n):
    idx = lax.broadcasted_iota(jnp.int32, score.shape, 1)
    rank = jnp.zeros(score.shape, F32)
    for j in range(n):
        col = score[:, j:j + 1]
        beats = (col > score) | ((col >= score) & (idx > j))
        rank = rank + jnp.where(beats, 1.0, 0.0)
    return rank < k_sel


def _topk_select_t(score, k_sel, n):
    st = score.T[0:n]
    idx = lax.broadcasted_iota(jnp.int32, st.shape, 0)
    rank = jnp.zeros(st.shape, F32)
    for j in range(n):
        row = st[j:j + 1, :]
        beats = (row > st) | ((row >= st) & (idx > j))
        rank = rank + jnp.where(beats, 1.0, 0.0)
    sel_t = jnp.where(rank < k_sel, 1.0, 0.0)
    sel_t = jnp.concatenate([sel_t, jnp.zeros((score.shape[1] - n, st.shape[1]), F32)], axis=0)
    return sel_t.T > 0.5


def _unstack_heads(o, g, tq):
    og = jnp.where(g == 0, o[:, 0:NSA_HEAD_DIM], o[:, NSA_HEAD_DIM:2 * NSA_HEAD_DIM])
    return jnp.concatenate([og[hl * tq:(hl + 1) * tq] for hl in range(NSA_HPG)], axis=1)


def _key_features(T, onehot):
    j = lax.broadcasted_iota(jnp.int32, (T, LANES), 0)
    lane = lax.broadcasted_iota(jnp.int32, (T, LANES), 1)
    blk = j // NSA_BLOCK
    f = jnp.where(lane == FEAT_BLK, blk.astype(F32),
                  jnp.where(lane == FEAT_OFF, (j % NSA_BLOCK).astype(F32), 0.0))
    if onehot:
        f = jnp.where(lane == blk, 1.0, f)
    return f.astype(BF16)


def _ones_column(rows):
    lane = lax.broadcasted_iota(jnp.int32, (rows, LANES), 1)
    return jnp.where(lane == 0, 1.0, 0.0).astype(BF16)


def _nsa_prompt_kernel(q_ref, sm_ref, kc_ref, vc_ref, ks_ref, vs_ref, kw_ref, vw_ref, o_ref,
                       ksb, vsb, kwb, vwb, s_sc, m_sc, acc_sc, *, T, TQ, WIN, CH):
    qt = pl.program_id(1)
    R = NSA_HPG * TQ
    R2 = NSA_KV_HEADS * R
    n_blk = T // NSA_BLOCK
    G = range(NSA_KV_HEADS)
    PAD = NSA_WINDOW

    @pl.when(qt == 0)
    def _():
        ksb[:, 0:LANES] = ks_ref[...].astype(BF16)
        ksb[:, LANES:2 * LANES] = _key_features(T, True)
        vsb[:, 0:LANES] = vs_ref[...].astype(BF16)
        vsb[:, LANES:2 * LANES] = _ones_column(T)
        lane = lax.broadcasted_iota(jnp.int32, (PAD, LANES), 1)
        kwb[0:PAD, 0:LANES] = jnp.zeros((PAD, LANES), BF16)
        kwb[0:PAD, LANES:2 * LANES] = jnp.where(lane == FEAT_PAD, 1.0, 0.0).astype(BF16)
        kwb[PAD:PAD + T, 0:LANES] = kw_ref[...].astype(BF16)
        kwb[PAD:PAD + T, LANES:2 * LANES] = _key_features(T, False)
        vwb[0:PAD, :] = jnp.zeros((PAD, 2 * LANES), BF16)
        vwb[PAD:PAD + T, 0:LANES] = vw_ref[...].astype(BF16)
        vwb[PAD:PAD + T, LANES:2 * LANES] = _ones_column(T)

    qs = jnp.concatenate([_stack_queries(q_ref[:, g * 256:(g + 1) * 256], g, TQ) for g in G], axis=0)
    slope = jnp.concatenate([_row_slopes(g, TQ) for g in G], axis=0)
    gates = [_gate_columns(sm_ref[...], g, TQ) for g in G]
    gc, gs, gw = [jnp.concatenate([gates[g][br] for g in G], axis=0) for br in range(3)]
    off_q = lax.broadcasted_iota(jnp.int32, (R2, 1), 0) % TQ
    tq_i = qt * TQ + off_q
    tq_f = tq_i.astype(F32)
    lane = lax.broadcasted_iota(jnp.int32, (R2, LANES), 1)
    feat = jnp.where(lane == FEAT_BLK, slope * NSA_BLOCK,
                     jnp.where(lane == FEAT_OFF, slope, jnp.where(lane == FEAT_PAD, -SEL_BIG, 0.0)))
    q_plain = jnp.concatenate([qs, feat.astype(BF16)], axis=1)

    zpad = jnp.zeros((LANES - n_blk, LANES), F32)
    n_i = lax.broadcasted_iota(jnp.int32, (1, LANES), 1)
    center = (n_i * NSA_BLOCK).astype(F32) + 0.5 * (NSA_BLOCK - 1)
    s_c = _mm_nt(qs, jnp.concatenate([kc_ref[0], zpad], axis=0)) - slope * (tq_f - center)
    p_c = _masked_softmax(s_c, (n_i * NSA_BLOCK + NSA_BLOCK - 1) <= tq_i)
    o_c = _mm(p_c, jnp.concatenate([vc_ref[0], zpad], axis=0))

    cur0 = pl.multiple_of(qt * TQ, TQ)
    nt_dims = (((1,), (1,)), ((), ()))

    s_w = lax.dot_general(q_plain, kwb[pl.ds(cur0, WIN), :], nt_dims, preferred_element_type=F32)
    c_first = lax.broadcasted_iota(jnp.int32, (1, LANES), 1)
    c_tail = lax.broadcasted_iota(jnp.int32, (1, WIN - PAD), 1)
    s_w = jnp.concatenate([jnp.where(c_first >= off_q, s_w[:, 0:LANES], NEG),
                           s_w[:, LANES:PAD],
                           jnp.where(c_tail <= off_q, s_w[:, PAD:WIN], NEG)], axis=1)
    e_w = jnp.exp(s_w - jnp.max(s_w, axis=-1, keepdims=True))
    acc_w = jnp.dot(e_w.astype(BF16), vwb[pl.ds(cur0, WIN), :], preferred_element_type=F32)
    o_w = acc_w[:, 0:LANES] * (1.0 / acc_w[:, LANES:LANES + 1])

    scores = []
    for g in G:
        sc = p_c[g * R:g * R + TQ]
        for hl in range(1, NSA_HPG):
            sc = sc + p_c[g * R + hl * TQ:g * R + (hl + 1) * TQ]
        scores.append(sc)
    score = jnp.concatenate(scores, axis=0)
    tq1 = tq_i[0:NSA_KV_HEADS * TQ]
    cur = tq1 // NSA_BLOCK
    forced = (n_i == 0) | (n_i == cur) | (n_i == cur - 1)
    visible = n_i * NSA_BLOCK <= tq1
    score = jnp.where(visible, jnp.where(forced, FORCED_SCORE, score), -1.0)
    sel = _topk_select_t(score, min(NSA_TOP_K, n_blk), n_blk)

    sel_bias = jnp.where(sel & (n_i < cur), 0.0, -SEL_BIG)
    sel_bias = jnp.concatenate([sel_bias[g * TQ:(g + 1) * TQ] for g in G for _ in range(NSA_HPG)], axis=0)
    q_sel = jnp.concatenate([qs, jnp.where(lane >= FEAT_BLK, feat, sel_bias).astype(BF16)], axis=1)
    s_cur = lax.dot_general(q_plain, ksb[pl.ds(cur0, TQ), :], nt_dims, preferred_element_type=F32)
    off_k = lax.broadcasted_iota(jnp.int32, (1, TQ), 1)
    s_cur = jnp.where(off_k <= off_q, s_cur, NEG)
    n_ch = T // CH
    m_sc[...] = jnp.full((R2, LANES), NEG, F32)
    for c in range(n_ch):
        @pl.when(c * CH < cur0)
        def _():
            s = lax.dot_general(q_sel, ksb[c * CH:(c + 1) * CH, :], nt_dims, preferred_element_type=F32)
            s_sc[:, c * CH:(c + 1) * CH] = s
            mm = m_sc[...]
            for i in range(CH // LANES):
                mm = jnp.maximum(mm, s[:, i * LANES:(i + 1) * LANES])
            m_sc[...] = mm
    m = jnp.maximum(jnp.max(m_sc[...], axis=-1, keepdims=True), jnp.max(s_cur, axis=-1, keepdims=True))
    e_cur = jnp.exp(s_cur - m)
    acc_sc[...] = jnp.dot(e_cur.astype(BF16), vsb[pl.ds(cur0, TQ), :], preferred_element_type=F32)
    for c in range(n_ch):
        @pl.when(c * CH < cur0)
        def _():
            e = jnp.exp(s_sc[:, c * CH:(c + 1) * CH] - m)
            acc_sc[...] += jnp.dot(e.astype(BF16), vsb[c * CH:(c + 1) * CH, :], preferred_element_type=F32)
    acc = acc_sc[...]
    o_s = acc[:, 0:LANES] * (1.0 / acc[:, LANES:LANES + 1])

    o = gc * o_c + gs * o_s + gw * o_w
    o_ref[...] = jnp.concatenate([_unstack_heads(o[g * R:(g + 1) * R], g, TQ) for g in G], axis=1).astype(o_ref.dtype)


def _nsa_prompt(proj, B, T, kc, vc):
    TQ = NSA_BLOCK
    nq = T // TQ
    WIN = NSA_WINDOW + TQ
    CH = min(512, T)
    R2 = NSA_KV_HEADS * NSA_HPG * TQ
    kv = lambda j: pl.BlockSpec((T, LANES), lambda b, t: (b, AB_KV_BLK + j))
    n_blk = T // NSA_BLOCK
    return pl.pallas_call(
        functools.partial(_nsa_prompt_kernel, T=T, TQ=TQ, WIN=WIN, CH=CH),
        out_shape=jax.ShapeDtypeStruct((B * T, NSA_Q_W), BF16),
        grid=(B, nq),
        in_specs=[
            pl.BlockSpec((TQ, NSA_Q_W), lambda b, t: (b * nq + t, 3)),
            pl.BlockSpec((TQ, LANES), lambda b, t: (b * nq + t, AB_SMALL_BLK)),
            pl.BlockSpec((1, n_blk, LANES), lambda b, t: (b, 0, 0)),
            pl.BlockSpec((1, n_blk, LANES), lambda b, t: (b, 0, 0)),
            kv(2), kv(3), kv(4), kv(5),
        ],
        out_specs=pl.BlockSpec((TQ, NSA_Q_W), lambda b, t: (b * nq + t, 0)),
        scratch_shapes=[pltpu.VMEM((T, 2 * LANES), BF16), pltpu.VMEM((T, 2 * LANES), BF16),
                        pltpu.VMEM((T + NSA_WINDOW, 2 * LANES), BF16), pltpu.VMEM((T + NSA_WINDOW, 2 * LANES), BF16),
                        pltpu.VMEM((R2, T), F32), pltpu.VMEM((R2, LANES), F32), pltpu.VMEM((R2, 2 * LANES), F32)],
        compiler_params=_cparams(("arbitrary", "arbitrary")),
        name="nsa_prompt",
    )(proj, proj, kc, vc, proj, proj, proj, proj)


def _nsa_sample_kernel(pt_ref, q_ref, sm_ref, kc_ref, vc_ref, poolk_ref, poolv_ref,
                       kn_ref, vn_ref, wk_ref, wv_ref, kwn_ref, vwn_ref,
                       o_ref, wko_ref, wvo_ref, bufk, bufv, sc_sc, sems, *, n_pages, TQ):
    b = pl.program_id(0)
    past = n_pages * PAGE_SIZE
    n_cmp = past // NSA_BLOCK
    n_sel = n_cmp + 1
    R = NSA_HPG * TQ
    R2 = NSA_KV_HEADS * R
    n_buf = wk_ref.shape[2]
    G = range(NSA_KV_HEADS)

    _gather_pages(pt_ref, b, poolk_ref, bufk, sems.at[0], n_pages, True)
    _gather_pages(pt_ref, b, poolv_ref, bufv, sems.at[1], n_pages, True)

    qs = jnp.concatenate([_stack_queries(q_ref[:, g * 256:(g + 1) * 256], g, TQ) for g in G], axis=0)
    slope = jnp.concatenate([_row_slopes(g, TQ) for g in G], axis=0)
    gates = [_gate_columns(sm_ref[...], g, TQ) for g in G]
    gc, gs, gw = [jnp.concatenate([gates[g][br] for g in G], axis=0) for br in range(3)]
    tq_i = past + lax.broadcasted_iota(jnp.int32, (R2, 1), 0) % TQ
    tq_f = tq_i.astype(F32)
    new_i = past + lax.broadcasted_iota(jnp.int32, (1, TQ), 1)

    n_i = lax.broadcasted_iota(jnp.int32, (1, n_cmp), 1)
    center = (n_i * NSA_BLOCK).astype(F32) + 0.5 * (NSA_BLOCK - 1)
    s_c = _mm_nt(qs, kc_ref[0]) - slope * (tq_f - center)
    p_c = _masked_softmax(s_c, (n_i * NSA_BLOCK + NSA_BLOCK - 1) <= tq_i)
    o_c = _mm(p_c, vc_ref[0])

    bias_rows = []
    for g in G:
        score = p_c[g * R:g * R + TQ]
        for hl in range(1, NSA_HPG):
            score = score + p_c[g * R + hl * TQ:g * R + (hl + 1) * TQ]
        forced = (n_i == 0) | (n_i == n_cmp - 1)
        score = jnp.where(forced, FORCED_SCORE, score)
        sel = _topk_select(score, min(NSA_TOP_K, n_sel) - 1, n_cmp)
        bias_rows += [jnp.where(sel, 0.0, NEG)] * NSA_HPG
    sel_bias = jnp.concatenate(bias_rows, axis=0)

    wk_t = wk_ref[0]
    wv_t = wv_ref[0]
    wpos = past - n_buf + lax.broadcasted_iota(jnp.int32, (1, n_buf), 1)
    d_o = tq_i - wpos
    d_n = tq_i - new_i
    m_o = (wpos >= 0) & (d_o >= 0) & (d_o <= NSA_WINDOW)
    m_n = (d_n >= 0) & (d_n <= NSA_WINDOW)
    s_wo = jnp.where(m_o, _mm(qs, wk_t) - slope * d_o.astype(F32), NEG)
    s_wn = jnp.where(m_n, _mm_nt(qs, kwn_ref[...]) - slope * d_n.astype(F32), NEG)
    mw = jnp.maximum(jnp.max(s_wo, axis=-1, keepdims=True), jnp.max(s_wn, axis=-1, keepdims=True))
    e_o = jnp.where(m_o, jnp.exp(s_wo - mw), 0.0)
    e_w = jnp.where(m_n, jnp.exp(s_wn - mw), 0.0)
    den_w = jnp.maximum(jnp.sum(e_o, axis=-1, keepdims=True) + jnp.sum(e_w, axis=-1, keepdims=True), 1e-30)
    o_w = (_mm_nt(e_o, wv_t) + _mm(e_w, vwn_ref[...])) * (1.0 / den_w)

    lane_w = lax.broadcasted_iota(jnp.int32, (LANES, n_buf), 1)

    def shifted(old_t, new):
        slots = jnp.concatenate([jnp.zeros((LANES - TQ, LANES), F32), new], axis=0)
        tail = jnp.concatenate([jnp.zeros((LANES, n_buf - LANES), F32), slots.T], axis=1)
        return jnp.where(lane_w >= n_buf - TQ, tail, pltpu.roll(old_t, n_buf - TQ, axis=1))

    wko_ref[0] = shifted(wk_t, kwn_ref[...])
    wvo_ref[0] = shifted(wv_t, vwn_ref[...])

    _gather_pages(pt_ref, b, poolk_ref, bufk, sems.at[0], n_pages, False)
    tok = lax.broadcasted_iota(jnp.int32, (1, PAGE_SIZE), 1)
    second = tok >= NSA_BLOCK
    m_run = jnp.full((R2, PAGE_SIZE), NEG, F32)
    for p in range(n_pages):
        s = jnp.dot(qs, bufk[p * PAGE_SIZE:(p + 1) * PAGE_SIZE, :].astype(BF16), preferred_element_type=F32)
        bias = jnp.where(second, sel_bias[:, 2 * p + 1:2 * p + 2], sel_bias[:, 2 * p:2 * p + 1])
        s = s + bias - slope * (tq_f - (p * PAGE_SIZE + tok).astype(F32))
        sc_sc[p] = s
        m_run = jnp.maximum(m_run, s)
    s_n = jnp.where(new_i <= tq_i, _mm_nt(qs, kn_ref[...]) - slope * (tq_i - new_i).astype(F32), NEG)
    m = jnp.maximum(jnp.max(m_run, axis=-1, keepdims=True), jnp.max(s_n, axis=-1, keepdims=True))

    _gather_pages(pt_ref, b, poolv_ref, bufv, sems.at[1], n_pages, False)
    e_n = jnp.exp(s_n - m)
    acc = _mm(e_n, vn_ref[...])
    den_run = jnp.zeros((R2, PAGE_SIZE), F32)
    for p in range(n_pages):
        e = jnp.exp(sc_sc[p] - m)
        den_run = den_run + e
        acc = acc + _mm_nt(e, bufv[p * PAGE_SIZE:(p + 1) * PAGE_SIZE, :])
    den = jnp.sum(den_run, axis=-1, keepdims=True) + jnp.sum(e_n, axis=-1, keepdims=True)
    o_s = acc * (1.0 / den)

    o = gc * o_c + gs * o_s + gw * o_w
    o_ref[...] = jnp.concatenate([_unstack_heads(o[g * R:(g + 1) * R], g, TQ) for g in G], axis=1).astype(o_ref.dtype)


def _nsa_sample(proj, page_table, kc, vc, pool_k, pool_v, win_k, win_v):
    B, n_pages = page_table.shape
    TQ = proj.shape[0] // B
    past = n_pages * PAGE_SIZE
    n_cmp = past // NSA_BLOCK
    n_buf = win_k.shape[2]
    R2 = NSA_KV_HEADS * NSA_HPG * TQ
    kvn = lambda j: pl.BlockSpec((TQ, LANES), lambda b, pt: (b, AB_KV_BLK + j))
    win = pl.BlockSpec((1, LANES, n_buf), lambda b, pt: (b, 0, 0))
    return pl.pallas_call(
        functools.partial(_nsa_sample_kernel, n_pages=n_pages, TQ=TQ),
        out_shape=(jax.ShapeDtypeStruct((B * TQ, NSA_Q_W), BF16),
                   jax.ShapeDtypeStruct((B, LANES, n_buf), F32),
                   jax.ShapeDtypeStruct((B, LANES, n_buf), F32)),
        grid_spec=pltpu.PrefetchScalarGridSpec(
            num_scalar_prefetch=1,
            grid=(B,),
            in_specs=[
                pl.BlockSpec((TQ, NSA_Q_W), lambda b, pt: (b, 3)),
                pl.BlockSpec((TQ, LANES), lambda b, pt: (b, AB_SMALL_BLK)),
                pl.BlockSpec((1, n_cmp, LANES), lambda b, pt: (b, 0, 0)),
                pl.BlockSpec((1, n_cmp, LANES), lambda b, pt: (b, 0, 0)),
                pl.BlockSpec(memory_space=pl.ANY),
                pl.BlockSpec(memory_space=pl.ANY),
                kvn(2), kvn(3), win, win, kvn(4), kvn(5),
            ],
            out_specs=(pl.BlockSpec((TQ, NSA_Q_W), lambda b, pt: (b, 0)), win, win),
            scratch_shapes=[
                pltpu.VMEM((past, LANES), F32),
                pltpu.VMEM((past, LANES), F32),
                pltpu.VMEM((n_pages, R2, PAGE_SIZE), F32),
                pltpu.SemaphoreType.DMA((2,)),
            ],
        ),
        compiler_params=_cparams(("arbitrary",)),
        name="nsa_sample",
    )(page_table, proj, proj, kc, vc, pool_k, pool_v, proj, proj, win_k, win_v, proj, proj)


def _gdn_kernel(qkv_ref, z_ref, sm_ref, cw_ref, alog_ref, dtb_ref, ng_ref, s0_ref, cb_ref,
                o_ref, sfin_ref, S_sc, prev_sc, *, C, NB):
    t = pl.program_id(1)

    @pl.when(t == 0)
    def _():
        S_sc[...] = s0_ref[...]
        prev_sc[...] = cb_ref[...]

    def conv(n, c0):
        xe = jnp.concatenate([prev_sc[n, :, c0:c0 + LANES], qkv_ref[n, :, c0:c0 + LANES]], axis=0)
        y = xe[SUBLANES:] * cw_ref[GDN_CONV - 1:GDN_CONV, c0:c0 + LANES]
        for s in range(1, GDN_CONV):
            y = y + pltpu.roll(xe, s, axis=0)[SUBLANES:] * cw_ref[GDN_CONV - 1 - s:GDN_CONV - s, c0:c0 + LANES]
        return _silu(y)

    ii = lax.broadcasted_iota(jnp.int32, (C, C), 0)
    jj = lax.broadcasted_iota(jnp.int32, (C, C), 1)
    eye = (ii == jj).astype(F32)

    CH = [(n, h) for n in range(NB) for h in range(GDN_HEADS)]
    X = range(len(CH))
    beta_c, d_c, d_r = [], [], []
    for n in range(NB):
        sm = sm_ref[n]
        beta = _sigmoid(sm)
        gt = -jnp.exp(alog_ref[...]) * _softplus(sm + dtb_ref[...])
        d = _cumsum_rows(gt)
        dT = d.T
        for h in range(GDN_HEADS):
            beta_c.append(beta[:, h:h + 1])
            d_c.append(d[:, GDN_A_LANE0 + h:GDN_A_LANE0 + h + 1])
            d_r.append(dT[GDN_A_LANE0 + h:GDN_A_LANE0 + h + 1, :])
    q, k, v = [], [], []
    for n, h in CH:
        qh = conv(n, h * GDN_DK)
        kh = conv(n, GDN_W + h * GDN_DK)
        q.append(qh * lax.rsqrt(jnp.sum(qh * qh, axis=-1, keepdims=True) + L2_EPS) * (GDN_DK ** -0.5))
        k.append(kh * lax.rsqrt(jnp.sum(kh * kh, axis=-1, keepdims=True) + L2_EPS))
        v.append(conv(n, 2 * GDN_W + h * GDN_DV))
    decay = [jnp.exp(jnp.minimum(d_c[x] - d_r[x], 0.0)) for x in X]
    kb = [k[x] * beta_c[x] for x in X]
    g_kk = [_mm_nt(kb[x], k[x]) for x in X]
    g_qk = [_mm_nt(q[x], k[x]) for x in X]
    a = [jnp.where(ii > jj, g_kk[x] * decay[x], 0.0) for x in X]
    qk = [jnp.where(ii >= jj, g_qk[x] * decay[x], 0.0) for x in X]
    tinv = [eye - a[x] for x in X]
    p_split = [_split_bf16(a[x]) for x in X]
    n2 = 2
    while n2 < C:
        p = [_mm3(p_split[x], p_split[x]) for x in X]
        p_split = [_split_bf16(p[x]) for x in X]
        tinv = [tinv[x] + _mm3(_split_bf16(tinv[x]), p_split[x]) for x in X]
        n2 *= 2
    u = [_mm(tinv[x], v[x] * beta_c[x]) for x in X]
    w = [_mm(tinv[x], kb[x] * jnp.exp(d_c[x])) for x in X]
    S = [S_sc[n, h] for n, h in CH]
    w_s = [_mm(w[x], S[x]) for x in X]
    q_s = [_mm(q[x] * jnp.exp(d_c[x]), S[x]) for x in X]
    v_new = [u[x] - w_s[x] for x in X]
    o = [q_s[x] + _mm(qk[x], v_new[x]) for x in X]
    d_last = [d_c[x][C - 1:C, :] for x in X]
    upd = [_mm_tn(k[x] * jnp.exp(d_last[x] - d_c[x]), v_new[x]) for x in X]
    for x, (n, h) in enumerate(CH):
        S_sc[n, h] = S[x] * jnp.exp(d_last[x]) + upd[x]
        ms = jnp.mean(o[x] * o[x], axis=-1, keepdims=True)
        zh = z_ref[n, :, h * GDN_DV:(h + 1) * GDN_DV]
        o_ref[n, :, h * GDN_DV:(h + 1) * GDN_DV] = (
            o[x] * lax.rsqrt(ms + RMS_EPS) * ng_ref[...] * _silu(zh)).astype(o_ref.dtype)

    for n in range(NB):
        prev_sc[n] = qkv_ref[n, C - SUBLANES:C, :]

    @pl.when(t == pl.num_programs(1) - 1)
    def _():
        sfin_ref[...] = S_sc[...]


def _gdn(proj, B, T, conv_w, a_log, dt_bias, norm_g, s0, conv_buf8, C):
    nt = T // C
    NB = 2 if B % 2 == 0 else 1
    pad = lambda v: jnp.zeros((1, LANES), F32).at[0, GDN_A_LANE0:GDN_A_LANE0 + GDN_HEADS].set(v)
    proj3 = proj.reshape(B, T, C_COLS)
    o, s_fin = pl.pallas_call(
        functools.partial(_gdn_kernel, C=C, NB=NB),
        out_shape=(jax.ShapeDtypeStruct((B, T, GDN_W), BF16),
                   jax.ShapeDtypeStruct((B, GDN_HEADS, GDN_DK, GDN_DV), F32)),
        grid=(B // NB, nt),
        in_specs=[
            pl.BlockSpec((NB, C, GDN_CONV_CH), lambda b, t: (b, t, 0)),
            pl.BlockSpec((NB, C, GDN_W), lambda b, t: (b, t, 3)),
            pl.BlockSpec((NB, C, LANES), lambda b, t: (b, t, C_SMALL_BLK)),
            pl.BlockSpec((GDN_CONV, GDN_CONV_CH), lambda b, t: (0, 0)),
            pl.BlockSpec((1, LANES), lambda b, t: (0, 0)),
            pl.BlockSpec((1, LANES), lambda b, t: (0, 0)),
            pl.BlockSpec((1, GDN_DV), lambda b, t: (0, 0)),
            pl.BlockSpec((NB, GDN_HEADS, GDN_DK, GDN_DV), lambda b, t: (b, 0, 0, 0)),
            pl.BlockSpec((NB, SUBLANES, GDN_CONV_CH), lambda b, t: (b, 0, 0)),
        ],
        out_specs=(pl.BlockSpec((NB, C, GDN_W), lambda b, t: (b, t, 0)),
                   pl.BlockSpec((NB, GDN_HEADS, GDN_DK, GDN_DV), lambda b, t: (b, 0, 0, 0))),
        scratch_shapes=[
            pltpu.VMEM((NB, GDN_HEADS, GDN_DK, GDN_DV), F32),
            pltpu.VMEM((NB, SUBLANES, GDN_CONV_CH), F32),
        ],
        compiler_params=_cparams(("arbitrary", "arbitrary")),
        name="gdn",
    )(proj3, proj3, proj3, conv_w, pad(a_log), pad(dt_bias), norm_g.reshape(1, GDN_DV), s0, conv_buf8)
    return o.reshape(B * T, GDN_W), s_fin


def _ab_in_weight(w):
    big = w[:, :GLA_KEY_W * 2 + GLA_VAL_W * 2]
    gk = w[:, 1536:1536 + GLA_LOWRANK]
    rest = w[:, 1536 + GLA_LOWRANK:]
    q_b = rest[:, :NSA_Q_W]
    kv = rest[:, NSA_Q_W:NSA_Q_W + 6 * NSA_KV_W]
    gate = rest[:, NSA_Q_W + 6 * NSA_KV_W:]
    small = jnp.concatenate([gk, gate, jnp.zeros((D_MODEL, LANES - GLA_LOWRANK - 3 * NSA_HEADS), w.dtype)], axis=1)
    return jnp.concatenate([big, q_b, kv, small], axis=1).astype(BF16)


def _c_in_weight(w):
    qkv = w[:, :GDN_CONV_CH]
    ba = w[:, GDN_CONV_CH:GDN_CONV_CH + 2 * GDN_HEADS]
    z = w[:, GDN_CONV_CH + 2 * GDN_HEADS:]
    small = jnp.concatenate([ba, jnp.zeros((D_MODEL, LANES - 2 * GDN_HEADS), w.dtype)], axis=1)
    return jnp.concatenate([qkv, z, small], axis=1).astype(BF16)


def _kv_out(proj, B, T, j):
    return proj[:, (AB_KV_BLK + j) * LANES:(AB_KV_BLK + j + 1) * LANES].reshape(B, T, NSA_KV_HEADS, NSA_HEAD_DIM)


PROMPT_ROWS = 512
FFN_ROWS = 1024


def kernel(x_prompt, x_sample, c_prompt, c_sample, page_table, cache_cmp_k, cache_cmp_v, cache_sel_k, cache_sel_v, state_win_k, state_win_v, state_gla, state_gdn, state_gdn_conv, w_ada, b_ada, ln_g, ln_b, w_ffn_in, w_ffn_out, ab_w_in, ab_w_gk2, ab_b_gk, ab_gla_norm, ab_cmp_pe, ab_cmp_w1, ab_cmp_w2, ab_w_out, c_w_in, c_conv_w, c_a_log, c_dt_bias, c_norm, c_w_out):
    Bp, Tp, _ = x_prompt.shape
    Bs, Ts, _ = x_sample.shape
    n_pool = cache_cmp_k.shape[1]

    mods = _adaln(jnp.concatenate([c_prompt, c_sample], axis=0), w_ada, b_ada)

    def layer_mods(layer):
        m = mods[layer]
        parts = [m[:, i * D_MODEL:(i + 1) * D_MODEL] for i in range(6)]
        return [p[:Bp] for p in parts], [p[Bp:] for p in parts]

    xp, xs = x_prompt, x_sample
    ab_p, ab_s, c_p, c_s = [], [], [], []
    for layer in range(DEPTH):
        mp, ms = layer_mods(layer)
        wf_in = w_ffn_in[layer].astype(BF16)
        wf_out = w_ffn_out[layer].astype(BF16)
        i = layer // 2
        if layer % 2 == 0:
            w_in = _ab_in_weight(ab_w_in[i])
            w_out = ab_w_out[i].astype(BF16)
            wo_a, wo_b = w_out[:GLA_VAL_W], w_out[GLA_VAL_W:]
            cw = _cmp_weights(ab_cmp_pe[i], ab_cmp_w1[i], ab_cmp_w2[i])
            cw_t = _cmp_weights_t(ab_cmp_pe[i], ab_cmp_w1[i], ab_cmp_w2[i])

            proj, kv_t = _modmm(xp, mp[0], mp[1], w_in, PROMPT_ROWS, t_cols=(AB_KV_BLK * LANES, 6))
            zero_state = jnp.zeros((Bp, GLA_HEADS, GLA_DK, GLA_DV), F32)
            o_a, s_a = _gla(proj, Bp, Tp, ab_w_gk2[i], ab_b_gk[i], ab_gla_norm[i], zero_state, min(64, Tp))
            kc, vc = _compress_dense(proj, Bp, Tp, cw)
            o_b = _nsa_prompt(proj, Bp, Tp, kc, vc)
            x1 = _outproj_ln([o_a, o_b], [wo_a, wo_b], xp, mp[2], ln_g[layer, 0], ln_b[layer, 0], PROMPT_ROWS)
            n_keep = min(NSA_WINDOW, Tp)
            kv_out_t = lambda a: jnp.transpose(a.reshape(Bp, NSA_KV_HEADS, NSA_HEAD_DIM, a.shape[-1]), (0, 3, 1, 2))
            ab_p.append(tuple(kv_out_t(kv_t[j]) for j in range(4))
                        + (kv_out_t(kv_t[4][:, :, Tp - n_keep:]), kv_out_t(kv_t[5][:, :, Tp - n_keep:]), s_a))
            xp = _ffn_ln(x1, mp[3], mp[4], mp[5], wf_in, wf_out, ln_g[layer, 1], ln_b[layer, 1], FFN_ROWS)

            proj = _modmm(xs, ms[0], ms[1], w_in, PROMPT_ROWS)
            o_a, s_a = _gla(proj, Bs, Ts, ab_w_gk2[i], ab_b_gk[i], ab_gla_norm[i], state_gla[i], min(64, Ts))
            pool = lambda c: jnp.transpose(c[i], (0, 2, 3, 1)).reshape(n_pool * PAGE_SIZE, LANES)
            n_buf = state_win_k.shape[2]
            win_t = lambda w: jnp.transpose(w[i], (0, 2, 3, 1)).reshape(Bs, LANES, n_buf)
            kc, vc = _compress_paged(page_table, pool(cache_cmp_k), pool(cache_cmp_v), cw_t)
            o_b, win_k, win_v = _nsa_sample(proj, page_table, kc, vc, pool(cache_sel_k), pool(cache_sel_v),
                                            win_t(state_win_k), win_t(state_win_v))
            win_out = lambda w: jnp.transpose(w.reshape(Bs, NSA_KV_HEADS, NSA_HEAD_DIM, n_buf), (0, 3, 1, 2))
            x1 = _outproj_ln([o_a, o_b], [wo_a, wo_b], xs, ms[2], ln_g[layer, 0], ln_b[layer, 0], PROMPT_ROWS)
            ab_s.append(tuple(_kv_out(proj, Bs, Ts, j) for j in range(4))
                        + (win_out(win_k), win_out(win_v), s_a))
            xs = _ffn_ln(x1, ms[3], ms[4], ms[5], wf_in, wf_out, ln_g[layer, 1], ln_b[layer, 1], FFN_ROWS)
        else:
            w_in = _c_in_weight(c_w_in[i])
            w_out = c_w_out[i].astype(BF16)
            keep = GDN_CONV - 1

            proj = _modmm(xp, mp[0], mp[1], w_in, PROMPT_ROWS // 2)
            o_c, s_c = _gdn(proj, Bp, Tp, c_conv_w[i], c_a_log[i], c_dt_bias[i], c_norm[i],
                            jnp.zeros((Bp, GDN_HEADS, GDN_DK, GDN_DV), F32),
                            jnp.zeros((Bp, SUBLANES, GDN_CONV_CH), F32), min(64, Tp))
            x1 = _outproj_ln([o_c], [w_out], xp, mp[2], ln_g[layer, 0], ln_b[layer, 0], PROMPT_ROWS)
            c_p.append((s_c, proj.reshape(Bp, Tp, C_COLS)[:, Tp - keep:, :GDN_CONV_CH]))
            xp = _ffn_ln(x1, mp[3], mp[4], mp[5], wf_in, wf_out, ln_g[layer, 1], ln_b[layer, 1], FFN_ROWS)

            proj = _modmm(xs, ms[0], ms[1], w_in, PROMPT_ROWS // 2)
            conv8 = jnp.concatenate([jnp.zeros((Bs, SUBLANES - keep, GDN_CONV_CH), F32), state_gdn_conv[i]], axis=1)
            o_c, s_c = _gdn(proj, Bs, Ts, c_conv_w[i], c_a_log[i], c_dt_bias[i], c_norm[i],
                            state_gdn[i], conv8, min(64, Ts))
            x1 = _outproj_ln([o_c], [w_out], xs, ms[2], ln_g[layer, 0], ln_b[layer, 0], PROMPT_ROWS)
            qkv_s = proj.reshape(Bs, Ts, C_COLS)[:, :, :GDN_CONV_CH]
            c_s.append((s_c, jnp.concatenate([state_gdn_conv[i], qkv_s], axis=1)[:, -keep:]))
            xs = _ffn_ln(x1, ms[3], ms[4], ms[5], wf_in, wf_out, ln_g[layer, 1], ln_b[layer, 1], FFN_ROWS)

    stack = lambda sts: [jnp.stack(z) for z in zip(*sts)]
    p_ab, s_ab = stack(ab_p), stack(ab_s)
    p_c, s_c = stack(c_p), stack(c_s)
    return (xp, xs, *p_ab, *p_c, *s_ab, *s_c)
```

```python
import functools

import jax
import jax.numpy as jnp
from jax import lax
from jax.experimental import pallas as pl
from jax.experimental.pallas import tpu as pltpu

F32 = jnp.float32
BF16 = jnp.bfloat16

D_MODEL = 1024
DEPTH = 2
PAGE_SIZE = 128
GLA_HEADS = 4
GLA_DK = 64
GLA_DV = 128
GLA_LOWRANK = 16
GLA_GATE_NORM = 16.0
NSA_HEAD_DIM = 64
NSA_HEADS = 8
NSA_KV_HEADS = 2
NSA_HPG = 4
NSA_BLOCK = 64
NSA_TOP_K = 16
NSA_WINDOW = 512
NSA_CMP_HIDDEN = 128
FORCED_SCORE = 1000.0
GDN_HEADS = 8
GDN_DK = 128
GDN_DV = 128
GDN_CONV = 4
FF_HIDDEN = 2816
DEEPNORM_ALPHA = (2.0 * DEPTH) ** 0.25
LN_EPS = 1e-5
RMS_EPS = 1e-6
L2_EPS = 1e-6
NEG = -1e30

GLA_KEY_W = GLA_HEADS * GLA_DK
GLA_VAL_W = GLA_HEADS * GLA_DV
NSA_Q_W = NSA_HEADS * NSA_HEAD_DIM
NSA_KV_W = NSA_KV_HEADS * NSA_HEAD_DIM
GDN_W = GDN_HEADS * GDN_DK
GDN_CONV_CH = 3 * GDN_W

LANES = 128
SUBLANES = 8
VMEM_LIMIT = 56 * 1024 * 1024

AB_COLS = 2944
AB_SMALL_BLK = 22
AB_KV_BLK = 16
GATE_LANE0 = GLA_LOWRANK
C_COLS = 4224
C_SMALL_BLK = 32
GDN_A_LANE0 = GDN_HEADS
FEAT_BLK = 64
FEAT_OFF = 65
FEAT_PAD = 66
SEL_BIG = 131072.0


def _cparams(sem):
    return pltpu.CompilerParams(dimension_semantics=sem, vmem_limit_bytes=VMEM_LIMIT)


def _silu(x):
    return x * (1.0 / (1.0 + jnp.exp(-x)))


def _sigmoid(x):
    return 1.0 / (1.0 + jnp.exp(-x))


def _softplus(x):
    return jnp.maximum(x, 0.0) + jnp.log(1.0 + jnp.exp(-jnp.abs(x)))


def _mm(a, b):
    return jnp.dot(a.astype(BF16), b.astype(BF16), preferred_element_type=F32)


def _split_bf16(x):
    hi = x.astype(BF16)
    return hi, (x - hi.astype(F32)).astype(BF16)


def _mm3(a, b):
    dot = lambda x, y: jnp.dot(x, y, preferred_element_type=F32)
    return dot(a[0], b[0]) + (dot(a[1], b[0]) + dot(a[0], b[1]))


def _mm_nt(a, b):
    return lax.dot_general(a.astype(BF16), b.astype(BF16), (((1,), (1,)), ((), ())),
                           preferred_element_type=F32)


def _mm_tn(a, b):
    return lax.dot_general(a.astype(BF16), b.astype(BF16), (((0,), (0,)), ((), ())),
                           preferred_element_type=F32)


def _cumsum_rows(x):
    n = x.shape[0]
    row = lax.broadcasted_iota(jnp.int32, x.shape, 0)
    s = 1
    while s < n:
        x = x + jnp.where(row >= s, pltpu.roll(x, s, axis=0), 0.0)
        s *= 2
    return x


def _masked_softmax(s, mask):
    s = jnp.where(mask, s, NEG)
    m = jnp.max(s, axis=-1, keepdims=True)
    e = jnp.where(mask, jnp.exp(s - m), 0.0)
    den = jnp.maximum(jnp.sum(e, axis=-1, keepdims=True), 1e-30)
    return e * (1.0 / den)


def _layernorm(z, g, b):
    mu = jnp.mean(z, axis=-1, keepdims=True)
    zc = z - mu
    var = jnp.mean(zc * zc, axis=-1, keepdims=True)
    return zc * lax.rsqrt(var + LN_EPS) * g + b


def _adaln_kernel(c_ref, w_ref, b_ref, o_ref):
    c = _silu(c_ref[...])
    o_ref[0] = _mm(c, w_ref[0]) + b_ref[0]


def _adaln(c_all, w_ada, b_ada):
    n = c_all.shape[0]
    tn = 1536
    nt = (6 * D_MODEL) // tn
    return pl.pallas_call(
        _adaln_kernel,
        out_shape=jax.ShapeDtypeStruct((DEPTH, n, 6 * D_MODEL), F32),
        grid=(DEPTH, nt),
        in_specs=[
            pl.BlockSpec((n, D_MODEL), lambda l, j: (0, 0)),
            pl.BlockSpec((1, D_MODEL, tn), lambda l, j: (l, 0, j)),
            pl.BlockSpec((1, 1, tn), lambda l, j: (l, 0, j)),
        ],
        out_specs=pl.BlockSpec((1, n, tn), lambda l, j: (l, 0, j)),
        compiler_params=_cparams(("arbitrary", "arbitrary")),
        name="adaln",
    )(c_all, w_ada, b_ada.reshape(DEPTH, 1, 6 * D_MODEL))


def _modmm_kernel(x_ref, sh_ref, sc_ref, w_ref, o_ref, *t_refs, t_col0):
    bb, tt, d = x_ref.shape
    h = x_ref[...] * (1.0 + sc_ref[...]) + sh_ref[...]
    res = _mm(h.reshape(bb * tt, d), w_ref[...])
    o_ref[...] = res
    for t_ref in t_refs:
        for j in range(t_ref.shape[0]):
            t_ref[j, 0] = res[:, t_col0 + j * LANES:t_col0 + (j + 1) * LANES].T


def _row_tiling(B, T, max_rows):
    if T >= max_rows:
        return 1, max_rows
    bb = min(B, max_rows // T)
    return bb, T


def _modmm(x, shift, scale, w_bf16, max_rows, t_cols=None):
    B, T, D = x.shape
    N = w_bf16.shape[1]
    bb, tt = _row_tiling(B, T, max_rows)
    nt = T // tt
    out_shape = jax.ShapeDtypeStruct((B * T, N), F32)
    out_specs = pl.BlockSpec((bb * tt, N), lambda i, j: (i * nt + j, 0))
    t_col0 = 0
    if t_cols is not None:
        assert bb == 1
        t_col0, n_t = t_cols
        out_shape = (out_shape, jax.ShapeDtypeStruct((n_t, B, LANES, T), F32))
        out_specs = (out_specs, pl.BlockSpec((n_t, 1, LANES, tt), lambda i, j: (0, i, 0, j)))
    return pl.pallas_call(
        functools.partial(_modmm_kernel, t_col0=t_col0),
        out_shape=out_shape,
        grid=(B // bb, nt),
        in_specs=[
            pl.BlockSpec((bb, tt, D), lambda i, j: (i, j, 0)),
            pl.BlockSpec((bb, 1, D), lambda i, j: (i, 0, 0)),
            pl.BlockSpec((bb, 1, D), lambda i, j: (i, 0, 0)),
            pl.BlockSpec((D, N), lambda i, j: (0, 0)),
        ],
        out_specs=out_specs,
        compiler_params=_cparams(("arbitrary", "arbitrary")),
        name="modmm",
    )(x, shift[:, None, :], scale[:, None, :], w_bf16)


def _outproj_kernel(*refs, n_in):
    a_refs = refs[:n_in]
    w_refs = refs[n_in:2 * n_in]
    x_ref, gate_ref, g_ref, b_ref, o_ref = refs[2 * n_in:]
    bb, tt, d = x_ref.shape
    acc = _mm(a_refs[0][...], w_refs[0][...])
    for a_ref, w_ref in zip(a_refs[1:], w_refs[1:]):
        acc = acc + _mm(a_ref[...], w_ref[...])
    z = DEEPNORM_ALPHA * x_ref[...] + gate_ref[...] * acc.reshape(bb, tt, d)
    o_ref[...] = _layernorm(z, g_ref[...], b_ref[...])


def _outproj_ln(acts, ws, x, gate, ln_g, ln_b, max_rows):
    B, T, D = x.shape
    bb, tt = _row_tiling(B, T, max_rows)
    nt = T // tt
    n_in = len(acts)
    in_specs = []
    for a in acts:
        in_specs.append(pl.BlockSpec((bb * tt, a.shape[1]), lambda i, j: (i * nt + j, 0)))
    for w in ws:
        in_specs.append(pl.BlockSpec(w.shape, lambda i, j: (0, 0)))
    in_specs += [
        pl.BlockSpec((bb, tt, D), lambda i, j: (i, j, 0)),
        pl.BlockSpec((bb, 1, D), lambda i, j: (i, 0, 0)),
        pl.BlockSpec((1, 1, D), lambda i, j: (0, 0, 0)),
        pl.BlockSpec((1, 1, D), lambda i, j: (0, 0, 0)),
    ]
    return pl.pallas_call(
        functools.partial(_outproj_kernel, n_in=n_in),
        out_shape=jax.ShapeDtypeStruct((B, T, D), F32),
        grid=(B // bb, nt),
        in_specs=in_specs,
        out_specs=pl.BlockSpec((bb, tt, D), lambda i, j: (i, j, 0)),
        compiler_params=_cparams(("arbitrary", "arbitrary")),
        name="outproj_ln",
    )(*acts, *ws, x, gate[:, None, :], ln_g.reshape(1, 1, D), ln_b.reshape(1, 1, D))


def _ffn_kernel(x_ref, sh_ref, sc_ref, gate_ref, wa_ref, wu_ref, wo_ref, g_ref, b_ref, o_ref,
                xm_sc, acc_sc):
    j = pl.program_id(2)
    bb, tt, d = x_ref.shape

    @pl.when(j == 0)
    def _():
        h = x_ref[...] * (1.0 + sc_ref[...]) + sh_ref[...]
        xm_sc[...] = h.reshape(bb * tt, d).astype(BF16)
        acc_sc[...] = jnp.zeros_like(acc_sc)

    xm = xm_sc[...]
    a = jnp.dot(xm, wa_ref[...], preferred_element_type=F32)
    u = jnp.dot(xm, wu_ref[...], preferred_element_type=F32)
    acc_sc[...] += _mm(_silu(a) * u, wo_ref[...])

    @pl.when(j == pl.num_programs(2) - 1)
    def _():
        z = DEEPNORM_ALPHA * x_ref[...] + gate_ref[...] * acc_sc[...].reshape(bb, tt, d)
        o_ref[...] = _layernorm(z, g_ref[...], b_ref[...])


def _ffn_ln(x, shift, scale, gate, w_in_bf16, w_out_bf16, ln_g, ln_b, max_rows):
    B, T, D = x.shape
    bb, tt = _row_tiling(B, T, max_rows)
    nt = T // tt
    th = 256
    nh = FF_HIDDEN // th
    vec = lambda v: v[:, None, :]
    return pl.pallas_call(
        _ffn_kernel,
        out_shape=jax.ShapeDtypeStruct((B, T, D), F32),
        grid=(B // bb, nt, nh),
        in_specs=[
            pl.BlockSpec((bb, tt, D), lambda i, t, j: (i, t, 0)),
            pl.BlockSpec((bb, 1, D), lambda i, t, j: (i, 0, 0)),
            pl.BlockSpec((bb, 1, D), lambda i, t, j: (i, 0, 0)),
            pl.BlockSpec((bb, 1, D), lambda i, t, j: (i, 0, 0)),
            pl.BlockSpec((D, th), lambda i, t, j: (0, j)),
            pl.BlockSpec((D, th), lambda i, t, j: (0, nh + j)),
            pl.BlockSpec((th, D), lambda i, t, j: (j, 0)),
            pl.BlockSpec((1, 1, D), lambda i, t, j: (0, 0, 0)),
            pl.BlockSpec((1, 1, D), lambda i, t, j: (0, 0, 0)),
        ],
        out_specs=pl.BlockSpec((bb, tt, D), lambda i, t, j: (i, t, 0)),
        scratch_shapes=[pltpu.VMEM((bb * tt, D), BF16), pltpu.VMEM((bb * tt, D), F32)],
        compiler_params=_cparams(("arbitrary", "arbitrary", "arbitrary")),
        name="ffn_ln",
    )(x, vec(shift), vec(scale), vec(gate), w_in_bf16, w_in_bf16, w_out_bf16,
      ln_g.reshape(1, 1, D), ln_b.reshape(1, 1, D))


def _gla_kernel(q_ref, k_ref, v_ref, r_ref, sm_ref, wgk_ref, bgk_ref, gn_ref, s0_ref,
                o_ref, sfin_ref, S_sc, q_sc, k_sc, b_sc, v_sc, o_sc, *, C):
    t = pl.program_id(1)
    KW, VW = GLA_KEY_W, GLA_VAL_W

    hk = lax.broadcasted_iota(jnp.int32, (KW, VW), 0) // GLA_DK
    hv = lax.broadcasted_iota(jnp.int32, (KW, VW), 1) // GLA_DV
    same_head = hk == hv

    @pl.when(t == 0)
    def _():
        rows = []
        for h in range(GLA_HEADS):
            pieces = [s0_ref[0, h] if h2 == h else jnp.zeros((GLA_DK, GLA_DV), F32)
                      for h2 in range(GLA_HEADS)]
            rows.append(jnp.concatenate(pieces, axis=1))
        S_sc[...] = jnp.concatenate(rows, axis=0)

    gk = sm_ref[:, 0:GLA_LOWRANK]
    pre = _mm(gk, wgk_ref[...]) + bgk_ref[...]
    log_a = (jnp.minimum(pre, 0.0) - jnp.log(1.0 + jnp.exp(-jnp.abs(pre)))) * (1.0 / GLA_GATE_NORM)
    b = _cumsum_rows(log_a)
    q = q_ref[...] * (GLA_DK ** -0.5)
    k = k_ref[...]
    v = v_ref[...]
    q_sc[...] = q
    k_sc[...] = k
    b_sc[...] = b
    v_sc[...] = v

    S = S_sc[...]
    o_sc[...] = _mm(q * jnp.exp(b), S)

    seg = same_head.astype(BF16)
    JG = min(C, 2 * SUBLANES)
    for r0 in range(0, C, JG):
        rows = C - r0
        qg = q_sc[r0:C, :]
        bg = b_sc[r0:C, :]
        local = lax.broadcasted_iota(jnp.int32, (JG, KW), 0)
        ps = []
        for jj in range(JG):
            j = r0 + jj
            p = qg * k_sc[j:j + 1, :] * jnp.exp(jnp.minimum(bg - b_sc[j:j + 1, :], 0.0))
            head = jnp.where(local >= jj, p[0:JG], 0.0)
            p = head if rows == JG else jnp.concatenate([head, p[JG:]], axis=0)
            ps.append(p.astype(BF16))
        s = jnp.dot(jnp.concatenate(ps, axis=0), seg, preferred_element_type=F32)
        contrib = s[0:rows] * v_sc[r0:r0 + 1, :]
        for jj in range(1, JG):
            contrib = contrib + s[jj * rows:(jj + 1) * rows] * v_sc[r0 + jj:r0 + jj + 1, :]
        o_sc[r0:C, :] += contrib
    o = o_sc[...]

    b_last = b[C - 1:C, :]
    kd = k * jnp.exp(b_last - b)
    upd = _mm_tn(kd, v)
    tail = jnp.broadcast_to(b_last, (SUBLANES, KW))
    dcol = jnp.exp(tail.T[:, 0:1])
    S_new = S * dcol + jnp.where(same_head, upd, 0.0)
    S_sc[...] = S_new

    outs = []
    for h in range(GLA_HEADS):
        oh = o[:, h * GLA_DV:(h + 1) * GLA_DV]
        ms = jnp.mean(oh * oh, axis=-1, keepdims=True)
        rh = r_ref[:, h * GLA_DV:(h + 1) * GLA_DV]
        outs.append(oh * lax.rsqrt(ms + RMS_EPS) * gn_ref[...] * _silu(rh))
    o_ref[...] = jnp.concatenate(outs, axis=1).astype(o_ref.dtype)

    @pl.when(t == pl.num_programs(1) - 1)
    def _():
        for h in range(GLA_HEADS):
            sfin_ref[0, h] = S_new[h * GLA_DK:(h + 1) * GLA_DK, h * GLA_DV:(h + 1) * GLA_DV]


def _gla(proj, B, T, w_gk2, b_gk, gla_norm, s0, C):
    nt = T // C
    return pl.pallas_call(
        functools.partial(_gla_kernel, C=C),
        out_shape=(jax.ShapeDtypeStruct((B * T, GLA_VAL_W), BF16),
                   jax.ShapeDtypeStruct((B, GLA_HEADS, GLA_DK, GLA_DV), F32)),
        grid=(B, nt),
        in_specs=[
            pl.BlockSpec((C, GLA_KEY_W), lambda b, t: (b * nt + t, 0)),
            pl.BlockSpec((C, GLA_KEY_W), lambda b, t: (b * nt + t, 1)),
            pl.BlockSpec((C, GLA_VAL_W), lambda b, t: (b * nt + t, 1)),
            pl.BlockSpec((C, GLA_VAL_W), lambda b, t: (b * nt + t, 2)),
            pl.BlockSpec((C, LANES), lambda b, t: (b * nt + t, AB_SMALL_BLK)),
            pl.BlockSpec((GLA_LOWRANK, GLA_KEY_W), lambda b, t: (0, 0)),
            pl.BlockSpec((1, GLA_KEY_W), lambda b, t: (0, 0)),
            pl.BlockSpec((1, GLA_DV), lambda b, t: (0, 0)),
            pl.BlockSpec((1, GLA_HEADS, GLA_DK, GLA_DV), lambda b, t: (b, 0, 0, 0)),
        ],
        out_specs=(pl.BlockSpec((C, GLA_VAL_W), lambda b, t: (b * nt + t, 0)),
                   pl.BlockSpec((1, GLA_HEADS, GLA_DK, GLA_DV), lambda b, t: (b, 0, 0, 0))),
        scratch_shapes=[
            pltpu.VMEM((GLA_KEY_W, GLA_VAL_W), F32),
            pltpu.VMEM((C, GLA_KEY_W), F32),
            pltpu.VMEM((C, GLA_KEY_W), F32),
            pltpu.VMEM((C, GLA_KEY_W), F32),
            pltpu.VMEM((C, GLA_VAL_W), F32),
            pltpu.VMEM((C, GLA_VAL_W), F32),
        ],
        compiler_params=_cparams(("arbitrary", "arbitrary")),
        name="gla",
    )(proj, proj, proj, proj, proj, w_gk2, b_gk.reshape(1, GLA_KEY_W), gla_norm.reshape(1, GLA_DV), s0)


def _compress_pages(x_ref, n_pages, pe_ref, w1_ref, w2_ref):
    outs = []
    for half in range(PAGE_SIZE // NSA_BLOCK):
        pieces = [x_ref[pl.ds(half * NSA_BLOCK + tk, n_pages, stride=PAGE_SIZE), :] for tk in range(NSA_BLOCK)]
        flat = jnp.concatenate(pieces, axis=1) + pe_ref[...]
        acc = _mm(flat, w1_ref[...])
        outs.append(_mm(_silu(acc), w2_ref[...]))
    return jnp.concatenate(outs, axis=1)


def _compress_dense_kernel(xk_ref, xv_ref, pek_ref, pev_ref, w1k_ref, w1v_ref, w2k_ref, w2v_ref,
                           ok_ref, ov_ref, *, n_pages):
    ok_ref[0] = _compress_pages(xk_ref, n_pages, pek_ref, w1k_ref, w2k_ref)
    ov_ref[0] = _compress_pages(xv_ref, n_pages, pev_ref, w1v_ref, w2v_ref)


def _cmp_weights(cmp_pe, cmp_w1, cmp_w2):
    out = []
    for i in range(2):
        pe2 = jnp.concatenate([cmp_pe[i], cmp_pe[i]], axis=1).reshape(1, NSA_BLOCK * LANES)
        w1 = cmp_w1[i].reshape(NSA_BLOCK, NSA_HEAD_DIM, NSA_CMP_HIDDEN)
        z1 = jnp.zeros_like(w1)
        w1bd = jnp.concatenate([jnp.concatenate([w1, z1], axis=2),
                                jnp.concatenate([z1, w1], axis=2)], axis=1).astype(BF16)
        w1bd = w1bd.reshape(NSA_BLOCK * LANES, 2 * NSA_CMP_HIDDEN)
        w2 = cmp_w2[i]
        z2 = jnp.zeros_like(w2)
        w2bd = jnp.concatenate([jnp.concatenate([w2, z2], axis=1),
                                jnp.concatenate([z2, w2], axis=1)], axis=0).astype(BF16)
        out.append((pe2, w1bd, w2bd))
    return out


def _compress_dense(proj, B, T, cw):
    n_pages = T // PAGE_SIZE
    (pek, w1k, w2k), (pev, w1v, w2v) = cw
    full = lambda a: pl.BlockSpec(a.shape, lambda b: (0,) * a.ndim)
    ok, ov = pl.pallas_call(
        functools.partial(_compress_dense_kernel, n_pages=n_pages),
        out_shape=(jax.ShapeDtypeStruct((B, n_pages, 2 * LANES), F32),) * 2,
        grid=(B,),
        in_specs=[
            pl.BlockSpec((T, LANES), lambda b: (b, AB_KV_BLK)),
            pl.BlockSpec((T, LANES), lambda b: (b, AB_KV_BLK + 1)),
            full(pek), full(pev), full(w1k), full(w1v), full(w2k), full(w2v),
        ],
        out_specs=(pl.BlockSpec((1, n_pages, 2 * LANES), lambda b: (b, 0, 0)),) * 2,
        compiler_params=_cparams(("arbitrary",)),
        name="compress_dense",
    )(proj, proj, pek, pev, w1k, w1v, w2k, w2v)
    n_blk = T // NSA_BLOCK
    return ok.reshape(B, n_blk, LANES), ov.reshape(B, n_blk, LANES)


def _gather_pages(pt_ref, b, pool_ref, buf_ref, sem, n_pages, start):
    def body(p, carry):
        page = pt_ref[b, p]
        cp = pltpu.make_async_copy(pool_ref.at[pl.ds(page * PAGE_SIZE, PAGE_SIZE), :],
                                   buf_ref.at[pl.ds(p * PAGE_SIZE, PAGE_SIZE), :], sem)
        if start:
            cp.start()
        else:
            cp.wait()
        return carry
    lax.fori_loop(0, n_pages, body, 0)


def _gather_pages_dmajor(pt_ref, b, pool_ref, buf_ref, sem, n_pages, start):
    def body(p, carry):
        page = pt_ref[b, p]
        cp = pltpu.make_async_copy(pool_ref.at[pl.ds(page * PAGE_SIZE, PAGE_SIZE), :],
                                   buf_ref.at[:, p, :], sem)
        if start:
            cp.start()
        else:
            cp.wait()
        return carry
    lax.fori_loop(0, n_pages, body, 0)


def _compress_pages_t(x_ref, n_pages, pe_ref, w1_ref, w2_ref):
    per_g = []
    for g in range(NSA_KV_HEADS):
        pieces = [x_ref[g * NSA_HEAD_DIM + d] for d in range(NSA_HEAD_DIM)]
        flat = jnp.concatenate(pieces, axis=1) + pe_ref[...]
        acc = _mm(flat, w1_ref[...])
        per_g.append(_mm(_silu(acc), w2_ref[...]))
    hd = NSA_HEAD_DIM
    return jnp.concatenate([per_g[0][:, 0:hd], per_g[1][:, 0:hd], per_g[0][:, hd:2 * hd], per_g[1][:, hd:2 * hd]], axis=1)


def _cmp_weights_t(cmp_pe, cmp_w1, cmp_w2):
    out = []
    for i in range(2):
        pe_t = jnp.concatenate([cmp_pe[i].T, cmp_pe[i].T], axis=1).reshape(1, NSA_HEAD_DIM * PAGE_SIZE)
        w1 = jnp.transpose(cmp_w1[i].reshape(NSA_BLOCK, NSA_HEAD_DIM, NSA_CMP_HIDDEN), (1, 0, 2))
        z1 = jnp.zeros_like(w1)
        w1t = jnp.concatenate([jnp.concatenate([w1, z1], axis=2),
                               jnp.concatenate([z1, w1], axis=2)], axis=1).astype(BF16)
        w1t = w1t.reshape(NSA_HEAD_DIM * PAGE_SIZE, 2 * NSA_CMP_HIDDEN)
        w2 = cmp_w2[i]
        z2 = jnp.zeros_like(w2)
        w2bd = jnp.concatenate([jnp.concatenate([w2, z2], axis=1),
                                jnp.concatenate([z2, w2], axis=1)], axis=0).astype(BF16)
        out.append((pe_t, w1t, w2bd))
    return out


def _compress_paged_kernel(pt_ref, poolk_ref, poolv_ref, pek_ref, pev_ref, w1k_ref, w1v_ref,
                           w2k_ref, w2v_ref, ok_ref, ov_ref, bufk, bufv, sems, *, n_pages):
    b = pl.program_id(0)
    _gather_pages_dmajor(pt_ref, b, poolk_ref, bufk, sems.at[0], n_pages, True)
    _gather_pages_dmajor(pt_ref, b, poolv_ref, bufv, sems.at[1], n_pages, True)
    _gather_pages_dmajor(pt_ref, b, poolk_ref, bufk, sems.at[0], n_pages, False)
    ok_ref[0] = _compress_pages_t(bufk, n_pages, pek_ref, w1k_ref, w2k_ref)
    _gather_pages_dmajor(pt_ref, b, poolv_ref, bufv, sems.at[1], n_pages, False)
    ov_ref[0] = _compress_pages_t(bufv, n_pages, pev_ref, w1v_ref, w2v_ref)


def _compress_paged(page_table, pool_k, pool_v, cw):
    B, n_pages = page_table.shape
    (pek, w1k, w2k), (pev, w1v, w2v) = cw
    full = lambda a: pl.BlockSpec(a.shape, lambda b, pt: (0,) * a.ndim)
    ok, ov = pl.pallas_call(
        functools.partial(_compress_paged_kernel, n_pages=n_pages),
        out_shape=(jax.ShapeDtypeStruct((B, n_pages, 2 * LANES), F32),) * 2,
        grid_spec=pltpu.PrefetchScalarGridSpec(
            num_scalar_prefetch=1,
            grid=(B,),
            in_specs=[
                pl.BlockSpec(memory_space=pl.ANY),
                pl.BlockSpec(memory_space=pl.ANY),
                full(pek), full(pev), full(w1k), full(w1v), full(w2k), full(w2v),
            ],
            out_specs=(pl.BlockSpec((1, n_pages, 2 * LANES), lambda b, pt: (b, 0, 0)),) * 2,
            scratch_shapes=[
                pltpu.VMEM((PAGE_SIZE, n_pages, LANES), F32),
                pltpu.VMEM((PAGE_SIZE, n_pages, LANES), F32),
                pltpu.SemaphoreType.DMA((2,)),
            ],
        ),
        compiler_params=_cparams(("arbitrary",)),
        name="compress_paged",
    )(page_table, pool_k, pool_v, pek, pev, w1k, w1v, w2k, w2v)
    n_blk = n_pages * (PAGE_SIZE // NSA_BLOCK)
    return ok.reshape(B, n_blk, LANES), ov.reshape(B, n_blk, LANES)


def _stack_queries(q, g, tq):
    rows = []
    for hl in range(NSA_HPG):
        qh = q[:, hl * NSA_HEAD_DIM:(hl + 1) * NSA_HEAD_DIM]
        rows.append(jnp.concatenate([qh, qh], axis=1))
    qs = jnp.concatenate(rows, axis=0) * (NSA_HEAD_DIM ** -0.5)
    half = lax.broadcasted_iota(jnp.int32, qs.shape, 1) // NSA_HEAD_DIM
    return jnp.where(half == g, qs, 0.0).astype(BF16)


def _row_slopes(g, tq):
    hl = lax.broadcasted_iota(jnp.int32, (NSA_HPG * tq, 1), 0) // tq
    s = jnp.where(hl == 0, 0.5, jnp.where(hl == 1, 0.25, jnp.where(hl == 2, 0.125, 0.0625)))
    return s * jnp.where(g == 0, 1.0, 0.0625)


def _gate_columns(sm, g, tq):
    sig = _sigmoid(sm)
    lane = lax.broadcasted_iota(jnp.int32, sm.shape, 1)
    cols = []
    for br in range(3):
        per_head = []
        for hl in range(NSA_HPG):
            target = GATE_LANE0 + 3 * (NSA_HPG * g + hl) + br
            per_head.append(jnp.sum(jnp.where(lane == target, sig, 0.0), axis=-1, keepdims=True))
        cols.append(jnp.concatenate(per_head, axis=0))
    return cols


def _topk_select(score, k_sel, n):
    idx = lax.broadcasted_iota(jnp.int32, score.shape, 1)
    rank = jnp.zeros(score.shape, F32)
    for j in range(n):
        col = score[:, j:j + 1]
        beats = (col > score) | ((col >= score) & (idx > j))
        rank = rank + jnp.where(beats, 1.0, 0.0)
    return rank < k_sel


def _topk_select_t(score, k_sel, n):
    st = score.T[0:n]
    idx = lax.broadcasted_iota(jnp.int32, st.shape, 0)
    rank = jnp.zeros(st.shape, F32)
    for j in range(n):
        row = st[j:j + 1, :]
        beats = (row > st) | ((row >= st) & (idx > j))
        rank = rank + jnp.where(beats, 1.0, 0.0)
    sel_t = jnp.where(rank < k_sel, 1.0, 0.0)
    sel_t = jnp.concatenate([sel_t, jnp.zeros((score.shape[1] - n, st.shape[1]), F32)], axis=0)
    return sel_t.T > 0.5


def _unstack_heads(o, g, tq):
    og = jnp.where(g == 0, o[:, 0:NSA_HEAD_DIM], o[:, NSA_HEAD_DIM:2 * NSA_HEAD_DIM])
    return jnp.concatenate([og[hl * tq:(hl + 1) * tq] for hl in range(NSA_HPG)], axis=1)


def _key_features(T, onehot):
    j = lax.broadcasted_iota(jnp.int32, (T, LANES), 0)
    lane = lax.broadcasted_iota(jnp.int32, (T, LANES), 1)
    blk = j // NSA_BLOCK
    f = jnp.where(lane == FEAT_BLK, blk.astype(F32),
                  jnp.where(lane == FEAT_OFF, (j % NSA_BLOCK).astype(F32), 0.0))
    if onehot:
        f = jnp.where(lane == blk, 1.0, f)
    return f.astype(BF16)


def _ones_column(rows):
    lane = lax.broadcasted_iota(jnp.int32, (rows, LANES), 1)
    return jnp.where(lane == 0, 1.0, 0.0).astype(BF16)


def _nsa_prompt_kernel(q_ref, sm_ref, kc_ref, vc_ref, ks_ref, vs_ref, kw_ref, vw_ref, o_ref,
                       ksb, vsb, kwb, vwb, s_sc, m_sc, acc_sc, *, T, TQ, WIN, CH, NB):
    qt = pl.program_id(1)
    R = NSA_HPG * TQ
    R2 = NSA_KV_HEADS * R
    n_blk = T // NSA_BLOCK
    G = range(NSA_KV_HEADS)
    N = range(NB)
    PAD = NSA_WINDOW

    @pl.when(qt == 0)
    def _():
        lane = lax.broadcasted_iota(jnp.int32, (PAD, LANES), 1)
        for n in N:
            ksb[n, :, 0:LANES] = ks_ref[n].astype(BF16)
            ksb[n, :, LANES:2 * LANES] = _key_features(T, True)
            vsb[n, :, 0:LANES] = vs_ref[n].astype(BF16)
            vsb[n, :, LANES:2 * LANES] = _ones_column(T)
            kwb[n, 0:PAD, 0:LANES] = jnp.zeros((PAD, LANES), BF16)
            kwb[n, 0:PAD, LANES:2 * LANES] = jnp.where(lane == FEAT_PAD, 1.0, 0.0).astype(BF16)
            kwb[n, PAD:PAD + T, 0:LANES] = kw_ref[n].astype(BF16)
            kwb[n, PAD:PAD + T, LANES:2 * LANES] = _key_features(T, False)
            vwb[n, 0:PAD, :] = jnp.zeros((PAD, 2 * LANES), BF16)
            vwb[n, PAD:PAD + T, 0:LANES] = vw_ref[n].astype(BF16)
            vwb[n, PAD:PAD + T, LANES:2 * LANES] = _ones_column(T)

    slope = jnp.concatenate([_row_slopes(g, TQ) for g in G], axis=0)
    off_q = lax.broadcasted_iota(jnp.int32, (R2, 1), 0) % TQ
    tq_i = qt * TQ + off_q
    tq_f = tq_i.astype(F32)
    lane = lax.broadcasted_iota(jnp.int32, (R2, LANES), 1)
    feat = jnp.where(lane == FEAT_BLK, slope * NSA_BLOCK,
                     jnp.where(lane == FEAT_OFF, slope, jnp.where(lane == FEAT_PAD, -SEL_BIG, 0.0)))
    feat_b = feat.astype(BF16)
    zpad = jnp.zeros((LANES - n_blk, LANES), F32)
    n_i = lax.broadcasted_iota(jnp.int32, (1, LANES), 1)
    center = (n_i * NSA_BLOCK).astype(F32) + 0.5 * (NSA_BLOCK - 1)
    cur0 = pl.multiple_of(qt * TQ, TQ)
    nt_dims = (((1,), (1,)), ((), ()))
    c_first = lax.broadcasted_iota(jnp.int32, (1, LANES), 1)
    c_tail = lax.broadcasted_iota(jnp.int32, (1, WIN - PAD), 1)
    tq1 = tq_i[0:NSA_KV_HEADS * TQ]
    cur = tq1 // NSA_BLOCK
    forced = (n_i == 0) | (n_i == cur) | (n_i == cur - 1)
    visible = n_i * NSA_BLOCK <= tq1
    off_k = lax.broadcasted_iota(jnp.int32, (1, TQ), 1)

    qs = [jnp.concatenate([_stack_queries(q_ref[n, :, g * 256:(g + 1) * 256], g, TQ) for g in G], axis=0) for n in N]
    q_plain = [jnp.concatenate([qs[n], feat_b], axis=1) for n in N]
    gates = [[_gate_columns(sm_ref[n], g, TQ) for g in G] for n in N]
    gcol = [[jnp.concatenate([gates[n][g][br] for g in G], axis=0) for br in range(3)] for n in N]

    s_c = [_mm_nt(qs[n], jnp.concatenate([kc_ref[n], zpad], axis=0)) - slope * (tq_f - center) for n in N]
    p_c = [_masked_softmax(s_c[n], (n_i * NSA_BLOCK + NSA_BLOCK - 1) <= tq_i) for n in N]
    o_c = [_mm(p_c[n], jnp.concatenate([vc_ref[n], zpad], axis=0)) for n in N]

    o_w = []
    for n in N:
        s_w = lax.dot_general(q_plain[n], kwb[n, pl.ds(cur0, WIN), :], nt_dims, preferred_element_type=F32)
        s_w = jnp.concatenate([jnp.where(c_first >= off_q, s_w[:, 0:LANES], NEG),
                               s_w[:, LANES:PAD],
                               jnp.where(c_tail <= off_q, s_w[:, PAD:WIN], NEG)], axis=1)
        e_w = jnp.exp(s_w - jnp.max(s_w, axis=-1, keepdims=True))
        acc_w = jnp.dot(e_w.astype(BF16), vwb[n, pl.ds(cur0, WIN), :], preferred_element_type=F32)
        o_w.append(acc_w[:, 0:LANES] * (1.0 / acc_w[:, LANES:LANES + 1]))

    q_sel, s_cur = [], []
    for n in N:
        scores = []
        for g in G:
            sc = p_c[n][g * R:g * R + TQ]
            for hl in range(1, NSA_HPG):
                sc = sc + p_c[n][g * R + hl * TQ:g * R + (hl + 1) * TQ]
            scores.append(sc)
        score = jnp.concatenate(scores, axis=0)
        score = jnp.where(visible, jnp.where(forced, FORCED_SCORE, score), -1.0)
        sel = _topk_select_t(score, min(NSA_TOP_K, n_blk), n_blk)
        sel_bias = jnp.where(sel & (n_i < cur), 0.0, -SEL_BIG)
        sel_bias = jnp.concatenate([sel_bias[g * TQ:(g + 1) * TQ] for g in G for _ in range(NSA_HPG)], axis=0)
        q_sel.append(jnp.concatenate([qs[n], jnp.where(lane >= FEAT_BLK, feat, sel_bias).astype(BF16)], axis=1))
        sc_own = lax.dot_general(q_plain[n], ksb[n, pl.ds(cur0, TQ), :], nt_dims, preferred_element_type=F32)
        s_cur.append(jnp.where(off_k <= off_q, sc_own, NEG))

    n_ch = T // CH
    m_sc[...] = jnp.full((NB, R2, LANES), NEG, F32)
    for c in range(n_ch):
        @pl.when(c * CH < cur0)
        def _():
            for n in N:
                s = lax.dot_general(q_sel[n], ksb[n, c * CH:(c + 1) * CH, :], nt_dims, preferred_element_type=F32)
                s_sc[n, :, c * CH:(c + 1) * CH] = s
                mm = m_sc[n]
                for i in range(CH // LANES):
                    mm = jnp.maximum(mm, s[:, i * LANES:(i + 1) * LANES])
                m_sc[n] = mm
    m = [jnp.maximum(jnp.max(m_sc[n], axis=-1, keepdims=True), jnp.max(s_cur[n], axis=-1, keepdims=True)) for n in N]
    for n in N:
        e_cur = jnp.exp(s_cur[n] - m[n])
        acc_sc[n] = jnp.dot(e_cur.astype(BF16), vsb[n, pl.ds(cur0, TQ), :], preferred_element_type=F32)
    for c in range(n_ch):
        @pl.when(c * CH < cur0)
        def _():
            for n in N:
                e = jnp.exp(s_sc[n, :, c * CH:(c + 1) * CH] - m[n])
                acc_sc[n] += jnp.dot(e.astype(BF16), vsb[n, c * CH:(c + 1) * CH, :], preferred_element_type=F32)
    for n in N:
        acc = acc_sc[n]
        o_s = acc[:, 0:LANES] * (1.0 / acc[:, LANES:LANES + 1])
        gc, gs, gw = gcol[n]
        o = gc * o_c[n] + gs * o_s + gw * o_w[n]
        o_ref[n] = jnp.concatenate([_unstack_heads(o[g * R:(g + 1) * R], g, TQ) for g in G], axis=1).astype(o_ref.dtype)


def _nsa_prompt(proj, B, T, kc, vc):
    TQ = NSA_BLOCK
    nq = T // TQ
    WIN = NSA_WINDOW + TQ
    CH = min(512, T)
    NB = 2 if B % 2 == 0 else 1
    R2 = NSA_KV_HEADS * NSA_HPG * TQ
    proj3 = proj.reshape(B, T, AB_COLS)
    kv = lambda j: pl.BlockSpec((NB, T, LANES), lambda b, t: (b, 0, AB_KV_BLK + j))
    n_blk = T // NSA_BLOCK
    o = pl.pallas_call(
        functools.partial(_nsa_prompt_kernel, T=T, TQ=TQ, WIN=WIN, CH=CH, NB=NB),
        out_shape=jax.ShapeDtypeStruct((B, T, NSA_Q_W), BF16),
        grid=(B // NB, nq),
        in_specs=[
            pl.BlockSpec((NB, TQ, NSA_Q_W), lambda b, t: (b, t, 3)),
            pl.BlockSpec((NB, TQ, LANES), lambda b, t: (b, t, AB_SMALL_BLK)),
            pl.BlockSpec((NB, n_blk, LANES), lambda b, t: (b, 0, 0)),
            pl.BlockSpec((NB, n_blk, LANES), lambda b, t: (b, 0, 0)),
            kv(2), kv(3), kv(4), kv(5),
        ],
        out_specs=pl.BlockSpec((NB, TQ, NSA_Q_W), lambda b, t: (b, t, 0)),
        scratch_shapes=[pltpu.VMEM((NB, T, 2 * LANES), BF16), pltpu.VMEM((NB, T, 2 * LANES), BF16),
                        pltpu.VMEM((NB, T + NSA_WINDOW, 2 * LANES), BF16),
                        pltpu.VMEM((NB, T + NSA_WINDOW, 2 * LANES), BF16),
                        pltpu.VMEM((NB, R2, T), F32), pltpu.VMEM((NB, R2, LANES), F32),
                        pltpu.VMEM((NB, R2, 2 * LANES), F32)],
        compiler_params=_cparams(("arbitrary", "arbitrary")),
        name="nsa_prompt",
    )(proj3, proj3, kc, vc, proj3, proj3, proj3, proj3)
    return o.reshape(B * T, NSA_Q_W)


def _nsa_sample_kernel(pt_ref, q_ref, sm_ref, kc_ref, vc_ref, poolk_ref, poolv_ref,
                       kn_ref, vn_ref, wk_ref, wv_ref, kwn_ref, vwn_ref,
                       o_ref, wko_ref, wvo_ref, bufk, bufv, sc_sc, sems, *, n_pages, TQ):
    b = pl.program_id(0)
    past = n_pages * PAGE_SIZE
    n_cmp = past // NSA_BLOCK
    n_sel = n_cmp + 1
    R = NSA_HPG * TQ
    R2 = NSA_KV_HEADS * R
    n_buf = wk_ref.shape[2]
    G = range(NSA_KV_HEADS)

    _gather_pages(pt_ref, b, poolk_ref, bufk, sems.at[0], n_pages, True)
    _gather_pages(pt_ref, b, poolv_ref, bufv, sems.at[1], n_pages, True)

    qs = jnp.concatenate([_stack_queries(q_ref[:, g * 256:(g + 1) * 256], g, TQ) for g in G], axis=0)
    slope = jnp.concatenate([_row_slopes(g, TQ) for g in G], axis=0)
    gates = [_gate_columns(sm_ref[...], g, TQ) for g in G]
    gc, gs, gw = [jnp.concatenate([gates[g][br] for g in G], axis=0) for br in range(3)]
    tq_i = past + lax.broadcasted_iota(jnp.int32, (R2, 1), 0) % TQ
    tq_f = tq_i.astype(F32)
    new_i = past + lax.broadcasted_iota(jnp.int32, (1, TQ), 1)

    n_i = lax.broadcasted_iota(jnp.int32, (1, n_cmp), 1)
    center = (n_i * NSA_BLOCK).astype(F32) + 0.5 * (NSA_BLOCK - 1)
    s_c = _mm_nt(qs, kc_ref[0]) - slope * (tq_f - center)
    p_c = _masked_softmax(s_c, (n_i * NSA_BLOCK + NSA_BLOCK - 1) <= tq_i)
    o_c = _mm(p_c, vc_ref[0])

    bias_rows = []
    for g in G:
        score = p_c[g * R:g * R + TQ]
        for hl in range(1, NSA_HPG):
            score = score + p_c[g * R + hl * TQ:g * R + (hl + 1) * TQ]
        forced = (n_i == 0) | (n_i == n_cmp - 1)
        score = jnp.where(forced, FORCED_SCORE, score)
        sel = _topk_select(score, min(NSA_TOP_K, n_sel) - 1, n_cmp)
        bias_rows += [jnp.where(sel, 0.0, NEG)] * NSA_HPG
    sel_bias = jnp.concatenate(bias_rows, axis=0)

    wk_t = wk_ref[0]
    wv_t = wv_ref[0]
    wpos = past - n_buf + lax.broadcasted_iota(jnp.int32, (1, n_buf), 1)
    d_o = tq_i - wpos
    d_n = tq_i - new_i
    m_o = (wpos >= 0) & (d_o >= 0) & (d_o <= NSA_WINDOW)
    m_n = (d_n >= 0) & (d_n <= NSA_WINDOW)
    s_wo = jnp.where(m_o, _mm(qs, wk_t) - slope * d_o.astype(F32), NEG)
    s_wn = jnp.where(m_n, _mm_nt(qs, kwn_ref[...]) - slope * d_n.astype(F32), NEG)
    mw = jnp.maximum(jnp.max(s_wo, axis=-1, keepdims=True), jnp.max(s_wn, axis=-1, keepdims=True))
    e_o = jnp.where(m_o, jnp.exp(s_wo - mw), 0.0)
    e_w = jnp.where(m_n, jnp.exp(s_wn - mw), 0.0)
    den_w = jnp.maximum(jnp.sum(e_o, axis=-1, keepdims=True) + jnp.sum(e_w, axis=-1, keepdims=True), 1e-30)
    o_w = (_mm_nt(e_o, wv_t) + _mm(e_w, vwn_ref[...])) * (1.0 / den_w)

    lane_w = lax.broadcasted_iota(jnp.int32, (LANES, n_buf), 1)

    def shifted(old_t, new):
        slots = jnp.concatenate([jnp.zeros((LANES - TQ, LANES), F32), new], axis=0)
        tail = jnp.concatenate([jnp.zeros((LANES, n_buf - LANES), F32), slots.T], axis=1)
        return jnp.where(lane_w >= n_buf - TQ, tail, pltpu.roll(old_t, n_buf - TQ, axis=1))

    wko_ref[0] = shifted(wk_t, kwn_ref[...])
    wvo_ref[0] = shifted(wv_t, vwn_ref[...])

    _gather_pages(pt_ref, b, poolk_ref, bufk, sems.at[0], n_pages, False)
    tok = lax.broadcasted_iota(jnp.int32, (1, PAGE_SIZE), 1)
    second = tok >= NSA_BLOCK
    m_run = jnp.full((R2, PAGE_SIZE), NEG, F32)
    for p in range(n_pages):
        s = jnp.dot(qs, bufk[p * PAGE_SIZE:(p + 1) * PAGE_SIZE, :].astype(BF16), preferred_element_type=F32)
        bias = jnp.where(second, sel_bias[:, 2 * p + 1:2 * p + 2], sel_bias[:, 2 * p:2 * p + 1])
        s = s + bias - slope * (tq_f - (p * PAGE_SIZE + tok).astype(F32))
        sc_sc[p] = s
        m_run = jnp.maximum(m_run, s)
    s_n = jnp.where(new_i <= tq_i, _mm_nt(qs, kn_ref[...]) - slope * (tq_i - new_i).astype(F32), NEG)
    m = jnp.maximum(jnp.max(m_run, axis=-1, keepdims=True), jnp.max(s_n, axis=-1, keepdims=True))

    _gather_pages(pt_ref, b, poolv_ref, bufv, sems.at[1], n_pages, False)
    e_n = jnp.exp(s_n - m)
    acc = _mm(e_n, vn_ref[...])
    den_run = jnp.zeros((R2, PAGE_SIZE), F32)
    for p in range(n_pages):
        e = jnp.exp(sc_sc[p] - m)
        den_run = den_run + e
        acc = acc + _mm_nt(e, bufv[p * PAGE_SIZE:(p + 1) * PAGE_SIZE, :])
    den = jnp.sum(den_run, axis=-1, keepdims=True) + jnp.sum(e_n, axis=-1, keepdims=True)
    o_s = acc * (1.0 / den)

    o = gc * o_c + gs * o_s + gw * o_w
    o_ref[...] = jnp.concatenate([_unstack_heads(o[g * R:(g + 1) * R], g, TQ) for g in G], axis=1).astype(o_ref.dtype)


def _nsa_sample(proj, page_table, kc, vc, pool_k, pool_v, win_k, win_v):
    B, n_pages = page_table.shape
    TQ = proj.shape[0] // B
    past = n_pages * PAGE_SIZE
    n_cmp = past // NSA_BLOCK
    n_buf = win_k.shape[2]
    R2 = NSA_KV_HEADS * NSA_HPG * TQ
    kvn = lambda j: pl.BlockSpec((TQ, LANES), lambda b, pt: (b, AB_KV_BLK + j))
    win = pl.BlockSpec((1, LANES, n_buf), lambda b, pt: (b, 0, 0))
    return pl.pallas_call(
        functools.partial(_nsa_sample_kernel, n_pages=n_pages, TQ=TQ),
        out_shape=(jax.ShapeDtypeStruct((B * TQ, NSA_Q_W), BF16),
                   jax.ShapeDtypeStruct((B, LANES, n_buf), F32),
                   jax.ShapeDtypeStruct((B, LANES, n_buf), F32)),
        grid_spec=pltpu.PrefetchScalarGridSpec(
            num_scalar_prefetch=1,
            grid=(B,),
            in_specs=[
                pl.BlockSpec((TQ, NSA_Q_W), lambda b, pt: (b, 3)),
                pl.BlockSpec((TQ, LANES), lambda b, pt: (b, AB_SMALL_BLK)),
                pl.BlockSpec((1, n_cmp, LANES), lambda b, pt: (b, 0, 0)),
                pl.BlockSpec((1, n_cmp, LANES), lambda b, pt: (b, 0, 0)),
                pl.BlockSpec(memory_space=pl.ANY),
                pl.BlockSpec(memory_space=pl.ANY),
                kvn(2), kvn(3), win, win, kvn(4), kvn(5),
            ],
            out_specs=(pl.BlockSpec((TQ, NSA_Q_W), lambda b, pt: (b, 0)), win, win),
            scratch_shapes=[
                pltpu.VMEM((past, LANES), F32),
                pltpu.VMEM((past, LANES), F32),
                pltpu.VMEM((n_pages, R2, PAGE_SIZE), F32),
                pltpu.SemaphoreType.DMA((2,)),
            ],
        ),
        compiler_params=_cparams(("arbitrary",)),
        name="nsa_sample",
    )(page_table, proj, proj, kc, vc, pool_k, pool_v, proj, proj, win_k, win_v, proj, proj)


def _gdn_kernel(qkv_ref, z_ref, sm_ref, cw_ref, alog_ref, dtb_ref, ng_ref, s0_ref, cb_ref,
                o_ref, sfin_ref, S_sc, prev_sc, *, C, NB):
    t = pl.program_id(1)

    @pl.when(t == 0)
    def _():
        S_sc[...] = s0_ref[...]
        prev_sc[...] = cb_ref[...]

    def conv(n, c0):
        xe = jnp.concatenate([prev_sc[n, :, c0:c0 + LANES], qkv_ref[n, :, c0:c0 + LANES]], axis=0)
        y = xe[SUBLANES:] * cw_ref[GDN_CONV - 1:GDN_CONV, c0:c0 + LANES]
        for s in range(1, GDN_CONV):
            y = y + pltpu.roll(xe, s, axis=0)[SUBLANES:] * cw_ref[GDN_CONV - 1 - s:GDN_CONV - s, c0:c0 + LANES]
        return _silu(y)

    ii = lax.broadcasted_iota(jnp.int32, (C, C), 0)
    jj = lax.broadcasted_iota(jnp.int32, (C, C), 1)
    eye = (ii == jj).astype(F32)

    CH = [(n, h) for n in range(NB) for h in range(GDN_HEADS)]
    X = range(len(CH))
    beta_c, d_c, d_r = [], [], []
    for n in range(NB):
        sm = sm_ref[n]
        beta = _sigmoid(sm)
        gt = -jnp.exp(alog_ref[...]) * _softplus(sm + dtb_ref[...])
        d = _cumsum_rows(gt)
        dT = d.T
        for h in range(GDN_HEADS):
            beta_c.append(beta[:, h:h + 1])
            d_c.append(d[:, GDN_A_LANE0 + h:GDN_A_LANE0 + h + 1])
            d_r.append(dT[GDN_A_LANE0 + h:GDN_A_LANE0 + h + 1, :])
    q, k, v = [], [], []
    for n, h in CH:
        qh = conv(n, h * GDN_DK)
        kh = conv(n, GDN_W + h * GDN_DK)
        q.append(qh * lax.rsqrt(jnp.sum(qh * qh, axis=-1, keepdims=True) + L2_EPS) * (GDN_DK ** -0.5))
        k.append(kh * lax.rsqrt(jnp.sum(kh * kh, axis=-1, keepdims=True) + L2_EPS))
        v.append(conv(n, 2 * GDN_W + h * GDN_DV))
    decay = [jnp.exp(jnp.minimum(d_c[x] - d_r[x], 0.0)) for x in X]
    kb = [k[x] * beta_c[x] for x in X]
    g_kk = [_mm_nt(kb[x], k[x]) for x in X]
    g_qk = [_mm_nt(q[x], k[x]) for x in X]
    a = [jnp.where(ii > jj, g_kk[x] * decay[x], 0.0) for x in X]
    qk = [jnp.where(ii >= jj, g_qk[x] * decay[x], 0.0) for x in X]
    tinv = [eye - a[x] for x in X]
    p_split = [_split_bf16(a[x]) for x in X]
    n2 = 2
    while n2 < C:
        p = [_mm3(p_split[x], p_split[x]) for x in X]
        p_split = [_split_bf16(p[x]) for x in X]
        tinv = [tinv[x] + _mm3(_split_bf16(tinv[x]), p_split[x]) for x in X]
        n2 *= 2
    u = [_mm(tinv[x], v[x] * beta_c[x]) for x in X]
    w = [_mm(tinv[x], kb[x] * jnp.exp(d_c[x])) for x in X]
    S = [S_sc[n, h] for n, h in CH]
    w_s = [_mm(w[x], S[x]) for x in X]
    q_s = [_mm(q[x] * jnp.exp(d_c[x]), S[x]) for x in X]
    v_new = [u[x] - w_s[x] for x in X]
    o = [q_s[x] + _mm(qk[x], v_new[x]) for x in X]
    d_last = [d_c[x][C - 1:C, :] for x in X]
    upd = [_mm_tn(k[x] * jnp.exp(d_last[x] - d_c[x]), v_new[x]) for x in X]
    for x, (n, h) in enumerate(CH):
        S_sc[n, h] = S[x] * jnp.exp(d_last[x]) + upd[x]
        ms = jnp.mean(o[x] * o[x], axis=-1, keepdims=True)
        zh = z_ref[n, :, h * GDN_DV:(h + 1) * GDN_DV]
        o_ref[n, :, h * GDN_DV:(h + 1) * GDN_DV] = (
            o[x] * lax.rsqrt(ms + RMS_EPS) * ng_ref[...] * _silu(zh)).astype(o_ref.dtype)

    for n in range(NB):
        prev_sc[n] = qkv_ref[n, C - SUBLANES:C, :]

    @pl.when(t == pl.num_programs(1) - 1)
    def _():
        sfin_ref[...] = S_sc[...]


def _gdn(proj, B, T, conv_w, a_log, dt_bias, norm_g, s0, conv_buf8, C):
    nt = T // C
    NB = 2 if B % 2 == 0 else 1
    pad = lambda v: jnp.zeros((1, LANES), F32).at[0, GDN_A_LANE0:GDN_A_LANE0 + GDN_HEADS].set(v)
    proj3 = proj.reshape(B, T, C_COLS)
    o, s_fin = pl.pallas_call(
        functools.partial(_gdn_kernel, C=C, NB=NB),
        out_shape=(jax.ShapeDtypeStruct((B, T, GDN_W), BF16),
                   jax.ShapeDtypeStruct((B, GDN_HEADS, GDN_DK, GDN_DV), F32)),
        grid=(B // NB, nt),
        in_specs=[
            pl.BlockSpec((NB, C, GDN_CONV_CH), lambda b, t: (b, t, 0)),
            pl.BlockSpec((NB, C, GDN_W), lambda b, t: (b, t, 3)),
            pl.BlockSpec((NB, C, LANES), lambda b, t: (b, t, C_SMALL_BLK)),
            pl.BlockSpec((GDN_CONV, GDN_CONV_CH), lambda b, t: (0, 0)),
            pl.BlockSpec((1, LANES), lambda b, t: (0, 0)),
            pl.BlockSpec((1, LANES), lambda b, t: (0, 0)),
            pl.BlockSpec((1, GDN_DV), lambda b, t: (0, 0)),
            pl.BlockSpec((NB, GDN_HEADS, GDN_DK, GDN_DV), lambda b, t: (b, 0, 0, 0)),
            pl.BlockSpec((NB, SUBLANES, GDN_CONV_CH), lambda b, t: (b, 0, 0)),
        ],
        out_specs=(pl.BlockSpec((NB, C, GDN_W), lambda b, t: (b, t, 0)),
                   pl.BlockSpec((NB, GDN_HEADS, GDN_DK, GDN_DV), lambda b, t: (b, 0, 0, 0))),
        scratch_shapes=[
            pltpu.VMEM((NB, GDN_HEADS, GDN_DK, GDN_DV), F32),
            pltpu.VMEM((NB, SUBLANES, GDN_CONV_CH), F32),
        ],
        compiler_params=_cparams(("arbitrary", "arbitrary")),
        name="gdn",
    )(proj3, proj3, proj3, conv_w, pad(a_log), pad(dt_bias), norm_g.reshape(1, GDN_DV), s0, conv_buf8)
    return o.reshape(B * T, GDN_W), s_fin


def _ab_in_weight(w):
    big = w[:, :GLA_KEY_W * 2 + GLA_VAL_W * 2]
    gk = w[:, 1536:1536 + GLA_LOWRANK]
    rest = w[:, 1536 + GLA_LOWRANK:]
    q_b = rest[:, :NSA_Q_W]
    kv = rest[:, NSA_Q_W:NSA_Q_W + 6 * NSA_KV_W]
    gate = rest[:, NSA_Q_W + 6 * NSA_KV_W:]
    small = jnp.concatenate([gk, gate, jnp.zeros((D_MODEL, LANES - GLA_LOWRANK - 3 * NSA_HEADS), w.dtype)], axis=1)
    return jnp.concatenate([big, q_b, kv, small], axis=1).astype(BF16)


def _c_in_weight(w):
    qkv = w[:, :GDN_CONV_CH]
    ba = w[:, GDN_CONV_CH:GDN_CONV_CH + 2 * GDN_HEADS]
    z = w[:, GDN_CONV_CH + 2 * GDN_HEADS:]
    small = jnp.concatenate([ba, jnp.zeros((D_MODEL, LANES - 2 * GDN_HEADS), w.dtype)], axis=1)
    return jnp.concatenate([qkv, z, small], axis=1).astype(BF16)


def _kv_out(proj, B, T, j):
    return proj[:, (AB_KV_BLK + j) * LANES:(AB_KV_BLK + j + 1) * LANES].reshape(B, T, NSA_KV_HEADS, NSA_HEAD_DIM)


PROMPT_ROWS = 512
FFN_ROWS = 1024


def kernel(x_prompt, x_sample, c_prompt, c_sample, page_table, cache_cmp_k, cache_cmp_v, cache_sel_k, cache_sel_v, state_win_k, state_win_v, state_gla, state_gdn, state_gdn_conv, w_ada, b_ada, ln_g, ln_b, w_ffn_in, w_ffn_out, ab_w_in, ab_w_gk2, ab_b_gk, ab_gla_norm, ab_cmp_pe, ab_cmp_w1, ab_cmp_w2, ab_w_out, c_w_in, c_conv_w, c_a_log, c_dt_bias, c_norm, c_w_out):
    Bp, Tp, _ = x_prompt.shape
    Bs, Ts, _ = x_sample.shape
    n_pool = cache_cmp_k.shape[1]

    mods = _adaln(jnp.concatenate([c_prompt, c_sample], axis=0), w_ada, b_ada)

    def layer_mods(layer):
        m = mods[layer]
        parts = [m[:, i * D_MODEL:(i + 1) * D_MODEL] for i in range(6)]
        return [p[:Bp] for p in parts], [p[Bp:] for p in parts]

    xp, xs = x_prompt, x_sample
    ab_p, ab_s, c_p, c_s = [], [], [], []
    for layer in range(DEPTH):
        mp, ms = layer_mods(layer)
        wf_in = w_ffn_in[layer].astype(BF16)
        wf_out = w_ffn_out[layer].astype(BF16)
        i = layer // 2
        if layer % 2 == 0:
            w_in = _ab_in_weight(ab_w_in[i])
            w_out = ab_w_out[i].astype(BF16)
            wo_a, wo_b = w_out[:GLA_VAL_W], w_out[GLA_VAL_W:]
            cw = _cmp_weights(ab_cmp_pe[i], ab_cmp_w1[i], ab_cmp_w2[i])
            cw_t = _cmp_weights_t(ab_cmp_pe[i], ab_cmp_w1[i], ab_cmp_w2[i])

            proj, kv_t = _modmm(xp, mp[0], mp[1], w_in, PROMPT_ROWS, t_cols=(AB_KV_BLK * LANES, 6))
            zero_state = jnp.zeros((Bp, GLA_HEADS, GLA_DK, GLA_DV), F32)
            o_a, s_a = _gla(proj, Bp, Tp, ab_w_gk2[i], ab_b_gk[i], ab_gla_norm[i], zero_state, min(64, Tp))
            kc, vc = _compress_dense(proj, Bp, Tp, cw)
            o_b = _nsa_prompt(proj, Bp, Tp, kc, vc)
            x1 = _outproj_ln([o_a, o_b], [wo_a, wo_b], xp, mp[2], ln_g[layer, 0], ln_b[layer, 0], PROMPT_ROWS)
            n_keep = min(NSA_WINDOW, Tp)
            kv_out_t = lambda a: jnp.transpose(a.reshape(Bp, NSA_KV_HEADS, NSA_HEAD_DIM, a.shape[-1]), (0, 3, 1, 2))
            ab_p.append(tuple(kv_out_t(kv_t[j]) for j in range(4))
                        + (kv_out_t(kv_t[4][:, :, Tp - n_keep:]), kv_out_t(kv_t[5][:, :, Tp - n_keep:]), s_a))
            xp = _ffn_ln(x1, mp[3], mp[4], mp[5], wf_in, wf_out, ln_g[layer, 1], ln_b[layer, 1], FFN_ROWS)

            proj = _modmm(xs, ms[0], ms[1], w_in, PROMPT_ROWS)
            o_a, s_a = _gla(proj, Bs, Ts, ab_w_gk2[i], ab_b_gk[i], ab_gla_norm[i], state_gla[i], min(64, Ts))
            pool = lambda c: jnp.transpose(c[i], (0, 2, 3, 1)).reshape(n_pool * PAGE_SIZE, LANES)
            n_buf = state_win_k.shape[2]
            win_t = lambda w: jnp.transpose(w[i], (0, 2, 3, 1)).reshape(Bs, LANES, n_buf)
            kc, vc = _compress_paged(page_table, pool(cache_cmp_k), pool(cache_cmp_v), cw_t)
            o_b, win_k, win_v = _nsa_sample(proj, page_table, kc, vc, pool(cache_sel_k), pool(cache_sel_v),
                                            win_t(state_win_k), win_t(state_win_v))
            win_out = lambda w: jnp.transpose(w.reshape(Bs, NSA_KV_HEADS, NSA_HEAD_DIM, n_buf), (0, 3, 1, 2))
            x1 = _outproj_ln([o_a, o_b], [wo_a, wo_b], xs, ms[2], ln_g[layer, 0], ln_b[layer, 0], PROMPT_ROWS)
            ab_s.append(tuple(_kv_out(proj, Bs, Ts, j) for j in range(4))
                        + (win_out(win_k), win_out(win_v), s_a))
            xs = _ffn_ln(x1, ms[3], ms[4], ms[5], wf_in, wf_out, ln_g[layer, 1], ln_b[layer, 1], FFN_ROWS)
        else:
            w_in = _c_in_weight(c_w_in[i])
            w_out = c_w_out[i].astype(BF16)
            keep = GDN_CONV - 1

            proj = _modmm(xp, mp[0], mp[1], w_in, PROMPT_ROWS // 2)
            o_c, s_c = _gdn(proj, Bp, Tp, c_conv_w[i], c_a_log[i], c_dt_bias[i], c_norm[i],
                            jnp.zeros((Bp, GDN_HEADS, GDN_DK, GDN_DV), F32),
                            jnp.zeros((Bp, SUBLANES, GDN_CONV_CH), F32), min(64, Tp))
            x1 = _outproj_ln([o_c], [w_out], xp, mp[2], ln_g[layer, 0], ln_b[layer, 0], PROMPT_ROWS)
            c_p.append((s_c, proj.reshape(Bp, Tp, C_COLS)[:, Tp - keep:, :GDN_CONV_CH]))
            xp = _ffn_ln(x1, mp[3], mp[4], mp[5], wf_in, wf_out, ln_g[layer, 1], ln_b[layer, 1], FFN_ROWS)

            proj = _modmm(xs, ms[0], ms[1], w_in, PROMPT_ROWS // 2)
            conv8 = jnp.concatenate([jnp.zeros((Bs, SUBLANES - keep, GDN_CONV_CH), F32), state_gdn_conv[i]], axis=1)
            o_c, s_c = _gdn(proj, Bs, Ts, c_conv_w[i], c_a_log[i], c_dt_bias[i], c_norm[i],
                            state_gdn[i], conv8, min(64, Ts))
            x1 = _outproj_ln([o_c], [w_out], xs, ms[2], ln_g[layer, 0], ln_b[layer, 0], PROMPT_ROWS)
            qkv_s = proj.reshape(Bs, Ts, C_COLS)[:, :, :GDN_CONV_CH]
            c_s.append((s_c, jnp.concatenate([state_gdn_conv[i], qkv_s], axis=1)[:, -keep:]))
            xs = _ffn_ln(x1, ms[3], ms[4], ms[5], wf_in, wf_out, ln_g[layer, 1], ln_b[layer, 1], FFN_ROWS)

    stack = lambda sts: [jnp.stack(z) for z in zip(*sts)]
    p_ab, s_ab = stack(ab_p), stack(ab_s)
    p_c, s_c = stack(c_p), stack(c_s)
    return (xp, xs, *p_ab, *p_c, *s_ab, *s_c)
```

```python
import functools

import jax
import jax.numpy as jnp
from jax import lax
from jax.experimental import pallas as pl
from jax.experimental.pallas import tpu as pltpu

F32 = jnp.float32
BF16 = jnp.bfloat16

D_MODEL = 1024
DEPTH = 2
PAGE_SIZE = 128
GLA_HEADS = 4
GLA_DK = 64
GLA_DV = 128
GLA_LOWRANK = 16
GLA_GATE_NORM = 16.0
NSA_HEAD_DIM = 64
NSA_HEADS = 8
NSA_KV_HEADS = 2
NSA_HPG = 4
NSA_BLOCK = 64
NSA_TOP_K = 16
NSA_WINDOW = 512
NSA_CMP_HIDDEN = 128
FORCED_SCORE = 1000.0
GDN_HEADS = 8
GDN_DK = 128
GDN_DV = 128
GDN_CONV = 4
FF_HIDDEN = 2816
DEEPNORM_ALPHA = (2.0 * DEPTH) ** 0.25
LN_EPS = 1e-5
RMS_EPS = 1e-6
L2_EPS = 1e-6
NEG = -1e30

GLA_KEY_W = GLA_HEADS * GLA_DK
GLA_VAL_W = GLA_HEADS * GLA_DV
NSA_Q_W = NSA_HEADS * NSA_HEAD_DIM
NSA_KV_W = NSA_KV_HEADS * NSA_HEAD_DIM
GDN_W = GDN_HEADS * GDN_DK
GDN_CONV_CH = 3 * GDN_W

LANES = 128
SUBLANES = 8
VMEM_LIMIT = 56 * 1024 * 1024

AB_COLS = 2944
AB_SMALL_BLK = 22
AB_KV_BLK = 16
GATE_LANE0 = GLA_LOWRANK
C_COLS = 4224
C_SMALL_BLK = 32
GDN_A_LANE0 = GDN_HEADS
FEAT_BLK = 64
FEAT_OFF = 65
FEAT_PAD = 66
SEL_BIG = 131072.0


def _cparams(sem):
    return pltpu.CompilerParams(dimension_semantics=sem, vmem_limit_bytes=VMEM_LIMIT)


def _silu(x):
    return x * (1.0 / (1.0 + jnp.exp(-x)))


def _sigmoid(x):
    return 1.0 / (1.0 + jnp.exp(-x))


def _softplus(x):
    return jnp.maximum(x, 0.0) + jnp.log(1.0 + jnp.exp(-jnp.abs(x)))


def _mm(a, b):
    return jnp.dot(a.astype(BF16), b.astype(BF16), preferred_element_type=F32)


def _split_bf16(x):
    hi = x.astype(BF16)
    return hi, (x - hi.astype(F32)).astype(BF16)


def _mm3(a, b):
    dot = lambda x, y: jnp.dot(x, y, preferred_element_type=F32)
    return dot(a[0], b[0]) + (dot(a[1], b[0]) + dot(a[0], b[1]))


def _mm_nt(a, b):
    return lax.dot_general(a.astype(BF16), b.astype(BF16), (((1,), (1,)), ((), ())),
                           preferred_element_type=F32)


def _mm_tn(a, b):
    return lax.dot_general(a.astype(BF16), b.astype(BF16), (((0,), (0,)), ((), ())),
                           preferred_element_type=F32)


def _cumsum_rows(x):
    n = x.shape[0]
    row = lax.broadcasted_iota(jnp.int32, x.shape, 0)
    s = 1
    while s < n:
        x = x + jnp.where(row >= s, pltpu.roll(x, s, axis=0), 0.0)
        s *= 2
    return x


def _masked_softmax(s, mask):
    s = jnp.where(mask, s, NEG)
    m = jnp.max(s, axis=-1, keepdims=True)
    e = jnp.where(mask, jnp.exp(s - m), 0.0)
    den = jnp.maximum(jnp.sum(e, axis=-1, keepdims=True), 1e-30)
    return e * (1.0 / den)


def _layernorm(z, g, b):
    mu = jnp.mean(z, axis=-1, keepdims=True)
    zc = z - mu
    var = jnp.mean(zc * zc, axis=-1, keepdims=True)
    return zc * lax.rsqrt(var + LN_EPS) * g + b


def _adaln_kernel(c_ref, w_ref, b_ref, o_ref):
    c = _silu(c_ref[...])
    o_ref[0] = _mm(c, w_ref[0]) + b_ref[0]


def _adaln(c_all, w_ada, b_ada):
    n = c_all.shape[0]
    tn = 1536
    nt = (6 * D_MODEL) // tn
    return pl.pallas_call(
        _adaln_kernel,
        out_shape=jax.ShapeDtypeStruct((DEPTH, n, 6 * D_MODEL), F32),
        grid=(DEPTH, nt),
        in_specs=[
            pl.BlockSpec((n, D_MODEL), lambda l, j: (0, 0)),
            pl.BlockSpec((1, D_MODEL, tn), lambda l, j: (l, 0, j)),
            pl.BlockSpec((1, 1, tn), lambda l, j: (l, 0, j)),
        ],
        out_specs=pl.BlockSpec((1, n, tn), lambda l, j: (l, 0, j)),
        compiler_params=_cparams(("arbitrary", "arbitrary")),
        name="adaln",
    )(c_all, w_ada, b_ada.reshape(DEPTH, 1, 6 * D_MODEL))


def _modmm_kernel(x_ref, sh_ref, sc_ref, w_ref, o_ref, *t_refs, t_col0):
    bb, tt, d = x_ref.shape
    h = x_ref[...] * (1.0 + sc_ref[...]) + sh_ref[...]
    res = _mm(h.reshape(bb * tt, d), w_ref[...])
    o_ref[...] = res
    for t_ref in t_refs:
        for j in range(t_ref.shape[0]):
            t_ref[j, 0] = res[:, t_col0 + j * LANES:t_col0 + (j + 1) * LANES].T


def _row_tiling(B, T, max_rows):
    if T >= max_rows:
        return 1, max_rows
    bb = min(B, max_rows // T)
    return bb, T


def _modmm(x, shift, scale, w_bf16, max_rows, t_cols=None):
    B, T, D = x.shape
    N = w_bf16.shape[1]
    bb, tt = _row_tiling(B, T, max_rows)
    nt = T // tt
    out_shape = jax.ShapeDtypeStruct((B * T, N), F32)
    out_specs = pl.BlockSpec((bb * tt, N), lambda i, j: (i * nt + j, 0))
    t_col0 = 0
    if t_cols is not None:
        assert bb == 1
        t_col0, n_t = t_cols
        out_shape = (out_shape, jax.ShapeDtypeStruct((n_t, B, LANES, T), F32))
        out_specs = (out_specs, pl.BlockSpec((n_t, 1, LANES, tt), lambda i, j: (0, i, 0, j)))
    return pl.pallas_call(
        functools.partial(_modmm_kernel, t_col0=t_col0),
        out_shape=out_shape,
        grid=(B // bb, nt),
        in_specs=[
            pl.BlockSpec((bb, tt, D), lambda i, j: (i, j, 0)),
            pl.BlockSpec((bb, 1, D), lambda i, j: (i, 0, 0)),
            pl.BlockSpec((bb, 1, D), lambda i, j: (i, 0, 0)),
            pl.BlockSpec((D, N), lambda i, j: (0, 0)),
        ],
        out_specs=out_specs,
        compiler_params=_cparams(("arbitrary", "arbitrary")),
        name="modmm",
    )(x, shift[:, None, :], scale[:, None, :], w_bf16)


def _outproj_kernel(*refs, n_in):
    a_refs = refs[:n_in]
    w_refs = refs[n_in:2 * n_in]
    x_ref, gate_ref, g_ref, b_ref, o_ref = refs[2 * n_in:]
    bb, tt, d = x_ref.shape
    acc = _mm(a_refs[0][...], w_refs[0][...])
    for a_ref, w_ref in zip(a_refs[1:], w_refs[1:]):
        acc = acc + _mm(a_ref[...], w_ref[...])
    z = DEEPNORM_ALPHA * x_ref[...] + gate_ref[...] * acc.reshape(bb, tt, d)
    o_ref[...] = _layernorm(z, g_ref[...], b_ref[...])


def _outproj_ln(acts, ws, x, gate, ln_g, ln_b, max_rows):
    B, T, D = x.shape
    bb, tt = _row_tiling(B, T, max_rows)
    nt = T // tt
    n_in = len(acts)
    in_specs = []
    for a in acts:
        in_specs.append(pl.BlockSpec((bb * tt, a.shape[1]), lambda i, j: (i * nt + j, 0)))
    for w in ws:
        in_specs.append(pl.BlockSpec(w.shape, lambda i, j: (0, 0)))
    in_specs += [
        pl.BlockSpec((bb, tt, D), lambda i, j: (i, j, 0)),
        pl.BlockSpec((bb, 1, D), lambda i, j: (i, 0, 0)),
        pl.BlockSpec((1, 1, D), lambda i, j: (0, 0, 0)),
        pl.BlockSpec((1, 1, D), lambda i, j: (0, 0, 0)),
    ]
    return pl.pallas_call(
        functools.partial(_outproj_kernel, n_in=n_in),
        out_shape=jax.ShapeDtypeStruct((B, T, D), F32),
        grid=(B // bb, nt),
        in_specs=in_specs,
        out_specs=pl.BlockSpec((bb, tt, D), lambda i, j: (i, j, 0)),
        compiler_params=_cparams(("arbitrary", "arbitrary")),
        name="outproj_ln",
    )(*acts, *ws, x, gate[:, None, :], ln_g.reshape(1, 1, D), ln_b.reshape(1, 1, D))


def _ffn_kernel(x_ref, sh_ref, sc_ref, gate_ref, wa_ref, wu_ref, wo_ref, g_ref, b_ref, o_ref,
                xm_sc, acc_sc):
    j = pl.program_id(2)
    bb, tt, d = x_ref.shape

    @pl.when(j == 0)
    def _():
        h = x_ref[...] * (1.0 + sc_ref[...]) + sh_ref[...]
        xm_sc[...] = h.reshape(bb * tt, d).astype(BF16)
        acc_sc[...] = jnp.zeros_like(acc_sc)

    xm = xm_sc[...]
    a = jnp.dot(xm, wa_ref[...], preferred_element_type=F32)
    u = jnp.dot(xm, wu_ref[...], preferred_element_type=F32)
    acc_sc[...] += _mm(_silu(a) * u, wo_ref[...])

    @pl.when(j == pl.num_programs(2) - 1)
    def _():
        z = DEEPNORM_ALPHA * x_ref[...] + gate_ref[...] * acc_sc[...].reshape(bb, tt, d)
        o_ref[...] = _layernorm(z, g_ref[...], b_ref[...])


def _ffn_ln(x, shift, scale, gate, w_in_bf16, w_out_bf16, ln_g, ln_b, max_rows):
    B, T, D = x.shape
    bb, tt = _row_tiling(B, T, max_rows)
    nt = T // tt
    th = 256
    nh = FF_HIDDEN // th
    vec = lambda v: v[:, None, :]
    return pl.pallas_call(
        _ffn_kernel,
        out_shape=jax.ShapeDtypeStruct((B, T, D), F32),
        grid=(B // bb, nt, nh),
        in_specs=[
            pl.BlockSpec((bb, tt, D), lambda i, t, j: (i, t, 0)),
            pl.BlockSpec((bb, 1, D), lambda i, t, j: (i, 0, 0)),
            pl.BlockSpec((bb, 1, D), lambda i, t, j: (i, 0, 0)),
            pl.BlockSpec((bb, 1, D), lambda i, t, j: (i, 0, 0)),
            pl.BlockSpec((D, th), lambda i, t, j: (0, j)),
            pl.BlockSpec((D, th), lambda i, t, j: (0, nh + j)),
            pl.BlockSpec((th, D), lambda i, t, j: (j, 0)),
            pl.BlockSpec((1, 1, D), lambda i, t, j: (0, 0, 0)),
            pl.BlockSpec((1, 1, D), lambda i, t, j: (0, 0, 0)),
        ],
        out_specs=pl.BlockSpec((bb, tt, D), lambda i, t, j: (i, t, 0)),
        scratch_shapes=[pltpu.VMEM((bb * tt, D), BF16), pltpu.VMEM((bb * tt, D), F32)],
        compiler_params=_cparams(("arbitrary", "arbitrary", "arbitrary")),
        name="ffn_ln",
    )(x, vec(shift), vec(scale), vec(gate), w_in_bf16, w_in_bf16, w_out_bf16,
      ln_g.reshape(1, 1, D), ln_b.reshape(1, 1, D))


def _gla_kernel(q_ref, k_ref, v_ref, r_ref, sm_ref, wgk_ref, bgk_ref, gn_ref, s0_ref,
                o_ref, sfin_ref, S_sc, q_sc, k_sc, b_sc, v_sc, o_sc, *, C, NB):
    t = pl.program_id(1)
    KW, VW = GLA_KEY_W, GLA_VAL_W

    hk = lax.broadcasted_iota(jnp.int32, (KW, VW), 0) // GLA_DK
    hv = lax.broadcasted_iota(jnp.int32, (KW, VW), 1) // GLA_DV
    same_head = hk == hv

    @pl.when(t == 0)
    def _():
        for n in range(NB):
            rows = []
            for h in range(GLA_HEADS):
                pieces = [s0_ref[n, h] if h2 == h else jnp.zeros((GLA_DK, GLA_DV), F32)
                          for h2 in range(GLA_HEADS)]
                rows.append(jnp.concatenate(pieces, axis=1))
            S_sc[n] = jnp.concatenate(rows, axis=0)

    seg = same_head.astype(BF16)
    JG = min(C, 2 * SUBLANES)
    local = lax.broadcasted_iota(jnp.int32, (JG, KW), 0)
    S_new = []
    for n in range(NB):
        gk = sm_ref[n, :, 0:GLA_LOWRANK]
        pre = _mm(gk, wgk_ref[...]) + bgk_ref[...]
        log_a = (jnp.minimum(pre, 0.0) - jnp.log(1.0 + jnp.exp(-jnp.abs(pre)))) * (1.0 / GLA_GATE_NORM)
        b = _cumsum_rows(log_a)
        q = q_ref[n] * (GLA_DK ** -0.5)
        k = k_ref[n]
        v = v_ref[n]
        q_sc[n] = q
        k_sc[n] = k
        b_sc[n] = b
        v_sc[n] = v

        S = S_sc[n]
        o_sc[n] = _mm(q * jnp.exp(b), S)

        for r0 in range(0, C, JG):
            rows = C - r0
            qg = q_sc[n, r0:C, :]
            bg = b_sc[n, r0:C, :]
            ps = []
            for jj in range(JG):
                j = r0 + jj
                p = qg * k_sc[n, j:j + 1, :] * jnp.exp(jnp.minimum(bg - b_sc[n, j:j + 1, :], 0.0))
                head = jnp.where(local >= jj, p[0:JG], 0.0)
                p = head if rows == JG else jnp.concatenate([head, p[JG:]], axis=0)
                ps.append(p.astype(BF16))
            s = jnp.dot(jnp.concatenate(ps, axis=0), seg, preferred_element_type=F32)
            contrib = s[0:rows] * v_sc[n, r0:r0 + 1, :]
            for jj in range(1, JG):
                contrib = contrib + s[jj * rows:(jj + 1) * rows] * v_sc[n, r0 + jj:r0 + jj + 1, :]
            o_sc[n, r0:C, :] += contrib
        o = o_sc[n]

        b_last = b[C - 1:C, :]
        kd = k * jnp.exp(b_last - b)
        upd = _mm_tn(kd, v)
        tail = jnp.broadcast_to(b_last, (SUBLANES, KW))
        dcol = jnp.exp(tail.T[:, 0:1])
        S_new.append(S * dcol + jnp.where(same_head, upd, 0.0))
        S_sc[n] = S_new[n]

        outs = []
        for h in range(GLA_HEADS):
            oh = o[:, h * GLA_DV:(h + 1) * GLA_DV]
            ms = jnp.mean(oh * oh, axis=-1, keepdims=True)
            rh = r_ref[n, :, h * GLA_DV:(h + 1) * GLA_DV]
            outs.append(oh * lax.rsqrt(ms + RMS_EPS) * gn_ref[...] * _silu(rh))
        o_ref[n] = jnp.concatenate(outs, axis=1).astype(o_ref.dtype)

    @pl.when(t == pl.num_programs(1) - 1)
    def _():
        for n in range(NB):
            for h in range(GLA_HEADS):
                sfin_ref[n, h] = S_new[n][h * GLA_DK:(h + 1) * GLA_DK, h * GLA_DV:(h + 1) * GLA_DV]


def _gla(proj, B, T, w_gk2, b_gk, gla_norm, s0, C):
    nt = T // C
    NB = 2 if B % 2 == 0 else 1
    proj3 = proj.reshape(B, T, AB_COLS)
    o, s_fin = pl.pallas_call(
        functools.partial(_gla_kernel, C=C, NB=NB),
        out_shape=(jax.ShapeDtypeStruct((B, T, GLA_VAL_W), BF16),
                   jax.ShapeDtypeStruct((B, GLA_HEADS, GLA_DK, GLA_DV), F32)),
        grid=(B // NB, nt),
        in_specs=[
            pl.BlockSpec((NB, C, GLA_KEY_W), lambda b, t: (b, t, 0)),
            pl.BlockSpec((NB, C, GLA_KEY_W), lambda b, t: (b, t, 1)),
            pl.BlockSpec((NB, C, GLA_VAL_W), lambda b, t: (b, t, 1)),
            pl.BlockSpec((NB, C, GLA_VAL_W), lambda b, t: (b, t, 2)),
            pl.BlockSpec((NB, C, LANES), lambda b, t: (b, t, AB_SMALL_BLK)),
            pl.BlockSpec((GLA_LOWRANK, GLA_KEY_W), lambda b, t: (0, 0)),
            pl.BlockSpec((1, GLA_KEY_W), lambda b, t: (0, 0)),
            pl.BlockSpec((1, GLA_DV), lambda b, t: (0, 0)),
            pl.BlockSpec((NB, GLA_HEADS, GLA_DK, GLA_DV), lambda b, t: (b, 0, 0, 0)),
        ],
        out_specs=(pl.BlockSpec((NB, C, GLA_VAL_W), lambda b, t: (b, t, 0)),
                   pl.BlockSpec((NB, GLA_HEADS, GLA_DK, GLA_DV), lambda b, t: (b, 0, 0, 0))),
        scratch_shapes=[
            pltpu.VMEM((NB, GLA_KEY_W, GLA_VAL_W), F32),
            pltpu.VMEM((NB, C, GLA_KEY_W), F32),
            pltpu.VMEM((NB, C, GLA_KEY_W), F32),
            pltpu.VMEM((NB, C, GLA_KEY_W), F32),
            pltpu.VMEM((NB, C, GLA_VAL_W), F32),
            pltpu.VMEM((NB, C, GLA_VAL_W), F32),
        ],
        compiler_params=_cparams(("arbitrary", "arbitrary")),
        name="gla",
    )(proj3, proj3, proj3, proj3, proj3, w_gk2, b_gk.reshape(1, GLA_KEY_W), gla_norm.reshape(1, GLA_DV), s0)
    return o.reshape(B * T, GLA_VAL_W), s_fin


def _compress_pages(x_ref, n_pages, pe_ref, w1_ref, w2_ref):
    outs = []
    for half in range(PAGE_SIZE // NSA_BLOCK):
        pieces = [x_ref[pl.ds(half * NSA_BLOCK + tk, n_pages, stride=PAGE_SIZE), :] for tk in range(NSA_BLOCK)]
        flat = jnp.concatenate(pieces, axis=1) + pe_ref[...]
        acc = _mm(flat, w1_ref[...])
        outs.append(_mm(_silu(acc), w2_ref[...]))
    return jnp.concatenate(outs, axis=1)


def _compress_dense_kernel(xk_ref, xv_ref, pek_ref, pev_ref, w1k_ref, w1v_ref, w2k_ref, w2v_ref,
                           ok_ref, ov_ref, *, n_pages):
    ok_ref[0] = _compress_pages(xk_ref, n_pages, pek_ref, w1k_ref, w2k_ref)
    ov_ref[0] = _compress_pages(xv_ref, n_pages, pev_ref, w1v_ref, w2v_ref)


def _cmp_weights(cmp_pe, cmp_w1, cmp_w2):
    out = []
    for i in range(2):
        pe2 = jnp.concatenate([cmp_pe[i], cmp_pe[i]], axis=1).reshape(1, NSA_BLOCK * LANES)
        w1 = cmp_w1[i].reshape(NSA_BLOCK, NSA_HEAD_DIM, NSA_CMP_HIDDEN)
        z1 = jnp.zeros_like(w1)
        w1bd = jnp.concatenate([jnp.concatenate([w1, z1], axis=2),
                                jnp.concatenate([z1, w1], axis=2)], axis=1).astype(BF16)
        w1bd = w1bd.reshape(NSA_BLOCK * LANES, 2 * NSA_CMP_HIDDEN)
        w2 = cmp_w2[i]
        z2 = jnp.zeros_like(w2)
        w2bd = jnp.concatenate([jnp.concatenate([w2, z2], axis=1),
                                jnp.concatenate([z2, w2], axis=1)], axis=0).astype(BF16)
        out.append((pe2, w1bd, w2bd))
    return out


def _compress_dense(proj, B, T, cw):
    n_pages = T // PAGE_SIZE
    (pek, w1k, w2k), (pev, w1v, w2v) = cw
    full = lambda a: pl.BlockSpec(a.shape, lambda b: (0,) * a.ndim)
    ok, ov = pl.pallas_call(
        functools.partial(_compress_dense_kernel, n_pages=n_pages),
        out_shape=(jax.ShapeDtypeStruct((B, n_pages, 2 * LANES), F32),) * 2,
        grid=(B,),
        in_specs=[
            pl.BlockSpec((T, LANES), lambda b: (b, AB_KV_BLK)),
            pl.BlockSpec((T, LANES), lambda b: (b, AB_KV_BLK + 1)),
            full(pek), full(pev), full(w1k), full(w1v), full(w2k), full(w2v),
        ],
        out_specs=(pl.BlockSpec((1, n_pages, 2 * LANES), lambda b: (b, 0, 0)),) * 2,
        compiler_params=_cparams(("arbitrary",)),
        name="compress_dense",
    )(proj, proj, pek, pev, w1k, w1v, w2k, w2v)
    n_blk = T // NSA_BLOCK
    return ok.reshape(B, n_blk, LANES), ov.reshape(B, n_blk, LANES)


def _gather_pages(pt_ref, b, pool_ref, buf_ref, sem, n_pages, start):
    def body(p, carry):
        page = pt_ref[b, p]
        cp = pltpu.make_async_copy(pool_ref.at[pl.ds(page * PAGE_SIZE, PAGE_SIZE), :],
                                   buf_ref.at[pl.ds(p * PAGE_SIZE, PAGE_SIZE), :], sem)
        if start:
            cp.start()
        else:
            cp.wait()
        return carry
    lax.fori_loop(0, n_pages, body, 0)


def _gather_pages_dmajor(pt_ref, b, pool_ref, buf_ref, sem, n_pages, start):
    def body(p, carry):
        page = pt_ref[b, p]
        cp = pltpu.make_async_copy(pool_ref.at[pl.ds(page * PAGE_SIZE, PAGE_SIZE), :],
                                   buf_ref.at[:, p, :], sem)
        if start:
            cp.start()
        else:
            cp.wait()
        return carry
    lax.fori_loop(0, n_pages, body, 0)


def _compress_pages_t(x_ref, n_pages, pe_ref, w1_ref, w2_ref):
    per_g = []
    for g in range(NSA_KV_HEADS):
        pieces = [x_ref[g * NSA_HEAD_DIM + d] for d in range(NSA_HEAD_DIM)]
        flat = jnp.concatenate(pieces, axis=1) + pe_ref[...]
        acc = _mm(flat, w1_ref[...])
        per_g.append(_mm(_silu(acc), w2_ref[...]))
    hd = NSA_HEAD_DIM
    return jnp.concatenate([per_g[0][:, 0:hd], per_g[1][:, 0:hd], per_g[0][:, hd:2 * hd], per_g[1][:, hd:2 * hd]], axis=1)


def _cmp_weights_t(cmp_pe, cmp_w1, cmp_w2):
    out = []
    for i in range(2):
        pe_t = jnp.concatenate([cmp_pe[i].T, cmp_pe[i].T], axis=1).reshape(1, NSA_HEAD_DIM * PAGE_SIZE)
        w1 = jnp.transpose(cmp_w1[i].reshape(NSA_BLOCK, NSA_HEAD_DIM, NSA_CMP_HIDDEN), (1, 0, 2))
        z1 = jnp.zeros_like(w1)
        w1t = jnp.concatenate([jnp.concatenate([w1, z1], axis=2),
                               jnp.concatenate([z1, w1], axis=2)], axis=1).astype(BF16)
        w1t = w1t.reshape(NSA_HEAD_DIM * PAGE_SIZE, 2 * NSA_CMP_HIDDEN)
        w2 = cmp_w2[i]
        z2 = jnp.zeros_like(w2)
        w2bd = jnp.concatenate([jnp.concatenate([w2, z2], axis=1),
                                jnp.concatenate([z2, w2], axis=1)], axis=0).astype(BF16)
        out.append((pe_t, w1t, w2bd))
    return out


def _compress_paged_kernel(pt_ref, poolk_ref, poolv_ref, pek_ref, pev_ref, w1k_ref, w1v_ref,
                           w2k_ref, w2v_ref, ok_ref, ov_ref, bufk, bufv, sems, *, n_pages):
    b = pl.program_id(0)
    _gather_pages_dmajor(pt_ref, b, poolk_ref, bufk, sems.at[0], n_pages, True)
    _gather_pages_dmajor(pt_ref, b, poolv_ref, bufv, sems.at[1], n_pages, True)
    _gather_pages_dmajor(pt_ref, b, poolk_ref, bufk, sems.at[0], n_pages, False)
    ok_ref[0] = _compress_pages_t(bufk, n_pages, pek_ref, w1k_ref, w2k_ref)
    _gather_pages_dmajor(pt_ref, b, poolv_ref, bufv, sems.at[1], n_pages, False)
    ov_ref[0] = _compress_pages_t(bufv, n_pages, pev_ref, w1v_ref, w2v_ref)


def _compress_paged(page_table, pool_k, pool_v, cw):
    B, n_pages = page_table.shape
    (pek, w1k, w2k), (pev, w1v, w2v) = cw
    full = lambda a: pl.BlockSpec(a.shape, lambda b, pt: (0,) * a.ndim)
    ok, ov = pl.pallas_call(
        functools.partial(_compress_paged_kernel, n_pages=n_pages),
        out_shape=(jax.ShapeDtypeStruct((B, n_pages, 2 * LANES), F32),) * 2,
        grid_spec=pltpu.PrefetchScalarGridSpec(
            num_scalar_prefetch=1,
            grid=(B,),
            in_specs=[
                pl.BlockSpec(memory_space=pl.ANY),
                pl.BlockSpec(memory_space=pl.ANY),
                full(pek), full(pev), full(w1k), full(w1v), full(w2k), full(w2v),
            ],
            out_specs=(pl.BlockSpec((1, n_pages, 2 * LANES), lambda b, pt: (b, 0, 0)),) * 2,
            scratch_shapes=[
                pltpu.VMEM((PAGE_SIZE, n_pages, LANES), F32),
                pltpu.VMEM((PAGE_SIZE, n_pages, LANES), F32),
                pltpu.SemaphoreType.DMA((2,)),
            ],
        ),
        compiler_params=_cparams(("arbitrary",)),
        name="compress_paged",
    )(page_table, pool_k, pool_v, pek, pev, w1k, w1v, w2k, w2v)
    n_blk = n_pages * (PAGE_SIZE // NSA_BLOCK)
    return ok.reshape(B, n_blk, LANES), ov.reshape(B, n_blk, LANES)


def _stack_queries(q, g, tq):
    rows = []
    for hl in range(NSA_HPG):
        qh = q[:, hl * NSA_HEAD_DIM:(hl + 1) * NSA_HEAD_DIM]
        rows.append(jnp.concatenate([qh, qh], axis=1))
    qs = jnp.concatenate(rows, axis=0) * (NSA_HEAD_DIM ** -0.5)
    half = lax.broadcasted_iota(jnp.int32, qs.shape, 1) // NSA_HEAD_DIM
    return jnp.where(half == g, qs, 0.0).astype(BF16)


def _row_slopes(g, tq):
    hl = lax.broadcasted_iota(jnp.int32, (NSA_HPG * tq, 1), 0) // tq
    s = jnp.where(hl == 0, 0.5, jnp.where(hl == 1, 0.25, jnp.where(hl == 2, 0.125, 0.0625)))
    return s * jnp.where(g == 0, 1.0, 0.0625)


def _gate_columns(sm, g, tq):
    sig = _sigmoid(sm)
    lane = lax.broadcasted_iota(jnp.int32, sm.shape, 1)
    cols = []
    for br in range(3):
        per_head = []
        for hl in range(NSA_HPG):
            target = GATE_LANE0 + 3 * (NSA_HPG * g + hl) + br
            per_head.append(jnp.sum(jnp.where(lane == target, sig, 0.0), axis=-1, keepdims=True))
        cols.append(jnp.concatenate(per_head, axis=0))
    return cols


def _topk_select(score, k_sel, n):
    idx = lax.broadcasted_iota(jnp.int32, score.shape, 1)
    rank = jnp.zeros(score.shape, F32)
    for j in range(n):
        col = score[:, j:j + 1]
        beats = (col > score) | ((col >= score) & (idx > j))
        rank = rank + jnp.where(beats, 1.0, 0.0)
    return rank < k_sel


def _topk_select_t(score, k_sel, n):
    st = score.T[0:n]
    idx = lax.broadcasted_iota(jnp.int32, st.shape, 0)
    rank = jnp.zeros(st.shape, F32)
    for j in range(n):
        row = st[j:j + 1, :]
        beats = (row > st) | ((row >= st) & (idx > j))
        rank = rank + jnp.where(beats, 1.0, 0.0)
    sel_t = jnp.where(rank < k_sel, 1.0, 0.0)
    sel_t = jnp.concatenate([sel_t, jnp.zeros((score.shape[1] - n, st.shape[1]), F32)], axis=0)
    return sel_t.T > 0.5


def _unstack_heads(o, g, tq):
    og = jnp.where(g == 0, o[:, 0:NSA_HEAD_DIM], o[:, NSA_HEAD_DIM:2 * NSA_HEAD_DIM])
    return jnp.concatenate([og[hl * tq:(hl + 1) * tq] for hl in range(NSA_HPG)], axis=1)


def _key_features(T, onehot):
    j = lax.broadcasted_iota(jnp.int32, (T, LANES), 0)
    lane = lax.broadcasted_iota(jnp.int32, (T, LANES), 1)
    blk = j // NSA_BLOCK
    f = jnp.where(lane == FEAT_BLK, blk.astype(F32),
                  jnp.where(lane == FEAT_OFF, (j % NSA_BLOCK).astype(F32), 0.0))
    if onehot:
        f = jnp.where(lane == blk, 1.0, f)
    return f.astype(BF16)


def _ones_column(rows):
    lane = lax.broadcasted_iota(jnp.int32, (rows, LANES), 1)
    return jnp.where(lane == 0, 1.0, 0.0).astype(BF16)


def _nsa_prompt_kernel(q_ref, sm_ref, kc_ref, vc_ref, ks_ref, vs_ref, kw_ref, vw_ref, o_ref,
                       ksb, vsb, kwb, vwb, s_sc, m_sc, acc_sc, *, T, TQ, WIN, CH, NB):
    qt = pl.program_id(1)
    R = NSA_HPG * TQ
    R2 = NSA_KV_HEADS * R
    n_blk = T // NSA_BLOCK
    G = range(NSA_KV_HEADS)
    N = range(NB)
    PAD = NSA_WINDOW

    @pl.when(qt == 0)
    def _():
        lane = lax.broadcasted_iota(jnp.int32, (PAD, LANES), 1)
        for n in N:
            ksb[n, :, 0:LANES] = ks_ref[n].astype(BF16)
            ksb[n, :, LANES:2 * LANES] = _key_features(T, True)
            vsb[n, :, 0:LANES] = vs_ref[n].astype(BF16)
            vsb[n, :, LANES:2 * LANES] = _ones_column(T)
            kwb[n, 0:PAD, 0:LANES] = jnp.zeros((PAD, LANES), BF16)
            kwb[n, 0:PAD, LANES:2 * LANES] = jnp.where(lane == FEAT_PAD, 1.0, 0.0).astype(BF16)
            kwb[n, PAD:PAD + T, 0:LANES] = kw_ref[n].astype(BF16)
            kwb[n, PAD:PAD + T, LANES:2 * LANES] = _key_features(T, False)
            vwb[n, 0:PAD, :] = jnp.zeros((PAD, 2 * LANES), BF16)
            vwb[n, PAD:PAD + T, 0:LANES] = vw_ref[n].astype(BF16)
            vwb[n, PAD:PAD + T, LANES:2 * LANES] = _ones_column(T)

    slope = jnp.concatenate([_row_slopes(g, TQ) for g in G], axis=0)
    off_q = lax.broadcasted_iota(jnp.int32, (R2, 1), 0) % TQ
    tq_i = qt * TQ + off_q
    tq_f = tq_i.astype(F32)
    lane = lax.broadcasted_iota(jnp.int32, (R2, LANES), 1)
    feat = jnp.where(lane == FEAT_BLK, slope * NSA_BLOCK,
                     jnp.where(lane == FEAT_OFF, slope, jnp.where(lane == FEAT_PAD, -SEL_BIG, 0.0)))
    feat_b = feat.astype(BF16)
    zpad = jnp.zeros((LANES - n_blk, LANES), F32)
    n_i = lax.broadcasted_iota(jnp.int32, (1, LANES), 1)
    center = (n_i * NSA_BLOCK).astype(F32) + 0.5 * (NSA_BLOCK - 1)
    cur0 = pl.multiple_of(qt * TQ, TQ)
    nt_dims = (((1,), (1,)), ((), ()))
    c_first = lax.broadcasted_iota(jnp.int32, (1, LANES), 1)
    c_tail = lax.broadcasted_iota(jnp.int32, (1, WIN - PAD), 1)
    tq1 = tq_i[0:NSA_KV_HEADS * TQ]
    cur = tq1 // NSA_BLOCK
    forced = (n_i == 0) | (n_i == cur) | (n_i == cur - 1)
    visible = n_i * NSA_BLOCK <= tq1
    off_k = lax.broadcasted_iota(jnp.int32, (1, TQ), 1)

    qs = [jnp.concatenate([_stack_queries(q_ref[n, :, g * 256:(g + 1) * 256], g, TQ) for g in G], axis=0) for n in N]
    q_plain = [jnp.concatenate([qs[n], feat_b], axis=1) for n in N]
    gates = [[_gate_columns(sm_ref[n], g, TQ) for g in G] for n in N]
    gcol = [[jnp.concatenate([gates[n][g][br] for g in G], axis=0) for br in range(3)] for n in N]

    s_c = [_mm_nt(qs[n], jnp.concatenate([kc_ref[n], zpad], axis=0)) - slope * (tq_f - center) for n in N]
    p_c = [_masked_softmax(s_c[n], (n_i * NSA_BLOCK + NSA_BLOCK - 1) <= tq_i) for n in N]
    o_c = [_mm(p_c[n], jnp.concatenate([vc_ref[n], zpad], axis=0)) for n in N]

    o_w = []
    for n in N:
        s_w = lax.dot_general(q_plain[n], kwb[n, pl.ds(cur0, WIN), :], nt_dims, preferred_element_type=F32)
        s_w = jnp.concatenate([jnp.where(c_first >= off_q, s_w[:, 0:LANES], NEG),
                               s_w[:, LANES:PAD],
                               jnp.where(c_tail <= off_q, s_w[:, PAD:WIN], NEG)], axis=1)
        e_w = jnp.exp(s_w - jnp.max(s_w, axis=-1, keepdims=True))
        acc_w = jnp.dot(e_w.astype(BF16), vwb[n, pl.ds(cur0, WIN), :], preferred_element_type=F32)
        o_w.append(acc_w[:, 0:LANES] * (1.0 / acc_w[:, LANES:LANES + 1]))

    q_sel, s_cur = [], []
    for n in N:
        scores = []
        for g in G:
            sc = p_c[n][g * R:g * R + TQ]
            for hl in range(1, NSA_HPG):
                sc = sc + p_c[n][g * R + hl * TQ:g * R + (hl + 1) * TQ]
            scores.append(sc)
        score = jnp.concatenate(scores, axis=0)
        score = jnp.where(visible, jnp.where(forced, FORCED_SCORE, score), -1.0)
        sel = _topk_select_t(score, min(NSA_TOP_K, n_blk), n_blk)
        sel_bias = jnp.where(sel & (n_i < cur), 0.0, -SEL_BIG)
        sel_bias = jnp.concatenate([sel_bias[g * TQ:(g + 1) * TQ] for g in G for _ in range(NSA_HPG)], axis=0)
        q_sel.append(jnp.concatenate([qs[n], jnp.where(lane >= FEAT_BLK, feat, sel_bias).astype(BF16)], axis=1))
        sc_own = lax.dot_general(q_plain[n], ksb[n, pl.ds(cur0, TQ), :], nt_dims, preferred_element_type=F32)
        s_cur.append(jnp.where(off_k <= off_q, sc_own, NEG))

    n_ch = T // CH
    m_sc[...] = jnp.full((NB, R2, LANES), NEG, F32)
    for c in range(n_ch):
        @pl.when(c * CH < cur0)
        def _():
            for n in N:
                s = lax.dot_general(q_sel[n], ksb[n, c * CH:(c + 1) * CH, :], nt_dims, preferred_element_type=F32)
                s_sc[n, :, c * CH:(c + 1) * CH] = s
                mm = m_sc[n]
                for i in range(CH // LANES):
                    mm = jnp.maximum(mm, s[:, i * LANES:(i + 1) * LANES])
                m_sc[n] = mm
    m = [jnp.maximum(jnp.max(m_sc[n], axis=-1, keepdims=True), jnp.max(s_cur[n], axis=-1, keepdims=True)) for n in N]
    for n in N:
        e_cur = jnp.exp(s_cur[n] - m[n])
        acc_sc[n] = jnp.dot(e_cur.astype(BF16), vsb[n, pl.ds(cur0, TQ), :], preferred_element_type=F32)
    for c in range(n_ch):
        @pl.when(c * CH < cur0)
        def _():
            for n in N:
                e = jnp.exp(s_sc[n, :, c * CH:(c + 1) * CH] - m[n])
                acc_sc[n] += jnp.dot(e.astype(BF16), vsb[n, c * CH:(c + 1) * CH, :], preferred_element_type=F32)
    for n in N:
        acc = acc_sc[n]
        o_s = acc[:, 0:LANES] * (1.0 / acc[:, LANES:LANES + 1])
        gc, gs, gw = gcol[n]
        o = gc * o_c[n] + gs * o_s + gw * o_w[n]
        o_ref[n] = jnp.concatenate([_unstack_heads(o[g * R:(g + 1) * R], g, TQ) for g in G], axis=1).astype(o_ref.dtype)


def _nsa_prompt(proj, B, T, kc, vc):
    TQ = NSA_BLOCK
    nq = T // TQ
    WIN = NSA_WINDOW + TQ
    CH = min(512, T)
    NB = 2 if B % 2 == 0 else 1
    R2 = NSA_KV_HEADS * NSA_HPG * TQ
    proj3 = proj.reshape(B, T, AB_COLS)
    kv = lambda j: pl.BlockSpec((NB, T, LANES), lambda b, t: (b, 0, AB_KV_BLK + j))
    n_blk = T // NSA_BLOCK
    o = pl.pallas_call(
        functools.partial(_nsa_prompt_kernel, T=T, TQ=TQ, WIN=WIN, CH=CH, NB=NB),
        out_shape=jax.ShapeDtypeStruct((B, T, NSA_Q_W), BF16),
        grid=(B // NB, nq),
        in_specs=[
            pl.BlockSpec((NB, TQ, NSA_Q_W), lambda b, t: (b, t, 3)),
            pl.BlockSpec((NB, TQ, LANES), lambda b, t: (b, t, AB_SMALL_BLK)),
            pl.BlockSpec((NB, n_blk, LANES), lambda b, t: (b, 0, 0)),
            pl.BlockSpec((NB, n_blk, LANES), lambda b, t: (b, 0, 0)),
            kv(2), kv(3), kv(4), kv(5),
        ],
        out_specs=pl.BlockSpec((NB, TQ, NSA_Q_W), lambda b, t: (b, t, 0)),
        scratch_shapes=[pltpu.VMEM((NB, T, 2 * LANES), BF16), pltpu.VMEM((NB, T, 2 * LANES), BF16),
                        pltpu.VMEM((NB, T + NSA_WINDOW, 2 * LANES), BF16),
                        pltpu.VMEM((NB, T + NSA_WINDOW, 2 * LANES), BF16),
                        pltpu.VMEM((NB, R2, T), F32), pltpu.VMEM((NB, R2, LANES), F32),
                        pltpu.VMEM((NB, R2, 2 * LANES), F32)],
        compiler_params=_cparams(("arbitrary", "arbitrary")),
        name="nsa_prompt",
    )(proj3, proj3, kc, vc, proj3, proj3, proj3, proj3)
    return o.reshape(B * T, NSA_Q_W)


def _nsa_sample_kernel(pt_ref, q_ref, sm_ref, kc_ref, vc_ref, poolk_ref, poolv_ref,
                       kn_ref, vn_ref, wk_ref, wv_ref, kwn_ref, vwn_ref,
                       o_ref, wko_ref, wvo_ref, bufk, bufv, sc_sc, sems, *, n_pages, TQ):
    b = pl.program_id(0)
    past = n_pages * PAGE_SIZE
    n_cmp = past // NSA_BLOCK
    n_sel = n_cmp + 1
    R = NSA_HPG * TQ
    R2 = NSA_KV_HEADS * R
    n_buf = wk_ref.shape[2]
    G = range(NSA_KV_HEADS)

    _gather_pages(pt_ref, b, poolk_ref, bufk, sems.at[0], n_pages, True)
    _gather_pages(pt_ref, b, poolv_ref, bufv, sems.at[1], n_pages, True)

    qs = jnp.concatenate([_stack_queries(q_ref[:, g * 256:(g + 1) * 256], g, TQ) for g in G], axis=0)
    slope = jnp.concatenate([_row_slopes(g, TQ) for g in G], axis=0)
    gates = [_gate_columns(sm_ref[...], g, TQ) for g in G]
    gc, gs, gw = [jnp.concatenate([gates[g][br] for g in G], axis=0) for br in range(3)]
    tq_i = past + lax.broadcasted_iota(jnp.int32, (R2, 1), 0) % TQ
    tq_f = tq_i.astype(F32)
    new_i = past + lax.broadcasted_iota(jnp.int32, (1, TQ), 1)

    n_i = lax.broadcasted_iota(jnp.int32, (1, n_cmp), 1)
    center = (n_i * NSA_BLOCK).astype(F32) + 0.5 * (NSA_BLOCK - 1)
    s_c = _mm_nt(qs, kc_ref[0]) - slope * (tq_f - center)
    p_c = _masked_softmax(s_c, (n_i * NSA_BLOCK + NSA_BLOCK - 1) <= tq_i)
    o_c = _mm(p_c, vc_ref[0])

    bias_rows = []
    for g in G:
        score = p_c[g * R:g * R + TQ]
        for hl in range(1, NSA_HPG):
            score = score + p_c[g * R + hl * TQ:g * R + (hl + 1) * TQ]
        forced = (n_i == 0) | (n_i == n_cmp - 1)
        score = jnp.where(forced, FORCED_SCORE, score)
        sel = _topk_select(score, min(NSA_TOP_K, n_sel) - 1, n_cmp)
        bias_rows += [jnp.where(sel, 0.0, NEG)] * NSA_HPG
    sel_bias = jnp.concatenate(bias_rows, axis=0)

    wk_t = wk_ref[0]
    wv_t = wv_ref[0]
    wpos = past - n_buf + lax.broadcasted_iota(jnp.int32, (1, n_buf), 1)
    d_o = tq_i - wpos
    d_n = tq_i - new_i
    m_o = (wpos >= 0) & (d_o >= 0) & (d_o <= NSA_WINDOW)
    m_n = (d_n >= 0) & (d_n <= NSA_WINDOW)
    s_wo = jnp.where(m_o, _mm(qs, wk_t) - slope * d_o.astype(F32), NEG)
    s_wn = jnp.where(m_n, _mm_nt(qs, kwn_ref[...]) - slope * d_n.astype(F32), NEG)
    mw = jnp.maximum(jnp.max(s_wo, axis=-1, keepdims=True), jnp.max(s_wn, axis=-1, keepdims=True))
    e_o = jnp.where(m_o, jnp.exp(s_wo - mw), 0.0)
    e_w = jnp.where(m_n, jnp.exp(s_wn - mw), 0.0)
    den_w = jnp.maximum(jnp.sum(e_o, axis=-1, keepdims=True) + jnp.sum(e_w, axis=-1, keepdims=True), 1e-30)
    o_w = (_mm_nt(e_o, wv_t) + _mm(e_w, vwn_ref[...])) * (1.0 / den_w)

    lane_w = lax.broadcasted_iota(jnp.int32, (LANES, n_buf), 1)

    def shifted(old_t, new):
        slots = jnp.concatenate([jnp.zeros((LANES - TQ, LANES), F32), new], axis=0)
        tail = jnp.concatenate([jnp.zeros((LANES, n_buf - LANES), F32), slots.T], axis=1)
        return jnp.where(lane_w >= n_buf - TQ, tail, pltpu.roll(old_t, n_buf - TQ, axis=1))

    wko_ref[0] = shifted(wk_t, kwn_ref[...])
    wvo_ref[0] = shifted(wv_t, vwn_ref[...])

    _gather_pages(pt_ref, b, poolk_ref, bufk, sems.at[0], n_pages, False)
    tok = lax.broadcasted_iota(jnp.int32, (1, PAGE_SIZE), 1)
    second = tok >= NSA_BLOCK
    m_run = jnp.full((R2, PAGE_SIZE), NEG, F32)
    for p in range(n_pages):
        s = jnp.dot(qs, bufk[p * PAGE_SIZE:(p + 1) * PAGE_SIZE, :].astype(BF16), preferred_element_type=F32)
        bias = jnp.where(second, sel_bias[:, 2 * p + 1:2 * p + 2], sel_bias[:, 2 * p:2 * p + 1])
        s = s + bias - slope * (tq_f - (p * PAGE_SIZE + tok).astype(F32))
        sc_sc[p] = s
        m_run = jnp.maximum(m_run, s)
    s_n = jnp.where(new_i <= tq_i, _mm_nt(qs, kn_ref[...]) - slope * (tq_i - new_i).astype(F32), NEG)
    m = jnp.maximum(jnp.max(m_run, axis=-1, keepdims=True), jnp.max(s_n, axis=-1, keepdims=True))

    _gather_pages(pt_ref, b, poolv_ref, bufv, sems.at[1], n_pages, False)
    e_n = jnp.exp(s_n - m)
    acc = _mm(e_n, vn_ref[...])
    den_run = jnp.zeros((R2, PAGE_SIZE), F32)
    for p in range(n_pages):
        e = jnp.exp(sc_sc[p] - m)
        den_run = den_run + e
        acc = acc + _mm_nt(e, bufv[p * PAGE_SIZE:(p + 1) * PAGE_SIZE, :])
    den = jnp.sum(den_run, axis=-1, keepdims=True) + jnp.sum(e_n, axis=-1, keepdims=True)
    o_s = acc * (1.0 / den)

    o = gc * o_c + gs * o_s + gw * o_w
    o_ref[...] = jnp.concatenate([_unstack_heads(o[g * R:(g + 1) * R], g, TQ) for g in G], axis=1).astype(o_ref.dtype)


def _nsa_sample(proj, page_table, kc, vc, pool_k, pool_v, win_k, win_v):
    B, n_pages = page_table.shape
    TQ = proj.shape[0] // B
    past = n_pages * PAGE_SIZE
    n_cmp = past // NSA_BLOCK
    n_buf = win_k.shape[2]
    R2 = NSA_KV_HEADS * NSA_HPG * TQ
    kvn = lambda j: pl.BlockSpec((TQ, LANES), lambda b, pt: (b, AB_KV_BLK + j))
    win = pl.BlockSpec((1, LANES, n_buf), lambda b, pt: (b, 0, 0))
    return pl.pallas_call(
        functools.partial(_nsa_sample_kernel, n_pages=n_pages, TQ=TQ),
        out_shape=(jax.ShapeDtypeStruct((B * TQ, NSA_Q_W), BF16),
                   jax.ShapeDtypeStruct((B, LANES, n_buf), F32),
                   jax.ShapeDtypeStruct((B, LANES, n_buf), F32)),
        grid_spec=pltpu.PrefetchScalarGridSpec(
            num_scalar_prefetch=1,
            grid=(B,),
            in_specs=[
                pl.BlockSpec((TQ, NSA_Q_W), lambda b, pt: (b, 3)),
                pl.BlockSpec((TQ, LANES), lambda b, pt: (b, AB_SMALL_BLK)),
                pl.BlockSpec((1, n_cmp, LANES), lambda b, pt: (b, 0, 0)),
                pl.BlockSpec((1, n_cmp, LANES), lambda b, pt: (b, 0, 0)),
                pl.BlockSpec(memory_space=pl.ANY),
                pl.BlockSpec(memory_space=pl.ANY),
                kvn(2), kvn(3), win, win, kvn(4), kvn(5),
            ],
            out_specs=(pl.BlockSpec((TQ, NSA_Q_W), lambda b, pt: (b, 0)), win, win),
            scratch_shapes=[
                pltpu.VMEM((past, LANES), F32),
                pltpu.VMEM((past, LANES), F32),
                pltpu.VMEM((n_pages, R2, PAGE_SIZE), F32),
                pltpu.SemaphoreType.DMA((2,)),
            ],
        ),
        compiler_params=_cparams(("arbitrary",)),
        name="nsa_sample",
    )(page_table, proj, proj, kc, vc, pool_k, pool_v, proj, proj, win_k, win_v, proj, proj)


def _gdn_kernel(qkv_ref, z_ref, sm_ref, cw_ref, alog_ref, dtb_ref, ng_ref, s0_ref, cb_ref,
                o_ref, sfin_ref, S_sc, prev_sc, *, C, NB):
    t = pl.program_id(1)

    @pl.when(t == 0)
    def _():
        S_sc[...] = s0_ref[...]
        prev_sc[...] = cb_ref[...]

    def conv(n, c0):
        xe = jnp.concatenate([prev_sc[n, :, c0:c0 + LANES], qkv_ref[n, :, c0:c0 + LANES]], axis=0)
        y = xe[SUBLANES:] * cw_ref[GDN_CONV - 1:GDN_CONV, c0:c0 + LANES]
        for s in range(1, GDN_CONV):
            y = y + pltpu.roll(xe, s, axis=0)[SUBLANES:] * cw_ref[GDN_CONV - 1 - s:GDN_CONV - s, c0:c0 + LANES]
        return _silu(y)

    ii = lax.broadcasted_iota(jnp.int32, (C, C), 0)
    jj = lax.broadcasted_iota(jnp.int32, (C, C), 1)
    eye = (ii == jj).astype(F32)

    CH = [(n, h) for n in range(NB) for h in range(GDN_HEADS)]
    X = range(len(CH))
    beta_c, d_c, d_r = [], [], []
    for n in range(NB):
        sm = sm_ref[n]
        beta = _sigmoid(sm)
        gt = -jnp.exp(alog_ref[...]) * _softplus(sm + dtb_ref[...])
        d = _cumsum_rows(gt)
        dT = d.T
        for h in range(GDN_HEADS):
            beta_c.append(beta[:, h:h + 1])
            d_c.append(d[:, GDN_A_LANE0 + h:GDN_A_LANE0 + h + 1])
            d_r.append(dT[GDN_A_LANE0 + h:GDN_A_LANE0 + h + 1, :])
    q, k, v = [], [], []
    for n, h in CH:
        qh = conv(n, h * GDN_DK)
        kh = conv(n, GDN_W + h * GDN_DK)
        q.append(qh * lax.rsqrt(jnp.sum(qh * qh, axis=-1, keepdims=True) + L2_EPS) * (GDN_DK ** -0.5))
        k.append(kh * lax.rsqrt(jnp.sum(kh * kh, axis=-1, keepdims=True) + L2_EPS))
        v.append(conv(n, 2 * GDN_W + h * GDN_DV))
    decay = [jnp.exp(jnp.minimum(d_c[x] - d_r[x], 0.0)) for x in X]
    kb = [k[x] * beta_c[x] for x in X]
    g_kk = [_mm_nt(kb[x], k[x]) for x in X]
    g_qk = [_mm_nt(q[x], k[x]) for x in X]
    a = [jnp.where(ii > jj, g_kk[x] * decay[x], 0.0) for x in X]
    qk = [jnp.where(ii >= jj, g_qk[x] * decay[x], 0.0) for x in X]
    tinv = [eye - a[x] for x in X]
    p_split = [_split_bf16(a[x]) for x in X]
    n2 = 2
    while n2 < C:
        p = [_mm3(p_split[x], p_split[x]) for x in X]
        p_split = [_split_bf16(p[x]) for x in X]
        tinv = [tinv[x] + _mm3(_split_bf16(tinv[x]), p_split[x]) for x in X]
        n2 *= 2
    u = [_mm(tinv[x], v[x] * beta_c[x]) for x in X]
    w = [_mm(tinv[x], kb[x] * jnp.exp(d_c[x])) for x in X]
    S = [S_sc[n, h] for n, h in CH]
    w_s = [_mm(w[x], S[x]) for x in X]
    q_s = [_mm(q[x] * jnp.exp(d_c[x]), S[x]) for x in X]
    v_new = [u[x] - w_s[x] for x in X]
    o = [q_s[x] + _mm(qk[x], v_new[x]) for x in X]
    d_last = [d_c[x][C - 1:C, :] for x in X]
    upd = [_mm_tn(k[x] * jnp.exp(d_last[x] - d_c[x]), v_new[x]) for x in X]
    for x, (n, h) in enumerate(CH):
        S_sc[n, h] = S[x] * jnp.exp(d_last[x]) + upd[x]
        ms = jnp.mean(o[x] * o[x], axis=-1, keepdims=True)
        zh = z_ref[n, :, h * GDN_DV:(h + 1) * GDN_DV]
        o_ref[n, :, h * GDN_DV:(h + 1) * GDN_DV] = (
            o[x] * lax.rsqrt(ms + RMS_EPS) * ng_ref[...] * _silu(zh)).astype(o_ref.dtype)

    for n in range(NB):
        prev_sc[n] = qkv_ref[n, C - SUBLANES:C, :]

    @pl.when(t == pl.num_programs(1) - 1)
    def _():
        sfin_ref[...] = S_sc[...]


def _gdn(proj, B, T, conv_w, a_log, dt_bias, norm_g, s0, conv_buf8, C):
    nt = T // C
    NB = 4 if B % 4 == 0 else (2 if B % 2 == 0 else 1)
    pad = lambda v: jnp.zeros((1, LANES), F32).at[0, GDN_A_LANE0:GDN_A_LANE0 + GDN_HEADS].set(v)
    proj3 = proj.reshape(B, T, C_COLS)
    o, s_fin = pl.pallas_call(
        functools.partial(_gdn_kernel, C=C, NB=NB),
        out_shape=(jax.ShapeDtypeStruct((B, T, GDN_W), BF16),
                   jax.ShapeDtypeStruct((B, GDN_HEADS, GDN_DK, GDN_DV), F32)),
        grid=(B // NB, nt),
        in_specs=[
            pl.BlockSpec((NB, C, GDN_CONV_CH), lambda b, t: (b, t, 0)),
            pl.BlockSpec((NB, C, GDN_W), lambda b, t: (b, t, 3)),
            pl.BlockSpec((NB, C, LANES), lambda b, t: (b, t, C_SMALL_BLK)),
            pl.BlockSpec((GDN_CONV, GDN_CONV_CH), lambda b, t: (0, 0)),
            pl.BlockSpec((1, LANES), lambda b, t: (0, 0)),
            pl.BlockSpec((1, LANES), lambda b, t: (0, 0)),
            pl.BlockSpec((1, GDN_DV), lambda b, t: (0, 0)),
            pl.BlockSpec((NB, GDN_HEADS, GDN_DK, GDN_DV), lambda b, t: (b, 0, 0, 0)),
            pl.BlockSpec((NB, SUBLANES, GDN_CONV_CH), lambda b, t: (b, 0, 0)),
        ],
        out_specs=(pl.BlockSpec((NB, C, GDN_W), lambda b, t: (b, t, 0)),
                   pl.BlockSpec((NB, GDN_HEADS, GDN_DK, GDN_DV), lambda b, t: (b, 0, 0, 0))),
        scratch_shapes=[
            pltpu.VMEM((NB, GDN_HEADS, GDN_DK, GDN_DV), F32),
            pltpu.VMEM((NB, SUBLANES, GDN_CONV_CH), F32),
        ],
        compiler_params=_cparams(("arbitrary", "arbitrary")),
        name="gdn",
    )(proj3, proj3, proj3, conv_w, pad(a_log), pad(dt_bias), norm_g.reshape(1, GDN_DV), s0, conv_buf8)
    return o.reshape(B * T, GDN_W), s_fin


def _ab_in_weight(w):
    big = w[:, :GLA_KEY_W * 2 + GLA_VAL_W * 2]
    gk = w[:, 1536:1536 + GLA_LOWRANK]
    rest = w[:, 1536 + GLA_LOWRANK:]
    q_b = rest[:, :NSA_Q_W]
    kv = rest[:, NSA_Q_W:NSA_Q_W + 6 * NSA_KV_W]
    gate = rest[:, NSA_Q_W + 6 * NSA_KV_W:]
    small = jnp.concatenate([gk, gate, jnp.zeros((D_MODEL, LANES - GLA_LOWRANK - 3 * NSA_HEADS), w.dtype)], axis=1)
    return jnp.concatenate([big, q_b, kv, small], axis=1).astype(BF16)


def _c_in_weight(w):
    qkv = w[:, :GDN_CONV_CH]
    ba = w[:, GDN_CONV_CH:GDN_CONV_CH + 2 * GDN_HEADS]
    z = w[:, GDN_CONV_CH + 2 * GDN_HEADS:]
    small = jnp.concatenate([ba, jnp.zeros((D_MODEL, LANES - 2 * GDN_HEADS), w.dtype)], axis=1)
    return jnp.concatenate([qkv, z, small], axis=1).astype(BF16)


def _kv_out(proj, B, T, j):
    return proj[:, (AB_KV_BLK + j) * LANES:(AB_KV_BLK + j + 1) * LANES].reshape(B, T, NSA_KV_HEADS, NSA_HEAD_DIM)


PROMPT_ROWS = 512
FFN_ROWS = 1024


def kernel(x_prompt, x_sample, c_prompt, c_sample, page_table, cache_cmp_k, cache_cmp_v, cache_sel_k, cache_sel_v, state_win_k, state_win_v, state_gla, state_gdn, state_gdn_conv, w_ada, b_ada, ln_g, ln_b, w_ffn_in, w_ffn_out, ab_w_in, ab_w_gk2, ab_b_gk, ab_gla_norm, ab_cmp_pe, ab_cmp_w1, ab_cmp_w2, ab_w_out, c_w_in, c_conv_w, c_a_log, c_dt_bias, c_norm, c_w_out):
    Bp, Tp, _ = x_prompt.shape
    Bs, Ts, _ = x_sample.shape
    n_pool = cache_cmp_k.shape[1]

    mods = _adaln(jnp.concatenate([c_prompt, c_sample], axis=0), w_ada, b_ada)

    def layer_mods(layer):
        m = mods[layer]
        parts = [m[:, i * D_MODEL:(i + 1) * D_MODEL] for i in range(6)]
        return [p[:Bp] for p in parts], [p[Bp:] for p in parts]

    xp, xs = x_prompt, x_sample
    ab_p, ab_s, c_p, c_s = [], [], [], []
    for layer in range(DEPTH):
        mp, ms = layer_mods(layer)
        wf_in = w_ffn_in[layer].astype(BF16)
        wf_out = w_ffn_out[layer].astype(BF16)
        i = layer // 2
        if layer % 2 == 0:
            w_in = _ab_in_weight(ab_w_in[i])
            w_out = ab_w_out[i].astype(BF16)
            wo_a, wo_b = w_out[:GLA_VAL_W], w_out[GLA_VAL_W:]
            cw = _cmp_weights(ab_cmp_pe[i], ab_cmp_w1[i], ab_cmp_w2[i])
            cw_t = _cmp_weights_t(ab_cmp_pe[i], ab_cmp_w1[i], ab_cmp_w2[i])

            proj, kv_t = _modmm(xp, mp[0], mp[1], w_in, PROMPT_ROWS, t_cols=(AB_KV_BLK * LANES, 6))
            zero_state = jnp.zeros((Bp, GLA_HEADS, GLA_DK, GLA_DV), F32)
            o_a, s_a = _gla(proj, Bp, Tp, ab_w_gk2[i], ab_b_gk[i], ab_gla_norm[i], zero_state, min(64, Tp))
            kc, vc = _compress_dense(proj, Bp, Tp, cw)
            o_b = _nsa_prompt(proj, Bp, Tp, kc, vc)
            x1 = _outproj_ln([o_a, o_b], [wo_a, wo_b], xp, mp[2], ln_g[layer, 0], ln_b[layer, 0], PROMPT_ROWS)
            n_keep = min(NSA_WINDOW, Tp)
            kv_out_t = lambda a: jnp.transpose(a.reshape(Bp, NSA_KV_HEADS, NSA_HEAD_DIM, a.shape[-1]), (0, 3, 1, 2))
            ab_p.append(tuple(kv_out_t(kv_t[j]) for j in range(4))
                        + (kv_out_t(kv_t[4][:, :, Tp - n_keep:]), kv_out_t(kv_t[5][:, :, Tp - n_keep:]), s_a))
            xp = _ffn_ln(x1, mp[3], mp[4], mp[5], wf_in, wf_out, ln_g[layer, 1], ln_b[layer, 1], FFN_ROWS)

            proj = _modmm(xs, ms[0], ms[1], w_in, PROMPT_ROWS)
            o_a, s_a = _gla(proj, Bs, Ts, ab_w_gk2[i], ab_b_gk[i], ab_gla_norm[i], state_gla[i], min(64, Ts))
            pool = lambda c: jnp.transpose(c[i], (0, 2, 3, 1)).reshape(n_pool * PAGE_SIZE, LANES)
            n_buf = state_win_k.shape[2]
            win_t = lambda w: jnp.transpose(w[i], (0, 2, 3, 1)).reshape(Bs, LANES, n_buf)
            kc, vc = _compress_paged(page_table, pool(cache_cmp_k), pool(cache_cmp_v), cw_t)
            o_b, win_k, win_v = _nsa_sample(proj, page_table, kc, vc, pool(cache_sel_k), pool(cache_sel_v),
                                            win_t(state_win_k), win_t(state_win_v))
            win_out = lambda w: jnp.transpose(w.reshape(Bs, NSA_KV_HEADS, NSA_HEAD_DIM, n_buf), (0, 3, 1, 2))
            x1 = _outproj_ln([o_a, o_b], [wo_a, wo_b], xs, ms[2], ln_g[layer, 0], ln_b[layer, 0], PROMPT_ROWS)
            ab_s.append(tuple(_kv_out(proj, Bs, Ts, j) for j in range(4))
                        + (win_out(win_k), win_out(win_v), s_a))
            xs = _ffn_ln(x1, ms[3], ms[4], ms[5], wf_in, wf_out, ln_g[layer, 1], ln_b[layer, 1], FFN_ROWS)
        else:
            w_in = _c_in_weight(c_w_in[i])
            w_out = c_w_out[i].astype(BF16)
            keep = GDN_CONV - 1

            proj = _modmm(xp, mp[0], mp[1], w_in, PROMPT_ROWS // 2)
            o_c, s_c = _gdn(proj, Bp, Tp, c_conv_w[i], c_a_log[i], c_dt_bias[i], c_norm[i],
                            jnp.zeros((Bp, GDN_HEADS, GDN_DK, GDN_DV), F32),
                            jnp.zeros((Bp, SUBLANES, GDN_CONV_CH), F32), min(64, Tp))
            x1 = _outproj_ln([o_c], [w_out], xp, mp[2], ln_g[layer, 0], ln_b[layer, 0], PROMPT_ROWS)
            c_p.append((s_c, proj.reshape(Bp, Tp, C_COLS)[:, Tp - keep:, :GDN_CONV_CH]))
            xp = _ffn_ln(x1, mp[3], mp[4], mp[5], wf_in, wf_out, ln_g[layer, 1], ln_b[layer, 1], FFN_ROWS)

            proj = _modmm(xs, ms[0], ms[1], w_in, PROMPT_ROWS // 2)
            conv8 = jnp.concatenate([jnp.zeros((Bs, SUBLANES - keep, GDN_CONV_CH), F32), state_gdn_conv[i]], axis=1)
            o_c, s_c = _gdn(proj, Bs, Ts, c_conv_w[i], c_a_log[i], c_dt_bias[i], c_norm[i],
                            state_gdn[i], conv8, min(64, Ts))
            x1 = _outproj_ln([o_c], [w_out], xs, ms[2], ln_g[layer, 0], ln_b[layer, 0], PROMPT_ROWS)
            qkv_s = proj.reshape(Bs, Ts, C_COLS)[:, :, :GDN_CONV_CH]
            c_s.append((s_c, jnp.concatenate([state_gdn_conv[i], qkv_s], axis=1)[:, -keep:]))
            xs = _ffn_ln(x1, ms[3], ms[4], ms[5], wf_in, wf_out, ln_g[layer, 1], ln_b[layer, 1], FFN_ROWS)

    stack = lambda sts: [jnp.stack(z) for z in zip(*sts)]
    p_ab, s_ab = stack(ab_p), stack(ab_s)
    p_c, s_c = stack(c_p), stack(c_s)
    return (xp, xs, *p_ab, *p_c, *s_ab, *s_c)
```

```python
import functools

import jax
import jax.numpy as jnp
from jax import lax
from jax.experimental import pallas as pl
from jax.experimental.pallas import tpu as pltpu

F32 = jnp.float32
BF16 = jnp.bfloat16

D_MODEL = 1024
DEPTH = 2
PAGE_SIZE = 128
GLA_HEADS = 4
GLA_DK = 64
GLA_DV = 128
GLA_LOWRANK = 16
GLA_GATE_NORM = 16.0
NSA_HEAD_DIM = 64
NSA_HEADS = 8
NSA_KV_HEADS = 2
NSA_HPG = 4
NSA_BLOCK = 64
NSA_TOP_K = 16
NSA_WINDOW = 512
NSA_CMP_HIDDEN = 128
FORCED_SCORE = 1000.0
GDN_HEADS = 8
GDN_DK = 128
GDN_DV = 128
GDN_CONV = 4
FF_HIDDEN = 2816
DEEPNORM_ALPHA = (2.0 * DEPTH) ** 0.25
LN_EPS = 1e-5
RMS_EPS = 1e-6
L2_EPS = 1e-6
NEG = -1e30

GLA_KEY_W = GLA_HEADS * GLA_DK
GLA_VAL_W = GLA_HEADS * GLA_DV
NSA_Q_W = NSA_HEADS * NSA_HEAD_DIM
NSA_KV_W = NSA_KV_HEADS * NSA_HEAD_DIM
GDN_W = GDN_HEADS * GDN_DK
GDN_CONV_CH = 3 * GDN_W

LANES = 128
SUBLANES = 8
VMEM_LIMIT = 56 * 1024 * 1024

AB_COLS = 2944
AB_SMALL_BLK = 22
AB_KV_BLK = 16
GATE_LANE0 = GLA_LOWRANK
C_COLS = 4224
C_SMALL_BLK = 32
GDN_A_LANE0 = GDN_HEADS
FEAT_BLK = 64
FEAT_OFF = 65
FEAT_PAD = 66
SEL_BIG = 131072.0


def _cparams(sem):
    return pltpu.CompilerParams(dimension_semantics=sem, vmem_limit_bytes=VMEM_LIMIT)


def _silu(x):
    return x * (1.0 / (1.0 + jnp.exp(-x)))


def _sigmoid(x):
    return 1.0 / (1.0 + jnp.exp(-x))


def _softplus(x):
    return jnp.maximum(x, 0.0) + jnp.log(1.0 + jnp.exp(-jnp.abs(x)))


def _mm(a, b):
    return jnp.dot(a.astype(BF16), b.astype(BF16), preferred_element_type=F32)


def _split_bf16(x):
    hi = x.astype(BF16)
    return hi, (x - hi.astype(F32)).astype(BF16)


def _mm3(a, b):
    dot = lambda x, y: jnp.dot(x, y, preferred_element_type=F32)
    return dot(a[0], b[0]) + (dot(a[1], b[0]) + dot(a[0], b[1]))


def _mm_nt(a, b):
    return lax.dot_general(a.astype(BF16), b.astype(BF16), (((1,), (1,)), ((), ())),
                           preferred_element_type=F32)


def _mm_tn(a, b):
    return lax.dot_general(a.astype(BF16), b.astype(BF16), (((0,), (0,)), ((), ())),
                           preferred_element_type=F32)


def _cumsum_rows(x):
    n = x.shape[0]
    row = lax.broadcasted_iota(jnp.int32, x.shape, 0)
    s = 1
    while s < n:
        x = x + jnp.where(row >= s, pltpu.roll(x, s, axis=0), 0.0)
        s *= 2
    return x


def _masked_softmax(s, mask):
    s = jnp.where(mask, s, NEG)
    m = jnp.max(s, axis=-1, keepdims=True)
    e = jnp.where(mask, jnp.exp(s - m), 0.0)
    den = jnp.maximum(jnp.sum(e, axis=-1, keepdims=True), 1e-30)
    return e * (1.0 / den)


def _layernorm(z, g, b):
    mu = jnp.mean(z, axis=-1, keepdims=True)
    zc = z - mu
    var = jnp.mean(zc * zc, axis=-1, keepdims=True)
    return zc * lax.rsqrt(var + LN_EPS) * g + b


def _adaln_kernel(c_ref, w_ref, b_ref, o_ref):
    c = _silu(c_ref[...])
    o_ref[0] = _mm(c, w_ref[0]) + b_ref[0]


def _adaln(c_all, w_ada, b_ada):
    n = c_all.shape[0]
    tn = 1536
    nt = (6 * D_MODEL) // tn
    return pl.pallas_call(
        _adaln_kernel,
        out_shape=jax.ShapeDtypeStruct((DEPTH, n, 6 * D_MODEL), F32),
        grid=(DEPTH, nt),
        in_specs=[
            pl.BlockSpec((n, D_MODEL), lambda l, j: (0, 0)),
            pl.BlockSpec((1, D_MODEL, tn), lambda l, j: (l, 0, j)),
            pl.BlockSpec((1, 1, tn), lambda l, j: (l, 0, j)),
        ],
        out_specs=pl.BlockSpec((1, n, tn), lambda l, j: (l, 0, j)),
        compiler_params=_cparams(("arbitrary", "arbitrary")),
        name="adaln",
    )(c_all, w_ada, b_ada.reshape(DEPTH, 1, 6 * D_MODEL))


def _modmm_kernel(x_ref, sh_ref, sc_ref, w_ref, o_ref, *t_refs, t_col0):
    bb, tt, d = x_ref.shape
    h = x_ref[...] * (1.0 + sc_ref[...]) + sh_ref[...]
    res = _mm(h.reshape(bb * tt, d), w_ref[...])
    o_ref[...] = res
    for t_ref in t_refs:
        for j in range(t_ref.shape[0]):
            t_ref[j, 0] = res[:, t_col0 + j * LANES:t_col0 + (j + 1) * LANES].T


def _row_tiling(B, T, max_rows):
    if T >= max_rows:
        return 1, max_rows
    bb = min(B, max_rows // T)
    return bb, T


def _modmm(x, shift, scale, w_bf16, max_rows, t_cols=None):
    B, T, D = x.shape
    N = w_bf16.shape[1]
    bb, tt = _row_tiling(B, T, max_rows)
    nt = T // tt
    out_shape = jax.ShapeDtypeStruct((B * T, N), F32)
    out_specs = pl.BlockSpec((bb * tt, N), lambda i, j: (i * nt + j, 0))
    t_col0 = 0
    if t_cols is not None:
        assert bb == 1
        t_col0, n_t = t_cols
        out_shape = (out_shape, jax.ShapeDtypeStruct((n_t, B, LANES, T), F32))
        out_specs = (out_specs, pl.BlockSpec((n_t, 1, LANES, tt), lambda i, j: (0, i, 0, j)))
    return pl.pallas_call(
        functools.partial(_modmm_kernel, t_col0=t_col0),
        out_shape=out_shape,
        grid=(B // bb, nt),
        in_specs=[
            pl.BlockSpec((bb, tt, D), lambda i, j: (i, j, 0)),
            pl.BlockSpec((bb, 1, D), lambda i, j: (i, 0, 0)),
            pl.BlockSpec((bb, 1, D), lambda i, j: (i, 0, 0)),
            pl.BlockSpec((D, N), lambda i, j: (0, 0)),
        ],
        out_specs=out_specs,
        compiler_params=_cparams(("arbitrary", "arbitrary")),
        name="modmm",
    )(x, shift[:, None, :], scale[:, None, :], w_bf16)


def _outproj_kernel(*refs, n_in):
    a_refs = refs[:n_in]
    w_refs = refs[n_in:2 * n_in]
    x_ref, gate_ref, g_ref, b_ref, o_ref = refs[2 * n_in:]
    bb, tt, d = x_ref.shape
    acc = _mm(a_refs[0][...], w_refs[0][...])
    for a_ref, w_ref in zip(a_refs[1:], w_refs[1:]):
        acc = acc + _mm(a_ref[...], w_ref[...])
    z = DEEPNORM_ALPHA * x_ref[...] + gate_ref[...] * acc.reshape(bb, tt, d)
    o_ref[...] = _layernorm(z, g_ref[...], b_ref[...])


def _outproj_ln(acts, ws, x, gate, ln_g, ln_b, max_rows):
    B, T, D = x.shape
    bb, tt = _row_tiling(B, T, max_rows)
    nt = T // tt
    n_in = len(acts)
    in_specs = []
    for a in acts:
        in_specs.append(pl.BlockSpec((bb * tt, a.shape[1]), lambda i, j: (i * nt + j, 0)))
    for w in ws:
        in_specs.append(pl.BlockSpec(w.shape, lambda i, j: (0, 0)))
    in_specs += [
        pl.BlockSpec((bb, tt, D), lambda i, j: (i, j, 0)),
        pl.BlockSpec((bb, 1, D), lambda i, j: (i, 0, 0)),
        pl.BlockSpec((1, 1, D), lambda i, j: (0, 0, 0)),
        pl.BlockSpec((1, 1, D), lambda i, j: (0, 0, 0)),
    ]
    return pl.pallas_call(
        functools.partial(_outproj_kernel, n_in=n_in),
        out_shape=jax.ShapeDtypeStruct((B, T, D), F32),
        grid=(B // bb, nt),
        in_specs=in_specs,
        out_specs=pl.BlockSpec((bb, tt, D), lambda i, j: (i, j, 0)),
        compiler_params=_cparams(("arbitrary", "arbitrary")),
        name="outproj_ln",
    )(*acts, *ws, x, gate[:, None, :], ln_g.reshape(1, 1, D), ln_b.reshape(1, 1, D))


def _ffn_kernel(x_ref, sh_ref, sc_ref, gate_ref, wa_ref, wu_ref, wo_ref, g_ref, b_ref, o_ref,
                xm_sc, acc_sc):
    j = pl.program_id(2)
    bb, tt, d = x_ref.shape

    @pl.when(j == 0)
    def _():
        h = x_ref[...] * (1.0 + sc_ref[...]) + sh_ref[...]
        xm_sc[...] = h.reshape(bb * tt, d).astype(BF16)
        acc_sc[...] = jnp.zeros_like(acc_sc)

    xm = xm_sc[...]
    a = jnp.dot(xm, wa_ref[...], preferred_element_type=F32)
    u = jnp.dot(xm, wu_ref[...], preferred_element_type=F32)
    acc_sc[...] += _mm(_silu(a) * u, wo_ref[...])

    @pl.when(j == pl.num_programs(2) - 1)
    def _():
        z = DEEPNORM_ALPHA * x_ref[...] + gate_ref[...] * acc_sc[...].reshape(bb, tt, d)
        o_ref[...] = _layernorm(z, g_ref[...], b_ref[...])


def _ffn_ln(x, shift, scale, gate, w_in_bf16, w_out_bf16, ln_g, ln_b, max_rows):
    B, T, D = x.shape
    bb, tt = _row_tiling(B, T, max_rows)
    nt = T // tt
    th = 256
    nh = FF_HIDDEN // th
    vec = lambda v: v[:, None, :]
    return pl.pallas_call(
        _ffn_kernel,
        out_shape=jax.ShapeDtypeStruct((B, T, D), F32),
        grid=(B // bb, nt, nh),
        in_specs=[
            pl.BlockSpec((bb, tt, D), lambda i, t, j: (i, t, 0)),
            pl.BlockSpec((bb, 1, D), lambda i, t, j: (i, 0, 0)),
            pl.BlockSpec((bb, 1, D), lambda i, t, j: (i, 0, 0)),
            pl.BlockSpec((bb, 1, D), lambda i, t, j: (i, 0, 0)),
            pl.BlockSpec((D, th), lambda i, t, j: (0, j)),
            pl.BlockSpec((D, th), lambda i, t, j: (0, nh + j)),
            pl.BlockSpec((th, D), lambda i, t, j: (j, 0)),
            pl.BlockSpec((1, 1, D), lambda i, t, j: (0, 0, 0)),
            pl.BlockSpec((1, 1, D), lambda i, t, j: (0, 0, 0)),
        ],
        out_specs=pl.BlockSpec((bb, tt, D), lambda i, t, j: (i, t, 0)),
        scratch_shapes=[pltpu.VMEM((bb * tt, D), BF16), pltpu.VMEM((bb * tt, D), F32)],
        compiler_params=_cparams(("arbitrary", "arbitrary", "arbitrary")),
        name="ffn_ln",
    )(x, vec(shift), vec(scale), vec(gate), w_in_bf16, w_in_bf16, w_out_bf16,
      ln_g.reshape(1, 1, D), ln_b.reshape(1, 1, D))


def _gla_kernel(q_ref, k_ref, v_ref, r_ref, sm_ref, wgk_ref, bgk_ref, gn_ref, s0_ref,
                o_ref, sfin_ref, S_sc, q_sc, k_sc, b_sc, v_sc, o_sc, *, C, NB):
    t = pl.program_id(1)
    KW, VW = GLA_KEY_W, GLA_VAL_W

    hk = lax.broadcasted_iota(jnp.int32, (KW, VW), 0) // GLA_DK
    hv = lax.broadcasted_iota(jnp.int32, (KW, VW), 1) // GLA_DV
    same_head = hk == hv

    @pl.when(t == 0)
    def _():
        for n in range(NB):
            rows = []
            for h in range(GLA_HEADS):
                pieces = [s0_ref[n, h] if h2 == h else jnp.zeros((GLA_DK, GLA_DV), F32)
                          for h2 in range(GLA_HEADS)]
                rows.append(jnp.concatenate(pieces, axis=1))
            S_sc[n] = jnp.concatenate(rows, axis=0)

    seg = same_head.astype(BF16)
    JG = min(C, 2 * SUBLANES)
    local = lax.broadcasted_iota(jnp.int32, (JG, KW), 0)
    S_new = []
    for n in range(NB):
        gk = sm_ref[n, :, 0:GLA_LOWRANK]
        pre = _mm(gk, wgk_ref[...]) + bgk_ref[...]
        log_a = (jnp.minimum(pre, 0.0) - jnp.log(1.0 + jnp.exp(-jnp.abs(pre)))) * (1.0 / GLA_GATE_NORM)
        b = _cumsum_rows(log_a)
        q = q_ref[n] * (GLA_DK ** -0.5)
        k = k_ref[n]
        v = v_ref[n]
        q_sc[n] = q
        k_sc[n] = k
        b_sc[n] = b
        v_sc[n] = v

        S = S_sc[n]
        o_sc[n] = _mm(q * jnp.exp(b), S)

        for r0 in range(0, C, JG):
            rows = C - r0
            qg = q_sc[n, r0:C, :]
            bg = b_sc[n, r0:C, :]
            ps = []
            for jj in range(JG):
                j = r0 + jj
                p = qg * k_sc[n, j:j + 1, :] * jnp.exp(jnp.minimum(bg - b_sc[n, j:j + 1, :], 0.0))
                head = jnp.where(local >= jj, p[0:JG], 0.0)
                p = head if rows == JG else jnp.concatenate([head, p[JG:]], axis=0)
                ps.append(p.astype(BF16))
            s = jnp.dot(jnp.concatenate(ps, axis=0), seg, preferred_element_type=F32)
            contrib = s[0:rows] * v_sc[n, r0:r0 + 1, :]
            for jj in range(1, JG):
                contrib = contrib + s[jj * rows:(jj + 1) * rows] * v_sc[n, r0 + jj:r0 + jj + 1, :]
            o_sc[n, r0:C, :] += contrib
        o = o_sc[n]

        b_last = b[C - 1:C, :]
        kd = k * jnp.exp(b_last - b)
        upd = _mm_tn(kd, v)
        tail = jnp.broadcast_to(b_last, (SUBLANES, KW))
        dcol = jnp.exp(tail.T[:, 0:1])
        S_new.append(S * dcol + jnp.where(same_head, upd, 0.0))
        S_sc[n] = S_new[n]

        outs = []
        for h in range(GLA_HEADS):
            oh = o[:, h * GLA_DV:(h + 1) * GLA_DV]
            ms = jnp.mean(oh * oh, axis=-1, keepdims=True)
            rh = r_ref[n, :, h * GLA_DV:(h + 1) * GLA_DV]
            outs.append(oh * lax.rsqrt(ms + RMS_EPS) * gn_ref[...] * _silu(rh))
        o_ref[n] = jnp.concatenate(outs, axis=1).astype(o_ref.dtype)

    @pl.when(t == pl.num_programs(1) - 1)
    def _():
        for n in range(NB):
            for h in range(GLA_HEADS):
                sfin_ref[n, h] = S_new[n][h * GLA_DK:(h + 1) * GLA_DK, h * GLA_DV:(h + 1) * GLA_DV]


def _gla(proj, B, T, w_gk2, b_gk, gla_norm, s0, C):
    nt = T // C
    NB = 2 if B % 2 == 0 else 1
    proj3 = proj.reshape(B, T, AB_COLS)
    o, s_fin = pl.pallas_call(
        functools.partial(_gla_kernel, C=C, NB=NB),
        out_shape=(jax.ShapeDtypeStruct((B, T, GLA_VAL_W), BF16),
                   jax.ShapeDtypeStruct((B, GLA_HEADS, GLA_DK, GLA_DV), F32)),
        grid=(B // NB, nt),
        in_specs=[
            pl.BlockSpec((NB, C, GLA_KEY_W), lambda b, t: (b, t, 0)),
            pl.BlockSpec((NB, C, GLA_KEY_W), lambda b, t: (b, t, 1)),
            pl.BlockSpec((NB, C, GLA_VAL_W), lambda b, t: (b, t, 1)),
            pl.BlockSpec((NB, C, GLA_VAL_W), lambda b, t: (b, t, 2)),
            pl.BlockSpec((NB, C, LANES), lambda b, t: (b, t, AB_SMALL_BLK)),
            pl.BlockSpec((GLA_LOWRANK, GLA_KEY_W), lambda b, t: (0, 0)),
            pl.BlockSpec((1, GLA_KEY_W), lambda b, t: (0, 0)),
            pl.BlockSpec((1, GLA_DV), lambda b, t: (0, 0)),
            pl.BlockSpec((NB, GLA_HEADS, GLA_DK, GLA_DV), lambda b, t: (b, 0, 0, 0)),
        ],
        out_specs=(pl.BlockSpec((NB, C, GLA_VAL_W), lambda b, t: (b, t, 0)),
                   pl.BlockSpec((NB, GLA_HEADS, GLA_DK, GLA_DV), lambda b, t: (b, 0, 0, 0))),
        scratch_shapes=[
            pltpu.VMEM((NB, GLA_KEY_W, GLA_VAL_W), F32),
            pltpu.VMEM((NB, C, GLA_KEY_W), F32),
            pltpu.VMEM((NB, C, GLA_KEY_W), F32),
            pltpu.VMEM((NB, C, GLA_KEY_W), F32),
            pltpu.VMEM((NB, C, GLA_VAL_W), F32),
            pltpu.VMEM((NB, C, GLA_VAL_W), F32),
        ],
        compiler_params=_cparams(("arbitrary", "arbitrary")),
        name="gla",
    )(proj3, proj3, proj3, proj3, proj3, w_gk2, b_gk.reshape(1, GLA_KEY_W), gla_norm.reshape(1, GLA_DV), s0)
    return o.reshape(B * T, GLA_VAL_W), s_fin


def _compress_pages(x_ref, n_pages, pe_ref, w1_ref, w2_ref):
    outs = []
    for half in range(PAGE_SIZE // NSA_BLOCK):
        pieces = [x_ref[pl.ds(half * NSA_BLOCK + tk, n_pages, stride=PAGE_SIZE), :] for tk in range(NSA_BLOCK)]
        flat = jnp.concatenate(pieces, axis=1) + pe_ref[...]
        acc = _mm(flat, w1_ref[...])
        outs.append(_mm(_silu(acc), w2_ref[...]))
    return jnp.concatenate(outs, axis=1)


def _compress_dense_kernel(xk_ref, xv_ref, pek_ref, pev_ref, w1k_ref, w1v_ref, w2k_ref, w2v_ref,
                           ok_ref, ov_ref, *, n_pages):
    ok_ref[0] = _compress_pages(xk_ref, n_pages, pek_ref, w1k_ref, w2k_ref)
    ov_ref[0] = _compress_pages(xv_ref, n_pages, pev_ref, w1v_ref, w2v_ref)


def _cmp_weights(cmp_pe, cmp_w1, cmp_w2):
    out = []
    for i in range(2):
        pe2 = jnp.concatenate([cmp_pe[i], cmp_pe[i]], axis=1).reshape(1, NSA_BLOCK * LANES)
        w1 = cmp_w1[i].reshape(NSA_BLOCK, NSA_HEAD_DIM, NSA_CMP_HIDDEN)
        z1 = jnp.zeros_like(w1)
        w1bd = jnp.concatenate([jnp.concatenate([w1, z1], axis=2),
                                jnp.concatenate([z1, w1], axis=2)], axis=1).astype(BF16)
        w1bd = w1bd.reshape(NSA_BLOCK * LANES, 2 * NSA_CMP_HIDDEN)
        w2 = cmp_w2[i]
        z2 = jnp.zeros_like(w2)
        w2bd = jnp.concatenate([jnp.concatenate([w2, z2], axis=1),
                                jnp.concatenate([z2, w2], axis=1)], axis=0).astype(BF16)
        out.append((pe2, w1bd, w2bd))
    return out


def _compress_dense(proj, B, T, cw):
    n_pages = T // PAGE_SIZE
    (pek, w1k, w2k), (pev, w1v, w2v) = cw
    full = lambda a: pl.BlockSpec(a.shape, lambda b: (0,) * a.ndim)
    ok, ov = pl.pallas_call(
        functools.partial(_compress_dense_kernel, n_pages=n_pages),
        out_shape=(jax.ShapeDtypeStruct((B, n_pages, 2 * LANES), F32),) * 2,
        grid=(B,),
        in_specs=[
            pl.BlockSpec((T, LANES), lambda b: (b, AB_KV_BLK)),
            pl.BlockSpec((T, LANES), lambda b: (b, AB_KV_BLK + 1)),
            full(pek), full(pev), full(w1k), full(w1v), full(w2k), full(w2v),
        ],
        out_specs=(pl.BlockSpec((1, n_pages, 2 * LANES), lambda b: (b, 0, 0)),) * 2,
        compiler_params=_cparams(("arbitrary",)),
        name="compress_dense",
    )(proj, proj, pek, pev, w1k, w1v, w2k, w2v)
    n_blk = T // NSA_BLOCK
    return ok.reshape(B, n_blk, LANES), ov.reshape(B, n_blk, LANES)


def _gather_pages(pt_ref, b, pool_ref, buf_ref, sem, n_pages, start):
    def body(p, carry):
        page = pt_ref[b, p]
        cp = pltpu.make_async_copy(pool_ref.at[pl.ds(page * PAGE_SIZE, PAGE_SIZE), :],
                                   buf_ref.at[pl.ds(p * PAGE_SIZE, PAGE_SIZE), :], sem)
        if start:
            cp.start()
        else:
            cp.wait()
        return carry
    lax.fori_loop(0, n_pages, body, 0)


def _gather_pages_dmajor(pt_ref, b, pool_ref, buf_ref, sem, n_pages, start):
    def body(p, carry):
        page = pt_ref[b, p]
        cp = pltpu.make_async_copy(pool_ref.at[pl.ds(page * PAGE_SIZE, PAGE_SIZE), :],
                                   buf_ref.at[:, p, :], sem)
        if start:
            cp.start()
        else:
            cp.wait()
        return carry
    lax.fori_loop(0, n_pages, body, 0)


def _compress_pages_t(x_ref, n_pages, pe_ref, w1_ref, w2_ref):
    per_g = []
    for g in range(NSA_KV_HEADS):
        pieces = [x_ref[g * NSA_HEAD_DIM + d] for d in range(NSA_HEAD_DIM)]
        flat = jnp.concatenate(pieces, axis=1) + pe_ref[...]
        acc = _mm(flat, w1_ref[...])
        per_g.append(_mm(_silu(acc), w2_ref[...]))
    hd = NSA_HEAD_DIM
    return jnp.concatenate([per_g[0][:, 0:hd], per_g[1][:, 0:hd], per_g[0][:, hd:2 * hd], per_g[1][:, hd:2 * hd]], axis=1)


def _cmp_weights_t(cmp_pe, cmp_w1, cmp_w2):
    out = []
    for i in range(2):
        pe_t = jnp.concatenate([cmp_pe[i].T, cmp_pe[i].T], axis=1).reshape(1, NSA_HEAD_DIM * PAGE_SIZE)
        w1 = jnp.transpose(cmp_w1[i].reshape(NSA_BLOCK, NSA_HEAD_DIM, NSA_CMP_HIDDEN), (1, 0, 2))
        z1 = jnp.zeros_like(w1)
        w1t = jnp.concatenate([jnp.concatenate([w1, z1], axis=2),
                               jnp.concatenate([z1, w1], axis=2)], axis=1).astype(BF16)
        w1t = w1t.reshape(NSA_HEAD_DIM * PAGE_SIZE, 2 * NSA_CMP_HIDDEN)
        w2 = cmp_w2[i]
        z2 = jnp.zeros_like(w2)
        w2bd = jnp.concatenate([jnp.concatenate([w2, z2], axis=1),
                                jnp.concatenate([z2, w2], axis=1)], axis=0).astype(BF16)
        out.append((pe_t, w1t, w2bd))
    return out


def _compress_paged_kernel(pt_ref, poolk_ref, poolv_ref, pek_ref, pev_ref, w1k_ref, w1v_ref,
                           w2k_ref, w2v_ref, ok_ref, ov_ref, bufk, bufv, sems, *, n_pages):
    b = pl.program_id(0)
    slot = b % 2
    @pl.when(b == 0)
    def _():
        _gather_pages_dmajor(pt_ref, b, poolk_ref, bufk.at[0], sems.at[0], n_pages, True)

    _gather_pages_dmajor(pt_ref, b, poolv_ref, bufv, sems.at[2], n_pages, True)

    @pl.when(b + 1 < pl.num_programs(0))
    def _():
        _gather_pages_dmajor(pt_ref, b + 1, poolk_ref, bufk.at[1 - slot], sems.at[1 - slot], n_pages, True)

    _gather_pages_dmajor(pt_ref, b, poolk_ref, bufk.at[slot], sems.at[slot], n_pages, False)
    ok_ref[0] = _compress_pages_t(bufk.at[slot], n_pages, pek_ref, w1k_ref, w2k_ref)
    _gather_pages_dmajor(pt_ref, b, poolv_ref, bufv, sems.at[2], n_pages, False)
    ov_ref[0] = _compress_pages_t(bufv, n_pages, pev_ref, w1v_ref, w2v_ref)


def _compress_paged(page_table, pool_k, pool_v, cw):
    B, n_pages = page_table.shape
    (pek, w1k, w2k), (pev, w1v, w2v) = cw
    full = lambda a: pl.BlockSpec(a.shape, lambda b, pt: (0,) * a.ndim)
    ok, ov = pl.pallas_call(
        functools.partial(_compress_paged_kernel, n_pages=n_pages),
        out_shape=(jax.ShapeDtypeStruct((B, n_pages, 2 * LANES), F32),) * 2,
        grid_spec=pltpu.PrefetchScalarGridSpec(
            num_scalar_prefetch=1,
            grid=(B,),
            in_specs=[
                pl.BlockSpec(memory_space=pl.ANY),
                pl.BlockSpec(memory_space=pl.ANY),
                full(pek), full(pev), full(w1k), full(w1v), full(w2k), full(w2v),
            ],
            out_specs=(pl.BlockSpec((1, n_pages, 2 * LANES), lambda b, pt: (b, 0, 0)),) * 2,
            scratch_shapes=[
                pltpu.VMEM((2, PAGE_SIZE, n_pages, LANES), F32),
                pltpu.VMEM((PAGE_SIZE, n_pages, LANES), F32),
                pltpu.SemaphoreType.DMA((3,)),
            ],
        ),
        compiler_params=_cparams(("arbitrary",)),
        name="compress_paged",
    )(page_table, pool_k, pool_v, pek, pev, w1k, w1v, w2k, w2v)
    n_blk = n_pages * (PAGE_SIZE // NSA_BLOCK)
    return ok.reshape(B, n_blk, LANES), ov.reshape(B, n_blk, LANES)


def _stack_queries(q, g, tq):
    rows = []
    for hl in range(NSA_HPG):
        qh = q[:, hl * NSA_HEAD_DIM:(hl + 1) * NSA_HEAD_DIM]
        rows.append(jnp.concatenate([qh, qh], axis=1))
    qs = jnp.concatenate(rows, axis=0) * (NSA_HEAD_DIM ** -0.5)
    half = lax.broadcasted_iota(jnp.int32, qs.shape, 1) // NSA_HEAD_DIM
    return jnp.where(half == g, qs, 0.0).astype(BF16)


def _row_slopes(g, tq):
    hl = lax.broadcasted_iota(jnp.int32, (NSA_HPG * tq, 1), 0) // tq
    s = jnp.where(hl == 0, 0.5, jnp.where(hl == 1, 0.25, jnp.where(hl == 2, 0.125, 0.0625)))
    return s * jnp.where(g == 0, 1.0, 0.0625)


def _gate_columns(sm, g, tq):
    sig = _sigmoid(sm)
    lane = lax.broadcasted_iota(jnp.int32, sm.shape, 1)
    cols = []
    for br in range(3):
        per_head = []
        for hl in range(NSA_HPG):
            target = GATE_LANE0 + 3 * (NSA_HPG * g + hl) + br
            per_head.append(jnp.sum(jnp.where(lane == target, sig, 0.0), axis=-1, keepdims=True))
        cols.append(jnp.concatenate(per_head, axis=0))
    return cols


def _topk_select(score, k_sel, n):
    idx = lax.broadcasted_iota(jnp.int32, score.shape, 1)
    rank = jnp.zeros(score.shape, F32)
    for j in range(n):
        col = score[:, j:j + 1]
        beats = (col > score) | ((col >= score) & (idx > j))
        rank = rank + jnp.where(beats, 1.0, 0.0)
    return rank < k_sel


def _topk_select_t(score, k_sel, n):
    st = score.T[0:n]
    idx = lax.broadcasted_iota(jnp.int32, st.shape, 0)
    rank = jnp.zeros(st.shape, F32)
    for j in range(n):
        row = st[j:j + 1, :]
        beats = (row > st) | ((row >= st) & (idx > j))
        rank = rank + jnp.where(beats, 1.0, 0.0)
    sel_t = jnp.where(rank < k_sel, 1.0, 0.0)
    sel_t = jnp.concatenate([sel_t, jnp.zeros((score.shape[1] - n, st.shape[1]), F32)], axis=0)
    return sel_t.T > 0.5


def _unstack_heads(o, g, tq):
    og = jnp.where(g == 0, o[:, 0:NSA_HEAD_DIM], o[:, NSA_HEAD_DIM:2 * NSA_HEAD_DIM])
    return jnp.concatenate([og[hl * tq:(hl + 1) * tq] for hl in range(NSA_HPG)], axis=1)


def _key_features(T, onehot):
    j = lax.broadcasted_iota(jnp.int32, (T, LANES), 0)
    lane = lax.broadcasted_iota(jnp.int32, (T, LANES), 1)
    blk = j // NSA_BLOCK
    f = jnp.where(lane == FEAT_BLK, blk.astype(F32),
                  jnp.where(lane == FEAT_OFF, (j % NSA_BLOCK).astype(F32), 0.0))
    if onehot:
        f = jnp.where(lane == blk, 1.0, f)
    return f.astype(BF16)


def _ones_column(rows):
    lane = lax.broadcasted_iota(jnp.int32, (rows, LANES), 1)
    return jnp.where(lane == 0, 1.0, 0.0).astype(BF16)


def _nsa_prompt_kernel(q_ref, sm_ref, kc_ref, vc_ref, ks_ref, vs_ref, kw_ref, vw_ref, o_ref,
                       ksb, vsb, kwb, vwb, s_sc, m_sc, acc_sc, *, T, TQ, WIN, CH, NB):
    qt = pl.program_id(1)
    R = NSA_HPG * TQ
    R2 = NSA_KV_HEADS * R
    n_blk = T // NSA_BLOCK
    G = range(NSA_KV_HEADS)
    N = range(NB)
    PAD = NSA_WINDOW

    @pl.when(qt == 0)
    def _():
        lane = lax.broadcasted_iota(jnp.int32, (PAD, LANES), 1)
        for n in N:
            ksb[n, :, 0:LANES] = ks_ref[n].astype(BF16)
            ksb[n, :, LANES:2 * LANES] = _key_features(T, True)
            vsb[n, :, 0:LANES] = vs_ref[n].astype(BF16)
            vsb[n, :, LANES:2 * LANES] = _ones_column(T)
            kwb[n, 0:PAD, 0:LANES] = jnp.zeros((PAD, LANES), BF16)
            kwb[n, 0:PAD, LANES:2 * LANES] = jnp.where(lane == FEAT_PAD, 1.0, 0.0).astype(BF16)
            kwb[n, PAD:PAD + T, 0:LANES] = kw_ref[n].astype(BF16)
            kwb[n, PAD:PAD + T, LANES:2 * LANES] = _key_features(T, False)
            vwb[n, 0:PAD, :] = jnp.zeros((PAD, 2 * LANES), BF16)
            vwb[n, PAD:PAD + T, 0:LANES] = vw_ref[n].astype(BF16)
            vwb[n, PAD:PAD + T, LANES:2 * LANES] = _ones_column(T)

    slope = jnp.concatenate([_row_slopes(g, TQ) for g in G], axis=0)
    off_q = lax.broadcasted_iota(jnp.int32, (R2, 1), 0) % TQ
    tq_i = qt * TQ + off_q
    tq_f = tq_i.astype(F32)
    lane = lax.broadcasted_iota(jnp.int32, (R2, LANES), 1)
    feat = jnp.where(lane == FEAT_BLK, slope * NSA_BLOCK,
                     jnp.where(lane == FEAT_OFF, slope, jnp.where(lane == FEAT_PAD, -SEL_BIG, 0.0)))
    feat_b = feat.astype(BF16)
    zpad = jnp.zeros((LANES - n_blk, LANES), F32)
    n_i = lax.broadcasted_iota(jnp.int32, (1, LANES), 1)
    center = (n_i * NSA_BLOCK).astype(F32) + 0.5 * (NSA_BLOCK - 1)
    cur0 = pl.multiple_of(qt * TQ, TQ)
    nt_dims = (((1,), (1,)), ((), ()))
    c_first = lax.broadcasted_iota(jnp.int32, (1, LANES), 1)
    c_tail = lax.broadcasted_iota(jnp.int32, (1, WIN - PAD), 1)
    tq1 = tq_i[0:NSA_KV_HEADS * TQ]
    cur = tq1 // NSA_BLOCK
    forced = (n_i == 0) | (n_i == cur) | (n_i == cur - 1)
    visible = n_i * NSA_BLOCK <= tq1
    off_k = lax.broadcasted_iota(jnp.int32, (1, TQ), 1)

    qs = [jnp.concatenate([_stack_queries(q_ref[n, :, g * 256:(g + 1) * 256], g, TQ) for g in G], axis=0) for n in N]
    q_plain = [jnp.concatenate([qs[n], feat_b], axis=1) for n in N]
    gates = [[_gate_columns(sm_ref[n], g, TQ) for g in G] for n in N]
    gcol = [[jnp.concatenate([gates[n][g][br] for g in G], axis=0) for br in range(3)] for n in N]

    s_c = [_mm_nt(qs[n], jnp.concatenate([kc_ref[n], zpad], axis=0)) - slope * (tq_f - center) for n in N]
    p_c = [_masked_softmax(s_c[n], (n_i * NSA_BLOCK + NSA_BLOCK - 1) <= tq_i) for n in N]
    o_c = [_mm(p_c[n], jnp.concatenate([vc_ref[n], zpad], axis=0)) for n in N]

    o_w = []
    for n in N:
        s_w = lax.dot_general(q_plain[n], kwb[n, pl.ds(cur0, WIN), :], nt_dims, preferred_element_type=F32)
        s_w = jnp.concatenate([jnp.where(c_first >= off_q, s_w[:, 0:LANES], NEG),
                               s_w[:, LANES:PAD],
                               jnp.where(c_tail <= off_q, s_w[:, PAD:WIN], NEG)], axis=1)
        e_w = jnp.exp(s_w - jnp.max(s_w, axis=-1, keepdims=True))
        acc_w = jnp.dot(e_w.astype(BF16), vwb[n, pl.ds(cur0, WIN), :], preferred_element_type=F32)
        o_w.append(acc_w[:, 0:LANES] * (1.0 / acc_w[:, LANES:LANES + 1]))

    q_sel, s_cur = [], []
    for n in N:
        scores = []
        for g in G:
            sc = p_c[n][g * R:g * R + TQ]
            for hl in range(1, NSA_HPG):
                sc = sc + p_c[n][g * R + hl * TQ:g * R + (hl + 1) * TQ]
            scores.append(sc)
        score = jnp.concatenate(scores, axis=0)
        score = jnp.where(visible, jnp.where(forced, FORCED_SCORE, score), -1.0)
        sel = _topk_select_t(score, min(NSA_TOP_K, n_blk), n_blk)
        sel_bias = jnp.where(sel & (n_i < cur), 0.0, -SEL_BIG)
        sel_bias = jnp.concatenate([sel_bias[g * TQ:(g + 1) * TQ] for g in G for _ in range(NSA_HPG)], axis=0)
        q_sel.append(jnp.concatenate([qs[n], jnp.where(lane >= FEAT_BLK, feat, sel_bias).astype(BF16)], axis=1))
        sc_own = lax.dot_general(q_plain[n], ksb[n, pl.ds(cur0, TQ), :], nt_dims, preferred_element_type=F32)
        s_cur.append(jnp.where(off_k <= off_q, sc_own, NEG))

    n_ch = T // CH
    m_sc[...] = jnp.full((NB, R2, LANES), NEG, F32)
    for c in range(n_ch):
        @pl.when(c * CH < cur0)
        def _():
            for n in N:
                s = lax.dot_general(q_sel[n], ksb[n, c * CH:(c + 1) * CH, :], nt_dims, preferred_element_type=F32)
                s_sc[n, :, c * CH:(c + 1) * CH] = s
                mm = m_sc[n]
                for i in range(CH // LANES):
                    mm = jnp.maximum(mm, s[:, i * LANES:(i + 1) * LANES])
                m_sc[n] = mm
    m = [jnp.maximum(jnp.max(m_sc[n], axis=-1, keepdims=True), jnp.max(s_cur[n], axis=-1, keepdims=True)) for n in N]
    for n in N:
        e_cur = jnp.exp(s_cur[n] - m[n])
        acc_sc[n] = jnp.dot(e_cur.astype(BF16), vsb[n, pl.ds(cur0, TQ), :], preferred_element_type=F32)
    for c in range(n_ch):
        @pl.when(c * CH < cur0)
        def _():
            for n in N:
                e = jnp.exp(s_sc[n, :, c * CH:(c + 1) * CH] - m[n])
                acc_sc[n] += jnp.dot(e.astype(BF16), vsb[n, c * CH:(c + 1) * CH, :], preferred_element_type=F32)
    for n in N:
        acc = acc_sc[n]
        o_s = acc[:, 0:LANES] * (1.0 / acc[:, LANES:LANES + 1])
        gc, gs, gw = gcol[n]
        o = gc * o_c[n] + gs * o_s + gw * o_w[n]
        o_ref[n] = jnp.concatenate([_unstack_heads(o[g * R:(g + 1) * R], g, TQ) for g in G], axis=1).astype(o_ref.dtype)


def _nsa_prompt(proj, B, T, kc, vc):
    TQ = NSA_BLOCK
    nq = T // TQ
    WIN = NSA_WINDOW + TQ
    CH = min(512, T)
    NB = 2 if B % 2 == 0 else 1
    R2 = NSA_KV_HEADS * NSA_HPG * TQ
    proj3 = proj.reshape(B, T, AB_COLS)
    kv = lambda j: pl.BlockSpec((NB, T, LANES), lambda b, t: (b, 0, AB_KV_BLK + j))
    n_blk = T // NSA_BLOCK
    o = pl.pallas_call(
        functools.partial(_nsa_prompt_kernel, T=T, TQ=TQ, WIN=WIN, CH=CH, NB=NB),
        out_shape=jax.ShapeDtypeStruct((B, T, NSA_Q_W), BF16),
        grid=(B // NB, nq),
        in_specs=[
            pl.BlockSpec((NB, TQ, NSA_Q_W), lambda b, t: (b, t, 3)),
            pl.BlockSpec((NB, TQ, LANES), lambda b, t: (b, t, AB_SMALL_BLK)),
            pl.BlockSpec((NB, n_blk, LANES), lambda b, t: (b, 0, 0)),
            pl.BlockSpec((NB, n_blk, LANES), lambda b, t: (b, 0, 0)),
            kv(2), kv(3), kv(4), kv(5),
        ],
        out_specs=pl.BlockSpec((NB, TQ, NSA_Q_W), lambda b, t: (b, t, 0)),
        scratch_shapes=[pltpu.VMEM((NB, T, 2 * LANES), BF16), pltpu.VMEM((NB, T, 2 * LANES), BF16),
                        pltpu.VMEM((NB, T + NSA_WINDOW, 2 * LANES), BF16),
                        pltpu.VMEM((NB, T + NSA_WINDOW, 2 * LANES), BF16),
                        pltpu.VMEM((NB, R2, T), F32), pltpu.VMEM((NB, R2, LANES), F32),
                        pltpu.VMEM((NB, R2, 2 * LANES), F32)],
        compiler_params=_cparams(("arbitrary", "arbitrary")),
        name="nsa_prompt",
    )(proj3, proj3, kc, vc, proj3, proj3, proj3, proj3)
    return o.reshape(B * T, NSA_Q_W)


def _nsa_sample_kernel(pt_ref, q_ref, sm_ref, kc_ref, vc_ref, poolk_ref, poolv_ref,
                       kn_ref, vn_ref, wk_ref, wv_ref, kwn_ref, vwn_ref,
                       o_ref, wko_ref, wvo_ref, bufk, bufv, sc_sc, sems, *, n_pages, TQ):
    b = pl.program_id(0)
    past = n_pages * PAGE_SIZE
    n_cmp = past // NSA_BLOCK
    n_sel = n_cmp + 1
    R = NSA_HPG * TQ
    R2 = NSA_KV_HEADS * R
    n_buf = wk_ref.shape[2]
    G = range(NSA_KV_HEADS)

    slot = b % 2
    @pl.when(b == 0)
    def _():
        _gather_pages(pt_ref, b, poolk_ref, bufk.at[0], sems.at[0], n_pages, True)

    _gather_pages(pt_ref, b, poolv_ref, bufv, sems.at[2], n_pages, True)

    @pl.when(b + 1 < pl.num_programs(0))
    def _():
        _gather_pages(pt_ref, b + 1, poolk_ref, bufk.at[1 - slot], sems.at[1 - slot], n_pages, True)

    qs = jnp.concatenate([_stack_queries(q_ref[:, g * 256:(g + 1) * 256], g, TQ) for g in G], axis=0)
    slope = jnp.concatenate([_row_slopes(g, TQ) for g in G], axis=0)
    gates = [_gate_columns(sm_ref[...], g, TQ) for g in G]
    gc, gs, gw = [jnp.concatenate([gates[g][br] for g in G], axis=0) for br in range(3)]
    tq_i = past + lax.broadcasted_iota(jnp.int32, (R2, 1), 0) % TQ
    tq_f = tq_i.astype(F32)
    new_i = past + lax.broadcasted_iota(jnp.int32, (1, TQ), 1)

    n_i = lax.broadcasted_iota(jnp.int32, (1, n_cmp), 1)
    center = (n_i * NSA_BLOCK).astype(F32) + 0.5 * (NSA_BLOCK - 1)
    s_c = _mm_nt(qs, kc_ref[0]) - slope * (tq_f - center)
    p_c = _masked_softmax(s_c, (n_i * NSA_BLOCK + NSA_BLOCK - 1) <= tq_i)
    o_c = _mm(p_c, vc_ref[0])

    bias_rows = []
    for g in G:
        score = p_c[g * R:g * R + TQ]
        for hl in range(1, NSA_HPG):
            score = score + p_c[g * R + hl * TQ:g * R + (hl + 1) * TQ]
        forced = (n_i == 0) | (n_i == n_cmp - 1)
        score = jnp.where(forced, FORCED_SCORE, score)
        sel = _topk_select(score, min(NSA_TOP_K, n_sel) - 1, n_cmp)
        bias_rows += [jnp.where(sel, 0.0, NEG)] * NSA_HPG
    sel_bias = jnp.concatenate(bias_rows, axis=0)

    wk_t = wk_ref[0]
    wv_t = wv_ref[0]
    wpos = past - n_buf + lax.broadcasted_iota(jnp.int32, (1, n_buf), 1)
    d_o = tq_i - wpos
    d_n = tq_i - new_i
    m_o = (wpos >= 0) & (d_o >= 0) & (d_o <= NSA_WINDOW)
    m_n = (d_n >= 0) & (d_n <= NSA_WINDOW)
    s_wo = jnp.where(m_o, _mm(qs, wk_t) - slope * d_o.astype(F32), NEG)
    s_wn = jnp.where(m_n, _mm_nt(qs, kwn_ref[...]) - slope * d_n.astype(F32), NEG)
    mw = jnp.maximum(jnp.max(s_wo, axis=-1, keepdims=True), jnp.max(s_wn, axis=-1, keepdims=True))
    e_o = jnp.where(m_o, jnp.exp(s_wo - mw), 0.0)
    e_w = jnp.where(m_n, jnp.exp(s_wn - mw), 0.0)
    den_w = jnp.maximum(jnp.sum(e_o, axis=-1, keepdims=True) + jnp.sum(e_w, axis=-1, keepdims=True), 1e-30)
    o_w = (_mm_nt(e_o, wv_t) + _mm(e_w, vwn_ref[...])) * (1.0 / den_w)

    lane_w = lax.broadcasted_iota(jnp.int32, (LANES, n_buf), 1)

    def shifted(old_t, new):
        slots = jnp.concatenate([jnp.zeros((LANES - TQ, LANES), F32), new], axis=0)
        tail = jnp.concatenate([jnp.zeros((LANES, n_buf - LANES), F32), slots.T], axis=1)
        return jnp.where(lane_w >= n_buf - TQ, tail, pltpu.roll(old_t, n_buf - TQ, axis=1))

    wko_ref[0] = shifted(wk_t, kwn_ref[...])
    wvo_ref[0] = shifted(wv_t, vwn_ref[...])

    _gather_pages(pt_ref, b, poolk_ref, bufk.at[slot], sems.at[slot], n_pages, False)
    tok = lax.broadcasted_iota(jnp.int32, (1, PAGE_SIZE), 1)
    second = tok >= NSA_BLOCK
    m_run = jnp.full((R2, PAGE_SIZE), NEG, F32)
    for p in range(n_pages):
        s = jnp.dot(qs, bufk[slot, p * PAGE_SIZE:(p + 1) * PAGE_SIZE, :].astype(BF16), preferred_element_type=F32)
        bias = jnp.where(second, sel_bias[:, 2 * p + 1:2 * p + 2], sel_bias[:, 2 * p:2 * p + 1])
        s = s + bias - slope * (tq_f - (p * PAGE_SIZE + tok).astype(F32))
        sc_sc[p] = s
        m_run = jnp.maximum(m_run, s)
    s_n = jnp.where(new_i <= tq_i, _mm_nt(qs, kn_ref[...]) - slope * (tq_i - new_i).astype(F32), NEG)
    m = jnp.maximum(jnp.max(m_run, axis=-1, keepdims=True), jnp.max(s_n, axis=-1, keepdims=True))

    _gather_pages(pt_ref, b, poolv_ref, bufv, sems.at[2], n_pages, False)
    e_n = jnp.exp(s_n - m)
    acc = _mm(e_n, vn_ref[...])
    den_run = jnp.zeros((R2, PAGE_SIZE), F32)
    for p in range(n_pages):
        e = jnp.exp(sc_sc[p] - m)
        den_run = den_run + e
        acc = acc + _mm_nt(e, bufv[p * PAGE_SIZE:(p + 1) * PAGE_SIZE, :])
    den = jnp.sum(den_run, axis=-1, keepdims=True) + jnp.sum(e_n, axis=-1, keepdims=True)
    o_s = acc * (1.0 / den)

    o = gc * o_c + gs * o_s + gw * o_w
    o_ref[...] = jnp.concatenate([_unstack_heads(o[g * R:(g + 1) * R], g, TQ) for g in G], axis=1).astype(o_ref.dtype)


def _nsa_sample(proj, page_table, kc, vc, pool_k, pool_v, win_k, win_v):
    B, n_pages = page_table.shape
    TQ = proj.shape[0] // B
    past = n_pages * PAGE_SIZE
    n_cmp = past // NSA_BLOCK
    n_buf = win_k.shape[2]
    R2 = NSA_KV_HEADS * NSA_HPG * TQ
    kvn = lambda j: pl.BlockSpec((TQ, LANES), lambda b, pt: (b, AB_KV_BLK + j))
    win = pl.BlockSpec((1, LANES, n_buf), lambda b, pt: (b, 0, 0))
    return pl.pallas_call(
        functools.partial(_nsa_sample_kernel, n_pages=n_pages, TQ=TQ),
        out_shape=(jax.ShapeDtypeStruct((B * TQ, NSA_Q_W), BF16),
                   jax.ShapeDtypeStruct((B, LANES, n_buf), F32),
                   jax.ShapeDtypeStruct((B, LANES, n_buf), F32)),
        grid_spec=pltpu.PrefetchScalarGridSpec(
            num_scalar_prefetch=1,
            grid=(B,),
            in_specs=[
                pl.BlockSpec((TQ, NSA_Q_W), lambda b, pt: (b, 3)),
                pl.BlockSpec((TQ, LANES), lambda b, pt: (b, AB_SMALL_BLK)),
                pl.BlockSpec((1, n_cmp, LANES), lambda b, pt: (b, 0, 0)),
                pl.BlockSpec((1, n_cmp, LANES), lambda b, pt: (b, 0, 0)),
                pl.BlockSpec(memory_space=pl.ANY),
                pl.BlockSpec(memory_space=pl.ANY),
                kvn(2), kvn(3), win, win, kvn(4), kvn(5),
            ],
            out_specs=(pl.BlockSpec((TQ, NSA_Q_W), lambda b, pt: (b, 0)), win, win),
            scratch_shapes=[
                pltpu.VMEM((2, past, LANES), F32),
                pltpu.VMEM((past, LANES), F32),
                pltpu.VMEM((n_pages, R2, PAGE_SIZE), F32),
                pltpu.SemaphoreType.DMA((3,)),
            ],
        ),
        compiler_params=_cparams(("arbitrary",)),
        name="nsa_sample",
    )(page_table, proj, proj, kc, vc, pool_k, pool_v, proj, proj, win_k, win_v, proj, proj)


def _gdn_kernel(qkv_ref, z_ref, sm_ref, cw_ref, alog_ref, dtb_ref, ng_ref, s0_ref, cb_ref,
                o_ref, sfin_ref, S_sc, prev_sc, *, C, NB):
    t = pl.program_id(1)

    @pl.when(t == 0)
    def _():
        S_sc[...] = s0_ref[...]
        prev_sc[...] = cb_ref[...]

    def conv(n, c0):
        xe = jnp.concatenate([prev_sc[n, :, c0:c0 + LANES], qkv_ref[n, :, c0:c0 + LANES]], axis=0)
        y = xe[SUBLANES:] * cw_ref[GDN_CONV - 1:GDN_CONV, c0:c0 + LANES]
        for s in range(1, GDN_CONV):
            y = y + pltpu.roll(xe, s, axis=0)[SUBLANES:] * cw_ref[GDN_CONV - 1 - s:GDN_CONV - s, c0:c0 + LANES]
        return _silu(y)

    ii = lax.broadcasted_iota(jnp.int32, (C, C), 0)
    jj = lax.broadcasted_iota(jnp.int32, (C, C), 1)
    eye = (ii == jj).astype(F32)

    CH = [(n, h) for n in range(NB) for h in range(GDN_HEADS)]
    X = range(len(CH))
    beta_c, d_c, d_r = [], [], []
    for n in range(NB):
        sm = sm_ref[n]
        beta = _sigmoid(sm)
        gt = -jnp.exp(alog_ref[...]) * _softplus(sm + dtb_ref[...])
        d = _cumsum_rows(gt)
        dT = d.T
        for h in range(GDN_HEADS):
            beta_c.append(beta[:, h:h + 1])
            d_c.append(d[:, GDN_A_LANE0 + h:GDN_A_LANE0 + h + 1])
            d_r.append(dT[GDN_A_LANE0 + h:GDN_A_LANE0 + h + 1, :])
    q, k, v = [], [], []
    for n, h in CH:
        qh = conv(n, h * GDN_DK)
        kh = conv(n, GDN_W + h * GDN_DK)
        q.append(qh * lax.rsqrt(jnp.sum(qh * qh, axis=-1, keepdims=True) + L2_EPS) * (GDN_DK ** -0.5))
        k.append(kh * lax.rsqrt(jnp.sum(kh * kh, axis=-1, keepdims=True) + L2_EPS))
        v.append(conv(n, 2 * GDN_W + h * GDN_DV))
    decay = [jnp.exp(jnp.minimum(d_c[x] - d_r[x], 0.0)) for x in X]
    kb = [k[x] * beta_c[x] for x in X]
    g_kk = [_mm_nt(kb[x], k[x]) for x in X]
    g_qk = [_mm_nt(q[x], k[x]) for x in X]
    a = [jnp.where(ii > jj, g_kk[x] * decay[x], 0.0) for x in X]
    qk = [jnp.where(ii >= jj, g_qk[x] * decay[x], 0.0) for x in X]
    tinv = [eye - a[x] for x in X]
    p_split = [_split_bf16(a[x]) for x in X]
    n2 = 2
    while n2 < C:
        p = [_mm3(p_split[x], p_split[x]) for x in X]
        p_split = [_split_bf16(p[x]) for x in X]
        tinv = [tinv[x] + _mm3(_split_bf16(tinv[x]), p_split[x]) for x in X]
        n2 *= 2
    u = [_mm(tinv[x], v[x] * beta_c[x]) for x in X]
    w = [_mm(tinv[x], kb[x] * jnp.exp(d_c[x])) for x in X]
    S = [S_sc[n, h] for n, h in CH]
    w_s = [_mm(w[x], S[x]) for x in X]
    q_s = [_mm(q[x] * jnp.exp(d_c[x]), S[x]) for x in X]
    v_new = [u[x] - w_s[x] for x in X]
    o = [q_s[x] + _mm(qk[x], v_new[x]) for x in X]
    d_last = [d_c[x][C - 1:C, :] for x in X]
    upd = [_mm_tn(k[x] * jnp.exp(d_last[x] - d_c[x]), v_new[x]) for x in X]
    for x, (n, h) in enumerate(CH):
        S_sc[n, h] = S[x] * jnp.exp(d_last[x]) + upd[x]
        ms = jnp.mean(o[x] * o[x], axis=-1, keepdims=True)
        zh = z_ref[n, :, h * GDN_DV:(h + 1) * GDN_DV]
        o_ref[n, :, h * GDN_DV:(h + 1) * GDN_DV] = (
            o[x] * lax.rsqrt(ms + RMS_EPS) * ng_ref[...] * _silu(zh)).astype(o_ref.dtype)

    for n in range(NB):
        prev_sc[n] = qkv_ref[n, C - SUBLANES:C, :]

    @pl.when(t == pl.num_programs(1) - 1)
    def _():
        sfin_ref[...] = S_sc[...]


def _gdn(proj, B, T, conv_w, a_log, dt_bias, norm_g, s0, conv_buf8, C):
    nt = T // C
    NB = 4 if B % 4 == 0 else (2 if B % 2 == 0 else 1)
    pad = lambda v: jnp.zeros((1, LANES), F32).at[0, GDN_A_LANE0:GDN_A_LANE0 + GDN_HEADS].set(v)
    proj3 = proj.reshape(B, T, C_COLS)
    o, s_fin = pl.pallas_call(
        functools.partial(_gdn_kernel, C=C, NB=NB),
        out_shape=(jax.ShapeDtypeStruct((B, T, GDN_W), BF16),
                   jax.ShapeDtypeStruct((B, GDN_HEADS, GDN_DK, GDN_DV), F32)),
        grid=(B // NB, nt),
        in_specs=[
            pl.BlockSpec((NB, C, GDN_CONV_CH), lambda b, t: (b, t, 0)),
            pl.BlockSpec((NB, C, GDN_W), lambda b, t: (b, t, 3)),
            pl.BlockSpec((NB, C, LANES), lambda b, t: (b, t, C_SMALL_BLK)),
            pl.BlockSpec((GDN_CONV, GDN_CONV_CH), lambda b, t: (0, 0)),
            pl.BlockSpec((1, LANES), lambda b, t: (0, 0)),
            pl.BlockSpec((1, LANES), lambda b, t: (0, 0)),
            pl.BlockSpec((1, GDN_DV), lambda b, t: (0, 0)),
            pl.BlockSpec((NB, GDN_HEADS, GDN_DK, GDN_DV), lambda b, t: (b, 0, 0, 0)),
            pl.BlockSpec((NB, SUBLANES, GDN_CONV_CH), lambda b, t: (b, 0, 0)),
        ],
        out_specs=(pl.BlockSpec((NB, C, GDN_W), lambda b, t: (b, t, 0)),
                   pl.BlockSpec((NB, GDN_HEADS, GDN_DK, GDN_DV), lambda b, t: (b, 0, 0, 0))),
        scratch_shapes=[
            pltpu.VMEM((NB, GDN_HEADS, GDN_DK, GDN_DV), F32),
            pltpu.VMEM((NB, SUBLANES, GDN_CONV_CH), F32),
        ],
        compiler_params=_cparams(("arbitrary", "arbitrary")),
        name="gdn",
    )(proj3, proj3, proj3, conv_w, pad(a_log), pad(dt_bias), norm_g.reshape(1, GDN_DV), s0, conv_buf8)
    return o.reshape(B * T, GDN_W), s_fin


def _ab_in_weight(w):
    big = w[:, :GLA_KEY_W * 2 + GLA_VAL_W * 2]
    gk = w[:, 1536:1536 + GLA_LOWRANK]
    rest = w[:, 1536 + GLA_LOWRANK:]
    q_b = rest[:, :NSA_Q_W]
    kv = rest[:, NSA_Q_W:NSA_Q_W + 6 * NSA_KV_W]
    gate = rest[:, NSA_Q_W + 6 * NSA_KV_W:]
    small = jnp.concatenate([gk, gate, jnp.zeros((D_MODEL, LANES - GLA_LOWRANK - 3 * NSA_HEADS), w.dtype)], axis=1)
    return jnp.concatenate([big, q_b, kv, small], axis=1).astype(BF16)


def _c_in_weight(w):
    qkv = w[:, :GDN_CONV_CH]
    ba = w[:, GDN_CONV_CH:GDN_CONV_CH + 2 * GDN_HEADS]
    z = w[:, GDN_CONV_CH + 2 * GDN_HEADS:]
    small = jnp.concatenate([ba, jnp.zeros((D_MODEL, LANES - 2 * GDN_HEADS), w.dtype)], axis=1)
    return jnp.concatenate([qkv, z, small], axis=1).astype(BF16)


def _kv_out(proj, B, T, j):
    return proj[:, (AB_KV_BLK + j) * LANES:(AB_KV_BLK + j + 1) * LANES].reshape(B, T, NSA_KV_HEADS, NSA_HEAD_DIM)


PROMPT_ROWS = 512
FFN_ROWS = 1024


def kernel(x_prompt, x_sample, c_prompt, c_sample, page_table, cache_cmp_k, cache_cmp_v, cache_sel_k, cache_sel_v, state_win_k, state_win_v, state_gla, state_gdn, state_gdn_conv, w_ada, b_ada, ln_g, ln_b, w_ffn_in, w_ffn_out, ab_w_in, ab_w_gk2, ab_b_gk, ab_gla_norm, ab_cmp_pe, ab_cmp_w1, ab_cmp_w2, ab_w_out, c_w_in, c_conv_w, c_a_log, c_dt_bias, c_norm, c_w_out):
    Bp, Tp, _ = x_prompt.shape
    Bs, Ts, _ = x_sample.shape
    n_pool = cache_cmp_k.shape[1]

    mods = _adaln(jnp.concatenate([c_prompt, c_sample], axis=0), w_ada, b_ada)

    def layer_mods(layer):
        m = mods[layer]
        parts = [m[:, i * D_MODEL:(i + 1) * D_MODEL] for i in range(6)]
        return [p[:Bp] for p in parts], [p[Bp:] for p in parts]

    xp, xs = x_prompt, x_sample
    ab_p, ab_s, c_p, c_s = [], [], [], []
    for layer in range(DEPTH):
        mp, ms = layer_mods(layer)
        wf_in = w_ffn_in[layer].astype(BF16)
        wf_out = w_ffn_out[layer].astype(BF16)
        i = layer // 2
        if layer % 2 == 0:
            w_in = _ab_in_weight(ab_w_in[i])
            w_out = ab_w_out[i].astype(BF16)
            wo_a, wo_b = w_out[:GLA_VAL_W], w_out[GLA_VAL_W:]
            cw = _cmp_weights(ab_cmp_pe[i], ab_cmp_w1[i], ab_cmp_w2[i])
            cw_t = _cmp_weights_t(ab_cmp_pe[i], ab_cmp_w1[i], ab_cmp_w2[i])

            proj, kv_t = _modmm(xp, mp[0], mp[1], w_in, PROMPT_ROWS, t_cols=(AB_KV_BLK * LANES, 6))
            zero_state = jnp.zeros((Bp, GLA_HEADS, GLA_DK, GLA_DV), F32)
            o_a, s_a = _gla(proj, Bp, Tp, ab_w_gk2[i], ab_b_gk[i], ab_gla_norm[i], zero_state, min(64, Tp))
            kc, vc = _compress_dense(proj, Bp, Tp, cw)
            o_b = _nsa_prompt(proj, Bp, Tp, kc, vc)
            x1 = _outproj_ln([o_a, o_b], [wo_a, wo_b], xp, mp[2], ln_g[layer, 0], ln_b[layer, 0], PROMPT_ROWS)
            n_keep = min(NSA_WINDOW, Tp)
            kv_out_t = lambda a: jnp.transpose(a.reshape(Bp, NSA_KV_HEADS, NSA_HEAD_DIM, a.shape[-1]), (0, 3, 1, 2))
            ab_p.append(tuple(kv_out_t(kv_t[j]) for j in range(4))
                        + (kv_out_t(kv_t[4][:, :, Tp - n_keep:]), kv_out_t(kv_t[5][:, :, Tp - n_keep:]), s_a))
            xp = _ffn_ln(x1, mp[3], mp[4], mp[5], wf_in, wf_out, ln_g[layer, 1], ln_b[layer, 1], FFN_ROWS)

            proj = _modmm(xs, ms[0], ms[1], w_in, PROMPT_ROWS)
            o_a, s_a = _gla(proj, Bs, Ts, ab_w_gk2[i], ab_b_gk[i], ab_gla_norm[i], state_gla[i], min(64, Ts))
            pool = lambda c: jnp.transpose(c[i], (0, 2, 3, 1)).reshape(n_pool * PAGE_SIZE, LANES)
            n_buf = state_win_k.shape[2]
            win_t = lambda w: jnp.transpose(w[i], (0, 2, 3, 1)).reshape(Bs, LANES, n_buf)
            kc, vc = _compress_paged(page_table, pool(cache_cmp_k), pool(cache_cmp_v), cw_t)
            o_b, win_k, win_v = _nsa_sample(proj, page_table, kc, vc, pool(cache_sel_k), pool(cache_sel_v),
                                            win_t(state_win_k), win_t(state_win_v))
            win_out = lambda w: jnp.transpose(w.reshape(Bs, NSA_KV_HEADS, NSA_HEAD_DIM, n_buf), (0, 3, 1, 2))
            x1 = _outproj_ln([o_a, o_b], [wo_a, wo_b], xs, ms[2], ln_g[layer, 0], ln_b[layer, 0], PROMPT_ROWS)
            ab_s.append(tuple(_kv_out(proj, Bs, Ts, j) for j in range(4))
                        + (win_out(win_k), win_out(win_v), s_a))
            xs = _ffn_ln(x1, ms[3], ms[4], ms[5], wf_in, wf_out, ln_g[layer, 1], ln_b[layer, 1], FFN_ROWS)
        else:
            w_in = _c_in_weight(c_w_in[i])
            w_out = c_w_out[i].astype(BF16)
            keep = GDN_CONV - 1

            proj = _modmm(xp, mp[0], mp[1], w_in, PROMPT_ROWS // 2)
            o_c, s_c = _gdn(proj, Bp, Tp, c_conv_w[i], c_a_log[i], c_dt_bias[i], c_norm[i],
                            jnp.zeros((Bp, GDN_HEADS, GDN_DK, GDN_DV), F32),
                            jnp.zeros((Bp, SUBLANES, GDN_CONV_CH), F32), min(64, Tp))
            x1 = _outproj_ln([o_c], [w_out], xp, mp[2], ln_g[layer, 0], ln_b[layer, 0], PROMPT_ROWS)
            c_p.append((s_c, proj.reshape(Bp, Tp, C_COLS)[:, Tp - keep:, :GDN_CONV_CH]))
            xp = _ffn_ln(x1, mp[3], mp[4], mp[5], wf_in, wf_out, ln_g[layer, 1], ln_b[layer, 1], FFN_ROWS)

            proj = _modmm(xs, ms[0], ms[1], w_in, PROMPT_ROWS // 2)
            conv8 = jnp.concatenate([jnp.zeros((Bs, SUBLANES - keep, GDN_CONV_CH), F32), state_gdn_conv[i]], axis=1)
            o_c, s_c = _gdn(proj, Bs, Ts, c_conv_w[i], c_a_log[i], c_dt_bias[i], c_norm[i],
                            state_gdn[i], conv8, min(64, Ts))
            x1 = _outproj_ln([o_c], [w_out], xs, ms[2], ln_g[layer, 0], ln_b[layer, 0], PROMPT_ROWS)
            qkv_s = proj.reshape(Bs, Ts, C_COLS)[:, :, :GDN_CONV_CH]
            c_s.append((s_c, jnp.concatenate([state_gdn_conv[i], qkv_s], axis=1)[:, -keep:]))
            xs = _ffn_ln(x1, ms[3], ms[4], ms[5], wf_in, wf_out, ln_g[layer, 1], ln_b[layer, 1], FFN_ROWS)

    stack = lambda sts: [jnp.stack(z) for z in zip(*sts)]
    p_ab, s_ab = stack(ab_p), stack(ab_s)
    p_c, s_c = stack(c_p), stack(c_s)
    return (xp, xs, *p_ab, *p_c, *s_ab, *s_c)
```

```python
import functools

import jax
import jax.numpy as jnp
from jax import lax
from jax.experimental import pallas as pl
from jax.experimental.pallas import tpu as pltpu

F32 = jnp.float32
BF16 = jnp.bfloat16

D_MODEL = 1024
DEPTH = 2
PAGE_SIZE = 128
GLA_HEADS = 4
GLA_DK = 64
GLA_DV = 128
GLA_LOWRANK = 16
GLA_GATE_NORM = 16.0
NSA_HEAD_DIM = 64
NSA_HEADS = 8
NSA_KV_HEADS = 2
NSA_HPG = 4
NSA_BLOCK = 64
NSA_TOP_K = 16
NSA_WINDOW = 512
NSA_CMP_HIDDEN = 128
FORCED_SCORE = 1000.0
GDN_HEADS = 8
GDN_DK = 128
GDN_DV = 128
GDN_CONV = 4
FF_HIDDEN = 2816
DEEPNORM_ALPHA = (2.0 * DEPTH) ** 0.25
LN_EPS = 1e-5
RMS_EPS = 1e-6
L2_EPS = 1e-6
NEG = -1e30

GLA_KEY_W = GLA_HEADS * GLA_DK
GLA_VAL_W = GLA_HEADS * GLA_DV
NSA_Q_W = NSA_HEADS * NSA_HEAD_DIM
NSA_KV_W = NSA_KV_HEADS * NSA_HEAD_DIM
GDN_W = GDN_HEADS * GDN_DK
GDN_CONV_CH = 3 * GDN_W

LANES = 128
SUBLANES = 8
VMEM_LIMIT = 56 * 1024 * 1024

AB_COLS = 2944
AB_SMALL_BLK = 22
AB_KV_BLK = 16
GATE_LANE0 = GLA_LOWRANK
C_COLS = 4224
C_SMALL_BLK = 32
GDN_A_LANE0 = GDN_HEADS
FEAT_BLK = 64
FEAT_OFF = 65
FEAT_PAD = 66
SEL_BIG = 2.0 ** 100


def _cparams(sem):
    return pltpu.CompilerParams(dimension_semantics=sem, vmem_limit_bytes=VMEM_LIMIT)


def _silu(x):
    return x * (1.0 / (1.0 + jnp.exp(-x)))


def _sigmoid(x):
    return 1.0 / (1.0 + jnp.exp(-x))


def _softplus(x):
    return jnp.maximum(x, 0.0) + jnp.log(1.0 + jnp.exp(-jnp.abs(x)))


def _mm(a, b):
    return jnp.dot(a.astype(BF16), b.astype(BF16), preferred_element_type=F32)


def _split_bf16(x):
    hi = x.astype(BF16)
    return hi, (x - hi.astype(F32)).astype(BF16)


def _mm3(a, b):
    dot = lambda x, y: jnp.dot(x, y, preferred_element_type=F32)
    return dot(a[0], b[0]) + (dot(a[1], b[0]) + dot(a[0], b[1]))


def _mm_nt(a, b):
    return lax.dot_general(a.astype(BF16), b.astype(BF16), (((1,), (1,)), ((), ())),
                           preferred_element_type=F32)


def _mm_tn(a, b):
    return lax.dot_general(a.astype(BF16), b.astype(BF16), (((0,), (0,)), ((), ())),
                           preferred_element_type=F32)


def _cumsum_rows(x):
    n = x.shape[0]
    row = lax.broadcasted_iota(jnp.int32, x.shape, 0)
    s = 1
    while s < n:
        x = x + jnp.where(row >= s, pltpu.roll(x, s, axis=0), 0.0)
        s *= 2
    return x


def _masked_softmax(s, mask):
    s = jnp.where(mask, s, NEG)
    m = jnp.max(s, axis=-1, keepdims=True)
    e = jnp.where(mask, jnp.exp(s - m), 0.0)
    den = jnp.maximum(jnp.sum(e, axis=-1, keepdims=True), 1e-30)
    return e * (1.0 / den)


def _layernorm(z, g, b):
    mu = jnp.mean(z, axis=-1, keepdims=True)
    zc = z - mu
    var = jnp.mean(zc * zc, axis=-1, keepdims=True)
    return zc * lax.rsqrt(var + LN_EPS) * g + b


def _adaln_kernel(c_ref, w_ref, b_ref, o_ref):
    c = _silu(c_ref[...])
    o_ref[0] = _mm(c, w_ref[0]) + b_ref[0]


def _adaln(c_all, w_ada, b_ada):
    n = c_all.shape[0]
    tn = 1536
    nt = (6 * D_MODEL) // tn
    return pl.pallas_call(
        _adaln_kernel,
        out_shape=jax.ShapeDtypeStruct((DEPTH, n, 6 * D_MODEL), F32),
        grid=(DEPTH, nt),
        in_specs=[
            pl.BlockSpec((n, D_MODEL), lambda l, j: (0, 0)),
            pl.BlockSpec((1, D_MODEL, tn), lambda l, j: (l, 0, j)),
            pl.BlockSpec((1, 1, tn), lambda l, j: (l, 0, j)),
        ],
        out_specs=pl.BlockSpec((1, n, tn), lambda l, j: (l, 0, j)),
        compiler_params=_cparams(("arbitrary", "arbitrary")),
        name="adaln",
    )(c_all, w_ada, b_ada.reshape(DEPTH, 1, 6 * D_MODEL))


def _modmm_kernel(x_ref, sh_ref, sc_ref, w_ref, o_ref, *t_refs, t_col0):
    bb, tt, d = x_ref.shape
    h = x_ref[...] * (1.0 + sc_ref[...]) + sh_ref[...]
    res = _mm(h.reshape(bb * tt, d), w_ref[...])
    o_ref[...] = res
    for t_ref in t_refs:
        for j in range(t_ref.shape[0]):
            t_ref[j, 0] = res[:, t_col0 + j * LANES:t_col0 + (j + 1) * LANES].T


def _row_tiling(B, T, max_rows):
    if T >= max_rows:
        return 1, max_rows
    bb = min(B, max_rows // T)
    return bb, T


def _modmm(x, shift, scale, w_bf16, max_rows, t_cols=None):
    B, T, D = x.shape
    N = w_bf16.shape[1]
    bb, tt = _row_tiling(B, T, max_rows)
    nt = T // tt
    out_shape = jax.ShapeDtypeStruct((B * T, N), F32)
    out_specs = pl.BlockSpec((bb * tt, N), lambda i, j: (i * nt + j, 0))
    t_col0 = 0
    if t_cols is not None:
        assert bb == 1
        t_col0, n_t = t_cols
        out_shape = (out_shape, jax.ShapeDtypeStruct((n_t, B, LANES, T), F32))
        out_specs = (out_specs, pl.BlockSpec((n_t, 1, LANES, tt), lambda i, j: (0, i, 0, j)))
    return pl.pallas_call(
        functools.partial(_modmm_kernel, t_col0=t_col0),
        out_shape=out_shape,
        grid=(B // bb, nt),
        in_specs=[
            pl.BlockSpec((bb, tt, D), lambda i, j: (i, j, 0)),
            pl.BlockSpec((bb, 1, D), lambda i, j: (i, 0, 0)),
            pl.BlockSpec((bb, 1, D), lambda i, j: (i, 0, 0)),
            pl.BlockSpec((D, N), lambda i, j: (0, 0)),
        ],
        out_specs=out_specs,
        compiler_params=_cparams(("arbitrary", "arbitrary")),
        name="modmm",
    )(x, shift[:, None, :], scale[:, None, :], w_bf16)


def _outproj_kernel(*refs, n_in):
    a_refs = refs[:n_in]
    w_refs = refs[n_in:2 * n_in]
    x_ref, gate_ref, g_ref, b_ref, o_ref = refs[2 * n_in:]
    bb, tt, d = x_ref.shape
    acc = _mm(a_refs[0][...], w_refs[0][...])
    for a_ref, w_ref in zip(a_refs[1:], w_refs[1:]):
        acc = acc + _mm(a_ref[...], w_ref[...])
    z = DEEPNORM_ALPHA * x_ref[...] + gate_ref[...] * acc.reshape(bb, tt, d)
    o_ref[...] = _layernorm(z, g_ref[...], b_ref[...])


def _outproj_ln(acts, ws, x, gate, ln_g, ln_b, max_rows):
    B, T, D = x.shape
    bb, tt = _row_tiling(B, T, max_rows)
    nt = T // tt
    n_in = len(acts)
    in_specs = []
    for a in acts:
        in_specs.append(pl.BlockSpec((bb * tt, a.shape[1]), lambda i, j: (i * nt + j, 0)))
    for w in ws:
        in_specs.append(pl.BlockSpec(w.shape, lambda i, j: (0, 0)))
    in_specs += [
        pl.BlockSpec((bb, tt, D), lambda i, j: (i, j, 0)),
        pl.BlockSpec((bb, 1, D), lambda i, j: (i, 0, 0)),
        pl.BlockSpec((1, 1, D), lambda i, j: (0, 0, 0)),
        pl.BlockSpec((1, 1, D), lambda i, j: (0, 0, 0)),
    ]
    return pl.pallas_call(
        functools.partial(_outproj_kernel, n_in=n_in),
        out_shape=jax.ShapeDtypeStruct((B, T, D), F32),
        grid=(B // bb, nt),
        in_specs=in_specs,
        out_specs=pl.BlockSpec((bb, tt, D), lambda i, j: (i, j, 0)),
        compiler_params=_cparams(("arbitrary", "arbitrary")),
        name="outproj_ln",
    )(*acts, *ws, x, gate[:, None, :], ln_g.reshape(1, 1, D), ln_b.reshape(1, 1, D))


def _ffn_kernel(x_ref, sh_ref, sc_ref, gate_ref, wa_ref, wu_ref, wo_ref, g_ref, b_ref, o_ref,
                xm_sc, acc_sc):
    j = pl.program_id(2)
    bb, tt, d = x_ref.shape

    @pl.when(j == 0)
    def _():
        h = x_ref[...] * (1.0 + sc_ref[...]) + sh_ref[...]
        xm_sc[...] = h.reshape(bb * tt, d).astype(BF16)
        acc_sc[...] = jnp.zeros_like(acc_sc)

    xm = xm_sc[...]
    a = jnp.dot(xm, wa_ref[...], preferred_element_type=F32)
    u = jnp.dot(xm, wu_ref[...], preferred_element_type=F32)
    acc_sc[...] += _mm(_silu(a) * u, wo_ref[...])

    @pl.when(j == pl.num_programs(2) - 1)
    def _():
        z = DEEPNORM_ALPHA * x_ref[...] + gate_ref[...] * acc_sc[...].reshape(bb, tt, d)
        o_ref[...] = _layernorm(z, g_ref[...], b_ref[...])


def _ffn_ln(x, shift, scale, gate, w_in_bf16, w_out_bf16, ln_g, ln_b, max_rows):
    B, T, D = x.shape
    bb, tt = _row_tiling(B, T, max_rows)
    nt = T // tt
    th = 256
    nh = FF_HIDDEN // th
    vec = lambda v: v[:, None, :]
    return pl.pallas_call(
        _ffn_kernel,
        out_shape=jax.ShapeDtypeStruct((B, T, D), F32),
        grid=(B // bb, nt, nh),
        in_specs=[
            pl.BlockSpec((bb, tt, D), lambda i, t, j: (i, t, 0)),
            pl.BlockSpec((bb, 1, D), lambda i, t, j: (i, 0, 0)),
            pl.BlockSpec((bb, 1, D), lambda i, t, j: (i, 0, 0)),
            pl.BlockSpec((bb, 1, D), lambda i, t, j: (i, 0, 0)),
            pl.BlockSpec((D, th), lambda i, t, j: (0, j)),
            pl.BlockSpec((D, th), lambda i, t, j: (0, nh + j)),
            pl.BlockSpec((th, D), lambda i, t, j: (j, 0)),
            pl.BlockSpec((1, 1, D), lambda i, t, j: (0, 0, 0)),
            pl.BlockSpec((1, 1, D), lambda i, t, j: (0, 0, 0)),
        ],
        out_specs=pl.BlockSpec((bb, tt, D), lambda i, t, j: (i, t, 0)),
        scratch_shapes=[pltpu.VMEM((bb * tt, D), BF16), pltpu.VMEM((bb * tt, D), F32)],
        compiler_params=_cparams(("arbitrary", "arbitrary", "arbitrary")),
        name="ffn_ln",
    )(x, vec(shift), vec(scale), vec(gate), w_in_bf16, w_in_bf16, w_out_bf16,
      ln_g.reshape(1, 1, D), ln_b.reshape(1, 1, D))


def _gla_kernel(q_ref, k_ref, v_ref, r_ref, sm_ref, wgk_ref, bgk_ref, gn_ref, s0_ref,
                o_ref, sfin_ref, S_sc, q_sc, k_sc, b_sc, v_sc, o_sc, *, C, NB):
    t = pl.program_id(1)
    KW, VW = GLA_KEY_W, GLA_VAL_W

    hk = lax.broadcasted_iota(jnp.int32, (KW, VW), 0) // GLA_DK
    hv = lax.broadcasted_iota(jnp.int32, (KW, VW), 1) // GLA_DV
    same_head = hk == hv

    @pl.when(t == 0)
    def _():
        for n in range(NB):
            rows = []
            for h in range(GLA_HEADS):
                pieces = [s0_ref[n, h] if h2 == h else jnp.zeros((GLA_DK, GLA_DV), F32)
                          for h2 in range(GLA_HEADS)]
                rows.append(jnp.concatenate(pieces, axis=1))
            S_sc[n] = jnp.concatenate(rows, axis=0)

    seg = same_head.astype(BF16)
    JG = min(C, 2 * SUBLANES)
    local = lax.broadcasted_iota(jnp.int32, (JG, KW), 0)
    S_new = []
    for n in range(NB):
        gk = sm_ref[n, :, 0:GLA_LOWRANK]
        pre = _mm(gk, wgk_ref[...]) + bgk_ref[...]
        log_a = (jnp.minimum(pre, 0.0) - jnp.log(1.0 + jnp.exp(-jnp.abs(pre)))) * (1.0 / GLA_GATE_NORM)
        b = _cumsum_rows(log_a)
        q = q_ref[n] * (GLA_DK ** -0.5)
        k = k_ref[n]
        v = v_ref[n]
        q_sc[n] = q
        k_sc[n] = k
        b_sc[n] = b
        v_sc[n] = v

        S = S_sc[n]
        o_sc[n] = _mm(q * jnp.exp(b), S)

        for r0 in range(0, C, JG):
            rows = C - r0
            qg = q_sc[n, r0:C, :]
            bg = b_sc[n, r0:C, :]
            ps = []
            for jj in range(JG):
                j = r0 + jj
                p = qg * k_sc[n, j:j + 1, :] * jnp.exp(jnp.minimum(bg - b_sc[n, j:j + 1, :], 0.0))
                head = jnp.where(local >= jj, p[0:JG], 0.0)
                p = head if rows == JG else jnp.concatenate([head, p[JG:]], axis=0)
                ps.append(p.astype(BF16))
            s = jnp.dot(jnp.concatenate(ps, axis=0), seg, preferred_element_type=F32)
            contrib = s[0:rows] * v_sc[n, r0:r0 + 1, :]
            for jj in range(1, JG):
                contrib = contrib + s[jj * rows:(jj + 1) * rows] * v_sc[n, r0 + jj:r0 + jj + 1, :]
            o_sc[n, r0:C, :] += contrib
        o = o_sc[n]

        b_last = b[C - 1:C, :]
        kd = k * jnp.exp(b_last - b)
        upd = _mm_tn(kd, v)
        tail = jnp.broadcast_to(b_last, (SUBLANES, KW))
        dcol = jnp.exp(tail.T[:, 0:1])
        S_new.append(S * dcol + jnp.where(same_head, upd, 0.0))
        S_sc[n] = S_new[n]

        outs = []
        for h in range(GLA_HEADS):
            oh = o[:, h * GLA_DV:(h + 1) * GLA_DV]
            ms = jnp.mean(oh * oh, axis=-1, keepdims=True)
            rh = r_ref[n, :, h * GLA_DV:(h + 1) * GLA_DV]
            outs.append(oh * lax.rsqrt(ms + RMS_EPS) * gn_ref[...] * _silu(rh))
        o_ref[n] = jnp.concatenate(outs, axis=1).astype(o_ref.dtype)

    @pl.when(t == pl.num_programs(1) - 1)
    def _():
        for n in range(NB):
            for h in range(GLA_HEADS):
                sfin_ref[n, h] = S_new[n][h * GLA_DK:(h + 1) * GLA_DK, h * GLA_DV:(h + 1) * GLA_DV]


def _gla(proj, B, T, w_gk2, b_gk, gla_norm, s0, C):
    nt = T // C
    NB = 2 if B % 2 == 0 else 1
    proj3 = proj.reshape(B, T, AB_COLS)
    o, s_fin = pl.pallas_call(
        functools.partial(_gla_kernel, C=C, NB=NB),
        out_shape=(jax.ShapeDtypeStruct((B, T, GLA_VAL_W), BF16),
                   jax.ShapeDtypeStruct((B, GLA_HEADS, GLA_DK, GLA_DV), F32)),
        grid=(B // NB, nt),
        in_specs=[
            pl.BlockSpec((NB, C, GLA_KEY_W), lambda b, t: (b, t, 0)),
            pl.BlockSpec((NB, C, GLA_KEY_W), lambda b, t: (b, t, 1)),
            pl.BlockSpec((NB, C, GLA_VAL_W), lambda b, t: (b, t, 1)),
            pl.BlockSpec((NB, C, GLA_VAL_W), lambda b, t: (b, t, 2)),
            pl.BlockSpec((NB, C, LANES), lambda b, t: (b, t, AB_SMALL_BLK)),
            pl.BlockSpec((GLA_LOWRANK, GLA_KEY_W), lambda b, t: (0, 0)),
            pl.BlockSpec((1, GLA_KEY_W), lambda b, t: (0, 0)),
            pl.BlockSpec((1, GLA_DV), lambda b, t: (0, 0)),
            pl.BlockSpec((NB, GLA_HEADS, GLA_DK, GLA_DV), lambda b, t: (b, 0, 0, 0)),
        ],
        out_specs=(pl.BlockSpec((NB, C, GLA_VAL_W), lambda b, t: (b, t, 0)),
                   pl.BlockSpec((NB, GLA_HEADS, GLA_DK, GLA_DV), lambda b, t: (b, 0, 0, 0))),
        scratch_shapes=[
            pltpu.VMEM((NB, GLA_KEY_W, GLA_VAL_W), F32),
            pltpu.VMEM((NB, C, GLA_KEY_W), F32),
            pltpu.VMEM((NB, C, GLA_KEY_W), F32),
            pltpu.VMEM((NB, C, GLA_KEY_W), F32),
            pltpu.VMEM((NB, C, GLA_VAL_W), F32),
            pltpu.VMEM((NB, C, GLA_VAL_W), F32),
        ],
        compiler_params=_cparams(("arbitrary", "arbitrary")),
        name="gla",
    )(proj3, proj3, proj3, proj3, proj3, w_gk2, b_gk.reshape(1, GLA_KEY_W), gla_norm.reshape(1, GLA_DV), s0)
    return o.reshape(B * T, GLA_VAL_W), s_fin


def _compress_pages(x_ref, n_pages, pe_ref, w1_ref, w2_ref):
    outs = []
    for half in range(PAGE_SIZE // NSA_BLOCK):
        pieces = [x_ref[pl.ds(half * NSA_BLOCK + tk, n_pages, stride=PAGE_SIZE), :] for tk in range(NSA_BLOCK)]
        flat = jnp.concatenate(pieces, axis=1) + pe_ref[...]
        acc = _mm(flat, w1_ref[...])
        outs.append(_mm(_silu(acc), w2_ref[...]))
    return jnp.concatenate(outs, axis=1)


def _compress_dense_kernel(xk_ref, xv_ref, pek_ref, pev_ref, w1k_ref, w1v_ref, w2k_ref, w2v_ref,
                           ok_ref, ov_ref, *, n_pages):
    ok_ref[0] = _compress_pages(xk_ref, n_pages, pek_ref, w1k_ref, w2k_ref)
    ov_ref[0] = _compress_pages(xv_ref, n_pages, pev_ref, w1v_ref, w2v_ref)


def _cmp_weights(cmp_pe, cmp_w1, cmp_w2):
    out = []
    for i in range(2):
        pe2 = jnp.concatenate([cmp_pe[i], cmp_pe[i]], axis=1).reshape(1, NSA_BLOCK * LANES)
        w1 = cmp_w1[i].reshape(NSA_BLOCK, NSA_HEAD_DIM, NSA_CMP_HIDDEN)
        z1 = jnp.zeros_like(w1)
        w1bd = jnp.concatenate([jnp.concatenate([w1, z1], axis=2),
                                jnp.concatenate([z1, w1], axis=2)], axis=1).astype(BF16)
        w1bd = w1bd.reshape(NSA_BLOCK * LANES, 2 * NSA_CMP_HIDDEN)
        w2 = cmp_w2[i]
        z2 = jnp.zeros_like(w2)
        w2bd = jnp.concatenate([jnp.concatenate([w2, z2], axis=1),
                                jnp.concatenate([z2, w2], axis=1)], axis=0).astype(BF16)
        out.append((pe2, w1bd, w2bd))
    return out


def _compress_dense(proj, B, T, cw):
    n_pages = T // PAGE_SIZE
    (pek, w1k, w2k), (pev, w1v, w2v) = cw
    full = lambda a: pl.BlockSpec(a.shape, lambda b: (0,) * a.ndim)
    ok, ov = pl.pallas_call(
        functools.partial(_compress_dense_kernel, n_pages=n_pages),
        out_shape=(jax.ShapeDtypeStruct((B, n_pages, 2 * LANES), F32),) * 2,
        grid=(B,),
        in_specs=[
            pl.BlockSpec((T, LANES), lambda b: (b, AB_KV_BLK)),
            pl.BlockSpec((T, LANES), lambda b: (b, AB_KV_BLK + 1)),
            full(pek), full(pev), full(w1k), full(w1v), full(w2k), full(w2v),
        ],
        out_specs=(pl.BlockSpec((1, n_pages, 2 * LANES), lambda b: (b, 0, 0)),) * 2,
        compiler_params=_cparams(("arbitrary",)),
        name="compress_dense",
    )(proj, proj, pek, pev, w1k, w1v, w2k, w2v)
    n_blk = T // NSA_BLOCK
    return ok.reshape(B, n_blk, LANES), ov.reshape(B, n_blk, LANES)


def _gather_pages(pt_ref, b, pool_ref, buf_ref, sem, n_pages, start):
    def body(p, carry):
        page = pt_ref[b, p]
        cp = pltpu.make_async_copy(pool_ref.at[pl.ds(page * PAGE_SIZE, PAGE_SIZE), :],
                                   buf_ref.at[pl.ds(p * PAGE_SIZE, PAGE_SIZE), :], sem)
        if start:
            cp.start()
        else:
            cp.wait()
        return carry
    lax.fori_loop(0, n_pages, body, 0)


def _gather_pages_dmajor(pt_ref, b, pool_ref, buf_ref, sem, n_pages, start):
    def body(p, carry):
        page = pt_ref[b, p]
        cp = pltpu.make_async_copy(pool_ref.at[pl.ds(page * PAGE_SIZE, PAGE_SIZE), :],
                                   buf_ref.at[:, p, :], sem)
        if start:
            cp.start()
        else:
            cp.wait()
        return carry
    lax.fori_loop(0, n_pages, body, 0)


def _compress_pages_t(x_ref, n_pages, pe_ref, w1_ref, w2_ref):
    per_g = []
    for g in range(NSA_KV_HEADS):
        pieces = [x_ref[g * NSA_HEAD_DIM + d] for d in range(NSA_HEAD_DIM)]
        flat = jnp.concatenate(pieces, axis=1) + pe_ref[...]
        acc = _mm(flat, w1_ref[...])
        per_g.append(_mm(_silu(acc), w2_ref[...]))
    hd = NSA_HEAD_DIM
    return jnp.concatenate([per_g[0][:, 0:hd], per_g[1][:, 0:hd], per_g[0][:, hd:2 * hd], per_g[1][:, hd:2 * hd]], axis=1)


def _cmp_weights_t(cmp_pe, cmp_w1, cmp_w2):
    out = []
    for i in range(2):
        pe_t = jnp.concatenate([cmp_pe[i].T, cmp_pe[i].T], axis=1).reshape(1, NSA_HEAD_DIM * PAGE_SIZE)
        w1 = jnp.transpose(cmp_w1[i].reshape(NSA_BLOCK, NSA_HEAD_DIM, NSA_CMP_HIDDEN), (1, 0, 2))
        z1 = jnp.zeros_like(w1)
        w1t = jnp.concatenate([jnp.concatenate([w1, z1], axis=2),
                               jnp.concatenate([z1, w1], axis=2)], axis=1).astype(BF16)
        w1t = w1t.reshape(NSA_HEAD_DIM * PAGE_SIZE, 2 * NSA_CMP_HIDDEN)
        w2 = cmp_w2[i]
        z2 = jnp.zeros_like(w2)
        w2bd = jnp.concatenate([jnp.concatenate([w2, z2], axis=1),
                                jnp.concatenate([z2, w2], axis=1)], axis=0).astype(BF16)
        out.append((pe_t, w1t, w2bd))
    return out


def _compress_paged_kernel(pt_ref, poolk_ref, poolv_ref, pek_ref, pev_ref, w1k_ref, w1v_ref,
                           w2k_ref, w2v_ref, ok_ref, ov_ref, bufk, bufv, sems, *, n_pages):
    b = pl.program_id(0)
    slot = b % 2
    @pl.when(b == 0)
    def _():
        _gather_pages_dmajor(pt_ref, b, poolk_ref, bufk.at[0], sems.at[0], n_pages, True)

    _gather_pages_dmajor(pt_ref, b, poolv_ref, bufv, sems.at[2], n_pages, True)

    @pl.when(b + 1 < pl.num_programs(0))
    def _():
        _gather_pages_dmajor(pt_ref, b + 1, poolk_ref, bufk.at[1 - slot], sems.at[1 - slot], n_pages, True)

    _gather_pages_dmajor(pt_ref, b, poolk_ref, bufk.at[slot], sems.at[slot], n_pages, False)
    ok_ref[0] = _compress_pages_t(bufk.at[slot], n_pages, pek_ref, w1k_ref, w2k_ref)
    _gather_pages_dmajor(pt_ref, b, poolv_ref, bufv, sems.at[2], n_pages, False)
    ov_ref[0] = _compress_pages_t(bufv, n_pages, pev_ref, w1v_ref, w2v_ref)


def _compress_paged(page_table, pool_k, pool_v, cw):
    B, n_pages = page_table.shape
    (pek, w1k, w2k), (pev, w1v, w2v) = cw
    full = lambda a: pl.BlockSpec(a.shape, lambda b, pt: (0,) * a.ndim)
    ok, ov = pl.pallas_call(
        functools.partial(_compress_paged_kernel, n_pages=n_pages),
        out_shape=(jax.ShapeDtypeStruct((B, n_pages, 2 * LANES), F32),) * 2,
        grid_spec=pltpu.PrefetchScalarGridSpec(
            num_scalar_prefetch=1,
            grid=(B,),
            in_specs=[
                pl.BlockSpec(memory_space=pl.ANY),
                pl.BlockSpec(memory_space=pl.ANY),
                full(pek), full(pev), full(w1k), full(w1v), full(w2k), full(w2v),
            ],
            out_specs=(pl.BlockSpec((1, n_pages, 2 * LANES), lambda b, pt: (b, 0, 0)),) * 2,
            scratch_shapes=[
                pltpu.VMEM((2, PAGE_SIZE, n_pages, LANES), F32),
                pltpu.VMEM((PAGE_SIZE, n_pages, LANES), F32),
                pltpu.SemaphoreType.DMA((3,)),
            ],
        ),
        compiler_params=_cparams(("arbitrary",)),
        name="compress_paged",
    )(page_table, pool_k, pool_v, pek, pev, w1k, w1v, w2k, w2v)
    n_blk = n_pages * (PAGE_SIZE // NSA_BLOCK)
    return ok.reshape(B, n_blk, LANES), ov.reshape(B, n_blk, LANES)


def _stack_queries(q, g, tq):
    rows = []
    for hl in range(NSA_HPG):
        qh = q[:, hl * NSA_HEAD_DIM:(hl + 1) * NSA_HEAD_DIM]
        rows.append(jnp.concatenate([qh, qh], axis=1))
    qs = jnp.concatenate(rows, axis=0) * (NSA_HEAD_DIM ** -0.5)
    half = lax.broadcasted_iota(jnp.int32, qs.shape, 1) // NSA_HEAD_DIM
    return jnp.where(half == g, qs, 0.0).astype(BF16)


def _row_slopes(g, tq):
    hl = lax.broadcasted_iota(jnp.int32, (NSA_HPG * tq, 1), 0) // tq
    s = jnp.where(hl == 0, 0.5, jnp.where(hl == 1, 0.25, jnp.where(hl == 2, 0.125, 0.0625)))
    return s * jnp.where(g == 0, 1.0, 0.0625)


def _gate_columns(sm, g, tq):
    sig = _sigmoid(sm)
    lane = lax.broadcasted_iota(jnp.int32, sm.shape, 1)
    cols = []
    for br in range(3):
        per_head = []
        for hl in range(NSA_HPG):
            target = GATE_LANE0 + 3 * (NSA_HPG * g + hl) + br
            per_head.append(jnp.sum(jnp.where(lane == target, sig, 0.0), axis=-1, keepdims=True))
        cols.append(jnp.concatenate(per_head, axis=0))
    return cols


def _topk_select(score, k_sel, n):
    idx = lax.broadcasted_iota(jnp.int32, score.shape, 1)
    rank = jnp.zeros(score.shape, F32)
    for j in range(n):
        col = score[:, j:j + 1]
        beats = (col > score) | ((col >= score) & (idx > j))
        rank = rank + jnp.where(beats, 1.0, 0.0)
    return rank < k_sel


def _topk_select_t(score, k_sel, n):
    st = score.T[0:n]
    idx = lax.broadcasted_iota(jnp.int32, st.shape, 0)
    rank = jnp.zeros(st.shape, F32)
    for j in range(n):
        row = st[j:j + 1, :]
        beats = (row > st) | ((row >= st) & (idx > j))
        rank = rank + jnp.where(beats, 1.0, 0.0)
    sel_t = jnp.where(rank < k_sel, 1.0, 0.0)
    sel_t = jnp.concatenate([sel_t, jnp.zeros((score.shape[1] - n, st.shape[1]), F32)], axis=0)
    return sel_t.T > 0.5


def _unstack_heads(o, g, tq):
    og = jnp.where(g == 0, o[:, 0:NSA_HEAD_DIM], o[:, NSA_HEAD_DIM:2 * NSA_HEAD_DIM])
    return jnp.concatenate([og[hl * tq:(hl + 1) * tq] for hl in range(NSA_HPG)], axis=1)


def _key_features(T, onehot):
    j = lax.broadcasted_iota(jnp.int32, (T, LANES), 0)
    lane = lax.broadcasted_iota(jnp.int32, (T, LANES), 1)
    blk = j // NSA_BLOCK
    f = jnp.where(lane == FEAT_BLK, blk.astype(F32),
                  jnp.where(lane == FEAT_OFF, (j % NSA_BLOCK).astype(F32), 0.0))
    if onehot:
        f = jnp.where(lane == blk, 1.0, f)
    return f.astype(BF16)


def _ones_column(rows):
    lane = lax.broadcasted_iota(jnp.int32, (rows, LANES), 1)
    return jnp.where(lane == 0, 1.0, 0.0).astype(BF16)


def _nsa_prompt_kernel(q_ref, sm_ref, kc_ref, vc_ref, ks_ref, vs_ref, kw_ref, vw_ref, o_ref,
                       ksb, vsb, kwb, vwb, s_sc, m_sc, acc_sc, *, T, TQ, WIN, CH, NB):
    qt = pl.program_id(1)
    R = NSA_HPG * TQ
    R2 = NSA_KV_HEADS * R
    n_blk = T // NSA_BLOCK
    G = range(NSA_KV_HEADS)
    N = range(NB)
    PAD = NSA_WINDOW

    @pl.when(qt == 0)
    def _():
        lane = lax.broadcasted_iota(jnp.int32, (PAD, LANES), 1)
        for n in N:
            ksb[n, :, 0:LANES] = ks_ref[n].astype(BF16)
            ksb[n, :, LANES:2 * LANES] = _key_features(T, True)
            vsb[n, :, 0:LANES] = vs_ref[n].astype(BF16)
            vsb[n, :, LANES:2 * LANES] = _ones_column(T)
            kwb[n, 0:PAD, 0:LANES] = jnp.zeros((PAD, LANES), BF16)
            kwb[n, 0:PAD, LANES:2 * LANES] = jnp.where(lane == FEAT_PAD, 1.0, 0.0).astype(BF16)
            kwb[n, PAD:PAD + T, 0:LANES] = kw_ref[n].astype(BF16)
            kwb[n, PAD:PAD + T, LANES:2 * LANES] = _key_features(T, False)
            vwb[n, 0:PAD, :] = jnp.zeros((PAD, 2 * LANES), BF16)
            vwb[n, PAD:PAD + T, 0:LANES] = vw_ref[n].astype(BF16)
            vwb[n, PAD:PAD + T, LANES:2 * LANES] = _ones_column(T)

    slope = jnp.concatenate([_row_slopes(g, TQ) for g in G], axis=0)
    off_q = lax.broadcasted_iota(jnp.int32, (R2, 1), 0) % TQ
    tq_i = qt * TQ + off_q
    tq_f = tq_i.astype(F32)
    lane = lax.broadcasted_iota(jnp.int32, (R2, LANES), 1)
    feat = jnp.where(lane == FEAT_BLK, slope * NSA_BLOCK,
                     jnp.where(lane == FEAT_OFF, slope, jnp.where(lane == FEAT_PAD, -SEL_BIG, 0.0)))
    feat_b = feat.astype(BF16)
    zpad = jnp.zeros((LANES - n_blk, LANES), F32)
    n_i = lax.broadcasted_iota(jnp.int32, (1, LANES), 1)
    center = (n_i * NSA_BLOCK).astype(F32) + 0.5 * (NSA_BLOCK - 1)
    cur0 = pl.multiple_of(qt * TQ, TQ)
    nt_dims = (((1,), (1,)), ((), ()))
    c_first = lax.broadcasted_iota(jnp.int32, (1, LANES), 1)
    c_tail = lax.broadcasted_iota(jnp.int32, (1, WIN - PAD), 1)
    tq1 = tq_i[0:NSA_KV_HEADS * TQ]
    cur = tq1 // NSA_BLOCK
    forced = (n_i == 0) | (n_i == cur) | (n_i == cur - 1)
    visible = n_i * NSA_BLOCK <= tq1
    off_k = lax.broadcasted_iota(jnp.int32, (1, TQ), 1)

    qs = [jnp.concatenate([_stack_queries(q_ref[n, :, g * 256:(g + 1) * 256], g, TQ) for g in G], axis=0) for n in N]
    q_plain = [jnp.concatenate([qs[n], feat_b], axis=1) for n in N]
    gates = [[_gate_columns(sm_ref[n], g, TQ) for g in G] for n in N]
    gcol = [[jnp.concatenate([gates[n][g][br] for g in G], axis=0) for br in range(3)] for n in N]

    s_c = [_mm_nt(qs[n], jnp.concatenate([kc_ref[n], zpad], axis=0)) - slope * (tq_f - center) for n in N]
    p_c = [_masked_softmax(s_c[n], (n_i * NSA_BLOCK + NSA_BLOCK - 1) <= tq_i) for n in N]
    o_c = [_mm(p_c[n], jnp.concatenate([vc_ref[n], zpad], axis=0)) for n in N]

    o_w = []
    for n in N:
        s_w = lax.dot_general(q_plain[n], kwb[n, pl.ds(cur0, WIN), :], nt_dims, preferred_element_type=F32)
        s_w = jnp.concatenate([jnp.where(c_first >= off_q, s_w[:, 0:LANES], NEG),
                               s_w[:, LANES:PAD],
                               jnp.where(c_tail <= off_q, s_w[:, PAD:WIN], NEG)], axis=1)
        e_w = jnp.exp(s_w - jnp.max(s_w, axis=-1, keepdims=True))
        acc_w = jnp.dot(e_w.astype(BF16), vwb[n, pl.ds(cur0, WIN), :], preferred_element_type=F32)
        o_w.append(acc_w[:, 0:LANES] * (1.0 / acc_w[:, LANES:LANES + 1]))

    q_sel, s_cur = [], []
    for n in N:
        scores = []
        for g in G:
            sc = p_c[n][g * R:g * R + TQ]
            for hl in range(1, NSA_HPG):
                sc = sc + p_c[n][g * R + hl * TQ:g * R + (hl + 1) * TQ]
            scores.append(sc)
        score = jnp.concatenate(scores, axis=0)
        score = jnp.where(visible, jnp.where(forced, FORCED_SCORE, score), -1.0)
        sel = _topk_select_t(score, min(NSA_TOP_K, n_blk), n_blk)
        sel_bias = jnp.where(sel & (n_i < cur), 0.0, -SEL_BIG)
        sel_bias = jnp.concatenate([sel_bias[g * TQ:(g + 1) * TQ] for g in G for _ in range(NSA_HPG)], axis=0)
        q_sel.append(jnp.concatenate([qs[n], jnp.where(lane >= FEAT_BLK, feat, sel_bias).astype(BF16)], axis=1))
        sc_own = lax.dot_general(q_plain[n], ksb[n, pl.ds(cur0, TQ), :], nt_dims, preferred_element_type=F32)
        s_cur.append(jnp.where(off_k <= off_q, sc_own, NEG))

    n_ch = T // CH
    m_sc[...] = jnp.full((NB, R2, LANES), NEG, F32)
    for c in range(n_ch):
        @pl.when(c * CH < cur0)
        def _():
            for n in N:
                s = lax.dot_general(q_sel[n], ksb[n, c * CH:(c + 1) * CH, :], nt_dims, preferred_element_type=F32)
                s_sc[n, :, c * CH:(c + 1) * CH] = s
                mm = m_sc[n]
                for i in range(CH // LANES):
                    mm = jnp.maximum(mm, s[:, i * LANES:(i + 1) * LANES])
                m_sc[n] = mm
    m = [jnp.maximum(jnp.max(m_sc[n], axis=-1, keepdims=True), jnp.max(s_cur[n], axis=-1, keepdims=True)) for n in N]
    for n in N:
        e_cur = jnp.exp(s_cur[n] - m[n])
        acc_sc[n] = jnp.dot(e_cur.astype(BF16), vsb[n, pl.ds(cur0, TQ), :], preferred_element_type=F32)
    for c in range(n_ch):
        @pl.when(c * CH < cur0)
        def _():
            for n in N:
                e = jnp.exp(s_sc[n, :, c * CH:(c + 1) * CH] - m[n])
                acc_sc[n] += jnp.dot(e.astype(BF16), vsb[n, c * CH:(c + 1) * CH, :], preferred_element_type=F32)
    for n in N:
        acc = acc_sc[n]
        o_s = acc[:, 0:LANES] * (1.0 / acc[:, LANES:LANES + 1])
        gc, gs, gw = gcol[n]
        o = gc * o_c[n] + gs * o_s + gw * o_w[n]
        o_ref[n] = jnp.concatenate([_unstack_heads(o[g * R:(g + 1) * R], g, TQ) for g in G], axis=1).astype(o_ref.dtype)


def _nsa_prompt(proj, B, T, kc, vc):
    TQ = NSA_BLOCK
    nq = T // TQ
    WIN = NSA_WINDOW + TQ
    CH = min(512, T)
    NB = 2 if B % 2 == 0 else 1
    R2 = NSA_KV_HEADS * NSA_HPG * TQ
    proj3 = proj.reshape(B, T, AB_COLS)
    kv = lambda j: pl.BlockSpec((NB, T, LANES), lambda b, t: (b, 0, AB_KV_BLK + j))
    n_blk = T // NSA_BLOCK
    o = pl.pallas_call(
        functools.partial(_nsa_prompt_kernel, T=T, TQ=TQ, WIN=WIN, CH=CH, NB=NB),
        out_shape=jax.ShapeDtypeStruct((B, T, NSA_Q_W), BF16),
        grid=(B // NB, nq),
        in_specs=[
            pl.BlockSpec((NB, TQ, NSA_Q_W), lambda b, t: (b, t, 3)),
            pl.BlockSpec((NB, TQ, LANES), lambda b, t: (b, t, AB_SMALL_BLK)),
            pl.BlockSpec((NB, n_blk, LANES), lambda b, t: (b, 0, 0)),
            pl.BlockSpec((NB, n_blk, LANES), lambda b, t: (b, 0, 0)),
            kv(2), kv(3), kv(4), kv(5),
        ],
        out_specs=pl.BlockSpec((NB, TQ, NSA_Q_W), lambda b, t: (b, t, 0)),
        scratch_shapes=[pltpu.VMEM((NB, T, 2 * LANES), BF16), pltpu.VMEM((NB, T, 2 * LANES), BF16),
                        pltpu.VMEM((NB, T + NSA_WINDOW, 2 * LANES), BF16),
                        pltpu.VMEM((NB, T + NSA_WINDOW, 2 * LANES), BF16),
                        pltpu.VMEM((NB, R2, T), F32), pltpu.VMEM((NB, R2, LANES), F32),
                        pltpu.VMEM((NB, R2, 2 * LANES), F32)],
        compiler_params=_cparams(("arbitrary", "arbitrary")),
        name="nsa_prompt",
    )(proj3, proj3, kc, vc, proj3, proj3, proj3, proj3)
    return o.reshape(B * T, NSA_Q_W)


def _nsa_sample_kernel(pt_ref, q_ref, sm_ref, kc_ref, vc_ref, poolk_ref, poolv_ref,
                       kn_ref, vn_ref, wk_ref, wv_ref, kwn_ref, vwn_ref,
                       o_ref, wko_ref, wvo_ref, bufk, bufv, sc_sc, sems, *, n_pages, TQ):
    b = pl.program_id(0)
    past = n_pages * PAGE_SIZE
    n_cmp = past // NSA_BLOCK
    n_sel = n_cmp + 1
    R = NSA_HPG * TQ
    R2 = NSA_KV_HEADS * R
    n_buf = wk_ref.shape[2]
    G = range(NSA_KV_HEADS)

    slot = b % 2
    @pl.when(b == 0)
    def _():
        _gather_pages(pt_ref, b, poolk_ref, bufk.at[0], sems.at[0], n_pages, True)

    _gather_pages(pt_ref, b, poolv_ref, bufv, sems.at[2], n_pages, True)

    @pl.when(b + 1 < pl.num_programs(0))
    def _():
        _gather_pages(pt_ref, b + 1, poolk_ref, bufk.at[1 - slot], sems.at[1 - slot], n_pages, True)

    qs = jnp.concatenate([_stack_queries(q_ref[:, g * 256:(g + 1) * 256], g, TQ) for g in G], axis=0)
    slope = jnp.concatenate([_row_slopes(g, TQ) for g in G], axis=0)
    gates = [_gate_columns(sm_ref[...], g, TQ) for g in G]
    gc, gs, gw = [jnp.concatenate([gates[g][br] for g in G], axis=0) for br in range(3)]
    tq_i = past + lax.broadcasted_iota(jnp.int32, (R2, 1), 0) % TQ
    tq_f = tq_i.astype(F32)
    new_i = past + lax.broadcasted_iota(jnp.int32, (1, TQ), 1)

    n_i = lax.broadcasted_iota(jnp.int32, (1, n_cmp), 1)
    center = (n_i * NSA_BLOCK).astype(F32) + 0.5 * (NSA_BLOCK - 1)
    s_c = _mm_nt(qs, kc_ref[0]) - slope * (tq_f - center)
    p_c = _masked_softmax(s_c, (n_i * NSA_BLOCK + NSA_BLOCK - 1) <= tq_i)
    o_c = _mm(p_c, vc_ref[0])

    bias_rows = []
    for g in G:
        score = p_c[g * R:g * R + TQ]
        for hl in range(1, NSA_HPG):
            score = score + p_c[g * R + hl * TQ:g * R + (hl + 1) * TQ]
        forced = (n_i == 0) | (n_i == n_cmp - 1)
        score = jnp.where(forced, FORCED_SCORE, score)
        sel = _topk_select(score, min(NSA_TOP_K, n_sel) - 1, n_cmp)
        bias_rows += [jnp.where(sel, 0.0, NEG)] * NSA_HPG
    sel_bias = jnp.concatenate(bias_rows, axis=0)

    wk_t = wk_ref[0]
    wv_t = wv_ref[0]
    wpos = past - n_buf + lax.broadcasted_iota(jnp.int32, (1, n_buf), 1)
    d_o = tq_i - wpos
    d_n = tq_i - new_i
    m_o = (wpos >= 0) & (d_o >= 0) & (d_o <= NSA_WINDOW)
    m_n = (d_n >= 0) & (d_n <= NSA_WINDOW)
    s_wo = jnp.where(m_o, _mm(qs, wk_t) - slope * d_o.astype(F32), NEG)
    s_wn = jnp.where(m_n, _mm_nt(qs, kwn_ref[...]) - slope * d_n.astype(F32), NEG)
    mw = jnp.maximum(jnp.max(s_wo, axis=-1, keepdims=True), jnp.max(s_wn, axis=-1, keepdims=True))
    e_o = jnp.where(m_o, jnp.exp(s_wo - mw), 0.0)
    e_w = jnp.where(m_n, jnp.exp(s_wn - mw), 0.0)
    den_w = jnp.maximum(jnp.sum(e_o, axis=-1, keepdims=True) + jnp.sum(e_w, axis=-1, keepdims=True), 1e-30)
    o_w = (_mm_nt(e_o, wv_t) + _mm(e_w, vwn_ref[...])) * (1.0 / den_w)

    lane_w = lax.broadcasted_iota(jnp.int32, (LANES, n_buf), 1)

    def shifted(old_t, new):
        slots = jnp.concatenate([jnp.zeros((LANES - TQ, LANES), F32), new], axis=0)
        tail = jnp.concatenate([jnp.zeros((LANES, n_buf - LANES), F32), slots.T], axis=1)
        return jnp.where(lane_w >= n_buf - TQ, tail, pltpu.roll(old_t, n_buf - TQ, axis=1))

    wko_ref[0] = shifted(wk_t, kwn_ref[...])
    wvo_ref[0] = shifted(wv_t, vwn_ref[...])

    _gather_pages(pt_ref, b, poolk_ref, bufk.at[slot], sems.at[slot], n_pages, False)
    tok = lax.broadcasted_iota(jnp.int32, (1, PAGE_SIZE), 1)
    second = tok >= NSA_BLOCK
    m_run = jnp.full((R2, PAGE_SIZE), NEG, F32)
    for p in range(n_pages):
        s = jnp.dot(qs, bufk[slot, p * PAGE_SIZE:(p + 1) * PAGE_SIZE, :].astype(BF16), preferred_element_type=F32)
        bias = jnp.where(second, sel_bias[:, 2 * p + 1:2 * p + 2], sel_bias[:, 2 * p:2 * p + 1])
        s = s + bias - slope * (tq_f - (p * PAGE_SIZE + tok).astype(F32))
        sc_sc[p] = s
        m_run = jnp.maximum(m_run, s)
    s_n = jnp.where(new_i <= tq_i, _mm_nt(qs, kn_ref[...]) - slope * (tq_i - new_i).astype(F32), NEG)
    m = jnp.maximum(jnp.max(m_run, axis=-1, keepdims=True), jnp.max(s_n, axis=-1, keepdims=True))

    _gather_pages(pt_ref, b, poolv_ref, bufv, sems.at[2], n_pages, False)
    e_n = jnp.exp(s_n - m)
    acc = _mm(e_n, vn_ref[...])
    den_run = jnp.zeros((R2, PAGE_SIZE), F32)
    for p in range(n_pages):
        e = jnp.exp(sc_sc[p] - m)
        den_run = den_run + e
        acc = acc + _mm_nt(e, bufv[p * PAGE_SIZE:(p + 1) * PAGE_SIZE, :])
    den = jnp.sum(den_run, axis=-1, keepdims=True) + jnp.sum(e_n, axis=-1, keepdims=True)
    o_s = acc * (1.0 / den)

    o = gc * o_c + gs * o_s + gw * o_w
    o_ref[...] = jnp.concatenate([_unstack_heads(o[g * R:(g + 1) * R], g, TQ) for g in G], axis=1).astype(o_ref.dtype)


def _nsa_sample(proj, page_table, kc, vc, pool_k, pool_v, win_k, win_v):
    B, n_pages = page_table.shape
    TQ = proj.shape[0] // B
    past = n_pages * PAGE_SIZE
    n_cmp = past // NSA_BLOCK
    n_buf = win_k.shape[2]
    R2 = NSA_KV_HEADS * NSA_HPG * TQ
    kvn = lambda j: pl.BlockSpec((TQ, LANES), lambda b, pt: (b, AB_KV_BLK + j))
    win = pl.BlockSpec((1, LANES, n_buf), lambda b, pt: (b, 0, 0))
    return pl.pallas_call(
        functools.partial(_nsa_sample_kernel, n_pages=n_pages, TQ=TQ),
        out_shape=(jax.ShapeDtypeStruct((B * TQ, NSA_Q_W), BF16),
                   jax.ShapeDtypeStruct((B, LANES, n_buf), F32),
                   jax.ShapeDtypeStruct((B, LANES, n_buf), F32)),
        grid_spec=pltpu.PrefetchScalarGridSpec(
            num_scalar_prefetch=1,
            grid=(B,),
            in_specs=[
                pl.BlockSpec((TQ, NSA_Q_W), lambda b, pt: (b, 3)),
                pl.BlockSpec((TQ, LANES), lambda b, pt: (b, AB_SMALL_BLK)),
                pl.BlockSpec((1, n_cmp, LANES), lambda b, pt: (b, 0, 0)),
                pl.BlockSpec((1, n_cmp, LANES), lambda b, pt: (b, 0, 0)),
                pl.BlockSpec(memory_space=pl.ANY),
                pl.BlockSpec(memory_space=pl.ANY),
                kvn(2), kvn(3), win, win, kvn(4), kvn(5),
            ],
            out_specs=(pl.BlockSpec((TQ, NSA_Q_W), lambda b, pt: (b, 0)), win, win),
            scratch_shapes=[
                pltpu.VMEM((2, past, LANES), F32),
                pltpu.VMEM((past, LANES), F32),
                pltpu.VMEM((n_pages, R2, PAGE_SIZE), F32),
                pltpu.SemaphoreType.DMA((3,)),
            ],
        ),
        compiler_params=_cparams(("arbitrary",)),
        name="nsa_sample",
    )(page_table, proj, proj, kc, vc, pool_k, pool_v, proj, proj, win_k, win_v, proj, proj)


def _gdn_kernel(qkv_ref, z_ref, sm_ref, cw_ref, alog_ref, dtb_ref, ng_ref, s0_ref, cb_ref,
                o_ref, sfin_ref, S_sc, prev_sc, *, C, NB):
    t = pl.program_id(1)

    @pl.when(t == 0)
    def _():
        S_sc[...] = s0_ref[...]
        prev_sc[...] = cb_ref[...]

    def conv(n, c0):
        xe = jnp.concatenate([prev_sc[n, :, c0:c0 + LANES], qkv_ref[n, :, c0:c0 + LANES]], axis=0)
        y = xe[SUBLANES:] * cw_ref[GDN_CONV - 1:GDN_CONV, c0:c0 + LANES]
        for s in range(1, GDN_CONV):
            y = y + pltpu.roll(xe, s, axis=0)[SUBLANES:] * cw_ref[GDN_CONV - 1 - s:GDN_CONV - s, c0:c0 + LANES]
        return _silu(y)

    ii = lax.broadcasted_iota(jnp.int32, (C, C), 0)
    jj = lax.broadcasted_iota(jnp.int32, (C, C), 1)
    eye = (ii == jj).astype(F32)

    CH = [(n, h) for n in range(NB) for h in range(GDN_HEADS)]
    X = range(len(CH))
    beta_c, d_c, d_r = [], [], []
    for n in range(NB):
        sm = sm_ref[n]
        beta = _sigmoid(sm)
        gt = -jnp.exp(alog_ref[...]) * _softplus(sm + dtb_ref[...])
        d = _cumsum_rows(gt)
        dT = d.T
        for h in range(GDN_HEADS):
            beta_c.append(beta[:, h:h + 1])
            d_c.append(d[:, GDN_A_LANE0 + h:GDN_A_LANE0 + h + 1])
            d_r.append(dT[GDN_A_LANE0 + h:GDN_A_LANE0 + h + 1, :])
    q, k, v = [], [], []
    for n, h in CH:
        qh = conv(n, h * GDN_DK)
        kh = conv(n, GDN_W + h * GDN_DK)
        q.append(qh * lax.rsqrt(jnp.sum(qh * qh, axis=-1, keepdims=True) + L2_EPS) * (GDN_DK ** -0.5))
        k.append(kh * lax.rsqrt(jnp.sum(kh * kh, axis=-1, keepdims=True) + L2_EPS))
        v.append(conv(n, 2 * GDN_W + h * GDN_DV))
    decay = [jnp.exp(jnp.minimum(d_c[x] - d_r[x], 0.0)) for x in X]
    kb = [k[x] * beta_c[x] for x in X]
    g_kk = [_mm_nt(kb[x], k[x]) for x in X]
    g_qk = [_mm_nt(q[x], k[x]) for x in X]
    a = [jnp.where(ii > jj, g_kk[x] * decay[x], 0.0) for x in X]
    qk = [jnp.where(ii >= jj, g_qk[x] * decay[x], 0.0) for x in X]
    tinv = [eye - a[x] for x in X]
    p_split = [_split_bf16(a[x]) for x in X]
    n2 = 2
    while n2 < C:
        p = [_mm3(p_split[x], p_split[x]) for x in X]
        p_split = [_split_bf16(p[x]) for x in X]
        tinv = [tinv[x] + _mm3(_split_bf16(tinv[x]), p_split[x]) for x in X]
        n2 *= 2
    u = [_mm(tinv[x], v[x] * beta_c[x]) for x in X]
    w = [_mm(tinv[x], kb[x] * jnp.exp(d_c[x])) for x in X]
    S = [S_sc[n, h] for n, h in CH]
    w_s = [_mm(w[x], S[x]) for x in X]
    q_s = [_mm(q[x] * jnp.exp(d_c[x]), S[x]) for x in X]
    v_new = [u[x] - w_s[x] for x in X]
    o = [q_s[x] + _mm(qk[x], v_new[x]) for x in X]
    d_last = [d_c[x][C - 1:C, :] for x in X]
    upd = [_mm_tn(k[x] * jnp.exp(d_last[x] - d_c[x]), v_new[x]) for x in X]
    for x, (n, h) in enumerate(CH):
        S_sc[n, h] = S[x] * jnp.exp(d_last[x]) + upd[x]
        ms = jnp.mean(o[x] * o[x], axis=-1, keepdims=True)
        zh = z_ref[n, :, h * GDN_DV:(h + 1) * GDN_DV]
        o_ref[n, :, h * GDN_DV:(h + 1) * GDN_DV] = (
            o[x] * lax.rsqrt(ms + RMS_EPS) * ng_ref[...] * _silu(zh)).astype(o_ref.dtype)

    for n in range(NB):
        prev_sc[n] = qkv_ref[n, C - SUBLANES:C, :]

    @pl.when(t == pl.num_programs(1) - 1)
    def _():
        sfin_ref[...] = S_sc[...]


def _gdn(proj, B, T, conv_w, a_log, dt_bias, norm_g, s0, conv_buf8, C):
    nt = T // C
    NB = 4 if B % 4 == 0 else (2 if B % 2 == 0 else 1)
    pad = lambda v: jnp.zeros((1, LANES), F32).at[0, GDN_A_LANE0:GDN_A_LANE0 + GDN_HEADS].set(v)
    proj3 = proj.reshape(B, T, C_COLS)
    o, s_fin = pl.pallas_call(
        functools.partial(_gdn_kernel, C=C, NB=NB),
        out_shape=(jax.ShapeDtypeStruct((B, T, GDN_W), BF16),
                   jax.ShapeDtypeStruct((B, GDN_HEADS, GDN_DK, GDN_DV), F32)),
        grid=(B // NB, nt),
        in_specs=[
            pl.BlockSpec((NB, C, GDN_CONV_CH), lambda b, t: (b, t, 0)),
            pl.BlockSpec((NB, C, GDN_W), lambda b, t: (b, t, 3)),
            pl.BlockSpec((NB, C, LANES), lambda b, t: (b, t, C_SMALL_BLK)),
            pl.BlockSpec((GDN_CONV, GDN_CONV_CH), lambda b, t: (0, 0)),
            pl.BlockSpec((1, LANES), lambda b, t: (0, 0)),
            pl.BlockSpec((1, LANES), lambda b, t: (0, 0)),
            pl.BlockSpec((1, GDN_DV), lambda b, t: (0, 0)),
            pl.BlockSpec((NB, GDN_HEADS, GDN_DK, GDN_DV), lambda b, t: (b, 0, 0, 0)),
            pl.BlockSpec((NB, SUBLANES, GDN_CONV_CH), lambda b, t: (b, 0, 0)),
        ],
        out_specs=(pl.BlockSpec((NB, C, GDN_W), lambda b, t: (b, t, 0)),
                   pl.BlockSpec((NB, GDN_HEADS, GDN_DK, GDN_DV), lambda b, t: (b, 0, 0, 0))),
        scratch_shapes=[
            pltpu.VMEM((NB, GDN_HEADS, GDN_DK, GDN_DV), F32),
            pltpu.VMEM((NB, SUBLANES, GDN_CONV_CH), F32),
        ],
        compiler_params=_cparams(("arbitrary", "arbitrary")),
        name="gdn",
    )(proj3, proj3, proj3, conv_w, pad(a_log), pad(dt_bias), norm_g.reshape(1, GDN_DV), s0, conv_buf8)
    return o.reshape(B * T, GDN_W), s_fin


def _ab_in_weight(w):
    big = w[:, :GLA_KEY_W * 2 + GLA_VAL_W * 2]
    gk = w[:, 1536:1536 + GLA_LOWRANK]
    rest = w[:, 1536 + GLA_LOWRANK:]
    q_b = rest[:, :NSA_Q_W]
    kv = rest[:, NSA_Q_W:NSA_Q_W + 6 * NSA_KV_W]
    gate = rest[:, NSA_Q_W + 6 * NSA_KV_W:]
    small = jnp.concatenate([gk, gate, jnp.zeros((D_MODEL, LANES - GLA_LOWRANK - 3 * NSA_HEADS), w.dtype)], axis=1)
    return jnp.concatenate([big, q_b, kv, small], axis=1).astype(BF16)


def _c_in_weight(w):
    qkv = w[:, :GDN_CONV_CH]
    ba = w[:, GDN_CONV_CH:GDN_CONV_CH + 2 * GDN_HEADS]
    z = w[:, GDN_CONV_CH + 2 * GDN_HEADS:]
    small = jnp.concatenate([ba, jnp.zeros((D_MODEL, LANES - 2 * GDN_HEADS), w.dtype)], axis=1)
    return jnp.concatenate([qkv, z, small], axis=1).astype(BF16)


def _kv_out(proj, B, T, j):
    return proj[:, (AB_KV_BLK + j) * LANES:(AB_KV_BLK + j + 1) * LANES].reshape(B, T, NSA_KV_HEADS, NSA_HEAD_DIM)


PROMPT_ROWS = 512
FFN_ROWS = 1024


def kernel(x_prompt, x_sample, c_prompt, c_sample, page_table, cache_cmp_k, cache_cmp_v, cache_sel_k, cache_sel_v, state_win_k, state_win_v, state_gla, state_gdn, state_gdn_conv, w_ada, b_ada, ln_g, ln_b, w_ffn_in, w_ffn_out, ab_w_in, ab_w_gk2, ab_b_gk, ab_gla_norm, ab_cmp_pe, ab_cmp_w1, ab_cmp_w2, ab_w_out, c_w_in, c_conv_w, c_a_log, c_dt_bias, c_norm, c_w_out):
    Bp, Tp, _ = x_prompt.shape
    Bs, Ts, _ = x_sample.shape
    n_pool = cache_cmp_k.shape[1]

    mods = _adaln(jnp.concatenate([c_prompt, c_sample], axis=0), w_ada, b_ada)

    def layer_mods(layer):
        m = mods[layer]
        parts = [m[:, i * D_MODEL:(i + 1) * D_MODEL] for i in range(6)]
        return [p[:Bp] for p in parts], [p[Bp:] for p in parts]

    xp, xs = x_prompt, x_sample
    ab_p, ab_s, c_p, c_s = [], [], [], []
    for layer in range(DEPTH):
        mp, ms = layer_mods(layer)
        wf_in = w_ffn_in[layer].astype(BF16)
        wf_out = w_ffn_out[layer].astype(BF16)
        i = layer // 2
        if layer % 2 == 0:
            w_in = _ab_in_weight(ab_w_in[i])
            w_out = ab_w_out[i].astype(BF16)
            wo_a, wo_b = w_out[:GLA_VAL_W], w_out[GLA_VAL_W:]
            cw = _cmp_weights(ab_cmp_pe[i], ab_cmp_w1[i], ab_cmp_w2[i])
            cw_t = _cmp_weights_t(ab_cmp_pe[i], ab_cmp_w1[i], ab_cmp_w2[i])

            proj, kv_t = _modmm(xp, mp[0], mp[1], w_in, PROMPT_ROWS, t_cols=(AB_KV_BLK * LANES, 6))
            zero_state = jnp.zeros((Bp, GLA_HEADS, GLA_DK, GLA_DV), F32)
            o_a, s_a = _gla(proj, Bp, Tp, ab_w_gk2[i], ab_b_gk[i], ab_gla_norm[i], zero_state, min(64, Tp))
            kc, vc = _compress_dense(proj, Bp, Tp, cw)
            o_b = _nsa_prompt(proj, Bp, Tp, kc, vc)
            x1 = _outproj_ln([o_a, o_b], [wo_a, wo_b], xp, mp[2], ln_g[layer, 0], ln_b[layer, 0], PROMPT_ROWS)
            n_keep = min(NSA_WINDOW, Tp)
            kv_out_t = lambda a: jnp.transpose(a.reshape(Bp, NSA_KV_HEADS, NSA_HEAD_DIM, a.shape[-1]), (0, 3, 1, 2))
            ab_p.append(tuple(kv_out_t(kv_t[j]) for j in range(4))
                        + (kv_out_t(kv_t[4][:, :, Tp - n_keep:]), kv_out_t(kv_t[5][:, :, Tp - n_keep:]), s_a))
            xp = _ffn_ln(x1, mp[3], mp[4], mp[5], wf_in, wf_out, ln_g[layer, 1], ln_b[layer, 1], FFN_ROWS)

            proj = _modmm(xs, ms[0], ms[1], w_in, PROMPT_ROWS)
            o_a, s_a = _gla(proj, Bs, Ts, ab_w_gk2[i], ab_b_gk[i], ab_gla_norm[i], state_gla[i], min(64, Ts))
            pool = lambda c: jnp.transpose(c[i], (0, 2, 3, 1)).reshape(n_pool * PAGE_SIZE, LANES)
            n_buf = state_win_k.shape[2]
            win_t = lambda w: jnp.transpose(w[i], (0, 2, 3, 1)).reshape(Bs, LANES, n_buf)
            kc, vc = _compress_paged(page_table, pool(cache_cmp_k), pool(cache_cmp_v), cw_t)
            o_b, win_k, win_v = _nsa_sample(proj, page_table, kc, vc, pool(cache_sel_k), pool(cache_sel_v),
                                            win_t(state_win_k), win_t(state_win_v))
            win_out = lambda w: jnp.transpose(w.reshape(Bs, NSA_KV_HEADS, NSA_HEAD_DIM, n_buf), (0, 3, 1, 2))
            x1 = _outproj_ln([o_a, o_b], [wo_a, wo_b], xs, ms[2], ln_g[layer, 0], ln_b[layer, 0], PROMPT_ROWS)
            ab_s.append(tuple(_kv_out(proj, Bs, Ts, j) for j in range(4))
                        + (win_out(win_k), win_out(win_v), s_a))
            xs = _ffn_ln(x1, ms[3], ms[4], ms[5], wf_in, wf_out, ln_g[layer, 1], ln_b[layer, 1], FFN_ROWS)
        else:
            w_in = _c_in_weight(c_w_in[i])
            w_out = c_w_out[i].astype(BF16)
            keep = GDN_CONV - 1

            proj = _modmm(xp, mp[0], mp[1], w_in, PROMPT_ROWS // 2)
            o_c, s_c = _gdn(proj, Bp, Tp, c_conv_w[i], c_a_log[i], c_dt_bias[i], c_norm[i],
                            jnp.zeros((Bp, GDN_HEADS, GDN_DK, GDN_DV), F32),
                            jnp.zeros((Bp, SUBLANES, GDN_CONV_CH), F32), min(64, Tp))
            x1 = _outproj_ln([o_c], [w_out], xp, mp[2], ln_g[layer, 0], ln_b[layer, 0], PROMPT_ROWS)
            c_p.append((s_c, proj.reshape(Bp, Tp, C_COLS)[:, Tp - keep:, :GDN_CONV_CH]))
            xp = _ffn_ln(x1, mp[3], mp[4], mp[5], wf_in, wf_out, ln_g[layer, 1], ln_b[layer, 1], FFN_ROWS)

            proj = _modmm(xs, ms[0], ms[1], w_in, PROMPT_ROWS // 2)
            conv8 = jnp.concatenate([jnp.zeros((Bs, SUBLANES - keep, GDN_CONV_CH), F32), state_gdn_conv[i]], axis=1)
            o_c, s_c = _gdn(proj, Bs, Ts, c_conv_w[i], c_a_log[i], c_dt_bias[i], c_norm[i],
                            state_gdn[i], conv8, min(64, Ts))
            x1 = _outproj_ln([o_c], [w_out], xs, ms[2], ln_g[layer, 0], ln_b[layer, 0], PROMPT_ROWS)
            qkv_s = proj.reshape(Bs, Ts, C_COLS)[:, :, :GDN_CONV_CH]
            c_s.append((s_c, jnp.concatenate([state_gdn_conv[i], qkv_s], axis=1)[:, -keep:]))
            xs = _ffn_ln(x1, ms[3], ms[4], ms[5], wf_in, wf_out, ln_g[layer, 1], ln_b[layer, 1], FFN_ROWS)

    stack = lambda sts: [jnp.stack(z) for z in zip(*sts)]
    p_ab, s_ab = stack(ab_p), stack(ab_s)
    p_c, s_c = stack(c_p), stack(c_s)
    return (xp, xs, *p_ab, *p_c, *s_ab, *s_c)
```
